```python
import math
import jax
import jax.numpy as jnp
from jax import lax
import numpy as np

D_MODEL = 1024
BATCH = 4
SEQ = 8192
DEPTH = 4

GRID_W = 64
CTX_LEN = 256
HEAD_DIM = 64
N_SLOTS = D_MODEL // HEAD_DIM
CONV_CH = D_MODEL // 2
CONV_W = 3
NA_HEADS = N_SLOTS // 2
NA_ROWS = 8
NA_COLS = 16
SWA_HEADS = N_SLOTS // 2
SWA_KV_HEADS = max(1, SWA_HEADS // 4)
SWA_WINDOW = 128
BLOCK = 128
DIFF_HEADS = N_SLOTS // 4
D_FF = 256 * math.ceil(8 * D_MODEL / 3 / 256)
N_EXPERTS = 8
TOP_K = 2
D_FF_EXPERT = D_FF // 2
ROPE_THETA = 10000.0
EPS = 1e-6
NEG = -1e30
N_EVEN = (DEPTH + 1) // 2
N_ODD = DEPTH // 2
NA_W = NA_HEADS * HEAD_DIM
EV_Q = 3 * CONV_CH + NA_W
EV_KV = 2 * NA_W
EV_OUT = CONV_CH + NA_W
C_Q = SWA_HEADS * HEAD_DIM
C_KV = SWA_KV_HEADS * HEAD_DIM
D_QK = DIFF_HEADS * 2 * HEAD_DIM
D_V = DIFF_HEADS * 2 * HEAD_DIM
OD_Q = C_Q + D_QK
OD_KV = 2 * C_KV + D_QK + D_V
OD_OUT = C_Q + D_V

kernel_name = 'hybrid_dit_conv_na_swa_diff_moe'


def rms_norm(x, g):
    xf = x.astype(jnp.float32)
    y = xf * lax.rsqrt(jnp.mean(xf * xf, -1, keepdims=True) + EPS)
    return (y * g.astype(jnp.float32)).astype(x.dtype)


def split_heads(t, h):
    b, n, _ = t.shape
    return t.reshape(b, n, h, -1).transpose(0, 2, 1, 3)


def merge_heads(t):
    b, h, n, d = t.shape
    return t.transpose(0, 2, 1, 3).reshape(b, n, h * d)


def rope_tables(n):
    t = jnp.arange(n)
    pos = jnp.stack([t // GRID_W, t % GRID_W], -1).astype(jnp.float32)
    nq = HEAD_DIM // 4
    inv = ROPE_THETA ** (-jnp.arange(nq, dtype=jnp.float32) / nq)
    ang = pos[:, :, None] * inv
    return jnp.cos(ang), jnp.sin(ang)


def apply_rope(x, cos, sin):
    shp = x.shape
    xr = x.reshape(shp[:-1] + (2, 2, HEAD_DIM // 4))
    a, b = xr[..., 0, :], xr[..., 1, :]
    cos = cos.astype(x.dtype)
    sin = sin.astype(x.dtype)
    return jnp.stack([a * cos - b * sin, b * cos + a * sin], -2).reshape(shp)


def joint_softmax(parts, extra=None):
    m = parts[0].max(-1, keepdims=True)
    for p in parts[1:]:
        m = jnp.maximum(m, p.max(-1, keepdims=True))
    if extra is not None:
        m = jnp.maximum(m, extra)
    es = [jnp.exp(p - m) for p in parts]
    den = es[0].sum(-1, keepdims=True)
    for e in es[1:]:
        den = den + e.sum(-1, keepdims=True)
    if extra is not None:
        den = den + jnp.exp(extra - m)
    return [e / den for e in es]


def dense_attention(q, k, v, sink=None):
    b, hq, lq, d = q.shape
    hkv = k.shape[1]
    g = hq // hkv
    qg = q.reshape(b, hkv, g, lq, d)
    s = jnp.einsum('bkgqd,bkld->bkgql', qg, k, preferred_element_type=jnp.float32) * d ** -0.5
    extra = None if sink is None else sink.astype(jnp.float32).reshape(1, hkv, g, 1, 1)
    (p,) = joint_softmax([s], extra)
    o = jnp.einsum('bkgql,bkld->bkgqd', p.astype(v.dtype), v)
    return o.reshape(b, hq, lq, d)


def dwconv3(u, w):
    return lax.conv_general_dilated(u, w[:, None, :].astype(u.dtype), window_strides=(1,),
                                    padding=[(1, 1)], dimension_numbers=('NWC', 'WIO', 'NWC'),
                                    feature_group_count=u.shape[-1])


def neighbourhood_attention(q, k, v, k_ctx, v_ctx, rpb):
    b, h, n, d = q.shape
    rows = n // GRID_W
    kh = min(NA_ROWS, rows)
    r = jnp.arange(rows)
    col = jnp.arange(GRID_W)
    r0 = jnp.clip(r - kh // 2, 0, rows - kh)
    row_idx = r0[:, None] + jnp.arange(kh)[None, :]
    c0 = jnp.clip(col - NA_COLS // 2, 0, GRID_W - NA_COLS)
    qg = q.reshape(b, h, rows, GRID_W, d)
    kg = k.reshape(b, h, rows, GRID_W, d)[:, :, row_idx]
    vg = v.reshape(b, h, rows, GRID_W, d)[:, :, row_idx]
    dr = row_idx - r[:, None] + NA_ROWS - 1
    dc = jnp.clip(col[None, :] - col[:, None] + NA_COLS - 1, 0, 2 * NA_COLS - 2)
    col_ok = (col[None, :] >= c0[:, None]) & (col[None, :] < c0[:, None] + NA_COLS)
    bias = rpb.astype(jnp.float32)[:, dr[:, None, :, None], dc[None, :, None, :]]
    bias = jnp.where(col_ok[None, None, :, None, :], bias, NEG)
    scale = d ** -0.5
    s_loc = jnp.einsum('bhrqd,bhrjkd->bhrqjk', qg, kg, preferred_element_type=jnp.float32) * scale + bias
    s_ctx = jnp.einsum('bhrqd,bhld->bhrql', qg, k_ctx, preferred_element_type=jnp.float32) * scale
    p_loc, p_ctx = joint_softmax([s_loc.reshape(b, h, rows, GRID_W, kh * GRID_W), s_ctx])
    p_loc = p_loc.reshape(b, h, rows, GRID_W, kh, GRID_W).astype(v.dtype)
    o = (jnp.einsum('bhrqjk,bhrjkd->bhrqd', p_loc, vg)
         + jnp.einsum('bhrql,bhld->bhrqd', p_ctx.astype(v.dtype), v_ctx))
    return o.reshape(b, h, n, d)


def window_gqa(q, k, v, k_ctx, v_ctx, sink):
    b, hq, n, d = q.shape
    hkv = k.shape[1]
    g = hq // hkv
    nb = n // BLOCK
    qb = q.reshape(b, hkv, g, nb, BLOCK, d)
    pad = ((0, 0), (0, 0), (BLOCK, BLOCK), (0, 0))
    band = (jnp.arange(nb) * BLOCK)[:, None] + jnp.arange(3 * BLOCK)[None, :]
    kb = jnp.pad(k, pad)[:, :, band]
    vb = jnp.pad(v, pad)[:, :, band]
    qpos = (jnp.arange(nb) * BLOCK)[:, None, None] + jnp.arange(BLOCK)[None, :, None]
    kpos = band[:, None, :] - BLOCK
    ok = (jnp.abs(kpos - qpos) <= SWA_WINDOW) & (kpos >= 0) & (kpos < n)
    scale = d ** -0.5
    s_loc = jnp.where(ok, jnp.einsum('bkgnqd,bknmd->bkgnqm', qb, kb, preferred_element_type=jnp.float32) * scale, NEG)
    s_ctx = jnp.einsum('bkgnqd,bkld->bkgnql', qb, k_ctx, preferred_element_type=jnp.float32) * scale
    extra = sink.astype(jnp.float32).reshape(1, hkv, g, 1, 1, 1)
    p_loc, p_ctx = joint_softmax([s_loc, s_ctx], extra)
    o = (jnp.einsum('bkgnqm,bknmd->bkgnqd', p_loc.astype(v.dtype), vb)
         + jnp.einsum('bkgnql,bkld->bkgnqd', p_ctx.astype(v.dtype), v_ctx))
    return o.reshape(b, hq, n, d)


def diff_attend(q, kvs, lam):
    scale = q.shape[-1] ** -0.5
    ss = [jnp.einsum('bhiqd,bhikd->bhiqk', q, k, preferred_element_type=jnp.float32) * scale for k, _ in kvs]
    ps = joint_softmax(ss)
    out = None
    for p, (_, v) in zip(ps, kvs):
        a = (p[:, :, 0] - lam * p[:, :, 1]).astype(v.dtype)
        o = jnp.einsum('bhqk,bhkd->bhqd', a, v)
        out = o if out is None else out + o
    return out


def diff_attention_latent(q, k, v, k_ctx, v_ctx, lam):
    b, h, _, n, d = q.shape
    nb = n // BLOCK
    qb = jnp.moveaxis(q.reshape(b, h, 2, nb, BLOCK, d), 3, 0)
    o = lax.map(lambda qi: diff_attend(qi, [(k, v), (k_ctx, v_ctx)], lam), qb)
    return jnp.moveaxis(o, 0, 2).reshape(b, h, n, v.shape[-1])


def swiglu(x, w1, w3, w2):
    return (jax.nn.silu(x @ w1) * (x @ w3)) @ w2


def moe_swiglu(x, router, w1, w3, w2):
    shp = x.shape
    xt = x.reshape(-1, shp[-1])
    logits = (xt @ router).astype(jnp.float32)
    top_v, top_i = lax.top_k(logits, TOP_K)
    top_w = jax.nn.softmax(top_v, axis=-1)
    gates = jnp.sum(jax.nn.one_hot(top_i, N_EXPERTS, dtype=jnp.float32) * top_w[..., None], axis=1)
    gates = gates.astype(x.dtype)
    out = gates[:, 0:1] * swiglu(xt, w1[0], w3[0], w2[0])
    for e in range(1, N_EXPERTS):
        out = out + gates[:, e:e + 1] * swiglu(xt, w1[e], w3[e], w2[e])
    return out.reshape(shp)


def even_mixer(h_lat, h_ctx, w_in, conv_w, q_g, k_g, rpb, w_out, need_ctx):
    def query_side(p):
        gb, gc, u, q = jnp.split(p, [CONV_CH, 2 * CONV_CH, 3 * CONV_CH], -1)
        y_conv = gb * dwconv3(gc * u, conv_w)
        return y_conv, rms_norm(split_heads(q, NA_HEADS), q_g)

    def kv_side(p):
        k, v = jnp.split(p, 2, -1)
        return rms_norm(split_heads(k, NA_HEADS), k_g), split_heads(v, NA_HEADS)

    p_lat = h_lat @ w_in
    yc_lat, q_lat = query_side(p_lat[..., :EV_Q])
    k_lat, v_lat = kv_side(p_lat[..., EV_Q:])
    p_ctx = h_ctx @ (w_in if need_ctx else w_in[:, EV_Q:])
    k_ctx, v_ctx = kv_side(p_ctx[..., -EV_KV:])
    y_na = merge_heads(neighbourhood_attention(q_lat, k_lat, v_lat, k_ctx, v_ctx, rpb))
    y_lat = jnp.concatenate([yc_lat, y_na], -1) @ w_out
    y_ctx = None
    if need_ctx:
        yc_ctx, q_ctx = query_side(p_ctx[..., :EV_Q])
        y_na_ctx = merge_heads(dense_attention(q_ctx, k_ctx, v_ctx))
        y_ctx = jnp.concatenate([yc_ctx, y_na_ctx], -1) @ w_out
    return y_lat, y_ctx


def odd_mixer(h_lat, h_ctx, w_in, cq_g, ck_g, sink, dq_g, dk_g, lam_q1, lam_k1, lam_q2, lam_k2,
              subln_g, w_out, lam_init, cos, sin, need_ctx):
    def diff_split(t):
        b, n, _ = t.shape
        return t.reshape(b, n, DIFF_HEADS, 2, HEAD_DIM).transpose(0, 2, 3, 1, 4)

    def query_side(p, rope):
        cq, dq = jnp.split(p, [C_Q], -1)
        cq = rms_norm(split_heads(cq, SWA_HEADS), cq_g)
        dq = rms_norm(diff_split(dq), dq_g)
        if rope:
            cq, dq = apply_rope(cq, cos, sin), apply_rope(dq, cos, sin)
        return cq, dq

    def kv_side(p, rope):
        ck, cv, dk, dv = jnp.split(p, [C_KV, 2 * C_KV, 2 * C_KV + D_QK], -1)
        ck = rms_norm(split_heads(ck, SWA_KV_HEADS), ck_g)
        dk = rms_norm(diff_split(dk), dk_g)
        if rope:
            ck, dk = apply_rope(ck, cos, sin), apply_rope(dk, cos, sin)
        return ck, split_heads(cv, SWA_KV_HEADS), dk, split_heads(dv, DIFF_HEADS)

    f32 = jnp.float32
    lam = (jnp.exp(jnp.sum(lam_q1.astype(f32) * lam_k1.astype(f32)))
           - jnp.exp(jnp.sum(lam_q2.astype(f32) * lam_k2.astype(f32))) + lam_init)

    def merge(yc, yd):
        yd = rms_norm(yd, subln_g) * (1.0 - lam_init)
        return jnp.concatenate([merge_heads(yc), merge_heads(yd)], -1) @ w_out

    p_lat = h_lat @ w_in
    cq_l, dq_l = query_side(p_lat[..., :OD_Q], True)
    ck_l, cv_l, dk_l, dv_l = kv_side(p_lat[..., OD_Q:], True)
    p_ctx = h_ctx @ (w_in if need_ctx else w_in[:, OD_Q:])
    ck_c, cv_c, dk_c, dv_c = kv_side(p_ctx[..., -OD_KV:], False)
    y_lat = merge(window_gqa(cq_l, ck_l, cv_l, ck_c, cv_c, sink),
                  diff_attention_latent(dq_l, dk_l, dv_l, dk_c, dv_c, lam))
    y_ctx = None
    if need_ctx:
        cq_c, dq_c = query_side(p_ctx[..., :OD_Q], False)
        y_ctx = merge(dense_attention(cq_c, ck_c, cv_c, sink), diff_attend(dq_c, [(dk_c, dv_c)], lam))
    return y_lat, y_ctx


def setup_inputs(seed: int = 0) -> dict:
    key = jax.random.key(seed)
    ks = iter(jax.random.split(key, 40))
    D = D_MODEL

    def nrm(shape, scale=1.0):
        return jax.random.normal(next(ks), shape, jnp.float32) * scale

    def gain(shape):
        return 1.0 + nrm(shape, 0.05)

    return {
        'x': nrm((BATCH, SEQ, D)),
        'c': nrm((BATCH, D)),
        'ctx': nrm((BATCH, CTX_LEN, D)),
        'c_ctx': nrm((D,)),
        'ada_w': nrm((DEPTH, D, 6 * D), 0.5 * D ** -0.5),
        'ada_b': nrm((DEPTH, 6 * D), 0.02),
        'norm1_g': gain((DEPTH, D)),
        'norm2_g': gain((DEPTH, D)),
        'ev_w_in': nrm((N_EVEN, D, EV_Q + EV_KV), D ** -0.5),
        'ev_conv_w': nrm((N_EVEN, CONV_W, CONV_CH), CONV_W ** -0.5),
        'ev_q_g': gain((N_EVEN, HEAD_DIM)),
        'ev_k_g': gain((N_EVEN, HEAD_DIM)),
        'ev_rpb': nrm((N_EVEN, NA_HEADS, 2 * NA_ROWS - 1, 2 * NA_COLS - 1), 0.1),
        'ev_w_out': nrm((N_EVEN, EV_OUT, D), EV_OUT ** -0.5),
        'ffn_w1': nrm((N_EVEN, D, D_FF), D ** -0.5),
        'ffn_w3': nrm((N_EVEN, D, D_FF), D ** -0.5),
        'ffn_w2': nrm((N_EVEN, D_FF, D), D_FF ** -0.5),
        'od_w_in': nrm((N_ODD, D, OD_Q + OD_KV), D ** -0.5),
        'od_cq_g': gain((N_ODD, HEAD_DIM)),
        'od_ck_g': gain((N_ODD, HEAD_DIM)),
        'od_sink': nrm((N_ODD, SWA_HEADS), 0.5),
        'od_dq_g': gain((N_ODD, HEAD_DIM)),
        'od_dk_g': gain((N_ODD, HEAD_DIM)),
        'od_lam_q1': nrm((N_ODD, HEAD_DIM), 0.1),
        'od_lam_k1': nrm((N_ODD, HEAD_DIM), 0.1),
        'od_lam_q2': nrm((N_ODD, HEAD_DIM), 0.1),
        'od_lam_k2': nrm((N_ODD, HEAD_DIM), 0.1),
        'od_subln_g': gain((N_ODD, 2 * HEAD_DIM)),
        'od_w_out': nrm((N_ODD, OD_OUT, D), OD_OUT ** -0.5),
        'moe_router': nrm((N_ODD, D, N_EXPERTS), D ** -0.5),
        'moe_w1': nrm((N_ODD, N_EXPERTS, D, D_FF_EXPERT), D ** -0.5),
        'moe_w3': nrm((N_ODD, N_EXPERTS, D, D_FF_EXPERT), D ** -0.5),
        'moe_w2': nrm((N_ODD, N_EXPERTS, D_FF_EXPERT, D), D_FF_EXPERT ** -0.5),
    }


def reference(x, c, ctx, c_ctx, ada_w, ada_b, norm1_g, norm2_g, ev_w_in, ev_conv_w, ev_q_g, ev_k_g,
              ev_rpb, ev_w_out, ffn_w1, ffn_w3, ffn_w2, od_w_in, od_cq_g, od_ck_g, od_sink, od_dq_g,
              od_dk_g, od_lam_q1, od_lam_k1, od_lam_q2, od_lam_k2, od_subln_g, od_w_out, moe_router,
              moe_w1, moe_w3, moe_w2):
    n = x.shape[1]
    cos, sin = rope_tables(n)
    h, hc = x, ctx
    s_lat = jax.nn.silu(c)
    s_ctx = jax.nn.silu(c_ctx)
    for l in range(DEPTH):
        last = l == DEPTH - 1
        i = l // 2
        mod = (s_lat @ ada_w[l] + ada_b[l])[:, None, :]
        sh1, sc1, g1, sh2, sc2, g2 = jnp.split(mod, 6, -1)
        n_mod = 2 if last else 6
        modc = jnp.split(s_ctx @ ada_w[l][:, :n_mod * D_MODEL] + ada_b[l][:n_mod * D_MODEL], n_mod)
        a_lat = rms_norm(h, norm1_g[l]) * (1 + sc1) + sh1
        a_ctx = rms_norm(hc, norm1_g[l]) * (1 + modc[1]) + modc[0]
        if l % 2 == 0:
            y, yc = even_mixer(a_lat, a_ctx, ev_w_in[i], ev_conv_w[i], ev_q_g[i], ev_k_g[i], ev_rpb[i],
                               ev_w_out[i], not last)
            ffn = lambda t: swiglu(t, ffn_w1[i], ffn_w3[i], ffn_w2[i])
        else:
            lam_init = 0.8 - 0.6 * math.exp(-0.3 * l)
            y, yc = odd_mixer(a_lat, a_ctx, od_w_in[i], od_cq_g[i], od_ck_g[i], od_sink[i], od_dq_g[i],
                              od_dk_g[i], od_lam_q1[i], od_lam_k1[i], od_lam_q2[i], od_lam_k2[i],
                              od_subln_g[i], od_w_out[i], lam_init, cos, sin, not last)
            ffn = lambda t: moe_swiglu(t, moe_router[i], moe_w1[i], moe_w3[i], moe_w2[i])
        h = h + g1 * y
        h = h + g2 * ffn(rms_norm(h, norm2_g[l]) * (1 + sc2) + sh2)
        if not last:
            hc = hc + modc[2] * yc
            hc = hc + modc[5] * ffn(rms_norm(hc, norm2_g[l]) * (1 + modc[4]) + modc[3])
    return h
```

```python
import functools
import math

import jax
import jax.numpy as jnp
from jax import lax
from jax.experimental import pallas as pl
from jax.experimental.pallas import tpu as pltpu

F32 = jnp.float32
BF16 = jnp.bfloat16

LANES = 128
BF16_SUBLANES = 16
VMEM_LIMIT = 56 * 1024 * 1024

HEAD_DIM = 64
GRID_W = 64
CONV_W = 3
NA_ROWS = 8
NA_COLS = 16
NA_QROWS = 8
SWA_WINDOW = 128
BLOCK = 128
N_EXPERTS = 8
ROPE_THETA = 10000.0
EPS = 1e-6
NEG = -1e30
QK_SCALE = HEAD_DIM ** -0.5

NT_DIMS = (((1,), (1,)), ((), ()))


def _params(sem):
    return pltpu.CompilerParams(dimension_semantics=sem, vmem_limit_bytes=VMEM_LIMIT)


def _resident(shape, index_map):
    return pl.BlockSpec(shape, index_map, pipeline_mode=pl.Buffered(1))


def _lane_iota():
    return lax.broadcasted_iota(jnp.int32, (1, LANES), 1)


def _swap_halves(x):
    return jnp.concatenate([x[:, HEAD_DIM:], x[:, :HEAD_DIM]], axis=1)


def _modulated_norm(x, g, sc, sh):
    ms = jnp.mean(x * x, axis=-1, keepdims=True)
    return (x * lax.rsqrt(ms + EPS)) * (g * (1.0 + sc)) + sh


def _mod_kernel(s_ref, w_ref, b_ref, o_ref):
    s = s_ref[...]
    s = s * (1.0 / (1.0 + jnp.exp(-s)))
    o_ref[0] = jnp.dot(s.astype(BF16), w_ref[0].astype(BF16), preferred_element_type=F32) + b_ref[0]


def _modulation(s_rows, ada_w, ada_b):
    depth, d, n_out = ada_w.shape
    tn = n_out // 4
    return pl.pallas_call(
        _mod_kernel,
        grid=(depth, n_out // tn),
        in_specs=[pl.BlockSpec(s_rows.shape, lambda l, j: (0, 0)),
                  pl.BlockSpec((1, d, tn), lambda l, j: (l, 0, j)),
                  pl.BlockSpec((1, 1, tn), lambda l, j: (l, 0, j))],
        out_specs=pl.BlockSpec((1, s_rows.shape[0], tn), lambda l, j: (l, 0, j)),
        out_shape=jax.ShapeDtypeStruct((depth, s_rows.shape[0], n_out), F32),
        compiler_params=_params(("arbitrary", "arbitrary")),
        name="modulation",
    )(s_rows, ada_w, ada_b.reshape(depth, 1, n_out))


def _head_sumsq(z):
    r = lax.broadcasted_iota(jnp.int32, (LANES, LANES), 0) // HEAD_DIM
    c = lax.broadcasted_iota(jnp.int32, (LANES, LANES), 1) // HEAD_DIM
    same_head = jnp.where(r == c, 1.0, 0.0).astype(BF16)
    z2 = z * z
    hi = z2.astype(BF16)
    lo = (z2 - hi.astype(F32)).astype(BF16)
    return (jnp.dot(hi, same_head, preferred_element_type=F32)
            + jnp.dot(lo, same_head, preferred_element_type=F32))


def _norm_proj_kernel(*refs, segs, rope):
    if rope:
        x_ref, g_ref, sc_ref, sh_ref, w_ref, gain_ref, cos_ref, sin_ref, o_ref = refs
    else:
        x_ref, g_ref, sc_ref, sh_ref, w_ref, gain_ref, o_ref = refs
    a = _modulated_norm(x_ref[...], g_ref[...], sc_ref[0], sh_ref[0]).astype(BF16)
    first_half = (_lane_iota() % (HEAD_DIM // 2)) < (HEAD_DIM // 4)
    for c0, c1, normed in segs:
        acc = jnp.dot(a, w_ref[:, c0:c1], preferred_element_type=F32)
        if not normed:
            o_ref[:, c0:c1] = acc.astype(BF16)
            continue
        for j in range((c1 - c0) // LANES):
            z = acc[:, j * LANES:(j + 1) * LANES]
            lo = c0 + j * LANES
            z = z * lax.rsqrt(_head_sumsq(z) * (1.0 / HEAD_DIM) + EPS) * gain_ref[:, lo:lo + LANES]
            if rope:
                partner = jnp.where(first_half,
                                    pltpu.roll(z, LANES - HEAD_DIM // 4, axis=1),
                                    pltpu.roll(z, HEAD_DIM // 4, axis=1))
                z = z * cos_ref[...] + partner * sin_ref[...]
            o_ref[:, lo:lo + LANES] = z.astype(BF16)


def _norm_proj(h, nb, g, sc, sh, w, gain, segs, rope_tabs, tm):
    t, d = h.shape
    n_out = w.shape[1]
    steps = t // nb // tm
    in_specs = [pl.BlockSpec((tm, d), lambda b, i: (b * steps + i, 0)),
                _resident((1, d), lambda b, i: (0, 0)),
                pl.BlockSpec((1, 1, d), lambda b, i: (b, 0, 0)),
                pl.BlockSpec((1, 1, d), lambda b, i: (b, 0, 0)),
                _resident((d, n_out), lambda b, i: (0, 0)),
                _resident((1, n_out), lambda b, i: (0, 0))]
    args = [h, g, sc, sh, w, gain]
    if rope_tabs is not None:
        in_specs += [pl.BlockSpec((tm, LANES), lambda b, i: (i, 0))] * 2
        args += list(rope_tabs)
    return pl.pallas_call(
        functools.partial(_norm_proj_kernel, segs=segs, rope=rope_tabs is not None),
        grid=(nb, steps),
        in_specs=in_specs,
        out_specs=pl.BlockSpec((tm, n_out), lambda b, i: (b * steps + i, 0)),
        out_shape=jax.ShapeDtypeStruct((t, n_out), BF16),
        compiler_params=_params(("parallel", "parallel")),
        name="norm_proj",
    )(*args)


def _gated_conv(gb_ref, gc_ref, u_ref, gcp_ref, up_ref, gcn_ref, un_ref, cw_ref):
    i, steps = pl.program_id(1), pl.num_programs(1)
    v = gc_ref[...].astype(F32) * u_ref[...].astype(F32)
    tm = v.shape[0]
    last = BF16_SUBLANES - 1
    prev_row = gcp_ref[last:last + 1, :].astype(F32) * up_ref[last:last + 1, :].astype(F32)
    next_row = gcn_ref[0:1, :].astype(F32) * un_ref[0:1, :].astype(F32)
    prev_row = jnp.where(i > 0, prev_row, 0.0)
    next_row = jnp.where(i < steps - 1, next_row, 0.0)
    row = lax.broadcasted_iota(jnp.int32, (tm, 1), 0)
    v_prev = jnp.where(row == 0, prev_row, pltpu.roll(v, 1, axis=0))
    v_next = jnp.where(row == tm - 1, next_row, pltpu.roll(v, tm - 1, axis=0))
    cw = cw_ref[...]
    conv = cw[0:1, :] * v_prev + cw[1:2, :] * v + cw[2:3, :] * v_next
    return gb_ref[...].astype(F32) * conv


def _out_proj_kernel(*refs, conv):
    if conv:
        (gb_ref, gc_ref, u_ref, gcp_ref, up_ref, gcn_ref, un_ref, cw_ref,
         yb_ref, wa_ref, wb_ref, h_ref, gate_ref, o_ref) = refs
        ya = _gated_conv(gb_ref, gc_ref, u_ref, gcp_ref, up_ref, gcn_ref, un_ref, cw_ref).astype(BF16)
    else:
        ya_ref, yb_ref, wa_ref, wb_ref, h_ref, gate_ref, o_ref = refs
        ya = ya_ref[...]
    y = (jnp.dot(ya, wa_ref[...], preferred_element_type=F32)
         + jnp.dot(yb_ref[...], wb_ref[...], preferred_element_type=F32))
    o_ref[...] = h_ref[...] + gate_ref[0] * y


def _out_proj(h, nb, gate, ya_src, yb, w_out, conv_w, tm):
    t, d = h.shape
    steps = t // nb // tm
    wa_rows = w_out.shape[0] - yb.shape[1]
    w_a, w_b = w_out[:wa_rows], w_out[wa_rows:]
    row = lambda b, i: (b * steps + i, 0)
    if conv_w is not None:
        cc = conv_w.shape[1]
        hb = tm // BF16_SUBLANES
        n_halo = t // BF16_SUBLANES
        prev = lambda col: (lambda b, i: (jnp.maximum((b * steps + i) * hb - 1, 0), col))
        nxt = lambda col: (lambda b, i: (jnp.minimum((b * steps + i + 1) * hb, n_halo - 1), col))
        in_specs = [pl.BlockSpec((tm, cc), lambda b, i: (b * steps + i, 0)),
                    pl.BlockSpec((tm, cc), lambda b, i: (b * steps + i, 1)),
                    pl.BlockSpec((tm, cc), lambda b, i: (b * steps + i, 2)),
                    pl.BlockSpec((BF16_SUBLANES, cc), prev(1)),
                    pl.BlockSpec((BF16_SUBLANES, cc), prev(2)),
                    pl.BlockSpec((BF16_SUBLANES, cc), nxt(1)),
                    pl.BlockSpec((BF16_SUBLANES, cc), nxt(2)),
                    _resident(conv_w.shape, lambda b, i: (0, 0))]
        args = [ya_src] * 7 + [conv_w]
    else:
        in_specs = [pl.BlockSpec((tm, wa_rows), row)]
        args = [ya_src]
    in_specs += [pl.BlockSpec((tm, yb.shape[1]), row),
                 _resident(w_a.shape, lambda b, i: (0, 0)),
                 _resident(w_b.shape, lambda b, i: (0, 0)),
                 pl.BlockSpec((tm, d), row),
                 pl.BlockSpec((1, 1, d), lambda b, i: (b, 0, 0))]
    args += [yb, w_a, w_b, h, gate]
    return pl.pallas_call(
        functools.partial(_out_proj_kernel, conv=conv_w is not None),
        grid=(nb, steps),
        in_specs=in_specs,
        out_specs=pl.BlockSpec((tm, d), row),
        out_shape=jax.ShapeDtypeStruct((t, d), F32),
        compiler_params=_params(("parallel", "parallel")),
        name="out_proj",
    )(*args)


def _na_bias(rpb, rows):
    nblk = rows // NA_QROWS
    qi = jnp.arange(NA_QROWS * GRID_W)
    kj = jnp.arange(2 * NA_QROWS * GRID_W)
    out = []
    for rb in (0, 1, nblk - 1):
        r = rb * NA_QROWS + qi // GRID_W
        c = qi % GRID_W
        r0 = jnp.clip(r - NA_ROWS // 2, 0, rows - NA_ROWS)
        c0 = jnp.clip(c - NA_COLS // 2, 0, GRID_W - NA_COLS)
        rk = rb * NA_QROWS - NA_ROWS // 2 + kj // GRID_W
        ck = kj % GRID_W
        ok = ((rk[None, :] >= r0[:, None]) & (rk[None, :] < r0[:, None] + NA_ROWS)
              & (ck[None, :] >= c0[:, None]) & (ck[None, :] < c0[:, None] + NA_COLS))
        dr = jnp.clip(rk[None, :] - r[:, None] + NA_ROWS - 1, 0, 2 * NA_ROWS - 2)
        dc = jnp.clip(ck[None, :] - c[:, None] + NA_COLS - 1, 0, 2 * NA_COLS - 2)
        out.append(jnp.where(ok[None], rpb.astype(F32)[:, dr, dc], NEG))
    return jnp.stack(out)


def _softmax_pv(s_parts, v_parts, extra=None):
    m = s_parts[0].max(axis=-1, keepdims=True)
    for s in s_parts[1:]:
        m = jnp.maximum(m, s.max(axis=-1, keepdims=True))
    if extra is not None:
        m = jnp.maximum(m, extra)
    den = None
    o = None
    for s, v in zip(s_parts, v_parts):
        e = jnp.exp(s - m)
        d = e.sum(axis=-1, keepdims=True)
        pv = jnp.dot(e.astype(BF16), v, preferred_element_type=F32)
        den = d if den is None else den + d
        o = pv if o is None else o + pv
    if extra is not None:
        den = den + jnp.exp(extra - m)
    return o / den


def _na_kernel(q_ref, k0, k1, k2, k3, v0, v1, v2, v3, kc_ref, vc_ref, bias_ref, o_ref):
    q = q_ref[...]
    k = jnp.concatenate([k0[...], k1[...], k2[...], k3[...]], axis=0)
    v = jnp.concatenate([v0[...], v1[...], v2[...], v3[...]], axis=0)
    kc, vc = kc_ref[...], vc_ref[...]
    lane = _lane_iota()
    outs = []
    for hh in range(2):
        qh = jnp.where((lane // HEAD_DIM) == hh, q, jnp.zeros_like(q))
        s_loc = lax.dot_general(qh, k, NT_DIMS, preferred_element_type=F32) + bias_ref[0, hh]
        s_ctx = lax.dot_general(qh, kc, NT_DIMS, preferred_element_type=F32)
        outs.append(_softmax_pv([s_loc, s_ctx], [v, vc]))
    o_ref[...] = jnp.where(lane < HEAD_DIM, outs[0], outs[1]).astype(BF16)


def _na_attention(p_lat, p_ctx, bias, nb, n, ctx_len, q_col, k_col, v_col, n_heads):
    tq = NA_QROWS * GRID_W
    tw = tq // 2
    qsteps = n // tq
    wsteps = n // tw

    def win(col, j):
        return pl.BlockSpec(
            (tw, LANES),
            lambda hp, b, rb: (b * wsteps + jnp.clip(2 * rb - 1 + j, 0, wsteps - 1), col + hp))

    variant = lambda rb: jnp.where(rb == 0, 0, jnp.where(rb == qsteps - 1, 2, 1))
    in_specs = ([pl.BlockSpec((tq, LANES), lambda hp, b, rb: (b * qsteps + rb, q_col + hp))]
                + [win(k_col, j) for j in range(4)] + [win(v_col, j) for j in range(4)]
                + [pl.BlockSpec((ctx_len, LANES), lambda hp, b, rb: (b, k_col + hp)),
                   pl.BlockSpec((ctx_len, LANES), lambda hp, b, rb: (b, v_col + hp)),
                   pl.BlockSpec((1, 2, tq, 2 * tq), lambda hp, b, rb: (variant(rb), hp, 0, 0))])
    return pl.pallas_call(
        _na_kernel,
        grid=(n_heads // 2, nb, qsteps),
        in_specs=in_specs,
        out_specs=pl.BlockSpec((tq, LANES), lambda hp, b, rb: (b * qsteps + rb, hp)),
        out_shape=jax.ShapeDtypeStruct((nb * n, n_heads * HEAD_DIM), BF16),
        compiler_params=_params(("parallel", "parallel", "arbitrary")),
        name="na_attention",
    )(*([p_lat] * 9 + [p_ctx, p_ctx, bias]))


def _gqa_kernel(*refs, n_q, n_kv, has_local, has_sink, n_tokens):
    refs = list(refs)
    q_ref = refs.pop(0)
    if has_local:
        kw = [refs.pop(0)[...] for _ in range(4)]
        vw = [refs.pop(0)[...] for _ in range(4)]
        k_loc = jnp.concatenate(kw, axis=0)
        v_loc = jnp.concatenate(vw, axis=0)
    kc_ref, vc_ref = refs.pop(0), refs.pop(0)
    sink_ref = refs.pop(0) if has_sink else None
    o_ref = refs.pop(0)

    tq = q_ref.shape[0]
    lane = _lane_iota()
    if has_local:
        base = pl.program_id(1) * tq
        qpos = base + lax.broadcasted_iota(jnp.int32, (tq, 1), 0)
        kpos = base - BLOCK + lax.broadcasted_iota(jnp.int32, (1, tq + 2 * BLOCK), 1)
        ok = (jnp.abs(kpos - qpos) <= SWA_WINDOW) & (kpos >= 0) & (kpos < n_tokens)

    group = n_q // n_kv
    cache = {}

    def kv_tiles(kvh, half):
        key = (kvh, half)
        if key not in cache:
            cg = kvh // 2
            kc = kc_ref[:, cg * LANES:(cg + 1) * LANES]
            vc = vc_ref[:, cg * LANES:(cg + 1) * LANES]
            tiles = [kc, vc] + ([k_loc, v_loc] if has_local else [])
            if kvh % 2 != half:
                tiles = [_swap_halves(x) for x in tiles]
            cache[key] = tiles
        return cache[key]

    for g in range(n_q // 2):
        qg = q_ref[:, g * LANES:(g + 1) * LANES]
        outs = []
        for half in range(2):
            h = 2 * g + half
            tiles = kv_tiles(h // group, half)
            qh = jnp.where((lane // HEAD_DIM) == half, qg, jnp.zeros_like(qg))
            s_parts = [lax.dot_general(qh, tiles[0], NT_DIMS, preferred_element_type=F32)]
            v_parts = [tiles[1]]
            if has_local:
                s_loc = lax.dot_general(qh, tiles[2], NT_DIMS, preferred_element_type=F32)
                s_parts.append(jnp.where(ok, s_loc, NEG))
                v_parts.append(tiles[3])
            extra = sink_ref[h:h + 1, 0:1] if has_sink else None
            outs.append(_softmax_pv(s_parts, v_parts, extra))
        o_ref[:, g * LANES:(g + 1) * LANES] = jnp.where(lane < HEAD_DIM, outs[0], outs[1]).astype(BF16)


def _gqa_attention(p_q, p_ctx, sink, nb, n, ctx_len, q_col, k_col, v_col, n_q, n_kv, has_local, tq):
    qsteps = n // tq
    kv_w = n_kv * HEAD_DIM
    q_w = n_q * HEAD_DIM
    in_specs = [pl.BlockSpec((tq, q_w), lambda b, i: (b * qsteps + i, q_col * LANES // q_w))]
    args = [p_q]
    if has_local:
        assert tq == 2 * BLOCK and kv_w == LANES
        wsteps = n // BLOCK

        def win(col, j):
            return pl.BlockSpec(
                (BLOCK, LANES),
                lambda b, i: (b * wsteps + jnp.clip(2 * i - 1 + j, 0, wsteps - 1), col))

        in_specs += [win(k_col, j) for j in range(4)] + [win(v_col, j) for j in range(4)]
        args += [p_q] * 8
    in_specs += [pl.BlockSpec((ctx_len, kv_w), lambda b, i: (b, k_col * LANES // kv_w)),
                 pl.BlockSpec((ctx_len, kv_w), lambda b, i: (b, v_col * LANES // kv_w))]
    args += [p_ctx, p_ctx]
    if sink is not None:
        in_specs.append(_resident(sink.shape, lambda b, i: (0, 0)))
        args.append(sink)
    return pl.pallas_call(
        functools.partial(_gqa_kernel, n_q=n_q, n_kv=n_kv, has_local=has_local,
                          has_sink=sink is not None, n_tokens=n),
        grid=(nb, qsteps),
        in_specs=in_specs,
        out_specs=pl.BlockSpec((tq, q_w), lambda b, i: (b * qsteps + i, 0)),
        out_shape=jax.ShapeDtypeStruct((nb * n, q_w), BF16),
        compiler_params=_params(("parallel", "parallel")),
        name="gqa_attention",
    )(*args)


def _diff_kernel(*refs, has_latent, lam_init):
    if has_latent:
        q_ref, k_ref, v_ref, kc_ref, vc_ref, lam_ref, subg_ref, o_ref, m_sc, l_sc, acc_sc = refs
    else:
        q_ref, kc_ref, vc_ref, lam_ref, subg_ref, o_ref, m_sc, l_sc, acc_sc = refs
    kb = pl.program_id(3)
    lane = _lane_iota()
    q = q_ref[...]
    zero = jnp.zeros_like(q)
    q_maps = [jnp.where(lane < HEAD_DIM, q, zero), jnp.where(lane >= HEAD_DIM, q, zero)]

    @pl.when(kb == 0)
    def _():
        kc, vc = kc_ref[...], vc_ref[...]
        for i in range(2):
            s = lax.dot_general(q_maps[i], kc, NT_DIMS, preferred_element_type=F32)
            m = s.max(axis=-1, keepdims=True)
            p = jnp.exp(s - m)
            m_sc[i] = m
            l_sc[i] = p.sum(axis=-1, keepdims=True)
            acc_sc[i] = jnp.dot(p.astype(BF16), vc, preferred_element_type=F32)

    if has_latent:
        k, v = k_ref[...], v_ref[...]
        for i in range(2):
            s = lax.dot_general(q_maps[i], k, NT_DIMS, preferred_element_type=F32)
            m_prev = m_sc[i]
            m_new = jnp.maximum(m_prev, s.max(axis=-1, keepdims=True))
            p = jnp.exp(s - m_new)
            alpha = jnp.exp(m_prev - m_new)
            l_sc[i] = alpha * l_sc[i] + p.sum(axis=-1, keepdims=True)
            acc_sc[i] = alpha * acc_sc[i] + jnp.dot(p.astype(BF16), v, preferred_element_type=F32)
            m_sc[i] = m_new

    @pl.when(kb == pl.num_programs(3) - 1)
    def _():
        lp = lam_ref[...]
        lam = (jnp.exp(jnp.sum(lp[0:1] * lp[1:2], axis=-1, keepdims=True))
               - jnp.exp(jnp.sum(lp[2:3] * lp[3:4], axis=-1, keepdims=True)) + lam_init)
        y = acc_sc[0] / l_sc[0] - lam * (acc_sc[1] / l_sc[1])
        ms = jnp.mean(y * y, axis=-1, keepdims=True)
        y = y * lax.rsqrt(ms + EPS) * subg_ref[...] * (1.0 - lam_init)
        o_ref[...] = y.astype(BF16)


def _diff_attention(p_q, p_lat, p_ctx, lam_rows, subg, nb, n_q, n_lat, ctx_len,
                    q_col, k_col, v_col, n_heads, lam_init, tq, tk):
    qsteps = n_q // tq
    has_latent = p_lat is not None
    ksteps = n_lat // tk if has_latent else 1
    in_specs = [pl.BlockSpec((tq, LANES), lambda b, h, i, j: (b * qsteps + i, q_col + h))]
    args = [p_q]
    if has_latent:
        in_specs += [pl.BlockSpec((tk, LANES), lambda b, h, i, j: (b * ksteps + j, k_col + h)),
                     pl.BlockSpec((tk, LANES), lambda b, h, i, j: (b * ksteps + j, v_col + h))]
        args += [p_lat, p_lat]
    in_specs += [pl.BlockSpec((ctx_len, LANES), lambda b, h, i, j: (b, k_col + h)),
                 pl.BlockSpec((ctx_len, LANES), lambda b, h, i, j: (b, v_col + h)),
                 _resident(lam_rows.shape, lambda b, h, i, j: (0, 0)),
                 _resident(subg.shape, lambda b, h, i, j: (0, 0))]
    args += [p_ctx, p_ctx, lam_rows, subg]
    return pl.pallas_call(
        functools.partial(_diff_kernel, has_latent=has_latent, lam_init=lam_init),
        grid=(nb, n_heads, qsteps, ksteps),
        in_specs=in_specs,
        out_specs=pl.BlockSpec((tq, LANES), lambda b, h, i, j: (b * qsteps + i, h)),
        out_shape=jax.ShapeDtypeStruct((nb * n_q, n_heads * LANES), BF16),
        scratch_shapes=[pltpu.VMEM((2, tq, 1), F32), pltpu.VMEM((2, tq, 1), F32),
                        pltpu.VMEM((2, tq, LANES), F32)],
        compiler_params=_params(("parallel", "parallel", "parallel", "arbitrary")),
        name="diff_attention",
    )(*args)


def _silu(x):
    return x * (1.0 / (1.0 + jnp.exp(-x)))


def _ffn_kernel(x_ref, g_ref, sc_ref, sh_ref, gate_ref, w1_ref, w3_ref, w2_ref, o_ref):
    x = x_ref[...]
    a = _modulated_norm(x, g_ref[...], sc_ref[0], sh_ref[0]).astype(BF16)
    h1 = jnp.dot(a, w1_ref[...], preferred_element_type=F32)
    h3 = jnp.dot(a, w3_ref[...], preferred_element_type=F32)
    y = jnp.dot((_silu(h1) * h3).astype(BF16), w2_ref[...], preferred_element_type=F32)
    o_ref[...] = x + gate_ref[0] * y


def _ffn(h, nb, g, sc, sh, gate, w1, w3, w2, tm):
    t, d = h.shape
    steps = t // nb // tm
    row = lambda b, i: (b * steps + i, 0)
    vec = pl.BlockSpec((1, 1, d), lambda b, i: (b, 0, 0))
    return pl.pallas_call(
        _ffn_kernel,
        grid=(nb, steps),
        in_specs=[pl.BlockSpec((tm, d), row), _resident((1, d), lambda b, i: (0, 0)), vec, vec, vec,
                  _resident(w1.shape, lambda b, i: (0, 0)),
                  _resident(w3.shape, lambda b, i: (0, 0)),
                  _resident(w2.shape, lambda b, i: (0, 0))],
        out_specs=pl.BlockSpec((tm, d), row),
        out_shape=jax.ShapeDtypeStruct((t, d), F32),
        compiler_params=_params(("parallel", "parallel")),
        name="ffn",
    )(h, g, sc, sh, gate, w1, w3, w2)


def _top2_gates(logits):
    lane = _lane_iota()
    big = jnp.int32(LANES)
    lg = jnp.where(lane < N_EXPERTS, logits, -jnp.inf)
    m1 = lg.max(axis=-1, keepdims=True)
    i1 = jnp.where(lg == m1, lane, big).min(axis=-1, keepdims=True)
    rest = jnp.where(lane == i1, -jnp.inf, lg)
    m2 = rest.max(axis=-1, keepdims=True)
    i2 = jnp.where(rest == m2, lane, big).min(axis=-1, keepdims=True)
    e2 = jnp.exp(m2 - m1)
    den = 1.0 + e2
    return jnp.where(lane == i1, 1.0 / den, 0.0) + jnp.where(lane == i2, e2 / den, 0.0)


def _moe_kernel(x_ref, g_ref, sc_ref, sh_ref, gate_ref, r_ref, w1_ref, w3_ref, w2_ref, o_ref,
                a_sc, gates_sc, acc_sc):
    e = pl.program_id(2)

    @pl.when(e == 0)
    def _():
        a = _modulated_norm(x_ref[...], g_ref[...], sc_ref[0], sh_ref[0])
        a_hi = a.astype(BF16)
        a_lo = (a - a_hi.astype(F32)).astype(BF16)
        r = r_ref[...]
        r_hi = r.astype(BF16)
        r_lo = (r - r_hi.astype(F32)).astype(BF16)
        logits = (jnp.dot(a_hi, r_hi, preferred_element_type=F32)
                  + jnp.dot(a_hi, r_lo, preferred_element_type=F32)
                  + jnp.dot(a_lo, r_hi, preferred_element_type=F32))
        a_sc[...] = a_hi
        gates_sc[...] = _top2_gates(logits)
        acc_sc[...] = jnp.zeros_like(acc_sc)

    a = a_sc[...]
    h1 = jnp.dot(a, w1_ref[0], preferred_element_type=F32)
    h3 = jnp.dot(a, w3_ref[0], preferred_element_type=F32)
    y = jnp.dot((_silu(h1) * h3).astype(BF16), w2_ref[0], preferred_element_type=F32)
    gate_e = jnp.sum(jnp.where(_lane_iota() == e, gates_sc[...], 0.0), axis=-1, keepdims=True)
    acc_sc[...] += gate_e * y

    @pl.when(e == pl.num_programs(2) - 1)
    def _():
        o_ref[...] = x_ref[...] + gate_ref[0] * acc_sc[...]


def _moe(h, nb, g, sc, sh, gate, router, w1, w3, w2, tm):
    t, d = h.shape
    steps = t // nb // tm
    n_e, _, f = w1.shape
    row = lambda b, i, e: (b * steps + i, 0)
    vec = pl.BlockSpec((1, 1, d), lambda b, i, e: (b, 0, 0))
    return pl.pallas_call(
        _moe_kernel,
        grid=(nb, steps, n_e),
        in_specs=[pl.BlockSpec((tm, d), row), _resident((1, d), lambda b, i, e: (0, 0)), vec, vec, vec,
                  _resident(router.shape, lambda b, i, e: (0, 0)),
                  pl.BlockSpec((1, d, f), lambda b, i, e: (e, 0, 0)),
                  pl.BlockSpec((1, d, f), lambda b, i, e: (e, 0, 0)),
                  pl.BlockSpec((1, f, d), lambda b, i, e: (e, 0, 0))],
        out_specs=pl.BlockSpec((tm, d), row),
        out_shape=jax.ShapeDtypeStruct((t, d), F32),
        scratch_shapes=[pltpu.VMEM((tm, d), BF16), pltpu.VMEM((tm, LANES), F32), pltpu.VMEM((tm, d), F32)],
        compiler_params=_params(("parallel", "parallel", "arbitrary")),
        name="moe",
    )(h, g, sc, sh, gate, router, w1, w3, w2)


def _rope_tables(n):
    t = jnp.arange(n)
    pos = jnp.stack([t // GRID_W, t % GRID_W], -1).astype(F32)
    nq = HEAD_DIM // 4
    inv = ROPE_THETA ** (-jnp.arange(nq, dtype=F32) / nq)
    ang = pos[:, :, None] * inv
    cos = jnp.repeat(jnp.cos(ang)[:, :, None, :], 2, axis=2)
    sin = jnp.stack([-jnp.sin(ang), jnp.sin(ang)], axis=2)
    cos = jnp.tile(cos.reshape(n, HEAD_DIM), (1, LANES // HEAD_DIM))
    sin = jnp.tile(sin.reshape(n, HEAD_DIM), (1, LANES // HEAD_DIM))
    return cos, sin


def _head_gain(parts, n_out):
    row = jnp.ones((n_out,), F32)
    for col, n_heads, gain, scale in parts:
        row = lax.dynamic_update_slice(row, jnp.tile(gain.astype(F32) * scale, n_heads), (col,))
    return row.reshape(1, n_out)


def kernel(x, c, ctx, c_ctx, ada_w, ada_b, norm1_g, norm2_g, ev_w_in, ev_conv_w, ev_q_g, ev_k_g, ev_rpb,
           ev_w_out, ffn_w1, ffn_w3, ffn_w2, od_w_in, od_cq_g, od_ck_g, od_sink, od_dq_g, od_dk_g,
           od_lam_q1, od_lam_k1, od_lam_q2, od_lam_k2, od_subln_g, od_w_out, moe_router,
           moe_w1, moe_w3, moe_w2):
    nb, n, d = x.shape
    ctx_len = ctx.shape[1]
    depth = ada_w.shape[0]
    n_slots = d // HEAD_DIM
    conv_ch = d // 2
    na_heads = swa_heads = n_slots // 2
    swa_kv = max(1, swa_heads // 4)
    diff_heads = n_slots // 4
    rows = n // GRID_W
    assert rows % NA_QROWS == 0 and rows >= 2 * NA_QROWS and n % 512 == 0
    assert nb + 1 <= 8 and ctx_len % BF16_SUBLANES == 0

    ev_q_col = 3 * conv_ch
    ev_k_col = ev_q_col + na_heads * HEAD_DIM
    ev_v_col = ev_k_col + na_heads * HEAD_DIM
    ev_n = ev_v_col + na_heads * HEAD_DIM
    od_dq_col = swa_heads * HEAD_DIM
    od_ck_col = od_dq_col + diff_heads * 2 * HEAD_DIM
    od_cv_col = od_ck_col + swa_kv * HEAD_DIM
    od_dk_col = od_cv_col + swa_kv * HEAD_DIM
    od_dv_col = od_dk_col + diff_heads * 2 * HEAD_DIM
    od_n = od_dv_col + diff_heads * 2 * HEAD_DIM
    ev_segs = ((0, ev_q_col, False), (ev_q_col, ev_v_col, True), (ev_v_col, ev_n, False))
    od_segs = ((0, od_cv_col, True), (od_cv_col, od_dk_col, False), (od_dk_col, od_dv_col, True),
               (od_dv_col, od_n, False))

    h = x.reshape(nb * n, d)
    hc = ctx.reshape(nb * ctx_len, d)
    tm_lat = 512
    tm_ctx = ctx_len

    s_rows = jnp.zeros((8, d), F32).at[:nb].set(c).at[nb].set(c_ctx)
    mod = _modulation(s_rows, ada_w, ada_b)
    rope_tabs = _rope_tables(n)

    def lat_vec(l, k):
        return mod[l, :nb, k * d:(k + 1) * d].reshape(nb, 1, d)

    def ctx_vec(l, k, copies):
        return jnp.broadcast_to(mod[l, nb, k * d:(k + 1) * d], (copies, 1, d))

    for l in range(depth):
        last = l == depth - 1
        i = l // 2
        g1 = norm1_g[l].reshape(1, d)
        g2 = norm2_g[l].reshape(1, d)
        if l % 2 == 0:
            w_in = ev_w_in[i].astype(BF16)
            gain = _head_gain([(ev_q_col, na_heads, ev_q_g[i], QK_SCALE), (ev_k_col, na_heads, ev_k_g[i], 1.0)],
                              ev_n)
            p_lat = _norm_proj(h, nb, g1, lat_vec(l, 1), lat_vec(l, 0), w_in, gain, ev_segs, None, tm_lat)
            p_ctx = _norm_proj(hc, nb, g1, ctx_vec(l, 1, nb), ctx_vec(l, 0, nb), w_in, gain, ev_segs, None,
                               tm_ctx)
            bias = _na_bias(ev_rpb[i], rows)
            y_na = _na_attention(p_lat, p_ctx, bias, nb, n, ctx_len, ev_q_col // LANES, ev_k_col // LANES,
                                 ev_v_col // LANES, na_heads)
            w_out = ev_w_out[i].astype(BF16)
            conv_w = ev_conv_w[i]
            h = _out_proj(h, nb, lat_vec(l, 2), p_lat, y_na, w_out, conv_w, tm_lat)
            if not last:
                y_na_c = _gqa_attention(p_ctx, p_ctx, None, nb, ctx_len, ctx_len, ev_q_col // LANES,
                                        ev_k_col // LANES, ev_v_col // LANES, na_heads, na_heads, False,
                                        ctx_len)
                hc = _out_proj(hc, nb, ctx_vec(l, 2, nb), p_ctx, y_na_c, w_out, conv_w, tm_ctx)
            w1, w3, w2 = ffn_w1[i].astype(BF16), ffn_w3[i].astype(BF16), ffn_w2[i].astype(BF16)
            h = _ffn(h, nb, g2, lat_vec(l, 4), lat_vec(l, 3), lat_vec(l, 5), w1, w3, w2, tm_lat)
            if not last:
                hc = _ffn(hc, 1, g2, ctx_vec(l, 4, 1), ctx_vec(l, 3, 1), ctx_vec(l, 5, 1), w1, w3, w2, tm_lat)
        else:
            lam_init = 0.8 - 0.6 * math.exp(-0.3 * l)
            w_in = od_w_in[i].astype(BF16)
            gain = _head_gain([(0, swa_heads, od_cq_g[i], QK_SCALE),
                               (od_dq_col, 2 * diff_heads, od_dq_g[i], QK_SCALE),
                               (od_ck_col, swa_kv, od_ck_g[i], 1.0),
                               (od_dk_col, 2 * diff_heads, od_dk_g[i], 1.0)], od_n)
            p_lat = _norm_proj(h, nb, g1, lat_vec(l, 1), lat_vec(l, 0), w_in, gain, od_segs, rope_tabs, tm_lat)
            p_ctx = _norm_proj(hc, nb, g1, ctx_vec(l, 1, nb), ctx_vec(l, 0, nb), w_in, gain, od_segs, None,
                               tm_ctx)
            sink = jnp.broadcast_to(od_sink[i].astype(F32)[:, None], (swa_heads, LANES))
            lam_rows = jnp.zeros((8, LANES), F32).at[:4, :HEAD_DIM].set(
                jnp.stack([od_lam_q1[i], od_lam_k1[i], od_lam_q2[i], od_lam_k2[i]]).astype(F32))
            subg = od_subln_g[i].astype(F32).reshape(1, 2 * HEAD_DIM)
            cols = (od_dq_col // LANES, od_dk_col // LANES, od_dv_col // LANES)
            y_c = _gqa_attention(p_lat, p_ctx, sink, nb, n, ctx_len, 0, od_ck_col // LANES, od_cv_col // LANES,
                                 swa_heads, swa_kv, True, 2 * BLOCK)
            y_d = _diff_attention(p_lat, p_lat, p_ctx, lam_rows, subg, nb, n, n, ctx_len, *cols, diff_heads,
                                  lam_init, 512, 512)
            w_out = od_w_out[i].astype(BF16)
            h = _out_proj(h, nb, lat_vec(l, 2), y_c, y_d, w_out, None, tm_lat)
            if not last:
                y_c_c = _gqa_attention(p_ctx, p_ctx, sink, nb, ctx_len, ctx_len, 0, od_ck_col // LANES,
                                       od_cv_col // LANES, swa_heads, swa_kv, False, ctx_len)
                y_d_c = _diff_attention(p_ctx, None, p_ctx, lam_rows, subg, nb, ctx_len, 0, ctx_len, *cols,
                                        diff_heads, lam_init, ctx_len, ctx_len)
                hc = _out_proj(hc, nb, ctx_vec(l, 2, nb), y_c_c, y_d_c, w_out, None, tm_ctx)
            router = jnp.zeros((d, LANES), F32).at[:, :N_EXPERTS].set(moe_router[i])
            w1, w3, w2 = moe_w1[i].astype(BF16), moe_w3[i].astype(BF16), moe_w2[i].astype(BF16)
            h = _moe(h, nb, g2, lat_vec(l, 4), lat_vec(l, 3), lat_vec(l, 5), router, w1, w3, w2, tm_lat)
            if not last:
                hc = _moe(hc, 1, g2, ctx_vec(l, 4, 1), ctx_vec(l, 3, 1), ctx_vec(l, 5, 1), router, w1, w3, w2,
                          tm_lat)
    return h.reshape(nb, n, d)
```

```python
import functools
import math

import jax
import jax.numpy as jnp
from jax import lax
from jax.experimental import pallas as pl
from jax.experimental.pallas import tpu as pltpu

F32 = jnp.float32
BF16 = jnp.bfloat16

LANES = 128
BF16_SUBLANES = 16
VMEM_LIMIT = 56 * 1024 * 1024

HEAD_DIM = 64
GRID_W = 64
CONV_W = 3
NA_ROWS = 8
NA_COLS = 16
NA_QROWS = 8
SWA_WINDOW = 128
BLOCK = 128
N_EXPERTS = 8
ROPE_THETA = 10000.0
EPS = 1e-6
NEG = -1e30
QK_SCALE = HEAD_DIM ** -0.5

NT_DIMS = (((1,), (1,)), ((), ()))


def _params(sem):
    return pltpu.CompilerParams(dimension_semantics=sem, vmem_limit_bytes=VMEM_LIMIT)


def _resident(shape, index_map):
    return pl.BlockSpec(shape, index_map, pipeline_mode=pl.Buffered(1))


def _lane_iota():
    return lax.broadcasted_iota(jnp.int32, (1, LANES), 1)


def _swap_halves(x):
    return jnp.concatenate([x[:, HEAD_DIM:], x[:, :HEAD_DIM]], axis=1)


def _modulated_norm(x, g, sc, sh):
    ms = jnp.mean(x * x, axis=-1, keepdims=True)
    return (x * lax.rsqrt(ms + EPS)) * (g * (1.0 + sc)) + sh


def _mod_kernel(s_ref, w_ref, b_ref, o_ref):
    s = s_ref[...]
    s = s * (1.0 / (1.0 + jnp.exp(-s)))
    o_ref[0] = jnp.dot(s.astype(BF16), w_ref[0].astype(BF16), preferred_element_type=F32) + b_ref[0]


def _modulation(s_rows, ada_w, ada_b):
    depth, d, n_out = ada_w.shape
    tn = n_out // 4
    return pl.pallas_call(
        _mod_kernel,
        grid=(depth, n_out // tn),
        in_specs=[pl.BlockSpec(s_rows.shape, lambda l, j: (0, 0)),
                  pl.BlockSpec((1, d, tn), lambda l, j: (l, 0, j)),
                  pl.BlockSpec((1, 1, tn), lambda l, j: (l, 0, j))],
        out_specs=pl.BlockSpec((1, s_rows.shape[0], tn), lambda l, j: (l, 0, j)),
        out_shape=jax.ShapeDtypeStruct((depth, s_rows.shape[0], n_out), F32),
        compiler_params=_params(("arbitrary", "arbitrary")),
        name="modulation",
    )(s_rows, ada_w, ada_b.reshape(depth, 1, n_out))


def _head_sumsq(z):
    r = lax.broadcasted_iota(jnp.int32, (LANES, LANES), 0) // HEAD_DIM
    c = lax.broadcasted_iota(jnp.int32, (LANES, LANES), 1) // HEAD_DIM
    same_head = jnp.where(r == c, 1.0, 0.0).astype(BF16)
    z2 = z * z
    hi = z2.astype(BF16)
    lo = (z2 - hi.astype(F32)).astype(BF16)
    return (jnp.dot(hi, same_head, preferred_element_type=F32)
            + jnp.dot(lo, same_head, preferred_element_type=F32))


def _norm_proj_kernel(*refs, segs, rope):
    if rope:
        x_ref, g_ref, sc_ref, sh_ref, w_ref, gain_ref, cos_ref, sin_ref, o_ref = refs
    else:
        x_ref, g_ref, sc_ref, sh_ref, w_ref, gain_ref, o_ref = refs
    a = _modulated_norm(x_ref[...], g_ref[...], sc_ref[0], sh_ref[0]).astype(BF16)
    first_half = (_lane_iota() % (HEAD_DIM // 2)) < (HEAD_DIM // 4)
    for c0, c1, normed in segs:
        acc = jnp.dot(a, w_ref[:, c0:c1], preferred_element_type=F32)
        if not normed:
            o_ref[:, c0:c1] = acc.astype(BF16)
            continue
        for j in range((c1 - c0) // LANES):
            z = acc[:, j * LANES:(j + 1) * LANES]
            lo = c0 + j * LANES
            z = z * lax.rsqrt(_head_sumsq(z) * (1.0 / HEAD_DIM) + EPS) * gain_ref[:, lo:lo + LANES]
            if rope:
                partner = jnp.where(first_half,
                                    pltpu.roll(z, LANES - HEAD_DIM // 4, axis=1),
                                    pltpu.roll(z, HEAD_DIM // 4, axis=1))
                z = z * cos_ref[...] + partner * sin_ref[...]
            o_ref[:, lo:lo + LANES] = z.astype(BF16)


def _norm_proj(h, nb, g, sc, sh, w, gain, segs, rope_tabs, tm):
    t, d = h.shape
    n_out = w.shape[1]
    steps = t // nb // tm
    in_specs = [pl.BlockSpec((tm, d), lambda b, i: (b * steps + i, 0)),
                _resident((1, d), lambda b, i: (0, 0)),
                pl.BlockSpec((1, 1, d), lambda b, i: (b, 0, 0)),
                pl.BlockSpec((1, 1, d), lambda b, i: (b, 0, 0)),
                _resident((d, n_out), lambda b, i: (0, 0)),
                _resident((1, n_out), lambda b, i: (0, 0))]
    args = [h, g, sc, sh, w, gain]
    if rope_tabs is not None:
        in_specs += [pl.BlockSpec((tm, LANES), lambda b, i: (i, 0))] * 2
        args += list(rope_tabs)
    return pl.pallas_call(
        functools.partial(_norm_proj_kernel, segs=segs, rope=rope_tabs is not None),
        grid=(nb, steps),
        in_specs=in_specs,
        out_specs=pl.BlockSpec((tm, n_out), lambda b, i: (b * steps + i, 0)),
        out_shape=jax.ShapeDtypeStruct((t, n_out), BF16),
        compiler_params=_params(("parallel", "parallel")),
        name="norm_proj",
    )(*args)


def _gated_conv(gb_ref, gc_ref, u_ref, gcp_ref, up_ref, gcn_ref, un_ref, cw_ref):
    i, steps = pl.program_id(1), pl.num_programs(1)
    v = gc_ref[...].astype(F32) * u_ref[...].astype(F32)
    tm = v.shape[0]
    last = BF16_SUBLANES - 1
    prev_row = gcp_ref[last:last + 1, :].astype(F32) * up_ref[last:last + 1, :].astype(F32)
    next_row = gcn_ref[0:1, :].astype(F32) * un_ref[0:1, :].astype(F32)
    prev_row = jnp.where(i > 0, prev_row, 0.0)
    next_row = jnp.where(i < steps - 1, next_row, 0.0)
    row = lax.broadcasted_iota(jnp.int32, (tm, 1), 0)
    v_prev = jnp.where(row == 0, prev_row, pltpu.roll(v, 1, axis=0))
    v_next = jnp.where(row == tm - 1, next_row, pltpu.roll(v, tm - 1, axis=0))
    cw = cw_ref[...]
    conv = cw[0:1, :] * v_prev + cw[1:2, :] * v + cw[2:3, :] * v_next
    return gb_ref[...].astype(F32) * conv


def _out_proj_kernel(*refs, conv):
    if conv:
        (gb_ref, gc_ref, u_ref, gcp_ref, up_ref, gcn_ref, un_ref, cw_ref,
         yb_ref, wa_ref, wb_ref, h_ref, gate_ref, o_ref) = refs
        ya = _gated_conv(gb_ref, gc_ref, u_ref, gcp_ref, up_ref, gcn_ref, un_ref, cw_ref).astype(BF16)
    else:
        ya_ref, yb_ref, wa_ref, wb_ref, h_ref, gate_ref, o_ref = refs
        ya = ya_ref[...]
    y = (jnp.dot(ya, wa_ref[...], preferred_element_type=F32)
         + jnp.dot(yb_ref[...], wb_ref[...], preferred_element_type=F32))
    o_ref[...] = h_ref[...] + gate_ref[0] * y


def _out_proj(h, nb, gate, ya_src, yb, w_out, conv_w, tm):
    t, d = h.shape
    steps = t // nb // tm
    wa_rows = w_out.shape[0] - yb.shape[1]
    w_a, w_b = w_out[:wa_rows], w_out[wa_rows:]
    row = lambda b, i: (b * steps + i, 0)
    if conv_w is not None:
        cc = conv_w.shape[1]
        hb = tm // BF16_SUBLANES
        n_halo = t // BF16_SUBLANES
        prev = lambda col: (lambda b, i: (jnp.maximum((b * steps + i) * hb - 1, 0), col))
        nxt = lambda col: (lambda b, i: (jnp.minimum((b * steps + i + 1) * hb, n_halo - 1), col))
        in_specs = [pl.BlockSpec((tm, cc), lambda b, i: (b * steps + i, 0)),
                    pl.BlockSpec((tm, cc), lambda b, i: (b * steps + i, 1)),
                    pl.BlockSpec((tm, cc), lambda b, i: (b * steps + i, 2)),
                    pl.BlockSpec((BF16_SUBLANES, cc), prev(1)),
                    pl.BlockSpec((BF16_SUBLANES, cc), prev(2)),
                    pl.BlockSpec((BF16_SUBLANES, cc), nxt(1)),
                    pl.BlockSpec((BF16_SUBLANES, cc), nxt(2)),
                    _resident(conv_w.shape, lambda b, i: (0, 0))]
        args = [ya_src] * 7 + [conv_w]
    else:
        in_specs = [pl.BlockSpec((tm, wa_rows), row)]
        args = [ya_src]
    in_specs += [pl.BlockSpec((tm, yb.shape[1]), row),
                 _resident(w_a.shape, lambda b, i: (0, 0)),
                 _resident(w_b.shape, lambda b, i: (0, 0)),
                 pl.BlockSpec((tm, d), row),
                 pl.BlockSpec((1, 1, d), lambda b, i: (b, 0, 0))]
    args += [yb, w_a, w_b, h, gate]
    return pl.pallas_call(
        functools.partial(_out_proj_kernel, conv=conv_w is not None),
        grid=(nb, steps),
        in_specs=in_specs,
        out_specs=pl.BlockSpec((tm, d), row),
        out_shape=jax.ShapeDtypeStruct((t, d), F32),
        compiler_params=_params(("parallel", "parallel")),
        name="out_proj",
    )(*args)


def _na_bias(rpb, rows):
    nblk = rows // NA_QROWS
    c = jnp.arange(GRID_W)
    c0 = jnp.clip(c - NA_COLS // 2, 0, GRID_W - NA_COLS)
    col_ok = (c[None, :] >= c0[:, None]) & (c[None, :] < c0[:, None] + NA_COLS)
    dc = jnp.clip(c[None, :] - c[:, None] + NA_COLS - 1, 0, 2 * NA_COLS - 2)
    pick_dc = ((dc[None] == jnp.arange(2 * NA_COLS - 1)[:, None, None]) & col_ok[None]).astype(F32)
    table = jnp.einsum('hrd,dcx->hrcx', rpb.astype(F32), pick_dc, precision=lax.Precision.HIGHEST)
    i = jnp.arange(NA_QROWS)
    j = jnp.arange(2 * NA_QROWS)
    pick_dr, row_ok = [], []
    for rb in (0, 1, nblk - 1):
        r = rb * NA_QROWS + i
        r0 = jnp.clip(r - NA_ROWS // 2, 0, rows - NA_ROWS)
        rk = rb * NA_QROWS - NA_ROWS // 2 + j
        ok = (rk[None, :] >= r0[:, None]) & (rk[None, :] < r0[:, None] + NA_ROWS)
        dr = rk[None, :] - r[:, None] + NA_ROWS - 1
        pick_dr.append(((dr[:, :, None] == jnp.arange(2 * NA_ROWS - 1)) & ok[:, :, None]).astype(F32))
        row_ok.append(ok)
    bias = jnp.einsum('vijr,hrcx->vhjxic', jnp.stack(pick_dr), table, precision=lax.Precision.HIGHEST)
    ok = jnp.stack(row_ok).transpose(0, 2, 1)[:, None, :, None, :, None] & col_ok.T[None, None, None, :, None, :]
    bias = jnp.where(ok, bias, NEG)
    return bias.reshape(3, rpb.shape[0], 2 * NA_QROWS * GRID_W, NA_QROWS * GRID_W)


def _softmax_pv_t(s_parts, vt_parts, extra=None):
    m = s_parts[0].max(axis=0, keepdims=True)
    for s in s_parts[1:]:
        m = jnp.maximum(m, s.max(axis=0, keepdims=True))
    if extra is not None:
        m = jnp.maximum(m, extra)
    den = None
    o = None
    for s, vt in zip(s_parts, vt_parts):
        e = jnp.exp(s - m)
        d = e.sum(axis=0, keepdims=True)
        pv = jnp.dot(vt, e.astype(BF16), preferred_element_type=F32)
        den = d if den is None else den + d
        o = pv if o is None else o + pv
    if extra is not None:
        den = den + jnp.exp(extra - m)
    return o / den


def _na_kernel(q_ref, k0, k1, k2, k3, vt0, vt1, vt2, vt3, kc_ref, vct_ref, bias_ref, o_ref):
    q = q_ref[...]
    k = jnp.concatenate([k0[...], k1[...], k2[...], k3[...]], axis=0)
    vt = jnp.concatenate([vt0[...], vt1[...], vt2[...], vt3[...]], axis=1)
    kc, vct = kc_ref[...], vct_ref[...]
    lane = _lane_iota()
    outs = []
    for hh in range(2):
        qh = jnp.where((lane // HEAD_DIM) == hh, q, jnp.zeros_like(q))
        s_loc = lax.dot_general(k, qh, NT_DIMS, preferred_element_type=F32) + bias_ref[0, hh]
        s_ctx = lax.dot_general(kc, qh, NT_DIMS, preferred_element_type=F32)
        rows = slice(hh * HEAD_DIM, (hh + 1) * HEAD_DIM)
        outs.append(_softmax_pv_t([s_loc, s_ctx], [vt[rows], vct[rows]]))
    o_ref[...] = jnp.concatenate(outs, axis=0).T.astype(BF16)


def _na_attention(p_lat, vt_lat, p_ctx, vt_ctx, bias, nb, n, ctx_len, q_col, k_col, n_heads):
    tq = NA_QROWS * GRID_W
    tw = tq // 2
    qsteps = n // tq
    wsteps = n // tw
    wblock = lambda rb, j: jnp.clip(2 * rb - 1 + j, 0, wsteps - 1)
    kwin = lambda j: pl.BlockSpec((tw, LANES), lambda hp, b, rb: (b * wsteps + wblock(rb, j), k_col + hp))
    vwin = lambda j: pl.BlockSpec((LANES, tw), lambda hp, b, rb: (hp, b * wsteps + wblock(rb, j)))
    variant = lambda rb: jnp.where(rb == 0, 0, jnp.where(rb == qsteps - 1, 2, 1))
    in_specs = ([pl.BlockSpec((tq, LANES), lambda hp, b, rb: (b * qsteps + rb, q_col + hp))]
                + [kwin(j) for j in range(4)] + [vwin(j) for j in range(4)]
                + [pl.BlockSpec((ctx_len, LANES), lambda hp, b, rb: (b, k_col + hp)),
                   pl.BlockSpec((LANES, ctx_len), lambda hp, b, rb: (hp, b)),
                   pl.BlockSpec((1, 2, 2 * tq, tq), lambda hp, b, rb: (variant(rb), hp, 0, 0))])
    return pl.pallas_call(
        _na_kernel,
        grid=(n_heads // 2, nb, qsteps),
        in_specs=in_specs,
        out_specs=pl.BlockSpec((tq, LANES), lambda hp, b, rb: (b * qsteps + rb, hp)),
        out_shape=jax.ShapeDtypeStruct((nb * n, n_heads * HEAD_DIM), BF16),
        compiler_params=_params(("parallel", "parallel", "arbitrary")),
        name="na_attention",
    )(*([p_lat] * 5 + [vt_lat] * 4 + [p_ctx, vt_ctx, bias]))


def _gqa_kernel(*refs, n_q, n_kv, has_local, has_sink, n_tokens):
    refs = list(refs)
    q_ref = refs.pop(0)
    if has_local:
        k_loc = jnp.concatenate([refs.pop(0)[...] for _ in range(4)], axis=0)
        vt_loc = jnp.concatenate([refs.pop(0)[...] for _ in range(4)], axis=1)
    kc_ref, vct_ref = refs.pop(0), refs.pop(0)
    sink_ref = refs.pop(0) if has_sink else None
    o_ref = refs.pop(0)

    tq = q_ref.shape[0]
    group = n_q // n_kv
    lane = _lane_iota()
    if has_local:
        base = pl.program_id(1) * tq
        kpos = base - BLOCK + lax.broadcasted_iota(jnp.int32, (tq + 2 * BLOCK, 1), 0)
        qpos = base + lax.broadcasted_iota(jnp.int32, (1, tq), 1)
        ok = (jnp.abs(kpos - qpos) <= SWA_WINDOW) & (kpos >= 0) & (kpos < n_tokens)
        band = jnp.where(ok, 0.0, NEG)
        band = jnp.concatenate([band] * group, axis=1)

    outs = []
    for kvh in range(n_kv):
        cg, half = kvh // 2, kvh % 2
        qs = []
        for h in range(kvh * group, (kvh + 1) * group):
            qg = q_ref[:, (h // 2) * LANES:(h // 2 + 1) * LANES]
            if h % 2 != half:
                qg = _swap_halves(qg)
            qs.append(jnp.where((lane // HEAD_DIM) == half, qg, jnp.zeros_like(qg)))
        qcat = jnp.concatenate(qs, axis=0) if group > 1 else qs[0]
        rows = slice(kvh * HEAD_DIM, (kvh + 1) * HEAD_DIM)
        kc = kc_ref[:, cg * LANES:(cg + 1) * LANES]
        s_parts = [lax.dot_general(kc, qcat, NT_DIMS, preferred_element_type=F32)]
        vt_parts = [vct_ref[rows, :]]
        if has_local:
            s_parts.append(lax.dot_general(k_loc, qcat, NT_DIMS, preferred_element_type=F32) + band)
            vt_parts.append(vt_loc[rows])
        extra = None
        if has_sink:
            extra = jnp.concatenate([sink_ref[h:h + 1, :] for h in range(kvh * group, (kvh + 1) * group)
                                     for _ in range(tq // LANES)], axis=1)
        o = _softmax_pv_t(s_parts, vt_parts, extra)
        outs += [o[:, g * tq:(g + 1) * tq] for g in range(group)]
    o_ref[...] = jnp.concatenate(outs, axis=0).T.astype(BF16)


def _gqa_attention(p_q, vt_q, p_ctx, vt_ctx, sink, nb, n, ctx_len, q_col, k_col, n_q, n_kv, has_local, tq):
    qsteps = n // tq
    kv_w = n_kv * HEAD_DIM
    q_w = n_q * HEAD_DIM
    in_specs = [pl.BlockSpec((tq, q_w), lambda b, i: (b * qsteps + i, q_col * LANES // q_w))]
    args = [p_q]
    if has_local:
        assert tq == 2 * BLOCK and kv_w == LANES
        wsteps = n // BLOCK
        wblock = lambda i, j: jnp.clip(2 * i - 1 + j, 0, wsteps - 1)
        in_specs += [pl.BlockSpec((BLOCK, LANES), functools.partial(
            lambda b, i, j: (b * wsteps + wblock(i, j), k_col), j=j)) for j in range(4)]
        in_specs += [pl.BlockSpec((LANES, BLOCK), functools.partial(
            lambda b, i, j: (0, b * wsteps + wblock(i, j)), j=j)) for j in range(4)]
        args += [p_q] * 4 + [vt_q] * 4
    in_specs += [pl.BlockSpec((ctx_len, kv_w), lambda b, i: (b, k_col * LANES // kv_w)),
                 pl.BlockSpec((kv_w, ctx_len), lambda b, i: (0, b))]
    args += [p_ctx, vt_ctx]
    if sink is not None:
        in_specs.append(_resident(sink.shape, lambda b, i: (0, 0)))
        args.append(sink)
    return pl.pallas_call(
        functools.partial(_gqa_kernel, n_q=n_q, n_kv=n_kv, has_local=has_local,
                          has_sink=sink is not None, n_tokens=n),
        grid=(nb, qsteps),
        in_specs=in_specs,
        out_specs=pl.BlockSpec((tq, q_w), lambda b, i: (b * qsteps + i, 0)),
        out_shape=jax.ShapeDtypeStruct((nb * n, q_w), BF16),
        compiler_params=_params(("parallel", "parallel")),
        name="gqa_attention",
    )(*args)


def _diff_kernel(*refs, n_kblocks, tk, lam_init):
    if n_kblocks:
        q_ref, k_ref, vt_ref, kc_ref, vct_ref, lam_ref, subg_ref, o_ref, m_sc, l_sc, acc_sc = refs
    else:
        q_ref, kc_ref, vct_ref, lam_ref, subg_ref, o_ref, m_sc, l_sc, acc_sc = refs
    lane = _lane_iota()
    q = q_ref[...]
    zero = jnp.zeros_like(q)
    q_maps = [jnp.where(lane < HEAD_DIM, q, zero), jnp.where(lane >= HEAD_DIM, q, zero)]

    kc, vct = kc_ref[...], vct_ref[...]
    for i in range(2):
        s = lax.dot_general(kc, q_maps[i], NT_DIMS, preferred_element_type=F32)
        m = s.max(axis=0, keepdims=True)
        p = jnp.exp(s - m)
        m_sc[i] = m
        l_sc[i] = p.sum(axis=0, keepdims=True)
        acc_sc[i] = jnp.dot(vct, p.astype(BF16), preferred_element_type=F32)

    if n_kblocks:
        def body(kb, carry):
            off = pl.multiple_of(kb * tk, tk)
            k = k_ref[pl.ds(off, tk), :]
            vt = vt_ref[:, pl.ds(off, tk)]
            for i in range(2):
                s = lax.dot_general(k, q_maps[i], NT_DIMS, preferred_element_type=F32)
                m_prev = m_sc[i]
                m_new = jnp.maximum(m_prev, s.max(axis=0, keepdims=True))
                p = jnp.exp(s - m_new)
                alpha = jnp.exp(m_prev - m_new)
                l_sc[i] = alpha * l_sc[i] + p.sum(axis=0, keepdims=True)
                acc_sc[i] = alpha * acc_sc[i] + jnp.dot(vt, p.astype(BF16), preferred_element_type=F32)
                m_sc[i] = m_new
            return carry

        lax.fori_loop(0, n_kblocks, body, 0)

    lp = lam_ref[...]
    lam = (jnp.exp(jnp.sum(lp[0:1] * lp[1:2], axis=-1, keepdims=True))
           - jnp.exp(jnp.sum(lp[2:3] * lp[3:4], axis=-1, keepdims=True)) + lam_init)
    y = (acc_sc[0] / l_sc[0] - lam * (acc_sc[1] / l_sc[1])).T
    ms = jnp.mean(y * y, axis=-1, keepdims=True)
    y = y * lax.rsqrt(ms + EPS) * subg_ref[...] * (1.0 - lam_init)
    o_ref[...] = y.astype(BF16)


def _diff_attention(p_q, p_lat, vt_lat, p_ctx, vt_ctx, lam_rows, subg, nb, n_q, n_lat, ctx_len,
                    q_col, k_col, n_heads, lam_init, tq, tk):
    qsteps = n_q // tq
    has_latent = p_lat is not None
    in_specs = [pl.BlockSpec((tq, LANES), lambda b, h, i: (b * qsteps + i, q_col + h))]
    args = [p_q]
    if has_latent:
        in_specs += [pl.BlockSpec((n_lat, LANES), lambda b, h, i: (b, k_col + h)),
                     pl.BlockSpec((LANES, n_lat), lambda b, h, i: (h, b))]
        args += [p_lat, vt_lat]
    in_specs += [pl.BlockSpec((ctx_len, LANES), lambda b, h, i: (b, k_col + h)),
                 pl.BlockSpec((LANES, ctx_len), lambda b, h, i: (h, b)),
                 _resident(lam_rows.shape, lambda b, h, i: (0, 0)),
                 _resident(subg.shape, lambda b, h, i: (0, 0))]
    args += [p_ctx, vt_ctx, lam_rows, subg]
    return pl.pallas_call(
        functools.partial(_diff_kernel, n_kblocks=n_lat // tk if has_latent else 0, tk=tk, lam_init=lam_init),
        grid=(nb, n_heads, qsteps),
        in_specs=in_specs,
        out_specs=pl.BlockSpec((tq, LANES), lambda b, h, i: (b * qsteps + i, h)),
        out_shape=jax.ShapeDtypeStruct((nb * n_q, n_heads * LANES), BF16),
        scratch_shapes=[pltpu.VMEM((2, 1, tq), F32), pltpu.VMEM((2, 1, tq), F32),
                        pltpu.VMEM((2, LANES, tq), F32)],
        compiler_params=_params(("parallel", "parallel", "arbitrary")),
        name="diff_attention",
    )(*args)


def _silu(x):
    return x * (1.0 / (1.0 + jnp.exp(-x)))


def _ffn_kernel(x_ref, g_ref, sc_ref, sh_ref, gate_ref, w1_ref, w3_ref, w2_ref, o_ref):
    x = x_ref[...]
    a = _modulated_norm(x, g_ref[...], sc_ref[0], sh_ref[0]).astype(BF16)
    h1 = jnp.dot(a, w1_ref[...], preferred_element_type=F32)
    h3 = jnp.dot(a, w3_ref[...], preferred_element_type=F32)
    y = jnp.dot((_silu(h1) * h3).astype(BF16), w2_ref[...], preferred_element_type=F32)
    o_ref[...] = x + gate_ref[0] * y


def _ffn(h, nb, g, sc, sh, gate, w1, w3, w2, tm):
    t, d = h.shape
    steps = t // nb // tm
    row = lambda b, i: (b * steps + i, 0)
    vec = pl.BlockSpec((1, 1, d), lambda b, i: (b, 0, 0))
    return pl.pallas_call(
        _ffn_kernel,
        grid=(nb, steps),
        in_specs=[pl.BlockSpec((tm, d), row), _resident((1, d), lambda b, i: (0, 0)), vec, vec, vec,
                  _resident(w1.shape, lambda b, i: (0, 0)),
                  _resident(w3.shape, lambda b, i: (0, 0)),
                  _resident(w2.shape, lambda b, i: (0, 0))],
        out_specs=pl.BlockSpec((tm, d), row),
        out_shape=jax.ShapeDtypeStruct((t, d), F32),
        compiler_params=_params(("parallel", "parallel")),
        name="ffn",
    )(h, g, sc, sh, gate, w1, w3, w2)


def _top2_gates(logits):
    lane = _lane_iota()
    big = jnp.int32(LANES)
    lg = jnp.where(lane < N_EXPERTS, logits, -jnp.inf)
    m1 = lg.max(axis=-1, keepdims=True)
    i1 = jnp.where(lg == m1, lane, big).min(axis=-1, keepdims=True)
    rest = jnp.where(lane == i1, -jnp.inf, lg)
    m2 = rest.max(axis=-1, keepdims=True)
    i2 = jnp.where(rest == m2, lane, big).min(axis=-1, keepdims=True)
    e2 = jnp.exp(m2 - m1)
    den = 1.0 + e2
    return jnp.where(lane == i1, 1.0 / den, 0.0) + jnp.where(lane == i2, e2 / den, 0.0)


def _moe_kernel(x_ref, g_ref, sc_ref, sh_ref, gate_ref, r_ref, w1_ref, w3_ref, w2_ref, o_ref,
                a_sc, gates_sc, acc_sc):
    e = pl.program_id(2)

    @pl.when(e == 0)
    def _():
        a = _modulated_norm(x_ref[...], g_ref[...], sc_ref[0], sh_ref[0])
        a_hi = a.astype(BF16)
        a_lo = (a - a_hi.astype(F32)).astype(BF16)
        r = r_ref[...]
        r_hi = r.astype(BF16)
        r_lo = (r - r_hi.astype(F32)).astype(BF16)
        logits = (jnp.dot(a_hi, r_hi, preferred_element_type=F32)
                  + jnp.dot(a_hi, r_lo, preferred_element_type=F32)
                  + jnp.dot(a_lo, r_hi, preferred_element_type=F32))
        a_sc[...] = a_hi
        gates_sc[...] = _top2_gates(logits)
        acc_sc[...] = jnp.zeros_like(acc_sc)

    a = a_sc[...]
    h1 = jnp.dot(a, w1_ref[0], preferred_element_type=F32)
    h3 = jnp.dot(a, w3_ref[0], preferred_element_type=F32)
    y = jnp.dot((_silu(h1) * h3).astype(BF16), w2_ref[0], preferred_element_type=F32)
    gate_e = jnp.sum(jnp.where(_lane_iota() == e, gates_sc[...], 0.0), axis=-1, keepdims=True)
    acc_sc[...] += gate_e * y

    @pl.when(e == pl.num_programs(2) - 1)
    def _():
        o_ref[...] = x_ref[...] + gate_ref[0] * acc_sc[...]


def _moe(h, nb, g, sc, sh, gate, router, w1, w3, w2, tm):
    t, d = h.shape
    steps = t // nb // tm
    n_e, _, f = w1.shape
    row = lambda b, i, e: (b * steps + i, 0)
    vec = pl.BlockSpec((1, 1, d), lambda b, i, e: (b, 0, 0))
    return pl.pallas_call(
        _moe_kernel,
        grid=(nb, steps, n_e),
        in_specs=[pl.BlockSpec((tm, d), row), _resident((1, d), lambda b, i, e: (0, 0)), vec, vec, vec,
                  _resident(router.shape, lambda b, i, e: (0, 0)),
                  pl.BlockSpec((1, d, f), lambda b, i, e: (e, 0, 0)),
                  pl.BlockSpec((1, d, f), lambda b, i, e: (e, 0, 0)),
                  pl.BlockSpec((1, f, d), lambda b, i, e: (e, 0, 0))],
        out_specs=pl.BlockSpec((tm, d), row),
        out_shape=jax.ShapeDtypeStruct((t, d), F32),
        scratch_shapes=[pltpu.VMEM((tm, d), BF16), pltpu.VMEM((tm, LANES), F32), pltpu.VMEM((tm, d), F32)],
        compiler_params=_params(("parallel", "parallel", "arbitrary")),
        name="moe",
    )(h, g, sc, sh, gate, router, w1, w3, w2)


def _rope_tables(n):
    t = jnp.arange(n)
    pos = jnp.stack([t // GRID_W, t % GRID_W], -1).astype(F32)
    nq = HEAD_DIM // 4
    inv = ROPE_THETA ** (-jnp.arange(nq, dtype=F32) / nq)
    ang = pos[:, :, None] * inv
    cos = jnp.repeat(jnp.cos(ang)[:, :, None, :], 2, axis=2)
    sin = jnp.stack([-jnp.sin(ang), jnp.sin(ang)], axis=2)
    cos = jnp.tile(cos.reshape(n, HEAD_DIM), (1, LANES // HEAD_DIM))
    sin = jnp.tile(sin.reshape(n, HEAD_DIM), (1, LANES // HEAD_DIM))
    return cos, sin


def _head_gain(parts, n_out):
    row = jnp.ones((n_out,), F32)
    for col, n_heads, gain, scale in parts:
        row = lax.dynamic_update_slice(row, jnp.tile(gain.astype(F32) * scale, n_heads), (col,))
    return row.reshape(1, n_out)


def kernel(x, c, ctx, c_ctx, ada_w, ada_b, norm1_g, norm2_g, ev_w_in, ev_conv_w, ev_q_g, ev_k_g, ev_rpb,
           ev_w_out, ffn_w1, ffn_w3, ffn_w2, od_w_in, od_cq_g, od_ck_g, od_sink, od_dq_g, od_dk_g,
           od_lam_q1, od_lam_k1, od_lam_q2, od_lam_k2, od_subln_g, od_w_out, moe_router,
           moe_w1, moe_w3, moe_w2):
    nb, n, d = x.shape
    ctx_len = ctx.shape[1]
    depth = ada_w.shape[0]
    n_slots = d // HEAD_DIM
    conv_ch = d // 2
    na_heads = swa_heads = n_slots // 2
    swa_kv = max(1, swa_heads // 4)
    diff_heads = n_slots // 4
    rows = n // GRID_W
    assert rows % NA_QROWS == 0 and rows >= 2 * NA_QROWS and n % 512 == 0
    assert nb + 1 <= 8 and ctx_len % LANES == 0

    ev_q_col = 3 * conv_ch
    ev_k_col = ev_q_col + na_heads * HEAD_DIM
    ev_v_col = ev_k_col + na_heads * HEAD_DIM
    ev_n = ev_v_col + na_heads * HEAD_DIM
    od_dq_col = swa_heads * HEAD_DIM
    od_ck_col = od_dq_col + diff_heads * 2 * HEAD_DIM
    od_cv_col = od_ck_col + swa_kv * HEAD_DIM
    od_dk_col = od_cv_col + swa_kv * HEAD_DIM
    od_dv_col = od_dk_col + diff_heads * 2 * HEAD_DIM
    od_n = od_dv_col + diff_heads * 2 * HEAD_DIM
    ev_segs = ((0, ev_q_col, False), (ev_q_col, ev_v_col, True), (ev_v_col, ev_n, False))
    od_segs = ((0, od_cv_col, True), (od_cv_col, od_dk_col, False), (od_dk_col, od_dv_col, True),
               (od_dv_col, od_n, False))

    h = x.reshape(nb * n, d)
    hc = ctx.reshape(nb * ctx_len, d)
    tm_lat = 512
    tm_ctx = ctx_len

    s_rows = jnp.zeros((8, d), F32).at[:nb].set(c).at[nb].set(c_ctx)
    mod = _modulation(s_rows, ada_w, ada_b)
    rope_tabs = _rope_tables(n)

    def lat_vec(l, k):
        return mod[l, :nb, k * d:(k + 1) * d].reshape(nb, 1, d)

    def ctx_vec(l, k, copies):
        return jnp.broadcast_to(mod[l, nb, k * d:(k + 1) * d], (copies, 1, d))

    for l in range(depth):
        last = l == depth - 1
        i = l // 2
        g1 = norm1_g[l].reshape(1, d)
        g2 = norm2_g[l].reshape(1, d)
        if l % 2 == 0:
            w_in = ev_w_in[i].astype(BF16)
            gain = _head_gain([(ev_q_col, na_heads, ev_q_g[i], QK_SCALE), (ev_k_col, na_heads, ev_k_g[i], 1.0)],
                              ev_n)
            p_lat = _norm_proj(h, nb, g1, lat_vec(l, 1), lat_vec(l, 0), w_in, gain, ev_segs, None, tm_lat)
            p_ctx = _norm_proj(hc, nb, g1, ctx_vec(l, 1, nb), ctx_vec(l, 0, nb), w_in, gain, ev_segs, None,
                               tm_ctx)
            bias = _na_bias(ev_rpb[i], rows)
            vt_lat, vt_ctx = p_lat[:, ev_v_col:].T, p_ctx[:, ev_v_col:].T
            y_na = _na_attention(p_lat, vt_lat, p_ctx, vt_ctx, bias, nb, n, ctx_len, ev_q_col // LANES,
                                 ev_k_col // LANES, na_heads)
            w_out = ev_w_out[i].astype(BF16)
            conv_w = ev_conv_w[i]
            h = _out_proj(h, nb, lat_vec(l, 2), p_lat, y_na, w_out, conv_w, tm_lat)
            if not last:
                y_na_c = _gqa_attention(p_ctx, None, p_ctx, vt_ctx, None, nb, ctx_len, ctx_len, ev_q_col // LANES,
                                        ev_k_col // LANES, na_heads, na_heads, False, ctx_len)
                hc = _out_proj(hc, nb, ctx_vec(l, 2, nb), p_ctx, y_na_c, w_out, conv_w, tm_ctx)
            w1, w3, w2 = ffn_w1[i].astype(BF16), ffn_w3[i].astype(BF16), ffn_w2[i].astype(BF16)
            h = _ffn(h, nb, g2, lat_vec(l, 4), lat_vec(l, 3), lat_vec(l, 5), w1, w3, w2, tm_lat)
            if not last:
                hc = _ffn(hc, 1, g2, ctx_vec(l, 4, 1), ctx_vec(l, 3, 1), ctx_vec(l, 5, 1), w1, w3, w2, tm_lat)
        else:
            lam_init = 0.8 - 0.6 * math.exp(-0.3 * l)
            w_in = od_w_in[i].astype(BF16)
            gain = _head_gain([(0, swa_heads, od_cq_g[i], QK_SCALE),
                               (od_dq_col, 2 * diff_heads, od_dq_g[i], QK_SCALE),
                               (od_ck_col, swa_kv, od_ck_g[i], 1.0),
                               (od_dk_col, 2 * diff_heads, od_dk_g[i], 1.0)], od_n)
            p_lat = _norm_proj(h, nb, g1, lat_vec(l, 1), lat_vec(l, 0), w_in, gain, od_segs, rope_tabs, tm_lat)
            p_ctx = _norm_proj(hc, nb, g1, ctx_vec(l, 1, nb), ctx_vec(l, 0, nb), w_in, gain, od_segs, None,
                               tm_ctx)
            sink = jnp.broadcast_to(od_sink[i].astype(F32)[:, None], (swa_heads, LANES))
            lam_rows = jnp.zeros((8, LANES), F32).at[:4, :HEAD_DIM].set(
                jnp.stack([od_lam_q1[i], od_lam_k1[i], od_lam_q2[i], od_lam_k2[i]]).astype(F32))
            subg = od_subln_g[i].astype(F32).reshape(1, 2 * HEAD_DIM)
            cols = (od_dq_col // LANES, od_dk_col // LANES)
            vt_lat, vt_ctx = p_lat[:, od_dv_col:].T, p_ctx[:, od_dv_col:].T
            cvt_lat, cvt_ctx = p_lat[:, od_cv_col:od_dk_col].T, p_ctx[:, od_cv_col:od_dk_col].T
            y_c = _gqa_attention(p_lat, cvt_lat, p_ctx, cvt_ctx, sink, nb, n, ctx_len, 0, od_ck_col // LANES,
                                 swa_heads, swa_kv, True, 2 * BLOCK)
            y_d = _diff_attention(p_lat, p_lat, vt_lat, p_ctx, vt_ctx, lam_rows, subg, nb, n, n, ctx_len, *cols,
                                  diff_heads, lam_init, 512, 512)
            w_out = od_w_out[i].astype(BF16)
            h = _out_proj(h, nb, lat_vec(l, 2), y_c, y_d, w_out, None, tm_lat)
            if not last:
                y_c_c = _gqa_attention(p_ctx, None, p_ctx, cvt_ctx, sink, nb, ctx_len, ctx_len, 0,
                                       od_ck_col // LANES, swa_heads, swa_kv, False, ctx_len)
                y_d_c = _diff_attention(p_ctx, None, None, p_ctx, vt_ctx, lam_rows, subg, nb, ctx_len, 0, ctx_len,
                                        *cols, diff_heads, lam_init, ctx_len, ctx_len)
                hc = _out_proj(hc, nb, ctx_vec(l, 2, nb), y_c_c, y_d_c, w_out, None, tm_ctx)
            router = jnp.zeros((d, LANES), F32).at[:, :N_EXPERTS].set(moe_router[i])
            w1, w3, w2 = moe_w1[i].astype(BF16), moe_w3[i].astype(BF16), moe_w2[i].astype(BF16)
            h = _moe(h, nb, g2, lat_vec(l, 4), lat_vec(l, 3), lat_vec(l, 5), router, w1, w3, w2, tm_lat)
            if not last:
                hc = _moe(hc, 1, g2, ctx_vec(l, 4, 1), ctx_vec(l, 3, 1), ctx_vec(l, 5, 1), router, w1, w3, w2,
                          tm_lat)
    return h.reshape(nb, n, d)
```

```python
import functools
import math

import jax
import jax.numpy as jnp
from jax import lax
from jax.experimental import pallas as pl
from jax.experimental.pallas import tpu as pltpu

F32 = jnp.float32
BF16 = jnp.bfloat16

LANES = 128
BF16_SUBLANES = 16
VMEM_LIMIT = 56 * 1024 * 1024

HEAD_DIM = 64
GRID_W = 64
CONV_W = 3
NA_ROWS = 8
NA_COLS = 16
NA_QROWS = 8
SWA_WINDOW = 128
MOE_CHUNK = 256
BLOCK = 128
N_EXPERTS = 8
EXPERT_ROWS = 16
ROPE_THETA = 10000.0
EPS = 1e-6
NEG = -1e30
QK_SCALE = HEAD_DIM ** -0.5
LOG2_E = math.log2(math.e)
RESCALE_MARGIN = 32.0

NT_DIMS = (((1,), (1,)), ((), ()))


def _params(sem):
    return pltpu.CompilerParams(dimension_semantics=sem, vmem_limit_bytes=VMEM_LIMIT)


def _resident(shape, index_map):
    return pl.BlockSpec(shape, index_map, pipeline_mode=pl.Buffered(1))


def _lane_iota():
    return lax.broadcasted_iota(jnp.int32, (1, LANES), 1)


def _swap_halves(x):
    return jnp.concatenate([x[:, HEAD_DIM:], x[:, :HEAD_DIM]], axis=1)


def _modulated_norm(x, g, sc, sh):
    ms = jnp.mean(x * x, axis=-1, keepdims=True)
    return (x * lax.rsqrt(ms + EPS)) * (g * (1.0 + sc)) + sh


def _mod_kernel(s_ref, w_ref, b_ref, o_ref):
    s = s_ref[...]
    s = s * (1.0 / (1.0 + jnp.exp(-s)))
    o_ref[0] = jnp.dot(s.astype(BF16), w_ref[0].astype(BF16), preferred_element_type=F32) + b_ref[0]


def _modulation(s_rows, ada_w, ada_b):
    depth, d, n_out = ada_w.shape
    tn = n_out // 4
    return pl.pallas_call(
        _mod_kernel,
        grid=(depth, n_out // tn),
        in_specs=[pl.BlockSpec(s_rows.shape, lambda l, j: (0, 0)),
                  pl.BlockSpec((1, d, tn), lambda l, j: (l, 0, j)),
                  pl.BlockSpec((1, 1, tn), lambda l, j: (l, 0, j))],
        out_specs=pl.BlockSpec((1, s_rows.shape[0], tn), lambda l, j: (l, 0, j)),
        out_shape=jax.ShapeDtypeStruct((depth, s_rows.shape[0], n_out), F32),
        compiler_params=_params(("arbitrary", "arbitrary")),
        name="modulation",
    )(s_rows, ada_w, ada_b.reshape(depth, 1, n_out))


def _head_sumsq(z):
    r = lax.broadcasted_iota(jnp.int32, (LANES, LANES), 0) // HEAD_DIM
    c = lax.broadcasted_iota(jnp.int32, (LANES, LANES), 1) // HEAD_DIM
    same_head = jnp.where(r == c, 1.0, 0.0).astype(BF16)
    z2 = z * z
    hi = z2.astype(BF16)
    lo = (z2 - hi.astype(F32)).astype(BF16)
    return (jnp.dot(hi, same_head, preferred_element_type=F32)
            + jnp.dot(lo, same_head, preferred_element_type=F32))


def _norm_proj_kernel(*refs, segs, rope):
    if rope:
        x_ref, g_ref, sc_ref, sh_ref, w_ref, gain_ref, cos_ref, sin_ref, o_ref = refs
    else:
        x_ref, g_ref, sc_ref, sh_ref, w_ref, gain_ref, o_ref = refs
    a = _modulated_norm(x_ref[...], g_ref[...], sc_ref[0], sh_ref[0]).astype(BF16)
    first_half = (_lane_iota() % (HEAD_DIM // 2)) < (HEAD_DIM // 4)
    for c0, c1, normed in segs:
        acc = jnp.dot(a, w_ref[:, c0:c1], preferred_element_type=F32)
        if not normed:
            o_ref[:, c0:c1] = acc.astype(BF16)
            continue
        for j in range((c1 - c0) // LANES):
            z = acc[:, j * LANES:(j + 1) * LANES]
            lo = c0 + j * LANES
            z = z * lax.rsqrt(_head_sumsq(z) * (1.0 / HEAD_DIM) + EPS) * gain_ref[:, lo:lo + LANES]
            if rope:
                partner = jnp.where(first_half,
                                    pltpu.roll(z, LANES - HEAD_DIM // 4, axis=1),
                                    pltpu.roll(z, HEAD_DIM // 4, axis=1))
                z = z * cos_ref[...] + partner * sin_ref[...]
            o_ref[:, lo:lo + LANES] = z.astype(BF16)


def _norm_proj(h, nb, g, sc, sh, w, gain, segs, rope_tabs, tm):
    t, d = h.shape
    n_out = w.shape[1]
    steps = t // nb // tm
    in_specs = [pl.BlockSpec((tm, d), lambda b, i: (b * steps + i, 0)),
                _resident((1, d), lambda b, i: (0, 0)),
                pl.BlockSpec((1, 1, d), lambda b, i: (b, 0, 0)),
                pl.BlockSpec((1, 1, d), lambda b, i: (b, 0, 0)),
                _resident((d, n_out), lambda b, i: (0, 0)),
                _resident((1, n_out), lambda b, i: (0, 0))]
    args = [h, g, sc, sh, w, gain]
    if rope_tabs is not None:
        in_specs += [pl.BlockSpec((tm, LANES), lambda b, i: (i, 0))] * 2
        args += list(rope_tabs)
    return pl.pallas_call(
        functools.partial(_norm_proj_kernel, segs=segs, rope=rope_tabs is not None),
        grid=(nb, steps),
        in_specs=in_specs,
        out_specs=pl.BlockSpec((tm, n_out), lambda b, i: (b * steps + i, 0)),
        out_shape=jax.ShapeDtypeStruct((t, n_out), BF16),
        compiler_params=_params(("parallel", "parallel")),
        name="norm_proj",
    )(*args)


def _gated_conv(gb_ref, gc_ref, u_ref, gcp_ref, up_ref, gcn_ref, un_ref, cw_ref):
    i, steps = pl.program_id(1), pl.num_programs(1)
    v = gc_ref[...].astype(F32) * u_ref[...].astype(F32)
    tm = v.shape[0]
    last = BF16_SUBLANES - 1
    prev_row = gcp_ref[last:last + 1, :].astype(F32) * up_ref[last:last + 1, :].astype(F32)
    next_row = gcn_ref[0:1, :].astype(F32) * un_ref[0:1, :].astype(F32)
    prev_row = jnp.where(i > 0, prev_row, 0.0)
    next_row = jnp.where(i < steps - 1, next_row, 0.0)
    row = lax.broadcasted_iota(jnp.int32, (tm, 1), 0)
    v_prev = jnp.where(row == 0, prev_row, pltpu.roll(v, 1, axis=0))
    v_next = jnp.where(row == tm - 1, next_row, pltpu.roll(v, tm - 1, axis=0))
    cw = cw_ref[...]
    conv = cw[0:1, :] * v_prev + cw[1:2, :] * v + cw[2:3, :] * v_next
    return gb_ref[...].astype(F32) * conv


def _out_proj_kernel(*refs, conv):
    if conv:
        (gb_ref, gc_ref, u_ref, gcp_ref, up_ref, gcn_ref, un_ref, cw_ref,
         yb_ref, wa_ref, wb_ref, h_ref, gate_ref, o_ref) = refs
        ya = _gated_conv(gb_ref, gc_ref, u_ref, gcp_ref, up_ref, gcn_ref, un_ref, cw_ref).astype(BF16)
    else:
        ya_ref, yb_ref, wa_ref, wb_ref, h_ref, gate_ref, o_ref = refs
        ya = ya_ref[...]
    y = (jnp.dot(ya, wa_ref[...], preferred_element_type=F32)
         + jnp.dot(yb_ref[...], wb_ref[...], preferred_element_type=F32))
    o_ref[...] = h_ref[...] + gate_ref[0] * y


def _out_proj(h, nb, gate, ya_src, yb, w_out, conv_w, tm):
    t, d = h.shape
    steps = t // nb // tm
    wa_rows = w_out.shape[0] - yb.shape[1]
    w_a, w_b = w_out[:wa_rows], w_out[wa_rows:]
    row = lambda b, i: (b * steps + i, 0)
    if conv_w is not None:
        cc = conv_w.shape[1]
        hb = tm // BF16_SUBLANES
        n_halo = t // BF16_SUBLANES
        prev = lambda col: (lambda b, i: (jnp.maximum((b * steps + i) * hb - 1, 0), col))
        nxt = lambda col: (lambda b, i: (jnp.minimum((b * steps + i + 1) * hb, n_halo - 1), col))
        in_specs = [pl.BlockSpec((tm, cc), lambda b, i: (b * steps + i, 0)),
                    pl.BlockSpec((tm, cc), lambda b, i: (b * steps + i, 1)),
                    pl.BlockSpec((tm, cc), lambda b, i: (b * steps + i, 2)),
                    pl.BlockSpec((BF16_SUBLANES, cc), prev(1)),
                    pl.BlockSpec((BF16_SUBLANES, cc), prev(2)),
                    pl.BlockSpec((BF16_SUBLANES, cc), nxt(1)),
                    pl.BlockSpec((BF16_SUBLANES, cc), nxt(2)),
                    _resident(conv_w.shape, lambda b, i: (0, 0))]
        args = [ya_src] * 7 + [conv_w]
    else:
        in_specs = [pl.BlockSpec((tm, wa_rows), row)]
        args = [ya_src]
    in_specs += [pl.BlockSpec((tm, yb.shape[1]), row),
                 _resident(w_a.shape, lambda b, i: (0, 0)),
                 _resident(w_b.shape, lambda b, i: (0, 0)),
                 pl.BlockSpec((tm, d), row),
                 pl.BlockSpec((1, 1, d), lambda b, i: (b, 0, 0))]
    args += [yb, w_a, w_b, h, gate]
    return pl.pallas_call(
        functools.partial(_out_proj_kernel, conv=conv_w is not None),
        grid=(nb, steps),
        in_specs=in_specs,
        out_specs=pl.BlockSpec((tm, d), row),
        out_shape=jax.ShapeDtypeStruct((t, d), F32),
        compiler_params=_params(("parallel", "parallel")),
        name="out_proj",
    )(*args)


def _na_bias(rpb, rows):
    nblk = rows // NA_QROWS
    c = jnp.arange(GRID_W)
    c0 = jnp.clip(c - NA_COLS // 2, 0, GRID_W - NA_COLS)
    col_ok = (c[None, :] >= c0[:, None]) & (c[None, :] < c0[:, None] + NA_COLS)
    dc = jnp.clip(c[None, :] - c[:, None] + NA_COLS - 1, 0, 2 * NA_COLS - 2)
    pick_dc = ((dc[None] == jnp.arange(2 * NA_COLS - 1)[:, None, None]) & col_ok[None]).astype(F32)
    table = jnp.einsum('hrd,dcx->hrcx', rpb.astype(F32), pick_dc, precision=lax.Precision.HIGHEST)
    i = jnp.arange(NA_QROWS)
    j = jnp.arange(2 * NA_QROWS)
    pick_dr, row_ok = [], []
    for rb in (0, 1, nblk - 1):
        r = rb * NA_QROWS + i
        r0 = jnp.clip(r - NA_ROWS // 2, 0, rows - NA_ROWS)
        rk = rb * NA_QROWS - NA_ROWS // 2 + j
        ok = (rk[None, :] >= r0[:, None]) & (rk[None, :] < r0[:, None] + NA_ROWS)
        dr = rk[None, :] - r[:, None] + NA_ROWS - 1
        pick_dr.append(((dr[:, :, None] == jnp.arange(2 * NA_ROWS - 1)) & ok[:, :, None]).astype(F32))
        row_ok.append(ok)
    bias = jnp.einsum('vijr,hrcx->vhjxic', jnp.stack(pick_dr), table, precision=lax.Precision.HIGHEST)
    ok = jnp.stack(row_ok).transpose(0, 2, 1)[:, None, :, None, :, None] & col_ok.T[None, None, None, :, None, :]
    bias = jnp.where(ok, bias, NEG)
    return bias.reshape(3, rpb.shape[0], 2 * NA_QROWS * GRID_W, NA_QROWS * GRID_W)


def _softmax_pv_t(s_parts, vt_parts, extra=None):
    m = s_parts[0].max(axis=0, keepdims=True)
    for s in s_parts[1:]:
        m = jnp.maximum(m, s.max(axis=0, keepdims=True))
    if extra is not None:
        m = jnp.maximum(m, extra)
    den = None
    o = None
    for s, vt in zip(s_parts, vt_parts):
        e = jnp.exp(s - m)
        d = e.sum(axis=0, keepdims=True)
        pv = jnp.dot(vt, e.astype(BF16), preferred_element_type=F32)
        den = d if den is None else den + d
        o = pv if o is None else o + pv
    if extra is not None:
        den = den + jnp.exp(extra - m)
    return o / den


def _na_kernel(q_ref, k0, k1, k2, k3, vt0, vt1, vt2, vt3, kc_ref, vct_ref, bias_ref, o_ref):
    q = q_ref[...]
    k = jnp.concatenate([k0[...], k1[...], k2[...], k3[...]], axis=0)
    vt = jnp.concatenate([vt0[...], vt1[...], vt2[...], vt3[...]], axis=1)
    kc, vct = kc_ref[...], vct_ref[...]
    lane = _lane_iota()
    outs = []
    for hh in range(2):
        qh = jnp.where((lane // HEAD_DIM) == hh, q, jnp.zeros_like(q))
        s_loc = lax.dot_general(k, qh, NT_DIMS, preferred_element_type=F32) + bias_ref[0, hh]
        s_ctx = lax.dot_general(kc, qh, NT_DIMS, preferred_element_type=F32)
        rows = slice(hh * HEAD_DIM, (hh + 1) * HEAD_DIM)
        outs.append(_softmax_pv_t([s_loc, s_ctx], [vt[rows], vct[rows]]))
    o_ref[...] = jnp.concatenate(outs, axis=0).T.astype(BF16)


def _na_attention(p_lat, vt_lat, p_ctx, vt_ctx, bias, nb, n, ctx_len, q_col, k_col, n_heads):
    tq = NA_QROWS * GRID_W
    tw = tq // 2
    qsteps = n // tq
    wsteps = n // tw
    wblock = lambda rb, j: jnp.clip(2 * rb - 1 + j, 0, wsteps - 1)
    kwin = lambda j: pl.BlockSpec((tw, LANES), lambda hp, b, rb: (b * wsteps + wblock(rb, j), k_col + hp))
    vwin = lambda j: pl.BlockSpec((LANES, tw), lambda hp, b, rb: (hp, b * wsteps + wblock(rb, j)))
    variant = lambda rb: jnp.where(rb == 0, 0, jnp.where(rb == qsteps - 1, 2, 1))
    in_specs = ([pl.BlockSpec((tq, LANES), lambda hp, b, rb: (b * qsteps + rb, q_col + hp))]
                + [kwin(j) for j in range(4)] + [vwin(j) for j in range(4)]
                + [pl.BlockSpec((ctx_len, LANES), lambda hp, b, rb: (b, k_col + hp)),
                   pl.BlockSpec((LANES, ctx_len), lambda hp, b, rb: (hp, b)),
                   pl.BlockSpec((1, 2, 2 * tq, tq), lambda hp, b, rb: (variant(rb), hp, 0, 0))])
    return pl.pallas_call(
        _na_kernel,
        grid=(n_heads // 2, nb, qsteps),
        in_specs=in_specs,
        out_specs=pl.BlockSpec((tq, LANES), lambda hp, b, rb: (b * qsteps + rb, hp)),
        out_shape=jax.ShapeDtypeStruct((nb * n, n_heads * HEAD_DIM), BF16),
        compiler_params=_params(("parallel", "parallel", "arbitrary")),
        name="na_attention",
    )(*([p_lat] * 5 + [vt_lat] * 4 + [p_ctx, vt_ctx, bias]))


def _gqa_kernel(*refs, n_q, n_kv, has_local, has_sink, n_tokens):
    refs = list(refs)
    q_ref = refs.pop(0)
    if has_local:
        k_loc = jnp.concatenate([refs.pop(0)[...] for _ in range(4)], axis=0)
        vt_loc = jnp.concatenate([refs.pop(0)[...] for _ in range(4)], axis=1)
    kc_ref, vct_ref = refs.pop(0), refs.pop(0)
    sink_ref = refs.pop(0) if has_sink else None
    o_ref = refs.pop(0)

    tq = q_ref.shape[0]
    group = n_q // n_kv
    lane = _lane_iota()
    if has_local:
        base = pl.program_id(1) * tq
        kpos = base - BLOCK + lax.broadcasted_iota(jnp.int32, (tq + 2 * BLOCK, 1), 0)
        qpos = base + lax.broadcasted_iota(jnp.int32, (1, tq), 1)
        ok = (jnp.abs(kpos - qpos) <= SWA_WINDOW) & (kpos >= 0) & (kpos < n_tokens)
        band = jnp.where(ok, 0.0, NEG)
        band = jnp.concatenate([band] * group, axis=1)

    outs = []
    for kvh in range(n_kv):
        cg, half = kvh // 2, kvh % 2
        qs = []
        for h in range(kvh * group, (kvh + 1) * group):
            qg = q_ref[:, (h // 2) * LANES:(h // 2 + 1) * LANES]
            if h % 2 != half:
                qg = _swap_halves(qg)
            qs.append(jnp.where((lane // HEAD_DIM) == half, qg, jnp.zeros_like(qg)))
        qcat = jnp.concatenate(qs, axis=0) if group > 1 else qs[0]
        rows = slice(kvh * HEAD_DIM, (kvh + 1) * HEAD_DIM)
        kc = kc_ref[:, cg * LANES:(cg + 1) * LANES]
        s_parts = [lax.dot_general(kc, qcat, NT_DIMS, preferred_element_type=F32)]
        vt_parts = [vct_ref[rows, :]]
        if has_local:
            s_parts.append(lax.dot_general(k_loc, qcat, NT_DIMS, preferred_element_type=F32) + band)
            vt_parts.append(vt_loc[rows])
        extra = None
        if has_sink:
            extra = jnp.concatenate([sink_ref[h:h + 1, :] for h in range(kvh * group, (kvh + 1) * group)
                                     for _ in range(tq // LANES)], axis=1)
        o = _softmax_pv_t(s_parts, vt_parts, extra)
        outs += [o[:, g * tq:(g + 1) * tq] for g in range(group)]
    o_ref[...] = jnp.concatenate(outs, axis=0).T.astype(BF16)


def _gqa_attention(p_q, vt_q, p_ctx, vt_ctx, sink, nb, n, ctx_len, q_col, k_col, n_q, n_kv, has_local, tq):
    qsteps = n // tq
    kv_w = n_kv * HEAD_DIM
    q_w = n_q * HEAD_DIM
    in_specs = [pl.BlockSpec((tq, q_w), lambda b, i: (b * qsteps + i, q_col * LANES // q_w))]
    args = [p_q]
    if has_local:
        assert tq == 2 * BLOCK and kv_w == LANES
        wsteps = n // BLOCK
        wblock = lambda i, j: jnp.clip(2 * i - 1 + j, 0, wsteps - 1)
        in_specs += [pl.BlockSpec((BLOCK, LANES), functools.partial(
            lambda b, i, j: (b * wsteps + wblock(i, j), k_col), j=j)) for j in range(4)]
        in_specs += [pl.BlockSpec((LANES, BLOCK), functools.partial(
            lambda b, i, j: (0, b * wsteps + wblock(i, j)), j=j)) for j in range(4)]
        args += [p_q] * 4 + [vt_q] * 4
    in_specs += [pl.BlockSpec((ctx_len, kv_w), lambda b, i: (b, k_col * LANES // kv_w)),
                 pl.BlockSpec((kv_w, ctx_len), lambda b, i: (0, b))]
    args += [p_ctx, vt_ctx]
    if sink is not None:
        in_specs.append(_resident(sink.shape, lambda b, i: (0, 0)))
        args.append(sink)
    return pl.pallas_call(
        functools.partial(_gqa_kernel, n_q=n_q, n_kv=n_kv, has_local=has_local,
                          has_sink=sink is not None, n_tokens=n),
        grid=(nb, qsteps),
        in_specs=in_specs,
        out_specs=pl.BlockSpec((tq, q_w), lambda b, i: (b * qsteps + i, 0)),
        out_shape=jax.ShapeDtypeStruct((nb * n, q_w), BF16),
        compiler_params=_params(("parallel", "parallel")),
        name="gqa_attention",
    )(*args)


def _diff_kernel(*refs, n_kblocks, tk, lam_init):
    if n_kblocks:
        q_ref, k_ref, vt_ref, kc_ref, vct_ref, lam_ref, subg_ref, o_ref, m_sc, acc_sc, p_sc = refs
    else:
        q_ref, kc_ref, vct_ref, lam_ref, subg_ref, o_ref, m_sc, acc_sc = refs
    lane = _lane_iota()
    q = q_ref[...]
    zero = jnp.zeros_like(q)
    q_maps = [jnp.where(lane < HEAD_DIM, q, zero), jnp.where(lane >= HEAD_DIM, q, zero)]

    kc, vct = kc_ref[...], vct_ref[0]
    for i in range(2):
        s = lax.dot_general(kc, q_maps[i], NT_DIMS, preferred_element_type=F32)
        m = s.max(axis=0, keepdims=True)
        p = jnp.exp2((s - m).astype(BF16))
        m_sc[i] = m
        acc_sc[i] = jnp.dot(vct, p, preferred_element_type=F32)

    if n_kblocks:
        def scores(kb):
            k = k_ref[pl.ds(pl.multiple_of(kb * tk, tk), tk), :]
            return [lax.dot_general(k, q_maps[i], NT_DIMS, preferred_element_type=F32) for i in range(2)]

        def probs(kb, slot):
            s = scores(kb)
            excess = None
            for i in range(2):
                m_ref = m_sc[i]
                p_sc[slot, i] = jnp.exp2((s[i] - m_ref).astype(BF16))
                over = jnp.max(s[i].max(axis=0, keepdims=True) - m_ref)
                excess = over if excess is None else jnp.maximum(excess, over)
            return excess

        def settle(kb, slot, excess):
            @pl.when(excess > RESCALE_MARGIN)
            def _():
                s = scores(kb)
                for i in range(2):
                    m_prev = m_sc[i]
                    m_new = jnp.maximum(m_prev, s[i].max(axis=0, keepdims=True))
                    p_sc[slot, i] = jnp.exp2((s[i] - m_new).astype(BF16))
                    acc_sc[i] = jnp.exp2(m_prev - m_new) * acc_sc[i]
                    m_sc[i] = m_new

        def accumulate(kb, slot):
            vt = vt_ref[0, :, pl.ds(pl.multiple_of(kb * tk, tk), tk)]
            for i in range(2):
                acc_sc[i] += jnp.dot(vt, p_sc[slot, i], preferred_element_type=F32)

        def body(kb, excess):
            settle(kb - 1, (kb - 1) % 2, excess)
            accumulate(kb - 1, (kb - 1) % 2)
            return probs(kb, kb % 2)

        excess = lax.fori_loop(1, n_kblocks, body, probs(0, 0))
        settle(n_kblocks - 1, (n_kblocks - 1) % 2, excess)
        accumulate(n_kblocks - 1, (n_kblocks - 1) % 2)

    lp = lam_ref[...]
    lam = (jnp.exp(jnp.sum(lp[0:1] * lp[1:2], axis=-1, keepdims=True))
           - jnp.exp(jnp.sum(lp[2:3] * lp[3:4], axis=-1, keepdims=True)) + lam_init)
    dv = 2 * HEAD_DIM
    y0 = acc_sc[0, :dv] / acc_sc[0, dv:dv + 1]
    y1 = acc_sc[1, :dv] / acc_sc[1, dv:dv + 1]
    y = (y0 - lam * y1).T
    ms = jnp.mean(y * y, axis=-1, keepdims=True)
    y = y * lax.rsqrt(ms + EPS) * subg_ref[...] * (1.0 - lam_init)
    o_ref[...] = y.astype(BF16)


def _values_with_ones(p, col, n_heads):
    t = p.shape[0]
    vt = p[:, col:].T.reshape(n_heads, 2 * HEAD_DIM, t)
    return jnp.concatenate([vt, jnp.ones((n_heads, BF16_SUBLANES, t), BF16)], axis=1)


def _diff_attention(p_q, p_lat, vt_lat, p_ctx, vt_ctx, lam_rows, subg, nb, n_q, n_lat, ctx_len,
                    q_col, k_col, n_heads, lam_init, tq, tk):
    qsteps = n_q // tq
    has_latent = p_lat is not None
    vrows = 2 * HEAD_DIM + BF16_SUBLANES
    in_specs = [pl.BlockSpec((tq, LANES), lambda b, h, i: (b * qsteps + i, q_col + h))]
    args = [p_q]
    if has_latent:
        in_specs += [pl.BlockSpec((n_lat, LANES), lambda b, h, i: (b, k_col + h)),
                     pl.BlockSpec((1, vrows, n_lat), lambda b, h, i: (h, 0, b))]
        args += [p_lat, vt_lat]
    in_specs += [pl.BlockSpec((ctx_len, LANES), lambda b, h, i: (b, k_col + h)),
                 pl.BlockSpec((1, vrows, ctx_len), lambda b, h, i: (h, 0, b)),
                 _resident(lam_rows.shape, lambda b, h, i: (0, 0)),
                 _resident(subg.shape, lambda b, h, i: (0, 0))]
    args += [p_ctx, vt_ctx, lam_rows, subg]
    return pl.pallas_call(
        functools.partial(_diff_kernel, n_kblocks=n_lat // tk if has_latent else 0, tk=tk, lam_init=lam_init),
        grid=(nb, n_heads, qsteps),
        in_specs=in_specs,
        out_specs=pl.BlockSpec((tq, LANES), lambda b, h, i: (b * qsteps + i, h)),
        out_shape=jax.ShapeDtypeStruct((nb * n_q, n_heads * LANES), BF16),
        scratch_shapes=([pltpu.VMEM((2, 1, tq), F32), pltpu.VMEM((2, vrows, tq), F32)]
                        + ([pltpu.VMEM((2, 2, tk, tq), BF16)] if has_latent else [])),
        compiler_params=_params(("parallel", "parallel", "arbitrary")),
        name="diff_attention",
    )(*args)


def _silu(x):
    return x * (1.0 / (1.0 + jnp.exp(-x)))


def _ffn_kernel(x_ref, g_ref, sc_ref, sh_ref, gate_ref, w1_ref, w3_ref, w2_ref, o_ref):
    x = x_ref[...]
    a = _modulated_norm(x, g_ref[...], sc_ref[0], sh_ref[0]).astype(BF16)
    h1 = jnp.dot(a, w1_ref[...], preferred_element_type=F32)
    h3 = jnp.dot(a, w3_ref[...], preferred_element_type=F32)
    y = jnp.dot((_silu(h1) * h3).astype(BF16), w2_ref[...], preferred_element_type=F32)
    o_ref[...] = x + gate_ref[0] * y


def _ffn(h, nb, g, sc, sh, gate, w1, w3, w2, tm):
    t, d = h.shape
    steps = t // nb // tm
    row = lambda b, i: (b * steps + i, 0)
    vec = pl.BlockSpec((1, 1, d), lambda b, i: (b, 0, 0))
    return pl.pallas_call(
        _ffn_kernel,
        grid=(nb, steps),
        in_specs=[pl.BlockSpec((tm, d), row), _resident((1, d), lambda b, i: (0, 0)), vec, vec, vec,
                  _resident(w1.shape, lambda b, i: (0, 0)),
                  _resident(w3.shape, lambda b, i: (0, 0)),
                  _resident(w2.shape, lambda b, i: (0, 0))],
        out_specs=pl.BlockSpec((tm, d), row),
        out_shape=jax.ShapeDtypeStruct((t, d), F32),
        compiler_params=_params(("parallel", "parallel")),
        name="ffn",
    )(h, g, sc, sh, gate, w1, w3, w2)


def _top2_gates(logits):
    lane = _lane_iota()
    big = jnp.int32(LANES)
    lg = jnp.where(lane < N_EXPERTS, logits, -jnp.inf)
    m1 = lg.max(axis=-1, keepdims=True)
    i1 = jnp.where(lg == m1, lane, big).min(axis=-1, keepdims=True)
    rest = jnp.where(lane == i1, -jnp.inf, lg)
    m2 = rest.max(axis=-1, keepdims=True)
    i2 = jnp.where(rest == m2, lane, big).min(axis=-1, keepdims=True)
    e2 = jnp.exp(m2 - m1)
    den = 1.0 + e2
    return jnp.where(lane == i1, 1.0 / den, 0.0) + jnp.where(lane == i2, e2 / den, 0.0)


def _moe_kernel(x_ref, g_ref, sc_ref, sh_ref, gate_ref, r_ref, w1_ref, w3_ref, w2_ref, o_ref,
                a_sc, gates_sc, rank_sc, acc_sc, *, chunk):
    e = pl.program_id(2)
    tm = x_ref.shape[0]

    @pl.when(e == 0)
    def _():
        a = _modulated_norm(x_ref[...], g_ref[...], sc_ref[0], sh_ref[0])
        a_hi = a.astype(BF16)
        a_lo = (a - a_hi.astype(F32)).astype(BF16)
        r = r_ref[...]
        r_hi = r.astype(BF16)
        r_lo = (r - r_hi.astype(F32)).astype(BF16)
        logits = (jnp.dot(a_hi, r_hi, preferred_element_type=F32)
                  + jnp.dot(a_hi, r_lo, preferred_element_type=F32)
                  + jnp.dot(a_lo, r_hi, preferred_element_type=F32))
        a_sc[...] = a_hi
        gates_t = _top2_gates(logits).T[:EXPERT_ROWS]
        gates_sc[...] = gates_t
        before = (lax.broadcasted_iota(jnp.int32, (tm, tm), 0)
                  < lax.broadcasted_iota(jnp.int32, (tm, tm), 1))
        routed = jnp.where(gates_t > 0.0, 1.0, 0.0).astype(BF16)
        rank_sc[...] = jnp.dot(routed, jnp.where(before, 1.0, 0.0).astype(BF16), preferred_element_type=F32)
        acc_sc[...] = jnp.zeros_like(acc_sc)

    gate_row = gates_sc[pl.ds(e, 1), :]
    rank_row = jnp.where(gate_row > 0.0, rank_sc[pl.ds(e, 1), :], -1.0)
    count = jnp.sum(jnp.where(gate_row > 0.0, 1.0, 0.0)).astype(jnp.int32)
    slot = lax.broadcasted_iota(jnp.int32, (chunk, 1), 0).astype(F32)

    for j in range(tm // chunk):
        @pl.when(count > j * chunk)
        def _():
            pick = rank_row == (slot + float(j * chunk))
            pick_f = jnp.where(pick, 1.0, 0.0)
            xs = jnp.dot(pick_f.astype(BF16), a_sc[...], preferred_element_type=F32).astype(BF16)
            h1 = jnp.dot(xs, w1_ref[0], preferred_element_type=F32)
            h3 = jnp.dot(xs, w3_ref[0], preferred_element_type=F32)
            y = jnp.dot((_silu(h1) * h3).astype(BF16), w2_ref[0], preferred_element_type=F32)
            y = y * jnp.sum(pick_f * gate_row, axis=-1, keepdims=True)
            acc_sc[...] += lax.dot_general(pick_f.astype(BF16), y.astype(BF16), (((0,), (0,)), ((), ())),
                                           preferred_element_type=F32)

    @pl.when(e == pl.num_programs(2) - 1)
    def _():
        o_ref[...] = x_ref[...] + gate_ref[0] * acc_sc[...]


def _moe(h, nb, g, sc, sh, gate, router, w1, w3, w2, tm, chunk):
    t, d = h.shape
    steps = t // nb // tm
    n_e, _, f = w1.shape
    row = lambda b, i, e: (b * steps + i, 0)
    vec = pl.BlockSpec((1, 1, d), lambda b, i, e: (b, 0, 0))
    return pl.pallas_call(
        functools.partial(_moe_kernel, chunk=chunk),
        grid=(nb, steps, n_e),
        in_specs=[pl.BlockSpec((tm, d), row), _resident((1, d), lambda b, i, e: (0, 0)), vec, vec, vec,
                  _resident(router.shape, lambda b, i, e: (0, 0)),
                  pl.BlockSpec((1, d, f), lambda b, i, e: (e, 0, 0)),
                  pl.BlockSpec((1, d, f), lambda b, i, e: (e, 0, 0)),
                  pl.BlockSpec((1, f, d), lambda b, i, e: (e, 0, 0))],
        out_specs=pl.BlockSpec((tm, d), row),
        out_shape=jax.ShapeDtypeStruct((t, d), F32),
        scratch_shapes=[pltpu.VMEM((tm, d), BF16), pltpu.VMEM((EXPERT_ROWS, tm), F32),
                        pltpu.VMEM((EXPERT_ROWS, tm), F32), pltpu.VMEM((tm, d), F32)],
        compiler_params=_params(("parallel", "parallel", "arbitrary")),
        name="moe",
    )(h, g, sc, sh, gate, router, w1, w3, w2)


def _rope_tables(n):
    t = jnp.arange(n)
    pos = jnp.stack([t // GRID_W, t % GRID_W], -1).astype(F32)
    nq = HEAD_DIM // 4
    inv = ROPE_THETA ** (-jnp.arange(nq, dtype=F32) / nq)
    ang = pos[:, :, None] * inv
    cos = jnp.repeat(jnp.cos(ang)[:, :, None, :], 2, axis=2)
    sin = jnp.stack([-jnp.sin(ang), jnp.sin(ang)], axis=2)
    cos = jnp.tile(cos.reshape(n, HEAD_DIM), (1, LANES // HEAD_DIM))
    sin = jnp.tile(sin.reshape(n, HEAD_DIM), (1, LANES // HEAD_DIM))
    return cos, sin


def _head_gain(parts, n_out):
    row = jnp.ones((n_out,), F32)
    for col, n_heads, gain, scale in parts:
        row = lax.dynamic_update_slice(row, jnp.tile(gain.astype(F32) * scale, n_heads), (col,))
    return row.reshape(1, n_out)


def kernel(x, c, ctx, c_ctx, ada_w, ada_b, norm1_g, norm2_g, ev_w_in, ev_conv_w, ev_q_g, ev_k_g, ev_rpb,
           ev_w_out, ffn_w1, ffn_w3, ffn_w2, od_w_in, od_cq_g, od_ck_g, od_sink, od_dq_g, od_dk_g,
           od_lam_q1, od_lam_k1, od_lam_q2, od_lam_k2, od_subln_g, od_w_out, moe_router,
           moe_w1, moe_w3, moe_w2):
    nb, n, d = x.shape
    ctx_len = ctx.shape[1]
    depth = ada_w.shape[0]
    n_slots = d // HEAD_DIM
    conv_ch = d // 2
    na_heads = swa_heads = n_slots // 2
    swa_kv = max(1, swa_heads // 4)
    diff_heads = n_slots // 4
    rows = n // GRID_W
    assert rows % NA_QROWS == 0 and rows >= 2 * NA_QROWS and n % 512 == 0
    assert nb + 1 <= 8 and ctx_len % LANES == 0

    ev_q_col = 3 * conv_ch
    ev_k_col = ev_q_col + na_heads * HEAD_DIM
    ev_v_col = ev_k_col + na_heads * HEAD_DIM
    ev_n = ev_v_col + na_heads * HEAD_DIM
    od_dq_col = swa_heads * HEAD_DIM
    od_ck_col = od_dq_col + diff_heads * 2 * HEAD_DIM
    od_cv_col = od_ck_col + swa_kv * HEAD_DIM
    od_dk_col = od_cv_col + swa_kv * HEAD_DIM
    od_dv_col = od_dk_col + diff_heads * 2 * HEAD_DIM
    od_n = od_dv_col + diff_heads * 2 * HEAD_DIM
    ev_segs = ((0, ev_q_col, False), (ev_q_col, ev_v_col, True), (ev_v_col, ev_n, False))
    od_segs = ((0, od_cv_col, True), (od_cv_col, od_dk_col, False), (od_dk_col, od_dv_col, True),
               (od_dv_col, od_n, False))

    h = x.reshape(nb * n, d)
    hc = ctx.reshape(nb * ctx_len, d)
    tm_lat = 512
    tm_ctx = ctx_len
    tm_moe = 1024

    s_rows = jnp.zeros((8, d), F32).at[:nb].set(c).at[nb].set(c_ctx)
    mod = _modulation(s_rows, ada_w, ada_b)
    rope_tabs = _rope_tables(n)

    def lat_vec(l, k):
        return mod[l, :nb, k * d:(k + 1) * d].reshape(nb, 1, d)

    def ctx_vec(l, k, copies):
        return jnp.broadcast_to(mod[l, nb, k * d:(k + 1) * d], (copies, 1, d))

    for l in range(depth):
        last = l == depth - 1
        i = l // 2
        g1 = norm1_g[l].reshape(1, d)
        g2 = norm2_g[l].reshape(1, d)
        if l % 2 == 0:
            w_in = ev_w_in[i].astype(BF16)
            gain = _head_gain([(ev_q_col, na_heads, ev_q_g[i], QK_SCALE), (ev_k_col, na_heads, ev_k_g[i], 1.0)],
                              ev_n)
            p_lat = _norm_proj(h, nb, g1, lat_vec(l, 1), lat_vec(l, 0), w_in, gain, ev_segs, None, tm_lat)
            p_ctx = _norm_proj(hc, nb, g1, ctx_vec(l, 1, nb), ctx_vec(l, 0, nb), w_in, gain, ev_segs, None,
                               tm_ctx)
            bias = _na_bias(ev_rpb[i], rows)
            vt_lat, vt_ctx = p_lat[:, ev_v_col:].T, p_ctx[:, ev_v_col:].T
            y_na = _na_attention(p_lat, vt_lat, p_ctx, vt_ctx, bias, nb, n, ctx_len, ev_q_col // LANES,
                                 ev_k_col // LANES, na_heads)
            w_out = ev_w_out[i].astype(BF16)
            conv_w = ev_conv_w[i]
            h = _out_proj(h, nb, lat_vec(l, 2), p_lat, y_na, w_out, conv_w, tm_lat)
            if not last:
                y_na_c = _gqa_attention(p_ctx, None, p_ctx, vt_ctx, None, nb, ctx_len, ctx_len, ev_q_col // LANES,
                                        ev_k_col // LANES, na_heads, na_heads, False, ctx_len)
                hc = _out_proj(hc, nb, ctx_vec(l, 2, nb), p_ctx, y_na_c, w_out, conv_w, tm_ctx)
            w1, w3, w2 = ffn_w1[i].astype(BF16), ffn_w3[i].astype(BF16), ffn_w2[i].astype(BF16)
            h = _ffn(h, nb, g2, lat_vec(l, 4), lat_vec(l, 3), lat_vec(l, 5), w1, w3, w2, tm_lat)
            if not last:
                hc = _ffn(hc, 1, g2, ctx_vec(l, 4, 1), ctx_vec(l, 3, 1), ctx_vec(l, 5, 1), w1, w3, w2, tm_lat)
        else:
            lam_init = 0.8 - 0.6 * math.exp(-0.3 * l)
            w_in = od_w_in[i].astype(BF16)
            gain = _head_gain([(0, swa_heads, od_cq_g[i], QK_SCALE),
                               (od_dq_col, 2 * diff_heads, od_dq_g[i], QK_SCALE * LOG2_E),
                               (od_ck_col, swa_kv, od_ck_g[i], 1.0),
                               (od_dk_col, 2 * diff_heads, od_dk_g[i], 1.0)], od_n)
            p_lat = _norm_proj(h, nb, g1, lat_vec(l, 1), lat_vec(l, 0), w_in, gain, od_segs, rope_tabs, tm_lat)
            p_ctx = _norm_proj(hc, nb, g1, ctx_vec(l, 1, nb), ctx_vec(l, 0, nb), w_in, gain, od_segs, None,
                               tm_ctx)
            sink = jnp.broadcast_to(od_sink[i].astype(F32)[:, None], (swa_heads, LANES))
            lam_rows = jnp.zeros((8, LANES), F32).at[:4, :HEAD_DIM].set(
                jnp.stack([od_lam_q1[i], od_lam_k1[i], od_lam_q2[i], od_lam_k2[i]]).astype(F32))
            subg = od_subln_g[i].astype(F32).reshape(1, 2 * HEAD_DIM)
            cols = (od_dq_col // LANES, od_dk_col // LANES)
            vt_lat = _values_with_ones(p_lat, od_dv_col, diff_heads)
            vt_ctx = _values_with_ones(p_ctx, od_dv_col, diff_heads)
            cvt_lat, cvt_ctx = p_lat[:, od_cv_col:od_dk_col].T, p_ctx[:, od_cv_col:od_dk_col].T
            y_c = _gqa_attention(p_lat, cvt_lat, p_ctx, cvt_ctx, sink, nb, n, ctx_len, 0, od_ck_col // LANES,
                                 swa_heads, swa_kv, True, 2 * BLOCK)
            y_d = _diff_attention(p_lat, p_lat, vt_lat, p_ctx, vt_ctx, lam_rows, subg, nb, n, n, ctx_len, *cols,
                                  diff_heads, lam_init, 512, 1024)
            w_out = od_w_out[i].astype(BF16)
            h = _out_proj(h, nb, lat_vec(l, 2), y_c, y_d, w_out, None, tm_lat)
            if not last:
                y_c_c = _gqa_attention(p_ctx, None, p_ctx, cvt_ctx, sink, nb, ctx_len, ctx_len, 0,
                                       od_ck_col // LANES, swa_heads, swa_kv, False, ctx_len)
                y_d_c = _diff_attention(p_ctx, None, None, p_ctx, vt_ctx, lam_rows, subg, nb, ctx_len, 0, ctx_len,
                                        *cols, diff_heads, lam_init, ctx_len, ctx_len)
                hc = _out_proj(hc, nb, ctx_vec(l, 2, nb), y_c_c, y_d_c, w_out, None, tm_ctx)
            router = jnp.zeros((d, LANES), F32).at[:, :N_EXPERTS].set(moe_router[i])
            w1, w3, w2 = moe_w1[i].astype(BF16), moe_w3[i].astype(BF16), moe_w2[i].astype(BF16)
            h = _moe(h, nb, g2, lat_vec(l, 4), lat_vec(l, 3), lat_vec(l, 5), router, w1, w3, w2, tm_moe, MOE_CHUNK)
            if not last:
                hc = _moe(hc, 1, g2, ctx_vec(l, 4, 1), ctx_vec(l, 3, 1), ctx_vec(l, 5, 1), router, w1, w3, w2,
                          min(tm_moe, nb * ctx_len), MOE_CHUNK)
    return h.reshape(nb, n, d)
```

```python
import functools
import math

import jax
import jax.numpy as jnp
from jax import lax
from jax.experimental import pallas as pl
from jax.experimental.pallas import tpu as pltpu

F32 = jnp.float32
BF16 = jnp.bfloat16

LANES = 128
BF16_SUBLANES = 16
VMEM_LIMIT = 56 * 1024 * 1024

HEAD_DIM = 64
GRID_W = 64
CONV_W = 3
NA_ROWS = 8
NA_COLS = 16
NA_QROWS = 8
SWA_WINDOW = 128
MOE_CHUNK = 256
BLOCK = 128
N_EXPERTS = 8
EXPERT_ROWS = 16
ROPE_THETA = 10000.0
EPS = 1e-6
NEG = -1e30
QK_SCALE = HEAD_DIM ** -0.5
LOG2_E = math.log2(math.e)
BF16_NORM_SLACK = 1.02
RESCALE_MARGIN = 32.0

NT_DIMS = (((1,), (1,)), ((), ()))


def _params(sem):
    return pltpu.CompilerParams(dimension_semantics=sem, vmem_limit_bytes=VMEM_LIMIT)


def _resident(shape, index_map):
    return pl.BlockSpec(shape, index_map, pipeline_mode=pl.Buffered(1))


def _lane_iota():
    return lax.broadcasted_iota(jnp.int32, (1, LANES), 1)


def _swap_halves(x):
    return jnp.concatenate([x[:, HEAD_DIM:], x[:, :HEAD_DIM]], axis=1)


def _modulated_norm(x, g, sc, sh):
    ms = jnp.mean(x * x, axis=-1, keepdims=True)
    return (x * lax.rsqrt(ms + EPS)) * (g * (1.0 + sc)) + sh


def _mod_kernel(s_ref, w_ref, b_ref, o_ref):
    s = s_ref[...]
    s = s * (1.0 / (1.0 + jnp.exp(-s)))
    o_ref[0] = jnp.dot(s.astype(BF16), w_ref[0].astype(BF16), preferred_element_type=F32) + b_ref[0]


def _modulation(s_rows, ada_w, ada_b):
    depth, d, n_out = ada_w.shape
    tn = n_out // 4
    return pl.pallas_call(
        _mod_kernel,
        grid=(depth, n_out // tn),
        in_specs=[pl.BlockSpec(s_rows.shape, lambda l, j: (0, 0)),
                  pl.BlockSpec((1, d, tn), lambda l, j: (l, 0, j)),
                  pl.BlockSpec((1, 1, tn), lambda l, j: (l, 0, j))],
        out_specs=pl.BlockSpec((1, s_rows.shape[0], tn), lambda l, j: (l, 0, j)),
        out_shape=jax.ShapeDtypeStruct((depth, s_rows.shape[0], n_out), F32),
        compiler_params=_params(("arbitrary", "arbitrary")),
        name="modulation",
    )(s_rows, ada_w, ada_b.reshape(depth, 1, n_out))


def _head_sumsq(z):
    r = lax.broadcasted_iota(jnp.int32, (LANES, LANES), 0) // HEAD_DIM
    c = lax.broadcasted_iota(jnp.int32, (LANES, LANES), 1) // HEAD_DIM
    same_head = jnp.where(r == c, 1.0, 0.0).astype(BF16)
    z2 = z * z
    hi = z2.astype(BF16)
    lo = (z2 - hi.astype(F32)).astype(BF16)
    return (jnp.dot(hi, same_head, preferred_element_type=F32)
            + jnp.dot(lo, same_head, preferred_element_type=F32))


def _norm_proj_kernel(*refs, segs, rope):
    if rope:
        x_ref, g_ref, sc_ref, sh_ref, w_ref, gain_ref, cos_ref, sin_ref, o_ref = refs
    else:
        x_ref, g_ref, sc_ref, sh_ref, w_ref, gain_ref, o_ref = refs
    a = _modulated_norm(x_ref[...], g_ref[...], sc_ref[0], sh_ref[0]).astype(BF16)
    first_half = (_lane_iota() % (HEAD_DIM // 2)) < (HEAD_DIM // 4)
    for c0, c1, normed in segs:
        acc = jnp.dot(a, w_ref[:, c0:c1], preferred_element_type=F32)
        if not normed:
            o_ref[:, c0:c1] = acc.astype(BF16)
            continue
        for j in range((c1 - c0) // LANES):
            z = acc[:, j * LANES:(j + 1) * LANES]
            lo = c0 + j * LANES
            z = z * lax.rsqrt(_head_sumsq(z) * (1.0 / HEAD_DIM) + EPS) * gain_ref[:, lo:lo + LANES]
            if rope:
                partner = jnp.where(first_half,
                                    pltpu.roll(z, LANES - HEAD_DIM // 4, axis=1),
                                    pltpu.roll(z, HEAD_DIM // 4, axis=1))
                z = z * cos_ref[...] + partner * sin_ref[...]
            o_ref[:, lo:lo + LANES] = z.astype(BF16)


def _norm_proj(h, nb, g, sc, sh, w, gain, segs, rope_tabs, tm):
    t, d = h.shape
    n_out = w.shape[1]
    steps = t // nb // tm
    in_specs = [pl.BlockSpec((tm, d), lambda b, i: (b * steps + i, 0)),
                _resident((1, d), lambda b, i: (0, 0)),
                pl.BlockSpec((1, 1, d), lambda b, i: (b, 0, 0)),
                pl.BlockSpec((1, 1, d), lambda b, i: (b, 0, 0)),
                _resident((d, n_out), lambda b, i: (0, 0)),
                _resident((1, n_out), lambda b, i: (0, 0))]
    args = [h, g, sc, sh, w, gain]
    if rope_tabs is not None:
        in_specs += [pl.BlockSpec((tm, LANES), lambda b, i: (i, 0))] * 2
        args += list(rope_tabs)
    return pl.pallas_call(
        functools.partial(_norm_proj_kernel, segs=segs, rope=rope_tabs is not None),
        grid=(nb, steps),
        in_specs=in_specs,
        out_specs=pl.BlockSpec((tm, n_out), lambda b, i: (b * steps + i, 0)),
        out_shape=jax.ShapeDtypeStruct((t, n_out), BF16),
        compiler_params=_params(("parallel", "parallel")),
        name="norm_proj",
    )(*args)


def _gated_conv(gb_ref, gc_ref, u_ref, gcp_ref, up_ref, gcn_ref, un_ref, cw_ref):
    i, steps = pl.program_id(1), pl.num_programs(1)
    v = gc_ref[...].astype(F32) * u_ref[...].astype(F32)
    tm = v.shape[0]
    last = BF16_SUBLANES - 1
    prev_row = gcp_ref[last:last + 1, :].astype(F32) * up_ref[last:last + 1, :].astype(F32)
    next_row = gcn_ref[0:1, :].astype(F32) * un_ref[0:1, :].astype(F32)
    prev_row = jnp.where(i > 0, prev_row, 0.0)
    next_row = jnp.where(i < steps - 1, next_row, 0.0)
    row = lax.broadcasted_iota(jnp.int32, (tm, 1), 0)
    v_prev = jnp.where(row == 0, prev_row, pltpu.roll(v, 1, axis=0))
    v_next = jnp.where(row == tm - 1, next_row, pltpu.roll(v, tm - 1, axis=0))
    cw = cw_ref[...]
    conv = cw[0:1, :] * v_prev + cw[1:2, :] * v + cw[2:3, :] * v_next
    return gb_ref[...].astype(F32) * conv


def _out_proj_kernel(*refs, conv):
    if conv:
        (gb_ref, gc_ref, u_ref, gcp_ref, up_ref, gcn_ref, un_ref, cw_ref,
         yb_ref, wa_ref, wb_ref, h_ref, gate_ref, o_ref) = refs
        ya = _gated_conv(gb_ref, gc_ref, u_ref, gcp_ref, up_ref, gcn_ref, un_ref, cw_ref).astype(BF16)
    else:
        ya_ref, yb_ref, wa_ref, wb_ref, h_ref, gate_ref, o_ref = refs
        ya = ya_ref[...]
    y = (jnp.dot(ya, wa_ref[...], preferred_element_type=F32)
         + jnp.dot(yb_ref[...], wb_ref[...], preferred_element_type=F32))
    o_ref[...] = h_ref[...] + gate_ref[0] * y


def _out_proj(h, nb, gate, ya_src, yb, w_out, conv_w, tm):
    t, d = h.shape
    steps = t // nb // tm
    wa_rows = w_out.shape[0] - yb.shape[1]
    w_a, w_b = w_out[:wa_rows], w_out[wa_rows:]
    row = lambda b, i: (b * steps + i, 0)
    if conv_w is not None:
        cc = conv_w.shape[1]
        hb = tm // BF16_SUBLANES
        n_halo = t // BF16_SUBLANES
        prev = lambda col: (lambda b, i: (jnp.maximum((b * steps + i) * hb - 1, 0), col))
        nxt = lambda col: (lambda b, i: (jnp.minimum((b * steps + i + 1) * hb, n_halo - 1), col))
        in_specs = [pl.BlockSpec((tm, cc), lambda b, i: (b * steps + i, 0)),
                    pl.BlockSpec((tm, cc), lambda b, i: (b * steps + i, 1)),
                    pl.BlockSpec((tm, cc), lambda b, i: (b * steps + i, 2)),
                    pl.BlockSpec((BF16_SUBLANES, cc), prev(1)),
                    pl.BlockSpec((BF16_SUBLANES, cc), prev(2)),
                    pl.BlockSpec((BF16_SUBLANES, cc), nxt(1)),
                    pl.BlockSpec((BF16_SUBLANES, cc), nxt(2)),
                    _resident(conv_w.shape, lambda b, i: (0, 0))]
        args = [ya_src] * 7 + [conv_w]
    else:
        in_specs = [pl.BlockSpec((tm, wa_rows), row)]
        args = [ya_src]
    in_specs += [pl.BlockSpec((tm, yb.shape[1]), row),
                 _resident(w_a.shape, lambda b, i: (0, 0)),
                 _resident(w_b.shape, lambda b, i: (0, 0)),
                 pl.BlockSpec((tm, d), row),
                 pl.BlockSpec((1, 1, d), lambda b, i: (b, 0, 0))]
    args += [yb, w_a, w_b, h, gate]
    return pl.pallas_call(
        functools.partial(_out_proj_kernel, conv=conv_w is not None),
        grid=(nb, steps),
        in_specs=in_specs,
        out_specs=pl.BlockSpec((tm, d), row),
        out_shape=jax.ShapeDtypeStruct((t, d), F32),
        compiler_params=_params(("parallel", "parallel")),
        name="out_proj",
    )(*args)


def _na_bias(rpb, rows):
    nblk = rows // NA_QROWS
    c = jnp.arange(GRID_W)
    c0 = jnp.clip(c - NA_COLS // 2, 0, GRID_W - NA_COLS)
    col_ok = (c[None, :] >= c0[:, None]) & (c[None, :] < c0[:, None] + NA_COLS)
    dc = jnp.clip(c[None, :] - c[:, None] + NA_COLS - 1, 0, 2 * NA_COLS - 2)
    pick_dc = ((dc[None] == jnp.arange(2 * NA_COLS - 1)[:, None, None]) & col_ok[None]).astype(F32)
    table = jnp.einsum('hrd,dcx->hrcx', rpb.astype(F32), pick_dc, precision=lax.Precision.HIGHEST)
    i = jnp.arange(NA_QROWS)
    j = jnp.arange(2 * NA_QROWS)
    pick_dr, row_ok = [], []
    for rb in (0, 1, nblk - 1):
        r = rb * NA_QROWS + i
        r0 = jnp.clip(r - NA_ROWS // 2, 0, rows - NA_ROWS)
        rk = rb * NA_QROWS - NA_ROWS // 2 + j
        ok = (rk[None, :] >= r0[:, None]) & (rk[None, :] < r0[:, None] + NA_ROWS)
        dr = rk[None, :] - r[:, None] + NA_ROWS - 1
        pick_dr.append(((dr[:, :, None] == jnp.arange(2 * NA_ROWS - 1)) & ok[:, :, None]).astype(F32))
        row_ok.append(ok)
    bias = jnp.einsum('vijr,hrcx->vhjxic', jnp.stack(pick_dr), table, precision=lax.Precision.HIGHEST)
    ok = jnp.stack(row_ok).transpose(0, 2, 1)[:, None, :, None, :, None] & col_ok.T[None, None, None, :, None, :]
    bias = jnp.where(ok, bias * LOG2_E, NEG)
    return bias.reshape(3, rpb.shape[0], 2 * NA_QROWS * GRID_W, NA_QROWS * GRID_W)


def _softmax_pv_t(s_parts, vt_parts, extra=None):
    m = s_parts[0].max(axis=0, keepdims=True)
    for s in s_parts[1:]:
        m = jnp.maximum(m, s.max(axis=0, keepdims=True))
    if extra is not None:
        m = jnp.maximum(m, extra)
    acc = None
    for s, vt in zip(s_parts, vt_parts):
        pv = jnp.dot(vt, jnp.exp2((s - m).astype(BF16)), preferred_element_type=F32)
        acc = pv if acc is None else acc + pv
    den = acc[HEAD_DIM:HEAD_DIM + 1]
    if extra is not None:
        den = den + jnp.exp2(extra - m)
    return acc[:HEAD_DIM] / den


def _na_kernel(q_ref, k0, k1, k2, k3, vt0, vt1, vt2, vt3, kc_ref, vct_ref, bias_ref, o_ref):
    q = q_ref[...]
    k = jnp.concatenate([k0[...], k1[...], k2[...], k3[...]], axis=0)
    vt = jnp.concatenate([vt0[...], vt1[...], vt2[...], vt3[...]], axis=2)
    kc, vct = kc_ref[...], vct_ref[...]
    lane = _lane_iota()
    scores = []
    for hh in range(2):
        qh = jnp.where((lane // HEAD_DIM) == hh, q, jnp.zeros_like(q))
        scores.append([lax.dot_general(k, qh, NT_DIMS, preferred_element_type=F32) + bias_ref[0, hh],
                       lax.dot_general(kc, qh, NT_DIMS, preferred_element_type=F32)])
    outs = [_softmax_pv_t(scores[hh], [vt[hh], vct[hh]]) for hh in range(2)]
    o_ref[...] = jnp.concatenate(outs, axis=0).T.astype(BF16)


def _na_attention(p_lat, vt_lat, p_ctx, vt_ctx, bias, nb, n, ctx_len, q_col, k_col, n_heads):
    tq = NA_QROWS * GRID_W
    tw = tq // 2
    qsteps = n // tq
    wsteps = n // tw
    wblock = lambda rb, j: jnp.clip(2 * rb - 1 + j, 0, wsteps - 1)
    kwin = lambda j: pl.BlockSpec((tw, LANES), lambda hp, b, rb: (b * wsteps + wblock(rb, j), k_col + hp))
    vrows = HEAD_DIM + BF16_SUBLANES
    vwin = lambda j: pl.BlockSpec((2, vrows, tw), lambda hp, b, rb: (hp, 0, b * wsteps + wblock(rb, j)))
    variant = lambda rb: jnp.where(rb == 0, 0, jnp.where(rb == qsteps - 1, 2, 1))
    in_specs = ([pl.BlockSpec((tq, LANES), lambda hp, b, rb: (b * qsteps + rb, q_col + hp))]
                + [kwin(j) for j in range(4)] + [vwin(j) for j in range(4)]
                + [pl.BlockSpec((ctx_len, LANES), lambda hp, b, rb: (b, k_col + hp)),
                   pl.BlockSpec((2, vrows, ctx_len), lambda hp, b, rb: (hp, 0, b)),
                   pl.BlockSpec((1, 2, 2 * tq, tq), lambda hp, b, rb: (variant(rb), hp, 0, 0))])
    return pl.pallas_call(
        _na_kernel,
        grid=(n_heads // 2, nb, qsteps),
        in_specs=in_specs,
        out_specs=pl.BlockSpec((tq, LANES), lambda hp, b, rb: (b * qsteps + rb, hp)),
        out_shape=jax.ShapeDtypeStruct((nb * n, n_heads * HEAD_DIM), BF16),
        compiler_params=_params(("parallel", "parallel", "arbitrary")),
        name="na_attention",
    )(*([p_lat] * 5 + [vt_lat] * 4 + [p_ctx, vt_ctx, bias]))


def _gqa_kernel(*refs, n_q, n_kv, has_local, has_sink, n_tokens):
    refs = list(refs)
    q_ref = refs.pop(0)
    if has_local:
        k_loc = jnp.concatenate([refs.pop(0)[...] for _ in range(4)], axis=0)
        vt_loc = jnp.concatenate([refs.pop(0)[...] for _ in range(4)], axis=2)
    kc_ref, vct_ref = refs.pop(0), refs.pop(0)
    sink_ref = refs.pop(0) if has_sink else None
    o_ref = refs.pop(0)

    tq = q_ref.shape[0]
    group = n_q // n_kv
    lane = _lane_iota()
    if has_local:
        base = pl.program_id(1) * tq
        kpos = base - BLOCK + lax.broadcasted_iota(jnp.int32, (tq + 2 * BLOCK, 1), 0)
        qpos = base + lax.broadcasted_iota(jnp.int32, (1, tq), 1)
        ok = (jnp.abs(kpos - qpos) <= SWA_WINDOW) & (kpos >= 0) & (kpos < n_tokens)
        band = jnp.where(ok, 0.0, NEG)
        band = jnp.concatenate([band] * group, axis=1)

    outs = []
    for kvh in range(n_kv):
        cg, half = kvh // 2, kvh % 2
        qs = []
        for h in range(kvh * group, (kvh + 1) * group):
            qg = q_ref[:, (h // 2) * LANES:(h // 2 + 1) * LANES]
            if h % 2 != half:
                qg = _swap_halves(qg)
            qs.append(jnp.where((lane // HEAD_DIM) == half, qg, jnp.zeros_like(qg)))
        qcat = jnp.concatenate(qs, axis=0) if group > 1 else qs[0]
        kc = kc_ref[:, cg * LANES:(cg + 1) * LANES]
        s_parts = [lax.dot_general(kc, qcat, NT_DIMS, preferred_element_type=F32)]
        vt_parts = [vct_ref[kvh]]
        if has_local:
            s_parts.append(lax.dot_general(k_loc, qcat, NT_DIMS, preferred_element_type=F32) + band)
            vt_parts.append(vt_loc[kvh])
        extra = None
        if has_sink:
            extra = jnp.concatenate([sink_ref[h:h + 1, :] for h in range(kvh * group, (kvh + 1) * group)
                                     for _ in range(tq // LANES)], axis=1)
        o = _softmax_pv_t(s_parts, vt_parts, extra)
        outs += [o[:, g * tq:(g + 1) * tq] for g in range(group)]
    o_ref[...] = jnp.concatenate(outs, axis=0).T.astype(BF16)


def _gqa_attention(p_q, vt_q, p_ctx, vt_ctx, sink, nb, n, ctx_len, q_col, k_col, n_q, n_kv, has_local, tq):
    qsteps = n // tq
    kv_w = n_kv * HEAD_DIM
    q_w = n_q * HEAD_DIM
    vrows = HEAD_DIM + BF16_SUBLANES
    in_specs = [pl.BlockSpec((tq, q_w), lambda b, i: (b * qsteps + i, q_col * LANES // q_w))]
    args = [p_q]
    if has_local:
        assert tq == 2 * BLOCK and kv_w == LANES
        wsteps = n // BLOCK
        wblock = lambda i, j: jnp.clip(2 * i - 1 + j, 0, wsteps - 1)
        in_specs += [pl.BlockSpec((BLOCK, LANES), functools.partial(
            lambda b, i, j: (b * wsteps + wblock(i, j), k_col), j=j)) for j in range(4)]
        in_specs += [pl.BlockSpec((n_kv, vrows, BLOCK), functools.partial(
            lambda b, i, j: (0, 0, b * wsteps + wblock(i, j)), j=j)) for j in range(4)]
        args += [p_q] * 4 + [vt_q] * 4
    in_specs += [pl.BlockSpec((ctx_len, kv_w), lambda b, i: (b, k_col * LANES // kv_w)),
                 pl.BlockSpec((n_kv, vrows, ctx_len), lambda b, i: (0, 0, b))]
    args += [p_ctx, vt_ctx]
    if sink is not None:
        in_specs.append(_resident(sink.shape, lambda b, i: (0, 0)))
        args.append(sink)
    return pl.pallas_call(
        functools.partial(_gqa_kernel, n_q=n_q, n_kv=n_kv, has_local=has_local,
                          has_sink=sink is not None, n_tokens=n),
        grid=(nb, qsteps),
        in_specs=in_specs,
        out_specs=pl.BlockSpec((tq, q_w), lambda b, i: (b * qsteps + i, 0)),
        out_shape=jax.ShapeDtypeStruct((nb * n, q_w), BF16),
        compiler_params=_params(("parallel", "parallel")),
        name="gqa_attention",
    )(*args)


def _diff_kernel(*refs, n_kblocks, tk, lam_init):
    if n_kblocks:
        q_ref, k_ref, vt_ref, kc_ref, vct_ref, lam_ref, subg_ref, bound_ref, o_ref, m_sc, acc_sc, p_sc = refs
    else:
        q_ref, kc_ref, vct_ref, lam_ref, subg_ref, bound_ref, o_ref, m_sc, acc_sc = refs
    lane = _lane_iota()
    q = q_ref[...]
    zero = jnp.zeros_like(q)
    q_maps = [jnp.where(lane < HEAD_DIM, q, zero), jnp.where(lane >= HEAD_DIM, q, zero)]

    kc, vct = kc_ref[...], vct_ref[0]
    for i in range(2):
        s = lax.dot_general(kc, q_maps[i], NT_DIMS, preferred_element_type=F32)
        m = s.max(axis=0, keepdims=True)
        p = jnp.exp2((s - m).astype(BF16))
        m_sc[i] = m
        acc_sc[i] = jnp.dot(vct, p, preferred_element_type=F32)

    if n_kblocks:
        def scores(kb):
            k = k_ref[pl.ds(pl.multiple_of(kb * tk, tk), tk), :]
            return [lax.dot_general(k, q_maps[i], NT_DIMS, preferred_element_type=F32) for i in range(2)]

        def probs(kb, slot):
            s = scores(kb)
            excess = None
            for i in range(2):
                m_ref = m_sc[i]
                p_sc[slot, i] = jnp.exp2((s[i] - m_ref).astype(BF16))
                over = jnp.max(s[i].max(axis=0, keepdims=True) - m_ref)
                excess = over if excess is None else jnp.maximum(excess, over)
            return excess

        def settle(kb, slot, excess):
            @pl.when(excess > RESCALE_MARGIN)
            def _():
                s = scores(kb)
                for i in range(2):
                    m_prev = m_sc[i]
                    m_new = jnp.maximum(m_prev, s[i].max(axis=0, keepdims=True))
                    p_sc[slot, i] = jnp.exp2((s[i] - m_new).astype(BF16))
                    acc_sc[i] = jnp.exp2(m_prev - m_new) * acc_sc[i]
                    m_sc[i] = m_new

        def accumulate(kb, slot):
            vt = vt_ref[0, :, pl.ds(pl.multiple_of(kb * tk, tk), tk)]
            for i in range(2):
                acc_sc[i] += jnp.dot(vt, p_sc[slot, i], preferred_element_type=F32)

        def probs_unchecked(kb, slot):
            s = scores(kb)
            for i in range(2):
                p_sc[slot, i] = jnp.exp2((s[i] - m_sc[i]).astype(BF16))

        lowest_ref = jnp.minimum(jnp.min(m_sc[0]), jnp.min(m_sc[1]))
        never_rescales = bound_ref[0, 0] - lowest_ref <= RESCALE_MARGIN

        @pl.when(never_rescales)
        def _():
            probs_unchecked(0, 0)

            def body(kb, carry):
                accumulate(kb - 1, (kb - 1) % 2)
                probs_unchecked(kb, kb % 2)
                return carry

            lax.fori_loop(1, n_kblocks, body, 0)
            accumulate(n_kblocks - 1, (n_kblocks - 1) % 2)

        @pl.when(jnp.logical_not(never_rescales))
        def _():
            def body(kb, excess):
                settle(kb - 1, (kb - 1) % 2, excess)
                accumulate(kb - 1, (kb - 1) % 2)
                return probs(kb, kb % 2)

            excess = lax.fori_loop(1, n_kblocks, body, probs(0, 0))
            settle(n_kblocks - 1, (n_kblocks - 1) % 2, excess)
            accumulate(n_kblocks - 1, (n_kblocks - 1) % 2)

    lp = lam_ref[...]
    lam = (jnp.exp(jnp.sum(lp[0:1] * lp[1:2], axis=-1, keepdims=True))
           - jnp.exp(jnp.sum(lp[2:3] * lp[3:4], axis=-1, keepdims=True)) + lam_init)
    dv = 2 * HEAD_DIM
    y0 = acc_sc[0, :dv] / acc_sc[0, dv:dv + 1]
    y1 = acc_sc[1, :dv] / acc_sc[1, dv:dv + 1]
    y = (y0 - lam * y1).T
    ms = jnp.mean(y * y, axis=-1, keepdims=True)
    y = y * lax.rsqrt(ms + EPS) * subg_ref[...] * (1.0 - lam_init)
    o_ref[...] = y.astype(BF16)


def _values_t(p, col0, col1, n_heads):
    t = p.shape[0]
    vt = p[:, col0:col1].T.reshape(n_heads, (col1 - col0) // n_heads, t)
    return jnp.concatenate([vt, jnp.ones((n_heads, BF16_SUBLANES, t), BF16)], axis=1)


def _diff_attention(p_q, p_lat, vt_lat, p_ctx, vt_ctx, lam_rows, subg, score_bound, nb, n_q, n_lat, ctx_len,
                    q_col, k_col, n_heads, lam_init, tq, tk):
    qsteps = n_q // tq
    has_latent = p_lat is not None
    vrows = 2 * HEAD_DIM + BF16_SUBLANES
    in_specs = [pl.BlockSpec((tq, LANES), lambda b, h, i: (b * qsteps + i, q_col + h))]
    args = [p_q]
    if has_latent:
        in_specs += [pl.BlockSpec((n_lat, LANES), lambda b, h, i: (b, k_col + h)),
                     pl.BlockSpec((1, vrows, n_lat), lambda b, h, i: (h, 0, b))]
        args += [p_lat, vt_lat]
    in_specs += [pl.BlockSpec((ctx_len, LANES), lambda b, h, i: (b, k_col + h)),
                 pl.BlockSpec((1, vrows, ctx_len), lambda b, h, i: (h, 0, b)),
                 _resident(lam_rows.shape, lambda b, h, i: (0, 0)),
                 _resident(subg.shape, lambda b, h, i: (0, 0)),
                 pl.BlockSpec(memory_space=pltpu.SMEM)]
    args += [p_ctx, vt_ctx, lam_rows, subg, score_bound]
    return pl.pallas_call(
        functools.partial(_diff_kernel, n_kblocks=n_lat // tk if has_latent else 0, tk=tk, lam_init=lam_init),
        grid=(nb, n_heads, qsteps),
        in_specs=in_specs,
        out_specs=pl.BlockSpec((tq, LANES), lambda b, h, i: (b * qsteps + i, h)),
        out_shape=jax.ShapeDtypeStruct((nb * n_q, n_heads * LANES), BF16),
        scratch_shapes=([pltpu.VMEM((2, 1, tq), F32), pltpu.VMEM((2, vrows, tq), F32)]
                        + ([pltpu.VMEM((2, 2, tk, tq), BF16)] if has_latent else [])),
        compiler_params=_params(("parallel", "parallel", "arbitrary")),
        name="diff_attention",
    )(*args)


def _silu(x):
    return x * (1.0 / (1.0 + jnp.exp(-x)))


def _ffn_kernel(x_ref, g_ref, sc_ref, sh_ref, gate_ref, w1_ref, w3_ref, w2_ref, o_ref):
    x = x_ref[...]
    a = _modulated_norm(x, g_ref[...], sc_ref[0], sh_ref[0]).astype(BF16)
    h1 = jnp.dot(a, w1_ref[...], preferred_element_type=F32)
    h3 = jnp.dot(a, w3_ref[...], preferred_element_type=F32)
    y = jnp.dot((_silu(h1) * h3).astype(BF16), w2_ref[...], preferred_element_type=F32)
    o_ref[...] = x + gate_ref[0] * y


def _ffn(h, nb, g, sc, sh, gate, w1, w3, w2, tm):
    t, d = h.shape
    steps = t // nb // tm
    row = lambda b, i: (b * steps + i, 0)
    vec = pl.BlockSpec((1, 1, d), lambda b, i: (b, 0, 0))
    return pl.pallas_call(
        _ffn_kernel,
        grid=(nb, steps),
        in_specs=[pl.BlockSpec((tm, d), row), _resident((1, d), lambda b, i: (0, 0)), vec, vec, vec,
                  _resident(w1.shape, lambda b, i: (0, 0)),
                  _resident(w3.shape, lambda b, i: (0, 0)),
                  _resident(w2.shape, lambda b, i: (0, 0))],
        out_specs=pl.BlockSpec((tm, d), row),
        out_shape=jax.ShapeDtypeStruct((t, d), F32),
        compiler_params=_params(("parallel", "parallel")),
        name="ffn",
    )(h, g, sc, sh, gate, w1, w3, w2)


def _top2_gates(logits):
    lane = _lane_iota()
    big = jnp.int32(LANES)
    lg = jnp.where(lane < N_EXPERTS, logits, -jnp.inf)
    m1 = lg.max(axis=-1, keepdims=True)
    i1 = jnp.where(lg == m1, lane, big).min(axis=-1, keepdims=True)
    rest = jnp.where(lane == i1, -jnp.inf, lg)
    m2 = rest.max(axis=-1, keepdims=True)
    i2 = jnp.where(rest == m2, lane, big).min(axis=-1, keepdims=True)
    e2 = jnp.exp(m2 - m1)
    den = 1.0 + e2
    return jnp.where(lane == i1, 1.0 / den, 0.0) + jnp.where(lane == i2, e2 / den, 0.0)


def _moe_kernel(x_ref, g_ref, sc_ref, sh_ref, gate_ref, r_ref, w1_ref, w3_ref, w2_ref, o_ref,
                a_sc, gates_sc, rank_sc, acc_sc, *, chunk):
    e = pl.program_id(2)
    tm = x_ref.shape[0]

    @pl.when(e == 0)
    def _():
        a = _modulated_norm(x_ref[...], g_ref[...], sc_ref[0], sh_ref[0])
        a_hi = a.astype(BF16)
        a_lo = (a - a_hi.astype(F32)).astype(BF16)
        r = r_ref[...]
        r_hi = r.astype(BF16)
        r_lo = (r - r_hi.astype(F32)).astype(BF16)
        logits = (jnp.dot(a_hi, r_hi, preferred_element_type=F32)
                  + jnp.dot(a_hi, r_lo, preferred_element_type=F32)
                  + jnp.dot(a_lo, r_hi, preferred_element_type=F32))
        a_sc[...] = a_hi
        gates_t = _top2_gates(logits).T[:EXPERT_ROWS]
        gates_sc[...] = gates_t
        before = (lax.broadcasted_iota(jnp.int32, (tm, tm), 0)
                  < lax.broadcasted_iota(jnp.int32, (tm, tm), 1))
        routed = jnp.where(gates_t > 0.0, 1.0, 0.0).astype(BF16)
        rank_sc[...] = jnp.dot(routed, jnp.where(before, 1.0, 0.0).astype(BF16), preferred_element_type=F32)
        acc_sc[...] = jnp.zeros_like(acc_sc)

    gate_row = gates_sc[pl.ds(e, 1), :]
    rank_row = jnp.where(gate_row > 0.0, rank_sc[pl.ds(e, 1), :], -1.0)
    count = jnp.sum(jnp.where(gate_row > 0.0, 1.0, 0.0)).astype(jnp.int32)
    slot = lax.broadcasted_iota(jnp.int32, (chunk, 1), 0).astype(F32)

    for j in range(tm // chunk):
        @pl.when(count > j * chunk)
        def _():
            pick = rank_row == (slot + float(j * chunk))
            pick_f = jnp.where(pick, 1.0, 0.0)
            xs = jnp.dot(pick_f.astype(BF16), a_sc[...], preferred_element_type=F32).astype(BF16)
            h1 = jnp.dot(xs, w1_ref[0], preferred_element_type=F32)
            h3 = jnp.dot(xs, w3_ref[0], preferred_element_type=F32)
            y = jnp.dot((_silu(h1) * h3).astype(BF16), w2_ref[0], preferred_element_type=F32)
            y = y * jnp.sum(pick_f * gate_row, axis=-1, keepdims=True)
            acc_sc[...] += lax.dot_general(pick_f.astype(BF16), y.astype(BF16), (((0,), (0,)), ((), ())),
                                           preferred_element_type=F32)

    @pl.when(e == pl.num_programs(2) - 1)
    def _():
        o_ref[...] = x_ref[...] + gate_ref[0] * acc_sc[...]


def _moe(h, nb, g, sc, sh, gate, router, w1, w3, w2, tm, chunk):
    t, d = h.shape
    steps = t // nb // tm
    n_e, _, f = w1.shape
    row = lambda b, i, e: (b * steps + i, 0)
    vec = pl.BlockSpec((1, 1, d), lambda b, i, e: (b, 0, 0))
    return pl.pallas_call(
        functools.partial(_moe_kernel, chunk=chunk),
        grid=(nb, steps, n_e),
        in_specs=[pl.BlockSpec((tm, d), row), _resident((1, d), lambda b, i, e: (0, 0)), vec, vec, vec,
                  _resident(router.shape, lambda b, i, e: (0, 0)),
                  pl.BlockSpec((1, d, f), lambda b, i, e: (e, 0, 0)),
                  pl.BlockSpec((1, d, f), lambda b, i, e: (e, 0, 0)),
                  pl.BlockSpec((1, f, d), lambda b, i, e: (e, 0, 0))],
        out_specs=pl.BlockSpec((tm, d), row),
        out_shape=jax.ShapeDtypeStruct((t, d), F32),
        scratch_shapes=[pltpu.VMEM((tm, d), BF16), pltpu.VMEM((EXPERT_ROWS, tm), F32),
                        pltpu.VMEM((EXPERT_ROWS, tm), F32), pltpu.VMEM((tm, d), F32)],
        compiler_params=_params(("parallel", "parallel", "arbitrary")),
        name="moe",
    )(h, g, sc, sh, gate, router, w1, w3, w2)


def _rope_tables(n):
    t = jnp.arange(n)
    pos = jnp.stack([t // GRID_W, t % GRID_W], -1).astype(F32)
    nq = HEAD_DIM // 4
    inv = ROPE_THETA ** (-jnp.arange(nq, dtype=F32) / nq)
    ang = pos[:, :, None] * inv
    cos = jnp.repeat(jnp.cos(ang)[:, :, None, :], 2, axis=2)
    sin = jnp.stack([-jnp.sin(ang), jnp.sin(ang)], axis=2)
    cos = jnp.tile(cos.reshape(n, HEAD_DIM), (1, LANES // HEAD_DIM))
    sin = jnp.tile(sin.reshape(n, HEAD_DIM), (1, LANES // HEAD_DIM))
    return cos, sin


def _head_gain(parts, n_out):
    row = jnp.ones((n_out,), F32)
    for col, n_heads, gain, scale in parts:
        row = lax.dynamic_update_slice(row, jnp.tile(gain.astype(F32) * scale, n_heads), (col,))
    return row.reshape(1, n_out)


def kernel(x, c, ctx, c_ctx, ada_w, ada_b, norm1_g, norm2_g, ev_w_in, ev_conv_w, ev_q_g, ev_k_g, ev_rpb,
           ev_w_out, ffn_w1, ffn_w3, ffn_w2, od_w_in, od_cq_g, od_ck_g, od_sink, od_dq_g, od_dk_g,
           od_lam_q1, od_lam_k1, od_lam_q2, od_lam_k2, od_subln_g, od_w_out, moe_router,
           moe_w1, moe_w3, moe_w2):
    nb, n, d = x.shape
    ctx_len = ctx.shape[1]
    depth = ada_w.shape[0]
    n_slots = d // HEAD_DIM
    conv_ch = d // 2
    na_heads = swa_heads = n_slots // 2
    swa_kv = max(1, swa_heads // 4)
    diff_heads = n_slots // 4
    rows = n // GRID_W
    assert rows % NA_QROWS == 0 and rows >= 2 * NA_QROWS and n % 512 == 0
    assert nb + 1 <= 8 and ctx_len % LANES == 0

    ev_q_col = 3 * conv_ch
    ev_k_col = ev_q_col + na_heads * HEAD_DIM
    ev_v_col = ev_k_col + na_heads * HEAD_DIM
    ev_n = ev_v_col + na_heads * HEAD_DIM
    od_dq_col = swa_heads * HEAD_DIM
    od_ck_col = od_dq_col + diff_heads * 2 * HEAD_DIM
    od_cv_col = od_ck_col + swa_kv * HEAD_DIM
    od_dk_col = od_cv_col + swa_kv * HEAD_DIM
    od_dv_col = od_dk_col + diff_heads * 2 * HEAD_DIM
    od_n = od_dv_col + diff_heads * 2 * HEAD_DIM
    ev_segs = ((0, ev_q_col, False), (ev_q_col, ev_v_col, True), (ev_v_col, ev_n, False))
    od_segs = ((0, od_cv_col, True), (od_cv_col, od_dk_col, False), (od_dk_col, od_dv_col, True),
               (od_dv_col, od_n, False))

    h = x.reshape(nb * n, d)
    hc = ctx.reshape(nb * ctx_len, d)
    tm_lat = 512
    tm_ctx = ctx_len
    tm_moe = 1024

    s_rows = jnp.zeros((8, d), F32).at[:nb].set(c).at[nb].set(c_ctx)
    mod = _modulation(s_rows, ada_w, ada_b)
    rope_tabs = _rope_tables(n)

    def lat_vec(l, k):
        return mod[l, :nb, k * d:(k + 1) * d].reshape(nb, 1, d)

    def ctx_vec(l, k, copies):
        return jnp.broadcast_to(mod[l, nb, k * d:(k + 1) * d], (copies, 1, d))

    for l in range(depth):
        last = l == depth - 1
        i = l // 2
        g1 = norm1_g[l].reshape(1, d)
        g2 = norm2_g[l].reshape(1, d)
        if l % 2 == 0:
            w_in = ev_w_in[i].astype(BF16)
            gain = _head_gain([(ev_q_col, na_heads, ev_q_g[i], QK_SCALE * LOG2_E), (ev_k_col, na_heads, ev_k_g[i], 1.0)],
                              ev_n)
            p_lat = _norm_proj(h, nb, g1, lat_vec(l, 1), lat_vec(l, 0), w_in, gain, ev_segs, None, tm_lat)
            p_ctx = _norm_proj(hc, nb, g1, ctx_vec(l, 1, nb), ctx_vec(l, 0, nb), w_in, gain, ev_segs, None,
                               tm_ctx)
            bias = _na_bias(ev_rpb[i], rows)
            vt_lat, vt_ctx = _values_t(p_lat, ev_v_col, ev_n, na_heads), _values_t(p_ctx, ev_v_col, ev_n, na_heads)
            y_na = _na_attention(p_lat, vt_lat, p_ctx, vt_ctx, bias, nb, n, ctx_len, ev_q_col // LANES,
                                 ev_k_col // LANES, na_heads)
            w_out = ev_w_out[i].astype(BF16)
            conv_w = ev_conv_w[i]
            h = _out_proj(h, nb, lat_vec(l, 2), p_lat, y_na, w_out, conv_w, tm_lat)
            if not last:
                y_na_c = _gqa_attention(p_ctx, None, p_ctx, vt_ctx, None, nb, ctx_len, ctx_len, ev_q_col // LANES,
                                        ev_k_col // LANES, na_heads, na_heads, False, ctx_len)
                hc = _out_proj(hc, nb, ctx_vec(l, 2, nb), p_ctx, y_na_c, w_out, conv_w, tm_ctx)
            w1, w3, w2 = ffn_w1[i].astype(BF16), ffn_w3[i].astype(BF16), ffn_w2[i].astype(BF16)
            h = _ffn(h, nb, g2, lat_vec(l, 4), lat_vec(l, 3), lat_vec(l, 5), w1, w3, w2, tm_lat)
            if not last:
                hc = _ffn(hc, 1, g2, ctx_vec(l, 4, 1), ctx_vec(l, 3, 1), ctx_vec(l, 5, 1), w1, w3, w2, tm_lat)
        else:
            lam_init = 0.8 - 0.6 * math.exp(-0.3 * l)
            w_in = od_w_in[i].astype(BF16)
            gain = _head_gain([(0, swa_heads, od_cq_g[i], QK_SCALE * LOG2_E),
                               (od_dq_col, 2 * diff_heads, od_dq_g[i], QK_SCALE * LOG2_E),
                               (od_ck_col, swa_kv, od_ck_g[i], 1.0),
                               (od_dk_col, 2 * diff_heads, od_dk_g[i], 1.0)], od_n)
            p_lat = _norm_proj(h, nb, g1, lat_vec(l, 1), lat_vec(l, 0), w_in, gain, od_segs, rope_tabs, tm_lat)
            p_ctx = _norm_proj(hc, nb, g1, ctx_vec(l, 1, nb), ctx_vec(l, 0, nb), w_in, gain, od_segs, None,
                               tm_ctx)
            sink = jnp.broadcast_to(od_sink[i].astype(F32)[:, None] * LOG2_E, (swa_heads, LANES))
            lam_rows = jnp.zeros((8, LANES), F32).at[:4, :HEAD_DIM].set(
                jnp.stack([od_lam_q1[i], od_lam_k1[i], od_lam_q2[i], od_lam_k2[i]]).astype(F32))
            subg = od_subln_g[i].astype(F32).reshape(1, 2 * HEAD_DIM)
            cols = (od_dq_col // LANES, od_dk_col // LANES)
            score_bound = (HEAD_DIM * QK_SCALE * LOG2_E * BF16_NORM_SLACK * jnp.max(jnp.abs(od_dq_g[i]))
                           * jnp.max(jnp.abs(od_dk_g[i]))).astype(F32).reshape(1, 1)
            vt_lat = _values_t(p_lat, od_dv_col, od_n, diff_heads)
            vt_ctx = _values_t(p_ctx, od_dv_col, od_n, diff_heads)
            cvt_lat = _values_t(p_lat, od_cv_col, od_dk_col, swa_kv)
            cvt_ctx = _values_t(p_ctx, od_cv_col, od_dk_col, swa_kv)
            y_c = _gqa_attention(p_lat, cvt_lat, p_ctx, cvt_ctx, sink, nb, n, ctx_len, 0, od_ck_col // LANES,
                                 swa_heads, swa_kv, True, 2 * BLOCK)
            y_d = _diff_attention(p_lat, p_lat, vt_lat, p_ctx, vt_ctx, lam_rows, subg, score_bound, nb, n, n, ctx_len,
                                  *cols, diff_heads, lam_init, 512, 1024)
            w_out = od_w_out[i].astype(BF16)
            h = _out_proj(h, nb, lat_vec(l, 2), y_c, y_d, w_out, None, tm_lat)
            if not last:
                y_c_c = _gqa_attention(p_ctx, None, p_ctx, cvt_ctx, sink, nb, ctx_len, ctx_len, 0,
                                       od_ck_col // LANES, swa_heads, swa_kv, False, ctx_len)
                y_d_c = _diff_attention(p_ctx, None, None, p_ctx, vt_ctx, lam_rows, subg, score_bound, nb, ctx_len, 0,
                                        ctx_len, *cols, diff_heads, lam_init, ctx_len, ctx_len)
                hc = _out_proj(hc, nb, ctx_vec(l, 2, nb), y_c_c, y_d_c, w_out, None, tm_ctx)
            router = jnp.zeros((d, LANES), F32).at[:, :N_EXPERTS].set(moe_router[i])
            w1, w3, w2 = moe_w1[i].astype(BF16), moe_w3[i].astype(BF16), moe_w2[i].astype(BF16)
            h = _moe(h, nb, g2, lat_vec(l, 4), lat_vec(l, 3), lat_vec(l, 5), router, w1, w3, w2, tm_moe, MOE_CHUNK)
            if not last:
                hc = _moe(hc, 1, g2, ctx_vec(l, 4, 1), ctx_vec(l, 3, 1), ctx_vec(l, 5, 1), router, w1, w3, w2,
                          min(tm_moe, nb * ctx_len), MOE_CHUNK)
    return h.reshape(nb, n, d)
```

```python
import functools
import math

import jax
import jax.numpy as jnp
from jax import lax
from jax.experimental import pallas as pl
from jax.experimental.pallas import tpu as pltpu

F32 = jnp.float32
BF16 = jnp.bfloat16

LANES = 128
BF16_SUBLANES = 16
VMEM_LIMIT = 56 * 1024 * 1024

HEAD_DIM = 64
GRID_W = 64
CONV_W = 3
NA_ROWS = 8
NA_COLS = 16
NA_QROWS = 8
SWA_WINDOW = 128
MOE_CHUNK = 288
BLOCK = 128
N_EXPERTS = 8
EXPERT_ROWS = 16
ROPE_THETA = 10000.0
EPS = 1e-6
NEG = -1e30
QK_SCALE = HEAD_DIM ** -0.5
LOG2_E = math.log2(math.e)
BF16_NORM_SLACK = 1.02
RESCALE_MARGIN = 32.0

NT_DIMS = (((1,), (1,)), ((), ()))


def _params(sem):
    return pltpu.CompilerParams(dimension_semantics=sem, vmem_limit_bytes=VMEM_LIMIT)


def _resident(shape, index_map):
    return pl.BlockSpec(shape, index_map, pipeline_mode=pl.Buffered(1))


def _lane_iota():
    return lax.broadcasted_iota(jnp.int32, (1, LANES), 1)


def _swap_halves(x):
    return jnp.concatenate([x[:, HEAD_DIM:], x[:, :HEAD_DIM]], axis=1)


def _modulated_norm(x, g, sc, sh):
    ms = jnp.mean(x * x, axis=-1, keepdims=True)
    return (x * lax.rsqrt(ms + EPS)) * (g * (1.0 + sc)) + sh


def _mod_kernel(s_ref, w_ref, b_ref, o_ref):
    s = s_ref[...]
    s = s * (1.0 / (1.0 + jnp.exp(-s)))
    o_ref[0] = jnp.dot(s.astype(BF16), w_ref[0].astype(BF16), preferred_element_type=F32) + b_ref[0]


def _modulation(s_rows, ada_w, ada_b):
    depth, d, n_out = ada_w.shape
    tn = n_out // 4
    return pl.pallas_call(
        _mod_kernel,
        grid=(depth, n_out // tn),
        in_specs=[pl.BlockSpec(s_rows.shape, lambda l, j: (0, 0)),
                  pl.BlockSpec((1, d, tn), lambda l, j: (l, 0, j)),
                  pl.BlockSpec((1, 1, tn), lambda l, j: (l, 0, j))],
        out_specs=pl.BlockSpec((1, s_rows.shape[0], tn), lambda l, j: (l, 0, j)),
        out_shape=jax.ShapeDtypeStruct((depth, s_rows.shape[0], n_out), F32),
        compiler_params=_params(("arbitrary", "arbitrary")),
        name="modulation",
    )(s_rows, ada_w, ada_b.reshape(depth, 1, n_out))


def _head_sumsq(z):
    r = lax.broadcasted_iota(jnp.int32, (LANES, LANES), 0) // HEAD_DIM
    c = lax.broadcasted_iota(jnp.int32, (LANES, LANES), 1) // HEAD_DIM
    same_head = jnp.where(r == c, 1.0, 0.0).astype(BF16)
    z2 = z * z
    hi = z2.astype(BF16)
    lo = (z2 - hi.astype(F32)).astype(BF16)
    return (jnp.dot(hi, same_head, preferred_element_type=F32)
            + jnp.dot(lo, same_head, preferred_element_type=F32))


def _norm_proj_kernel(*refs, segs, rope, n_vt):
    refs = list(refs)
    vt_refs = [refs.pop() for _ in range(n_vt)][::-1]
    if rope:
        x_ref, g_ref, sc_ref, sh_ref, w_ref, gain_ref, cos_ref, sin_ref, o_ref = refs
    else:
        x_ref, g_ref, sc_ref, sh_ref, w_ref, gain_ref, o_ref = refs
    a = _modulated_norm(x_ref[...], g_ref[...], sc_ref[0], sh_ref[0]).astype(BF16)
    first_half = (_lane_iota() % (HEAD_DIM // 2)) < (HEAD_DIM // 4)
    for c0, c1, normed, vt_heads in segs:
        acc = jnp.dot(a, w_ref[:, c0:c1], preferred_element_type=F32)
        if vt_heads:
            vt_ref = vt_refs.pop(0)
            dv = (c1 - c0) // vt_heads
            vt = acc.T.astype(BF16)
            for hh in range(vt_heads):
                vt_ref[hh, :dv, :] = vt[hh * dv:(hh + 1) * dv]
                vt_ref[hh, dv:, :] = jnp.ones((BF16_SUBLANES, vt.shape[1]), BF16)
        if not normed:
            o_ref[:, c0:c1] = acc.astype(BF16)
            continue
        for j in range((c1 - c0) // LANES):
            z = acc[:, j * LANES:(j + 1) * LANES]
            lo = c0 + j * LANES
            z = z * lax.rsqrt(_head_sumsq(z) * (1.0 / HEAD_DIM) + EPS) * gain_ref[:, lo:lo + LANES]
            if rope:
                partner = jnp.where(first_half,
                                    pltpu.roll(z, LANES - HEAD_DIM // 4, axis=1),
                                    pltpu.roll(z, HEAD_DIM // 4, axis=1))
                z = z * cos_ref[...] + partner * sin_ref[...]
            o_ref[:, lo:lo + LANES] = z.astype(BF16)


def _norm_proj(h, nb, g, sc, sh, w, gain, segs, rope_tabs, tm):
    t, d = h.shape
    n_out = w.shape[1]
    steps = t // nb // tm
    in_specs = [pl.BlockSpec((tm, d), lambda b, i: (b * steps + i, 0)),
                _resident((1, d), lambda b, i: (0, 0)),
                pl.BlockSpec((1, 1, d), lambda b, i: (b, 0, 0)),
                pl.BlockSpec((1, 1, d), lambda b, i: (b, 0, 0)),
                _resident((d, n_out), lambda b, i: (0, 0)),
                _resident((1, n_out), lambda b, i: (0, 0))]
    args = [h, g, sc, sh, w, gain]
    if rope_tabs is not None:
        in_specs += [pl.BlockSpec((tm, LANES), lambda b, i: (i, 0))] * 2
        args += list(rope_tabs)
    out_specs = [pl.BlockSpec((tm, n_out), lambda b, i: (b * steps + i, 0))]
    out_shape = [jax.ShapeDtypeStruct((t, n_out), BF16)]
    for c0, c1, _, vt_heads in segs:
        if vt_heads:
            rows = (c1 - c0) // vt_heads + BF16_SUBLANES
            out_specs.append(pl.BlockSpec((vt_heads, rows, tm), lambda b, i: (0, 0, b * steps + i)))
            out_shape.append(jax.ShapeDtypeStruct((vt_heads, rows, t), BF16))
    return pl.pallas_call(
        functools.partial(_norm_proj_kernel, segs=segs, rope=rope_tabs is not None, n_vt=len(out_specs) - 1),
        grid=(nb, steps),
        in_specs=in_specs,
        out_specs=out_specs,
        out_shape=out_shape,
        compiler_params=_params(("parallel", "parallel")),
        name="norm_proj",
    )(*args)


def _gated_conv(gb_ref, gc_ref, u_ref, gcp_ref, up_ref, gcn_ref, un_ref, cw_ref):
    i, steps = pl.program_id(1), pl.num_programs(1)
    v = gc_ref[...].astype(F32) * u_ref[...].astype(F32)
    tm = v.shape[0]
    last = BF16_SUBLANES - 1
    prev_row = gcp_ref[last:last + 1, :].astype(F32) * up_ref[last:last + 1, :].astype(F32)
    next_row = gcn_ref[0:1, :].astype(F32) * un_ref[0:1, :].astype(F32)
    prev_row = jnp.where(i > 0, prev_row, 0.0)
    next_row = jnp.where(i < steps - 1, next_row, 0.0)
    row = lax.broadcasted_iota(jnp.int32, (tm, 1), 0)
    v_prev = jnp.where(row == 0, prev_row, pltpu.roll(v, 1, axis=0))
    v_next = jnp.where(row == tm - 1, next_row, pltpu.roll(v, tm - 1, axis=0))
    cw = cw_ref[...]
    conv = cw[0:1, :] * v_prev + cw[1:2, :] * v + cw[2:3, :] * v_next
    return gb_ref[...].astype(F32) * conv


def _out_proj_kernel(*refs, conv):
    if conv:
        (gb_ref, gc_ref, u_ref, gcp_ref, up_ref, gcn_ref, un_ref, cw_ref,
         yb_ref, wa_ref, wb_ref, h_ref, gate_ref, o_ref) = refs
        ya = _gated_conv(gb_ref, gc_ref, u_ref, gcp_ref, up_ref, gcn_ref, un_ref, cw_ref).astype(BF16)
    else:
        ya_ref, yb_ref, wa_ref, wb_ref, h_ref, gate_ref, o_ref = refs
        ya = ya_ref[...]
    y = (jnp.dot(ya, wa_ref[...], preferred_element_type=F32)
         + jnp.dot(yb_ref[...], wb_ref[...], preferred_element_type=F32))
    o_ref[...] = h_ref[...] + gate_ref[0] * y


def _out_proj(h, nb, gate, ya_src, yb, w_out, conv_w, tm):
    t, d = h.shape
    steps = t // nb // tm
    wa_rows = w_out.shape[0] - yb.shape[1]
    w_a, w_b = w_out[:wa_rows], w_out[wa_rows:]
    row = lambda b, i: (b * steps + i, 0)
    if conv_w is not None:
        cc = conv_w.shape[1]
        hb = tm // BF16_SUBLANES
        n_halo = t // BF16_SUBLANES
        prev = lambda col: (lambda b, i: (jnp.maximum((b * steps + i) * hb - 1, 0), col))
        nxt = lambda col: (lambda b, i: (jnp.minimum((b * steps + i + 1) * hb, n_halo - 1), col))
        in_specs = [pl.BlockSpec((tm, cc), lambda b, i: (b * steps + i, 0)),
                    pl.BlockSpec((tm, cc), lambda b, i: (b * steps + i, 1)),
                    pl.BlockSpec((tm, cc), lambda b, i: (b * steps + i, 2)),
                    pl.BlockSpec((BF16_SUBLANES, cc), prev(1)),
                    pl.BlockSpec((BF16_SUBLANES, cc), prev(2)),
                    pl.BlockSpec((BF16_SUBLANES, cc), nxt(1)),
                    pl.BlockSpec((BF16_SUBLANES, cc), nxt(2)),
                    _resident(conv_w.shape, lambda b, i: (0, 0))]
        args = [ya_src] * 7 + [conv_w]
    else:
        in_specs = [pl.BlockSpec((tm, wa_rows), row)]
        args = [ya_src]
    in_specs += [pl.BlockSpec((tm, yb.shape[1]), row),
                 _resident(w_a.shape, lambda b, i: (0, 0)),
                 _resident(w_b.shape, lambda b, i: (0, 0)),
                 pl.BlockSpec((tm, d), row),
                 pl.BlockSpec((1, 1, d), lambda b, i: (b, 0, 0))]
    args += [yb, w_a, w_b, h, gate]
    return pl.pallas_call(
        functools.partial(_out_proj_kernel, conv=conv_w is not None),
        grid=(nb, steps),
        in_specs=in_specs,
        out_specs=pl.BlockSpec((tm, d), row),
        out_shape=jax.ShapeDtypeStruct((t, d), F32),
        compiler_params=_params(("parallel", "parallel")),
        name="out_proj",
    )(*args)


def _na_bias(rpb, rows):
    nblk = rows // NA_QROWS
    c = jnp.arange(GRID_W)
    c0 = jnp.clip(c - NA_COLS // 2, 0, GRID_W - NA_COLS)
    col_ok = (c[None, :] >= c0[:, None]) & (c[None, :] < c0[:, None] + NA_COLS)
    dc = jnp.clip(c[None, :] - c[:, None] + NA_COLS - 1, 0, 2 * NA_COLS - 2)
    pick_dc = ((dc[None] == jnp.arange(2 * NA_COLS - 1)[:, None, None]) & col_ok[None]).astype(F32)
    table = jnp.einsum('hrd,dcx->hrcx', rpb.astype(F32), pick_dc, precision=lax.Precision.HIGHEST)
    i = jnp.arange(NA_QROWS)
    j = jnp.arange(2 * NA_QROWS)
    pick_dr, row_ok = [], []
    for rb in (0, 1, nblk - 1):
        r = rb * NA_QROWS + i
        r0 = jnp.clip(r - NA_ROWS // 2, 0, rows - NA_ROWS)
        rk = rb * NA_QROWS - NA_ROWS // 2 + j
        ok = (rk[None, :] >= r0[:, None]) & (rk[None, :] < r0[:, None] + NA_ROWS)
        dr = rk[None, :] - r[:, None] + NA_ROWS - 1
        pick_dr.append(((dr[:, :, None] == jnp.arange(2 * NA_ROWS - 1)) & ok[:, :, None]).astype(F32))
        row_ok.append(ok)
    bias = jnp.einsum('vijr,hrcx->vhjxic', jnp.stack(pick_dr), table, precision=lax.Precision.HIGHEST)
    ok = jnp.stack(row_ok).transpose(0, 2, 1)[:, None, :, None, :, None] & col_ok.T[None, None, None, :, None, :]
    bias = jnp.where(ok, bias * LOG2_E, NEG)
    return bias.reshape(3, rpb.shape[0], 2 * NA_QROWS * GRID_W, NA_QROWS * GRID_W)


def _softmax_pv_t(s_parts, vt_parts, extra=None):
    m = s_parts[0].max(axis=0, keepdims=True)
    for s in s_parts[1:]:
        m = jnp.maximum(m, s.max(axis=0, keepdims=True))
    if extra is not None:
        m = jnp.maximum(m, extra)
    acc = None
    for s, vt in zip(s_parts, vt_parts):
        pv = jnp.dot(vt, jnp.exp2((s - m).astype(BF16)), preferred_element_type=F32)
        acc = pv if acc is None else acc + pv
    den = acc[HEAD_DIM:HEAD_DIM + 1]
    if extra is not None:
        den = den + jnp.exp2(extra - m)
    return acc[:HEAD_DIM] / den


def _na_kernel(q_ref, k0, k1, k2, k3, vt0, vt1, vt2, vt3, kc_ref, vct_ref, bias_ref, o_ref):
    q = q_ref[...]
    k = jnp.concatenate([k0[...], k1[...], k2[...], k3[...]], axis=0)
    vt = jnp.concatenate([vt0[...], vt1[...], vt2[...], vt3[...]], axis=2)
    kc, vct = kc_ref[...], vct_ref[...]
    lane = _lane_iota()
    scores = []
    for hh in range(2):
        qh = jnp.where((lane // HEAD_DIM) == hh, q, jnp.zeros_like(q))
        scores.append([lax.dot_general(k, qh, NT_DIMS, preferred_element_type=F32) + bias_ref[0, hh],
                       lax.dot_general(kc, qh, NT_DIMS, preferred_element_type=F32)])
    outs = [_softmax_pv_t(scores[hh], [vt[hh], vct[hh]]) for hh in range(2)]
    o_ref[...] = jnp.concatenate(outs, axis=0).T.astype(BF16)


def _na_attention(p_lat, vt_lat, p_ctx, vt_ctx, bias, nb, n, ctx_len, q_col, k_col, n_heads):
    tq = NA_QROWS * GRID_W
    tw = tq // 2
    qsteps = n // tq
    wsteps = n // tw
    wblock = lambda rb, j: jnp.clip(2 * rb - 1 + j, 0, wsteps - 1)
    kwin = lambda j: pl.BlockSpec((tw, LANES), lambda hp, b, rb: (b * wsteps + wblock(rb, j), k_col + hp))
    vrows = HEAD_DIM + BF16_SUBLANES
    vwin = lambda j: pl.BlockSpec((2, vrows, tw), lambda hp, b, rb: (hp, 0, b * wsteps + wblock(rb, j)))
    variant = lambda rb: jnp.where(rb == 0, 0, jnp.where(rb == qsteps - 1, 2, 1))
    in_specs = ([pl.BlockSpec((tq, LANES), lambda hp, b, rb: (b * qsteps + rb, q_col + hp))]
                + [kwin(j) for j in range(4)] + [vwin(j) for j in range(4)]
                + [pl.BlockSpec((ctx_len, LANES), lambda hp, b, rb: (b, k_col + hp)),
                   pl.BlockSpec((2, vrows, ctx_len), lambda hp, b, rb: (hp, 0, b)),
                   pl.BlockSpec((1, 2, 2 * tq, tq), lambda hp, b, rb: (variant(rb), hp, 0, 0))])
    return pl.pallas_call(
        _na_kernel,
        grid=(n_heads // 2, nb, qsteps),
        in_specs=in_specs,
        out_specs=pl.BlockSpec((tq, LANES), lambda hp, b, rb: (b * qsteps + rb, hp)),
        out_shape=jax.ShapeDtypeStruct((nb * n, n_heads * HEAD_DIM), BF16),
        compiler_params=_params(("parallel", "parallel", "arbitrary")),
        name="na_attention",
    )(*([p_lat] * 5 + [vt_lat] * 4 + [p_ctx, vt_ctx, bias]))


def _gqa_kernel(*refs, n_q, n_kv, has_local, has_sink, n_tokens):
    refs = list(refs)
    q_ref = refs.pop(0)
    if has_local:
        k_loc = jnp.concatenate([refs.pop(0)[...] for _ in range(4)], axis=0)
        vt_loc = jnp.concatenate([refs.pop(0)[...] for _ in range(4)], axis=2)
    kc_ref, vct_ref = refs.pop(0), refs.pop(0)
    sink_ref = refs.pop(0) if has_sink else None
    o_ref = refs.pop(0)

    tq = q_ref.shape[0]
    group = n_q // n_kv
    lane = _lane_iota()
    if has_local:
        base = pl.program_id(1) * tq
        kpos = base - BLOCK + lax.broadcasted_iota(jnp.int32, (tq + 2 * BLOCK, 1), 0)
        qpos = base + lax.broadcasted_iota(jnp.int32, (1, tq), 1)
        ok = (jnp.abs(kpos - qpos) <= SWA_WINDOW) & (kpos >= 0) & (kpos < n_tokens)
        band = jnp.where(ok, 0.0, NEG)
        band = jnp.concatenate([band] * group, axis=1)

    outs = []
    for kvh in range(n_kv):
        cg, half = kvh // 2, kvh % 2
        qs = []
        for h in range(kvh * group, (kvh + 1) * group):
            qg = q_ref[:, (h // 2) * LANES:(h // 2 + 1) * LANES]
            if h % 2 != half:
                qg = _swap_halves(qg)
            qs.append(jnp.where((lane // HEAD_DIM) == half, qg, jnp.zeros_like(qg)))
        qcat = jnp.concatenate(qs, axis=0) if group > 1 else qs[0]
        kc = kc_ref[:, cg * LANES:(cg + 1) * LANES]
        s_parts = [lax.dot_general(kc, qcat, NT_DIMS, preferred_element_type=F32)]
        vt_parts = [vct_ref[kvh]]
        if has_local:
            s_parts.append(lax.dot_general(k_loc, qcat, NT_DIMS, preferred_element_type=F32) + band)
            vt_parts.append(vt_loc[kvh])
        extra = None
        if has_sink:
            extra = jnp.concatenate([sink_ref[h:h + 1, :] for h in range(kvh * group, (kvh + 1) * group)
                                     for _ in range(tq // LANES)], axis=1)
        o = _softmax_pv_t(s_parts, vt_parts, extra)
        outs += [o[:, g * tq:(g + 1) * tq] for g in range(group)]
    o_ref[...] = jnp.concatenate(outs, axis=0).T.astype(BF16)


def _gqa_attention(p_q, vt_q, p_ctx, vt_ctx, sink, nb, n, ctx_len, q_col, k_col, n_q, n_kv, has_local, tq):
    qsteps = n // tq
    kv_w = n_kv * HEAD_DIM
    q_w = n_q * HEAD_DIM
    vrows = HEAD_DIM + BF16_SUBLANES
    in_specs = [pl.BlockSpec((tq, q_w), lambda b, i: (b * qsteps + i, q_col * LANES // q_w))]
    args = [p_q]
    if has_local:
        assert tq == 2 * BLOCK and kv_w == LANES
        wsteps = n // BLOCK
        wblock = lambda i, j: jnp.clip(2 * i - 1 + j, 0, wsteps - 1)
        in_specs += [pl.BlockSpec((BLOCK, LANES), functools.partial(
            lambda b, i, j: (b * wsteps + wblock(i, j), k_col), j=j)) for j in range(4)]
        in_specs += [pl.BlockSpec((n_kv, vrows, BLOCK), functools.partial(
            lambda b, i, j: (0, 0, b * wsteps + wblock(i, j)), j=j)) for j in range(4)]
        args += [p_q] * 4 + [vt_q] * 4
    in_specs += [pl.BlockSpec((ctx_len, kv_w), lambda b, i: (b, k_col * LANES // kv_w)),
                 pl.BlockSpec((n_kv, vrows, ctx_len), lambda b, i: (0, 0, b))]
    args += [p_ctx, vt_ctx]
    if sink is not None:
        in_specs.append(_resident(sink.shape, lambda b, i: (0, 0)))
        args.append(sink)
    return pl.pallas_call(
        functools.partial(_gqa_kernel, n_q=n_q, n_kv=n_kv, has_local=has_local,
                          has_sink=sink is not None, n_tokens=n),
        grid=(nb, qsteps),
        in_specs=in_specs,
        out_specs=pl.BlockSpec((tq, q_w), lambda b, i: (b * qsteps + i, 0)),
        out_shape=jax.ShapeDtypeStruct((nb * n, q_w), BF16),
        compiler_params=_params(("parallel", "parallel")),
        name="gqa_attention",
    )(*args)


def _diff_kernel(*refs, n_kblocks, tk, lam_init):
    if n_kblocks:
        q_ref, k_ref, vt_ref, kc_ref, vct_ref, lam_ref, subg_ref, bound_ref, o_ref, m_sc, acc_sc, p_sc = refs
    else:
        q_ref, kc_ref, vct_ref, lam_ref, subg_ref, bound_ref, o_ref, m_sc, acc_sc = refs
    lane = _lane_iota()
    q = q_ref[...]
    zero = jnp.zeros_like(q)
    q_maps = [jnp.where(lane < HEAD_DIM, q, zero), jnp.where(lane >= HEAD_DIM, q, zero)]

    kc, vct = kc_ref[...], vct_ref[0]
    for i in range(2):
        s = lax.dot_general(kc, q_maps[i], NT_DIMS, preferred_element_type=F32)
        m = s.max(axis=0, keepdims=True)
        p = jnp.exp2((s - m).astype(BF16))
        m_sc[i] = m
        acc_sc[i] = jnp.dot(vct, p, preferred_element_type=F32)

    if n_kblocks:
        def scores(kb):
            k = k_ref[pl.ds(pl.multiple_of(kb * tk, tk), tk), :]
            return [lax.dot_general(k, q_maps[i], NT_DIMS, preferred_element_type=F32) for i in range(2)]

        def probs(kb, slot):
            s = scores(kb)
            excess = None
            for i in range(2):
                m_ref = m_sc[i]
                p_sc[slot, i] = jnp.exp2((s[i] - m_ref).astype(BF16))
                over = jnp.max(s[i].max(axis=0, keepdims=True) - m_ref)
                excess = over if excess is None else jnp.maximum(excess, over)
            return excess

        def settle(kb, slot, excess):
            @pl.when(excess > RESCALE_MARGIN)
            def _():
                s = scores(kb)
                for i in range(2):
                    m_prev = m_sc[i]
                    m_new = jnp.maximum(m_prev, s[i].max(axis=0, keepdims=True))
                    p_sc[slot, i] = jnp.exp2((s[i] - m_new).astype(BF16))
                    acc_sc[i] = jnp.exp2(m_prev - m_new) * acc_sc[i]
                    m_sc[i] = m_new

        def accumulate(kb, slot):
            vt = vt_ref[0, :, pl.ds(pl.multiple_of(kb * tk, tk), tk)]
            for i in range(2):
                acc_sc[i] += jnp.dot(vt, p_sc[slot, i], preferred_element_type=F32)

        def probs_unchecked(kb, slot):
            s = scores(kb)
            for i in range(2):
                p_sc[slot, i] = jnp.exp2((s[i] - m_sc[i]).astype(BF16))

        lowest_ref = jnp.minimum(jnp.min(m_sc[0]), jnp.min(m_sc[1]))
        never_rescales = bound_ref[0, 0] - lowest_ref <= RESCALE_MARGIN

        @pl.when(never_rescales)
        def _():
            probs_unchecked(0, 0)

            def body(kb, carry):
                accumulate(kb - 1, (kb - 1) % 2)
                probs_unchecked(kb, kb % 2)
                return carry

            lax.fori_loop(1, n_kblocks, body, 0)
            accumulate(n_kblocks - 1, (n_kblocks - 1) % 2)

        @pl.when(jnp.logical_not(never_rescales))
        def _():
            def body(kb, excess):
                settle(kb - 1, (kb - 1) % 2, excess)
                accumulate(kb - 1, (kb - 1) % 2)
                return probs(kb, kb % 2)

            excess = lax.fori_loop(1, n_kblocks, body, probs(0, 0))
            settle(n_kblocks - 1, (n_kblocks - 1) % 2, excess)
            accumulate(n_kblocks - 1, (n_kblocks - 1) % 2)

    lp = lam_ref[...]
    lam = (jnp.exp(jnp.sum(lp[0:1] * lp[1:2], axis=-1, keepdims=True))
           - jnp.exp(jnp.sum(lp[2:3] * lp[3:4], axis=-1, keepdims=True)) + lam_init)
    dv = 2 * HEAD_DIM
    y0 = acc_sc[0, :dv] / acc_sc[0, dv:dv + 1]
    y1 = acc_sc[1, :dv] / acc_sc[1, dv:dv + 1]
    y = (y0 - lam * y1).T
    ms = jnp.mean(y * y, axis=-1, keepdims=True)
    y = y * lax.rsqrt(ms + EPS) * subg_ref[...] * (1.0 - lam_init)
    o_ref[...] = y.astype(BF16)


def _diff_attention(p_q, p_lat, vt_lat, p_ctx, vt_ctx, lam_rows, subg, score_bound, nb, n_q, n_lat, ctx_len,
                    q_col, k_col, n_heads, lam_init, tq, tk):
    qsteps = n_q // tq
    has_latent = p_lat is not None
    vrows = 2 * HEAD_DIM + BF16_SUBLANES
    in_specs = [pl.BlockSpec((tq, LANES), lambda b, h, i: (b * qsteps + i, q_col + h))]
    args = [p_q]
    if has_latent:
        in_specs += [pl.BlockSpec((n_lat, LANES), lambda b, h, i: (b, k_col + h)),
                     pl.BlockSpec((1, vrows, n_lat), lambda b, h, i: (h, 0, b))]
        args += [p_lat, vt_lat]
    in_specs += [pl.BlockSpec((ctx_len, LANES), lambda b, h, i: (b, k_col + h)),
                 pl.BlockSpec((1, vrows, ctx_len), lambda b, h, i: (h, 0, b)),
                 _resident(lam_rows.shape, lambda b, h, i: (0, 0)),
                 _resident(subg.shape, lambda b, h, i: (0, 0)),
                 pl.BlockSpec(memory_space=pltpu.SMEM)]
    args += [p_ctx, vt_ctx, lam_rows, subg, score_bound]
    return pl.pallas_call(
        functools.partial(_diff_kernel, n_kblocks=n_lat // tk if has_latent else 0, tk=tk, lam_init=lam_init),
        grid=(nb, n_heads, qsteps),
        in_specs=in_specs,
        out_specs=pl.BlockSpec((tq, LANES), lambda b, h, i: (b * qsteps + i, h)),
        out_shape=jax.ShapeDtypeStruct((nb * n_q, n_heads * LANES), BF16),
        scratch_shapes=([pltpu.VMEM((2, 1, tq), F32), pltpu.VMEM((2, vrows, tq), F32)]
                        + ([pltpu.VMEM((2, 2, tk, tq), BF16)] if has_latent else [])),
        compiler_params=_params(("parallel", "parallel", "arbitrary")),
        name="diff_attention",
    )(*args)


def _silu(x):
    return x * (1.0 / (1.0 + jnp.exp(-x)))


def _ffn_kernel(x_ref, g_ref, sc_ref, sh_ref, gate_ref, w1_ref, w3_ref, w2_ref, o_ref):
    x = x_ref[...]
    a = _modulated_norm(x, g_ref[...], sc_ref[0], sh_ref[0]).astype(BF16)
    h1 = jnp.dot(a, w1_ref[...], preferred_element_type=F32)
    h3 = jnp.dot(a, w3_ref[...], preferred_element_type=F32)
    y = jnp.dot((_silu(h1) * h3).astype(BF16), w2_ref[...], preferred_element_type=F32)
    o_ref[...] = x + gate_ref[0] * y


def _ffn(h, nb, g, sc, sh, gate, w1, w3, w2, tm):
    t, d = h.shape
    steps = t // nb // tm
    row = lambda b, i: (b * steps + i, 0)
    vec = pl.BlockSpec((1, 1, d), lambda b, i: (b, 0, 0))
    return pl.pallas_call(
        _ffn_kernel,
        grid=(nb, steps),
        in_specs=[pl.BlockSpec((tm, d), row), _resident((1, d), lambda b, i: (0, 0)), vec, vec, vec,
                  _resident(w1.shape, lambda b, i: (0, 0)),
                  _resident(w3.shape, lambda b, i: (0, 0)),
                  _resident(w2.shape, lambda b, i: (0, 0))],
        out_specs=pl.BlockSpec((tm, d), row),
        out_shape=jax.ShapeDtypeStruct((t, d), F32),
        compiler_params=_params(("parallel", "parallel")),
        name="ffn",
    )(h, g, sc, sh, gate, w1, w3, w2)


def _top2_gates(logits):
    lane = _lane_iota()
    big = jnp.int32(LANES)
    lg = jnp.where(lane < N_EXPERTS, logits, -jnp.inf)
    m1 = lg.max(axis=-1, keepdims=True)
    i1 = jnp.where(lg == m1, lane, big).min(axis=-1, keepdims=True)
    rest = jnp.where(lane == i1, -jnp.inf, lg)
    m2 = rest.max(axis=-1, keepdims=True)
    i2 = jnp.where(rest == m2, lane, big).min(axis=-1, keepdims=True)
    e2 = jnp.exp(m2 - m1)
    den = 1.0 + e2
    return jnp.where(lane == i1, 1.0 / den, 0.0) + jnp.where(lane == i2, e2 / den, 0.0)


def _moe_kernel(x_ref, g_ref, sc_ref, sh_ref, gate_ref, r_ref, w1_ref, w3_ref, w2_ref, o_ref,
                a_sc, gates_sc, rank_sc, acc_sc, *, chunk):
    e = pl.program_id(2)
    tm = x_ref.shape[0]

    @pl.when(e == 0)
    def _():
        a = _modulated_norm(x_ref[...], g_ref[...], sc_ref[0], sh_ref[0])
        a_hi = a.astype(BF16)
        a_lo = (a - a_hi.astype(F32)).astype(BF16)
        r = r_ref[...]
        r_hi = r.astype(BF16)
        r_lo = (r - r_hi.astype(F32)).astype(BF16)
        logits = (jnp.dot(a_hi, r_hi, preferred_element_type=F32)
                  + jnp.dot(a_hi, r_lo, preferred_element_type=F32)
                  + jnp.dot(a_lo, r_hi, preferred_element_type=F32))
        a_sc[...] = a_hi
        gates_t = _top2_gates(logits).T[:EXPERT_ROWS]
        gates_sc[...] = gates_t
        before = (lax.broadcasted_iota(jnp.int32, (tm, tm), 0)
                  < lax.broadcasted_iota(jnp.int32, (tm, tm), 1))
        routed = jnp.where(gates_t > 0.0, 1.0, 0.0).astype(BF16)
        rank_sc[...] = jnp.dot(routed, jnp.where(before, 1.0, 0.0).astype(BF16), preferred_element_type=F32)
        acc_sc[...] = jnp.zeros_like(acc_sc)

    gate_row = gates_sc[pl.ds(e, 1), :]
    rank_row = jnp.where(gate_row > 0.0, rank_sc[pl.ds(e, 1), :], -1.0)
    count = jnp.sum(jnp.where(gate_row > 0.0, 1.0, 0.0)).astype(jnp.int32)
    slot = lax.broadcasted_iota(jnp.int32, (chunk, 1), 0).astype(F32)

    for j in range(pl.cdiv(tm, chunk)):
        @pl.when(count > j * chunk)
        def _():
            pick = rank_row == (slot + float(j * chunk))
            pick_f = jnp.where(pick, 1.0, 0.0)
            xs = jnp.dot(pick_f.astype(BF16), a_sc[...], preferred_element_type=F32).astype(BF16)
            h1 = jnp.dot(xs, w1_ref[0], preferred_element_type=F32)
            h3 = jnp.dot(xs, w3_ref[0], preferred_element_type=F32)
            y = jnp.dot((_silu(h1) * h3).astype(BF16), w2_ref[0], preferred_element_type=F32)
            y = y * jnp.sum(pick_f * gate_row, axis=-1, keepdims=True)
            acc_sc[...] += lax.dot_general(pick_f.astype(BF16), y.astype(BF16), (((0,), (0,)), ((), ())),
                                           preferred_element_type=F32)

    @pl.when(e == pl.num_programs(2) - 1)
    def _():
        o_ref[...] = x_ref[...] + gate_ref[0] * acc_sc[...]


def _moe(h, nb, g, sc, sh, gate, router, w1, w3, w2, tm, chunk):
    t, d = h.shape
    steps = t // nb // tm
    n_e, _, f = w1.shape
    row = lambda b, i, e: (b * steps + i, 0)
    vec = pl.BlockSpec((1, 1, d), lambda b, i, e: (b, 0, 0))
    return pl.pallas_call(
        functools.partial(_moe_kernel, chunk=chunk),
        grid=(nb, steps, n_e),
        in_specs=[pl.BlockSpec((tm, d), row), _resident((1, d), lambda b, i, e: (0, 0)), vec, vec, vec,
                  _resident(router.shape, lambda b, i, e: (0, 0)),
                  pl.BlockSpec((1, d, f), lambda b, i, e: (e, 0, 0)),
                  pl.BlockSpec((1, d, f), lambda b, i, e: (e, 0, 0)),
                  pl.BlockSpec((1, f, d), lambda b, i, e: (e, 0, 0))],
        out_specs=pl.BlockSpec((tm, d), row),
        out_shape=jax.ShapeDtypeStruct((t, d), F32),
        scratch_shapes=[pltpu.VMEM((tm, d), BF16), pltpu.VMEM((EXPERT_ROWS, tm), F32),
                        pltpu.VMEM((EXPERT_ROWS, tm), F32), pltpu.VMEM((tm, d), F32)],
        compiler_params=_params(("parallel", "parallel", "arbitrary")),
        name="moe",
    )(h, g, sc, sh, gate, router, w1, w3, w2)


def _rope_tables(n):
    t = jnp.arange(n)
    pos = jnp.stack([t // GRID_W, t % GRID_W], -1).astype(F32)
    nq = HEAD_DIM // 4
    inv = ROPE_THETA ** (-jnp.arange(nq, dtype=F32) / nq)
    ang = pos[:, :, None] * inv
    cos = jnp.repeat(jnp.cos(ang)[:, :, None, :], 2, axis=2)
    sin = jnp.stack([-jnp.sin(ang), jnp.sin(ang)], axis=2)
    cos = jnp.tile(cos.reshape(n, HEAD_DIM), (1, LANES // HEAD_DIM))
    sin = jnp.tile(sin.reshape(n, HEAD_DIM), (1, LANES // HEAD_DIM))
    return cos, sin


def _head_gain(parts, n_out):
    row = jnp.ones((n_out,), F32)
    for col, n_heads, gain, scale in parts:
        row = lax.dynamic_update_slice(row, jnp.tile(gain.astype(F32) * scale, n_heads), (col,))
    return row.reshape(1, n_out)


def kernel(x, c, ctx, c_ctx, ada_w, ada_b, norm1_g, norm2_g, ev_w_in, ev_conv_w, ev_q_g, ev_k_g, ev_rpb,
           ev_w_out, ffn_w1, ffn_w3, ffn_w2, od_w_in, od_cq_g, od_ck_g, od_sink, od_dq_g, od_dk_g,
           od_lam_q1, od_lam_k1, od_lam_q2, od_lam_k2, od_subln_g, od_w_out, moe_router,
           moe_w1, moe_w3, moe_w2):
    nb, n, d = x.shape
    ctx_len = ctx.shape[1]
    depth = ada_w.shape[0]
    n_slots = d // HEAD_DIM
    conv_ch = d // 2
    na_heads = swa_heads = n_slots // 2
    swa_kv = max(1, swa_heads // 4)
    diff_heads = n_slots // 4
    rows = n // GRID_W
    assert rows % NA_QROWS == 0 and rows >= 2 * NA_QROWS and n % 512 == 0
    assert nb + 1 <= 8 and ctx_len % LANES == 0

    ev_q_col = 3 * conv_ch
    ev_k_col = ev_q_col + na_heads * HEAD_DIM
    ev_v_col = ev_k_col + na_heads * HEAD_DIM
    ev_n = ev_v_col + na_heads * HEAD_DIM
    od_dq_col = swa_heads * HEAD_DIM
    od_ck_col = od_dq_col + diff_heads * 2 * HEAD_DIM
    od_cv_col = od_ck_col + swa_kv * HEAD_DIM
    od_dk_col = od_cv_col + swa_kv * HEAD_DIM
    od_dv_col = od_dk_col + diff_heads * 2 * HEAD_DIM
    od_n = od_dv_col + diff_heads * 2 * HEAD_DIM
    ev_segs = ((0, ev_q_col, False, 0), (ev_q_col, ev_v_col, True, 0), (ev_v_col, ev_n, False, na_heads))
    od_segs = ((0, od_cv_col, True, 0), (od_cv_col, od_dk_col, False, swa_kv), (od_dk_col, od_dv_col, True, 0),
               (od_dv_col, od_n, False, diff_heads))

    h = x.reshape(nb * n, d)
    hc = ctx.reshape(nb * ctx_len, d)
    tm_lat = 512
    tm_ctx = ctx_len
    tm_moe = 1024

    s_rows = jnp.zeros((8, d), F32).at[:nb].set(c).at[nb].set(c_ctx)
    mod = _modulation(s_rows, ada_w, ada_b)
    rope_tabs = _rope_tables(n)

    def lat_vec(l, k):
        return mod[l, :nb, k * d:(k + 1) * d].reshape(nb, 1, d)

    def ctx_vec(l, k, copies):
        return jnp.broadcast_to(mod[l, nb, k * d:(k + 1) * d], (copies, 1, d))

    for l in range(depth):
        last = l == depth - 1
        i = l // 2
        g1 = norm1_g[l].reshape(1, d)
        g2 = norm2_g[l].reshape(1, d)
        if l % 2 == 0:
            w_in = ev_w_in[i].astype(BF16)
            gain = _head_gain([(ev_q_col, na_heads, ev_q_g[i], QK_SCALE * LOG2_E), (ev_k_col, na_heads, ev_k_g[i], 1.0)],
                              ev_n)
            p_lat, vt_lat = _norm_proj(h, nb, g1, lat_vec(l, 1), lat_vec(l, 0), w_in, gain, ev_segs, None, tm_lat)
            p_ctx, vt_ctx = _norm_proj(hc, nb, g1, ctx_vec(l, 1, nb), ctx_vec(l, 0, nb), w_in, gain, ev_segs, None,
                                       tm_ctx)
            bias = _na_bias(ev_rpb[i], rows)
            y_na = _na_attention(p_lat, vt_lat, p_ctx, vt_ctx, bias, nb, n, ctx_len, ev_q_col // LANES,
                                 ev_k_col // LANES, na_heads)
            w_out = ev_w_out[i].astype(BF16)
            conv_w = ev_conv_w[i]
            h = _out_proj(h, nb, lat_vec(l, 2), p_lat, y_na, w_out, conv_w, tm_lat)
            if not last:
                y_na_c = _gqa_attention(p_ctx, None, p_ctx, vt_ctx, None, nb, ctx_len, ctx_len, ev_q_col // LANES,
                                        ev_k_col // LANES, na_heads, na_heads, False, ctx_len)
                hc = _out_proj(hc, nb, ctx_vec(l, 2, nb), p_ctx, y_na_c, w_out, conv_w, tm_ctx)
            w1, w3, w2 = ffn_w1[i].astype(BF16), ffn_w3[i].astype(BF16), ffn_w2[i].astype(BF16)
            h = _ffn(h, nb, g2, lat_vec(l, 4), lat_vec(l, 3), lat_vec(l, 5), w1, w3, w2, tm_lat)
            if not last:
                hc = _ffn(hc, 1, g2, ctx_vec(l, 4, 1), ctx_vec(l, 3, 1), ctx_vec(l, 5, 1), w1, w3, w2, tm_lat)
        else:
            lam_init = 0.8 - 0.6 * math.exp(-0.3 * l)
            w_in = od_w_in[i].astype(BF16)
            gain = _head_gain([(0, swa_heads, od_cq_g[i], QK_SCALE * LOG2_E),
                               (od_dq_col, 2 * diff_heads, od_dq_g[i], QK_SCALE * LOG2_E),
                               (od_ck_col, swa_kv, od_ck_g[i], 1.0),
                               (od_dk_col, 2 * diff_heads, od_dk_g[i], 1.0)], od_n)
            p_lat, cvt_lat, vt_lat = _norm_proj(h, nb, g1, lat_vec(l, 1), lat_vec(l, 0), w_in, gain, od_segs, rope_tabs,
                                                tm_lat)
            p_ctx, cvt_ctx, vt_ctx = _norm_proj(hc, nb, g1, ctx_vec(l, 1, nb), ctx_vec(l, 0, nb), w_in, gain, od_segs,
                                                None, tm_ctx)
            sink = jnp.broadcast_to(od_sink[i].astype(F32)[:, None] * LOG2_E, (swa_heads, LANES))
            lam_rows = jnp.zeros((8, LANES), F32).at[:4, :HEAD_DIM].set(
                jnp.stack([od_lam_q1[i], od_lam_k1[i], od_lam_q2[i], od_lam_k2[i]]).astype(F32))
            subg = od_subln_g[i].astype(F32).reshape(1, 2 * HEAD_DIM)
            cols = (od_dq_col // LANES, od_dk_col // LANES)
            score_bound = (HEAD_DIM * QK_SCALE * LOG2_E * BF16_NORM_SLACK * jnp.max(jnp.abs(od_dq_g[i]))
                           * jnp.max(jnp.abs(od_dk_g[i]))).astype(F32).reshape(1, 1)
            y_c = _gqa_attention(p_lat, cvt_lat, p_ctx, cvt_ctx, sink, nb, n, ctx_len, 0, od_ck_col // LANES,
                                 swa_heads, swa_kv, True, 2 * BLOCK)
            y_d = _diff_attention(p_lat, p_lat, vt_lat, p_ctx, vt_ctx, lam_rows, subg, score_bound, nb, n, n, ctx_len,
                                  *cols, diff_heads, lam_init, 512, 1024)
            w_out = od_w_out[i].astype(BF16)
            h = _out_proj(h, nb, lat_vec(l, 2), y_c, y_d, w_out, None, tm_lat)
            if not last:
                y_c_c = _gqa_attention(p_ctx, None, p_ctx, cvt_ctx, sink, nb, ctx_len, ctx_len, 0,
                                       od_ck_col // LANES, swa_heads, swa_kv, False, ctx_len)
                y_d_c = _diff_attention(p_ctx, None, None, p_ctx, vt_ctx, lam_rows, subg, score_bound, nb, ctx_len, 0,
                                        ctx_len, *cols, diff_heads, lam_init, ctx_len, ctx_len)
                hc = _out_proj(hc, nb, ctx_vec(l, 2, nb), y_c_c, y_d_c, w_out, None, tm_ctx)
            router = jnp.zeros((d, LANES), F32).at[:, :N_EXPERTS].set(moe_router[i])
            w1, w3, w2 = moe_w1[i].astype(BF16), moe_w3[i].astype(BF16), moe_w2[i].astype(BF16)
            h = _moe(h, nb, g2, lat_vec(l, 4), lat_vec(l, 3), lat_vec(l, 5), router, w1, w3, w2, tm_moe, MOE_CHUNK)
            if not last:
                hc = _moe(hc, 1, g2, ctx_vec(l, 4, 1), ctx_vec(l, 3, 1), ctx_vec(l, 5, 1), router, w1, w3, w2,
                          min(tm_moe, nb * ctx_len), MOE_CHUNK)
    return h.reshape(nb, n, d)
```

```python
import functools
import math

import jax
import jax.numpy as jnp
from jax import lax
from jax.experimental import pallas as pl
from jax.experimental.pallas import tpu as pltpu

F32 = jnp.float32
BF16 = jnp.bfloat16

LANES = 128
BF16_SUBLANES = 16
VMEM_LIMIT = 56 * 1024 * 1024

HEAD_DIM = 64
GRID_W = 64
CONV_W = 3
NA_ROWS = 8
NA_COLS = 16
NA_QROWS = 8
SWA_WINDOW = 128
MOE_CHUNK = 256
BLOCK = 128
N_EXPERTS = 8
EXPERT_ROWS = 16
ROPE_THETA = 10000.0
EPS = 1e-6
NEG = -1e30
QK_SCALE = HEAD_DIM ** -0.5
LOG2_E = math.log2(math.e)
BF16_NORM_SLACK = 1.02
RESCALE_MARGIN = 32.0

NT_DIMS = (((1,), (1,)), ((), ()))


def _params(sem):
    return pltpu.CompilerParams(dimension_semantics=sem, vmem_limit_bytes=VMEM_LIMIT)


def _resident(shape, index_map):
    return pl.BlockSpec(shape, index_map, pipeline_mode=pl.Buffered(1))


def _lane_iota():
    return lax.broadcasted_iota(jnp.int32, (1, LANES), 1)


def _swap_halves(x):
    return jnp.concatenate([x[:, HEAD_DIM:], x[:, :HEAD_DIM]], axis=1)


def _modulated_norm(x, g, sc, sh):
    ms = jnp.mean(x * x, axis=-1, keepdims=True)
    return (x * lax.rsqrt(ms + EPS)) * (g * (1.0 + sc)) + sh


def _mod_kernel(s_ref, w_ref, b_ref, o_ref):
    s = s_ref[...]
    s = s * (1.0 / (1.0 + jnp.exp(-s)))
    o_ref[0] = jnp.dot(s.astype(BF16), w_ref[0].astype(BF16), preferred_element_type=F32) + b_ref[0]


def _modulation(s_rows, ada_w, ada_b):
    depth, d, n_out = ada_w.shape
    tn = n_out // 4
    return pl.pallas_call(
        _mod_kernel,
        grid=(depth, n_out // tn),
        in_specs=[pl.BlockSpec(s_rows.shape, lambda l, j: (0, 0)),
                  pl.BlockSpec((1, d, tn), lambda l, j: (l, 0, j)),
                  pl.BlockSpec((1, 1, tn), lambda l, j: (l, 0, j))],
        out_specs=pl.BlockSpec((1, s_rows.shape[0], tn), lambda l, j: (l, 0, j)),
        out_shape=jax.ShapeDtypeStruct((depth, s_rows.shape[0], n_out), F32),
        compiler_params=_params(("arbitrary", "arbitrary")),
        name="modulation",
    )(s_rows, ada_w, ada_b.reshape(depth, 1, n_out))


def _head_sumsq(z):
    r = lax.broadcasted_iota(jnp.int32, (LANES, LANES), 0) // HEAD_DIM
    c = lax.broadcasted_iota(jnp.int32, (LANES, LANES), 1) // HEAD_DIM
    same_head = jnp.where(r == c, 1.0, 0.0).astype(BF16)
    z2 = z * z
    hi = z2.astype(BF16)
    lo = (z2 - hi.astype(F32)).astype(BF16)
    return (jnp.dot(hi, same_head, preferred_element_type=F32)
            + jnp.dot(lo, same_head, preferred_element_type=F32))


def _norm_proj_kernel(*refs, segs, rope, n_vt):
    refs = list(refs)
    vt_refs = [refs.pop() for _ in range(n_vt)][::-1]
    if rope:
        x_ref, g_ref, sc_ref, sh_ref, w_ref, gain_ref, cos_ref, sin_ref, o_ref = refs
    else:
        x_ref, g_ref, sc_ref, sh_ref, w_ref, gain_ref, o_ref = refs
    a = _modulated_norm(x_ref[...], g_ref[...], sc_ref[0], sh_ref[0]).astype(BF16)
    first_half = (_lane_iota() % (HEAD_DIM // 2)) < (HEAD_DIM // 4)
    for c0, c1, normed, vt_heads in segs:
        acc = jnp.dot(a, w_ref[:, c0:c1], preferred_element_type=F32)
        if vt_heads:
            vt_ref = vt_refs.pop(0)
            dv = (c1 - c0) // vt_heads
            vt = acc.T.astype(BF16)
            for hh in range(vt_heads):
                vt_ref[hh, :dv, :] = vt[hh * dv:(hh + 1) * dv]
                vt_ref[hh, dv:, :] = jnp.ones((BF16_SUBLANES, vt.shape[1]), BF16)
        if not normed:
            o_ref[:, c0:c1] = acc.astype(BF16)
            continue
        for j in range((c1 - c0) // LANES):
            z = acc[:, j * LANES:(j + 1) * LANES]
            lo = c0 + j * LANES
            z = z * lax.rsqrt(_head_sumsq(z) * (1.0 / HEAD_DIM) + EPS) * gain_ref[:, lo:lo + LANES]
            if rope:
                partner = jnp.where(first_half,
                                    pltpu.roll(z, LANES - HEAD_DIM // 4, axis=1),
                                    pltpu.roll(z, HEAD_DIM // 4, axis=1))
                z = z * cos_ref[...] + partner * sin_ref[...]
            o_ref[:, lo:lo + LANES] = z.astype(BF16)


def _norm_proj(h, nb, g, sc, sh, w, gain, segs, rope_tabs, tm):
    t, d = h.shape
    n_out = w.shape[1]
    steps = t // nb // tm
    in_specs = [pl.BlockSpec((tm, d), lambda b, i: (b * steps + i, 0)),
                _resident((1, d), lambda b, i: (0, 0)),
                pl.BlockSpec((1, 1, d), lambda b, i: (b, 0, 0)),
                pl.BlockSpec((1, 1, d), lambda b, i: (b, 0, 0)),
                _resident((d, n_out), lambda b, i: (0, 0)),
                _resident((1, n_out), lambda b, i: (0, 0))]
    args = [h, g, sc, sh, w, gain]
    if rope_tabs is not None:
        in_specs += [pl.BlockSpec((tm, LANES), lambda b, i: (i, 0))] * 2
        args += list(rope_tabs)
    out_specs = [pl.BlockSpec((tm, n_out), lambda b, i: (b * steps + i, 0))]
    out_shape = [jax.ShapeDtypeStruct((t, n_out), BF16)]
    for c0, c1, _, vt_heads in segs:
        if vt_heads:
            rows = (c1 - c0) // vt_heads + BF16_SUBLANES
            out_specs.append(pl.BlockSpec((vt_heads, rows, tm), lambda b, i: (0, 0, b * steps + i)))
            out_shape.append(jax.ShapeDtypeStruct((vt_heads, rows, t), BF16))
    return pl.pallas_call(
        functools.partial(_norm_proj_kernel, segs=segs, rope=rope_tabs is not None, n_vt=len(out_specs) - 1),
        grid=(nb, steps),
        in_specs=in_specs,
        out_specs=out_specs,
        out_shape=out_shape,
        compiler_params=_params(("parallel", "parallel")),
        name="norm_proj",
    )(*args)


def _gated_conv(gb_ref, gc_ref, u_ref, gcp_ref, up_ref, gcn_ref, un_ref, cw_ref):
    i, steps = pl.program_id(1), pl.num_programs(1)
    v = gc_ref[...].astype(F32) * u_ref[...].astype(F32)
    tm = v.shape[0]
    last = BF16_SUBLANES - 1
    prev_row = gcp_ref[last:last + 1, :].astype(F32) * up_ref[last:last + 1, :].astype(F32)
    next_row = gcn_ref[0:1, :].astype(F32) * un_ref[0:1, :].astype(F32)
    prev_row = jnp.where(i > 0, prev_row, 0.0)
    next_row = jnp.where(i < steps - 1, next_row, 0.0)
    row = lax.broadcasted_iota(jnp.int32, (tm, 1), 0)
    v_prev = jnp.where(row == 0, prev_row, pltpu.roll(v, 1, axis=0))
    v_next = jnp.where(row == tm - 1, next_row, pltpu.roll(v, tm - 1, axis=0))
    cw = cw_ref[...]
    conv = cw[0:1, :] * v_prev + cw[1:2, :] * v + cw[2:3, :] * v_next
    return gb_ref[...].astype(F32) * conv


def _out_proj_kernel(*refs, conv):
    if conv:
        (gb_ref, gc_ref, u_ref, gcp_ref, up_ref, gcn_ref, un_ref, cw_ref,
         yb_ref, wa_ref, wb_ref, h_ref, gate_ref, o_ref) = refs
        ya = _gated_conv(gb_ref, gc_ref, u_ref, gcp_ref, up_ref, gcn_ref, un_ref, cw_ref).astype(BF16)
    else:
        ya_ref, yb_ref, wa_ref, wb_ref, h_ref, gate_ref, o_ref = refs
        ya = ya_ref[...]
    y = (jnp.dot(ya, wa_ref[...], preferred_element_type=F32)
         + jnp.dot(yb_ref[...], wb_ref[...], preferred_element_type=F32))
    o_ref[...] = h_ref[...] + gate_ref[0] * y


def _out_proj(h, nb, gate, ya_src, yb, w_out, conv_w, tm):
    t, d = h.shape
    steps = t // nb // tm
    wa_rows = w_out.shape[0] - yb.shape[1]
    w_a, w_b = w_out[:wa_rows], w_out[wa_rows:]
    row = lambda b, i: (b * steps + i, 0)
    if conv_w is not None:
        cc = conv_w.shape[1]
        hb = tm // BF16_SUBLANES
        n_halo = t // BF16_SUBLANES
        prev = lambda col: (lambda b, i: (jnp.maximum((b * steps + i) * hb - 1, 0), col))
        nxt = lambda col: (lambda b, i: (jnp.minimum((b * steps + i + 1) * hb, n_halo - 1), col))
        in_specs = [pl.BlockSpec((tm, cc), lambda b, i: (b * steps + i, 0)),
                    pl.BlockSpec((tm, cc), lambda b, i: (b * steps + i, 1)),
                    pl.BlockSpec((tm, cc), lambda b, i: (b * steps + i, 2)),
                    pl.BlockSpec((BF16_SUBLANES, cc), prev(1)),
                    pl.BlockSpec((BF16_SUBLANES, cc), prev(2)),
                    pl.BlockSpec((BF16_SUBLANES, cc), nxt(1)),
                    pl.BlockSpec((BF16_SUBLANES, cc), nxt(2)),
                    _resident(conv_w.shape, lambda b, i: (0, 0))]
        args = [ya_src] * 7 + [conv_w]
    else:
        in_specs = [pl.BlockSpec((tm, wa_rows), row)]
        args = [ya_src]
    in_specs += [pl.BlockSpec((tm, yb.shape[1]), row),
                 _resident(w_a.shape, lambda b, i: (0, 0)),
                 _resident(w_b.shape, lambda b, i: (0, 0)),
                 pl.BlockSpec((tm, d), row),
                 pl.BlockSpec((1, 1, d), lambda b, i: (b, 0, 0))]
    args += [yb, w_a, w_b, h, gate]
    return pl.pallas_call(
        functools.partial(_out_proj_kernel, conv=conv_w is not None),
        grid=(nb, steps),
        in_specs=in_specs,
        out_specs=pl.BlockSpec((tm, d), row),
        out_shape=jax.ShapeDtypeStruct((t, d), F32),
        compiler_params=_params(("parallel", "parallel")),
        name="out_proj",
    )(*args)


def _na_bias_tables(rpb, rows):
    n_heads = rpb.shape[0]
    n_dr = 2 * NA_ROWS - 1
    c = jnp.arange(GRID_W)
    c0 = jnp.clip(c - NA_COLS // 2, 0, GRID_W - NA_COLS)
    col_ok = (c[None, :] >= c0[:, None]) & (c[None, :] < c0[:, None] + NA_COLS)
    dc = jnp.clip(c[None, :] - c[:, None] + NA_COLS - 1, 0, 2 * NA_COLS - 2)
    pick_dc = ((dc[None] == jnp.arange(2 * NA_COLS - 1)[:, None, None]) & col_ok[None]).astype(F32)
    table = jnp.einsum('hrd,dcx->hrxc', rpb.astype(F32), pick_dc, precision=lax.Precision.HIGHEST)
    table = jnp.where(col_ok.T[None, None], table * LOG2_E, NEG)

    def paired(tab):
        pad = jnp.full((n_heads, NA_ROWS // 2, GRID_W, GRID_W), NEG, F32)
        ext = jnp.concatenate([pad, tab, pad], axis=1)
        return jnp.concatenate([ext[:, 1:], ext[:, :-1]], axis=-1)

    dr = jnp.arange(n_dr)
    in_window = (dr >= NA_ROWS // 2 - 1) & (dr < NA_ROWS // 2 - 1 + NA_ROWS)
    interior = jnp.where(in_window[None, :, None, None], table, NEG)

    nblk = rows // NA_QROWS
    i = jnp.arange(NA_QROWS)
    j = jnp.arange(2 * NA_QROWS)
    masks = []
    for rb in (0, 1, nblk - 1):
        r = rb * NA_QROWS + i
        r0 = jnp.clip(r - NA_ROWS // 2, 0, rows - NA_ROWS)
        rk = rb * NA_QROWS - NA_ROWS // 2 + j
        ok = (rk[:, None] >= r0[None, :]) & (rk[:, None] < r0[None, :] + NA_ROWS)
        masks.append(jnp.repeat(jnp.where(ok, 0.0, NEG), GRID_W, axis=1))
    return paired(table), paired(interior), jnp.stack(masks)


def _na_bias_tile(tbl_ref, hh, mask_ref):
    rows = []
    for j in range(2 * NA_QROWS):
        blk = jnp.concatenate([tbl_ref[hh, j - 2 * ii + NA_QROWS - 2] for ii in range(NA_QROWS // 2)], axis=1)
        if mask_ref is not None:
            blk = blk + mask_ref[0, j:j + 1, :]
        rows.append(blk)
    return jnp.concatenate(rows, axis=0)


def _softmax_pv_t(s_parts, vt_parts, extra=None):
    m = s_parts[0].max(axis=0, keepdims=True)
    for s in s_parts[1:]:
        m = jnp.maximum(m, s.max(axis=0, keepdims=True))
    if extra is not None:
        m = jnp.maximum(m, extra)
    acc = None
    for s, vt in zip(s_parts, vt_parts):
        pv = jnp.dot(vt, jnp.exp2((s - m).astype(BF16)), preferred_element_type=F32)
        acc = pv if acc is None else acc + pv
    den = acc[HEAD_DIM:HEAD_DIM + 1]
    if extra is not None:
        den = den + jnp.exp2(extra - m)
    return acc[:HEAD_DIM] / den


def _na_kernel(q_ref, k0, k1, k2, k3, vt0, vt1, vt2, vt3, kc_ref, vct_ref, edge_tbl_ref, tbl_ref, mask_ref, o_ref):
    def attend(tbl, mask):
        q = q_ref[...]
        k = jnp.concatenate([k0[...], k1[...], k2[...], k3[...]], axis=0)
        vt = jnp.concatenate([vt0[...], vt1[...], vt2[...], vt3[...]], axis=2)
        kc, vct = kc_ref[...], vct_ref[...]
        lane = _lane_iota()
        scores = []
        for hh in range(2):
            qh = jnp.where((lane // HEAD_DIM) == hh, q, jnp.zeros_like(q))
            s_loc = lax.dot_general(k, qh, NT_DIMS, preferred_element_type=F32) + _na_bias_tile(tbl, hh, mask)
            scores.append([s_loc, lax.dot_general(kc, qh, NT_DIMS, preferred_element_type=F32)])
        outs = [_softmax_pv_t(scores[hh], [vt[hh], vct[hh]]) for hh in range(2)]
        o_ref[...] = jnp.concatenate(outs, axis=0).T.astype(BF16)

    rb = pl.program_id(2)
    is_edge = (rb == 0) | (rb == pl.num_programs(2) - 1)

    @pl.when(jnp.logical_not(is_edge))
    def _():
        attend(tbl_ref, None)

    @pl.when(is_edge)
    def _():
        attend(edge_tbl_ref, mask_ref)


def _na_attention(p_lat, vt_lat, p_ctx, vt_ctx, bias_tables, nb, n, ctx_len, q_col, k_col, n_heads):
    tq = NA_QROWS * GRID_W
    tw = tq // 2
    qsteps = n // tq
    wsteps = n // tw
    edge_tbl, tbl, masks = bias_tables
    wblock = lambda rb, j: jnp.clip(2 * rb - 1 + j, 0, wsteps - 1)
    kwin = lambda j: pl.BlockSpec((tw, LANES), lambda hp, b, rb: (b * wsteps + wblock(rb, j), k_col + hp))
    vrows = HEAD_DIM + BF16_SUBLANES
    vwin = lambda j: pl.BlockSpec((2, vrows, tw), lambda hp, b, rb: (hp, 0, b * wsteps + wblock(rb, j)))
    variant = lambda rb: jnp.where(rb == 0, 0, jnp.where(rb == qsteps - 1, 2, 1))
    tbl_spec = pl.BlockSpec((2,) + tbl.shape[1:], lambda hp, b, rb: (hp, 0, 0, 0))
    in_specs = ([pl.BlockSpec((tq, LANES), lambda hp, b, rb: (b * qsteps + rb, q_col + hp))]
                + [kwin(j) for j in range(4)] + [vwin(j) for j in range(4)]
                + [pl.BlockSpec((ctx_len, LANES), lambda hp, b, rb: (b, k_col + hp)),
                   pl.BlockSpec((2, vrows, ctx_len), lambda hp, b, rb: (hp, 0, b)),
                   tbl_spec, tbl_spec,
                   pl.BlockSpec((1,) + masks.shape[1:], lambda hp, b, rb: (variant(rb), 0, 0))])
    return pl.pallas_call(
        _na_kernel,
        grid=(n_heads // 2, nb, qsteps),
        in_specs=in_specs,
        out_specs=pl.BlockSpec((tq, LANES), lambda hp, b, rb: (b * qsteps + rb, hp)),
        out_shape=jax.ShapeDtypeStruct((nb * n, n_heads * HEAD_DIM), BF16),
        compiler_params=_params(("parallel", "parallel", "arbitrary")),
        name="na_attention",
    )(*([p_lat] * 5 + [vt_lat] * 4 + [p_ctx, vt_ctx, edge_tbl, tbl, masks]))


def _gqa_kernel(*refs, n_q, n_kv, has_local, has_sink, n_tokens):
    refs = list(refs)
    q_ref = refs.pop(0)
    if has_local:
        k_loc = jnp.concatenate([refs.pop(0)[...] for _ in range(4)], axis=0)
        vt_loc = jnp.concatenate([refs.pop(0)[...] for _ in range(4)], axis=2)
    kc_ref, vct_ref = refs.pop(0), refs.pop(0)
    sink_ref = refs.pop(0) if has_sink else None
    o_ref = refs.pop(0)

    tq = q_ref.shape[0]
    group = n_q // n_kv
    lane = _lane_iota()
    if has_local:
        base = pl.program_id(1) * tq
        kpos = base - BLOCK + lax.broadcasted_iota(jnp.int32, (tq + 2 * BLOCK, 1), 0)
        qpos = base + lax.broadcasted_iota(jnp.int32, (1, tq), 1)
        ok = (jnp.abs(kpos - qpos) <= SWA_WINDOW) & (kpos >= 0) & (kpos < n_tokens)
        band = jnp.where(ok, 0.0, NEG)
        band = jnp.concatenate([band] * group, axis=1)

    outs = []
    for kvh in range(n_kv):
        cg, half = kvh // 2, kvh % 2
        qs = []
        for h in range(kvh * group, (kvh + 1) * group):
            qg = q_ref[:, (h // 2) * LANES:(h // 2 + 1) * LANES]
            if h % 2 != half:
                qg = _swap_halves(qg)
            qs.append(jnp.where((lane // HEAD_DIM) == half, qg, jnp.zeros_like(qg)))
        qcat = jnp.concatenate(qs, axis=0) if group > 1 else qs[0]
        kc = kc_ref[:, cg * LANES:(cg + 1) * LANES]
        s_parts = [lax.dot_general(kc, qcat, NT_DIMS, preferred_element_type=F32)]
        vt_parts = [vct_ref[kvh]]
        if has_local:
            s_parts.append(lax.dot_general(k_loc, qcat, NT_DIMS, preferred_element_type=F32) + band)
            vt_parts.append(vt_loc[kvh])
        extra = None
        if has_sink:
            extra = jnp.concatenate([sink_ref[h:h + 1, :] for h in range(kvh * group, (kvh + 1) * group)
                                     for _ in range(tq // LANES)], axis=1)
        o = _softmax_pv_t(s_parts, vt_parts, extra)
        outs += [o[:, g * tq:(g + 1) * tq] for g in range(group)]
    o_ref[...] = jnp.concatenate(outs, axis=0).T.astype(BF16)


def _gqa_attention(p_q, vt_q, p_ctx, vt_ctx, sink, nb, n, ctx_len, q_col, k_col, n_q, n_kv, has_local, tq):
    qsteps = n // tq
    kv_w = n_kv * HEAD_DIM
    q_w = n_q * HEAD_DIM
    vrows = HEAD_DIM + BF16_SUBLANES
    in_specs = [pl.BlockSpec((tq, q_w), lambda b, i: (b * qsteps + i, q_col * LANES // q_w))]
    args = [p_q]
    if has_local:
        assert tq == 2 * BLOCK and kv_w == LANES
        wsteps = n // BLOCK
        wblock = lambda i, j: jnp.clip(2 * i - 1 + j, 0, wsteps - 1)
        in_specs += [pl.BlockSpec((BLOCK, LANES), functools.partial(
            lambda b, i, j: (b * wsteps + wblock(i, j), k_col), j=j)) for j in range(4)]
        in_specs += [pl.BlockSpec((n_kv, vrows, BLOCK), functools.partial(
            lambda b, i, j: (0, 0, b * wsteps + wblock(i, j)), j=j)) for j in range(4)]
        args += [p_q] * 4 + [vt_q] * 4
    in_specs += [pl.BlockSpec((ctx_len, kv_w), lambda b, i: (b, k_col * LANES // kv_w)),
                 pl.BlockSpec((n_kv, vrows, ctx_len), lambda b, i: (0, 0, b))]
    args += [p_ctx, vt_ctx]
    if sink is not None:
        in_specs.append(_resident(sink.shape, lambda b, i: (0, 0)))
        args.append(sink)
    return pl.pallas_call(
        functools.partial(_gqa_kernel, n_q=n_q, n_kv=n_kv, has_local=has_local,
                          has_sink=sink is not None, n_tokens=n),
        grid=(nb, qsteps),
        in_specs=in_specs,
        out_specs=pl.BlockSpec((tq, q_w), lambda b, i: (b * qsteps + i, 0)),
        out_shape=jax.ShapeDtypeStruct((nb * n, q_w), BF16),
        compiler_params=_params(("parallel", "parallel")),
        name="gqa_attention",
    )(*args)


def _diff_kernel(*refs, n_kblocks, tk, lam_init):
    if n_kblocks:
        q_ref, k_ref, vt_ref, kc_ref, vct_ref, lam_ref, subg_ref, bound_ref, o_ref, m_sc, acc_sc, p_sc = refs
    else:
        q_ref, kc_ref, vct_ref, lam_ref, subg_ref, bound_ref, o_ref, m_sc, acc_sc = refs
    lane = _lane_iota()
    q = q_ref[...]
    zero = jnp.zeros_like(q)
    q_maps = [jnp.where(lane < HEAD_DIM, q, zero), jnp.where(lane >= HEAD_DIM, q, zero)]

    kc, vct = kc_ref[...], vct_ref[0]
    for i in range(2):
        s = lax.dot_general(kc, q_maps[i], NT_DIMS, preferred_element_type=F32)
        m = s.max(axis=0, keepdims=True)
        p = jnp.exp2((s - m).astype(BF16))
        m_sc[i] = m
        acc_sc[i] = jnp.dot(vct, p, preferred_element_type=F32)

    if n_kblocks:
        def scores(kb):
            k = k_ref[pl.ds(pl.multiple_of(kb * tk, tk), tk), :]
            return [lax.dot_general(k, q_maps[i], NT_DIMS, preferred_element_type=F32) for i in range(2)]

        def probs(kb, slot):
            s = scores(kb)
            excess = None
            for i in range(2):
                m_ref = m_sc[i]
                p_sc[slot, i] = jnp.exp2((s[i] - m_ref).astype(BF16))
                over = jnp.max(s[i].max(axis=0, keepdims=True) - m_ref)
                excess = over if excess is None else jnp.maximum(excess, over)
            return excess

        def settle(kb, slot, excess):
            @pl.when(excess > RESCALE_MARGIN)
            def _():
                s = scores(kb)
                for i in range(2):
                    m_prev = m_sc[i]
                    m_new = jnp.maximum(m_prev, s[i].max(axis=0, keepdims=True))
                    p_sc[slot, i] = jnp.exp2((s[i] - m_new).astype(BF16))
                    acc_sc[i] = jnp.exp2(m_prev - m_new) * acc_sc[i]
                    m_sc[i] = m_new

        def accumulate(kb, slot):
            vt = vt_ref[0, :, pl.ds(pl.multiple_of(kb * tk, tk), tk)]
            for i in range(2):
                acc_sc[i] += jnp.dot(vt, p_sc[slot, i], preferred_element_type=F32)

        def probs_unchecked(kb, slot):
            s = scores(kb)
            for i in range(2):
                p_sc[slot, i] = jnp.exp2((s[i] - m_sc[i]).astype(BF16))

        lowest_ref = jnp.minimum(jnp.min(m_sc[0]), jnp.min(m_sc[1]))
        never_rescales = bound_ref[0, 0] - lowest_ref <= RESCALE_MARGIN
        first_excess = probs(0, 0)

        @pl.when(never_rescales)
        def _():
            def body(kb, carry):
                accumulate(kb - 1, (kb - 1) % 2)
                probs_unchecked(kb, kb % 2)
                return carry

            lax.fori_loop(1, n_kblocks, body, 0)
            accumulate(n_kblocks - 1, (n_kblocks - 1) % 2)

        @pl.when(jnp.logical_not(never_rescales))
        def _():
            def body(kb, excess):
                settle(kb - 1, (kb - 1) % 2, excess)
                accumulate(kb - 1, (kb - 1) % 2)
                return probs(kb, kb % 2)

            excess = lax.fori_loop(1, n_kblocks, body, first_excess)
            settle(n_kblocks - 1, (n_kblocks - 1) % 2, excess)
            accumulate(n_kblocks - 1, (n_kblocks - 1) % 2)

    lp = lam_ref[...]
    lam = (jnp.exp(jnp.sum(lp[0:1] * lp[1:2], axis=-1, keepdims=True))
           - jnp.exp(jnp.sum(lp[2:3] * lp[3:4], axis=-1, keepdims=True)) + lam_init)
    dv = 2 * HEAD_DIM
    y0 = acc_sc[0, :dv] / acc_sc[0, dv:dv + 1]
    y1 = acc_sc[1, :dv] / acc_sc[1, dv:dv + 1]
    y = (y0 - lam * y1).T
    ms = jnp.mean(y * y, axis=-1, keepdims=True)
    y = y * lax.rsqrt(ms + EPS) * subg_ref[...] * (1.0 - lam_init)
    o_ref[...] = y.astype(BF16)


def _diff_attention(p_q, p_lat, vt_lat, p_ctx, vt_ctx, lam_rows, subg, score_bound, nb, n_q, n_lat, ctx_len,
                    q_col, k_col, n_heads, lam_init, tq, tk):
    qsteps = n_q // tq
    has_latent = p_lat is not None
    vrows = 2 * HEAD_DIM + BF16_SUBLANES
    in_specs = [pl.BlockSpec((tq, LANES), lambda b, h, i: (b * qsteps + i, q_col + h))]
    args = [p_q]
    if has_latent:
        in_specs += [pl.BlockSpec((n_lat, LANES), lambda b, h, i: (b, k_col + h)),
                     pl.BlockSpec((1, vrows, n_lat), lambda b, h, i: (h, 0, b))]
        args += [p_lat, vt_lat]
    in_specs += [pl.BlockSpec((ctx_len, LANES), lambda b, h, i: (b, k_col + h)),
                 pl.BlockSpec((1, vrows, ctx_len), lambda b, h, i: (h, 0, b)),
                 _resident(lam_rows.shape, lambda b, h, i: (0, 0)),
                 _resident(subg.shape, lambda b, h, i: (0, 0)),
                 pl.BlockSpec(memory_space=pltpu.SMEM)]
    args += [p_ctx, vt_ctx, lam_rows, subg, score_bound]
    return pl.pallas_call(
        functools.partial(_diff_kernel, n_kblocks=n_lat // tk if has_latent else 0, tk=tk, lam_init=lam_init),
        grid=(nb, n_heads, qsteps),
        in_specs=in_specs,
        out_specs=pl.BlockSpec((tq, LANES), lambda b, h, i: (b * qsteps + i, h)),
        out_shape=jax.ShapeDtypeStruct((nb * n_q, n_heads * LANES), BF16),
        scratch_shapes=([pltpu.VMEM((2, 1, tq), F32), pltpu.VMEM((2, vrows, tq), F32)]
                        + ([pltpu.VMEM((2, 2, tk, tq), BF16)] if has_latent else [])),
        compiler_params=_params(("parallel", "parallel", "arbitrary")),
        name="diff_attention",
    )(*args)


def _silu(x):
    return x * (1.0 / (1.0 + jnp.exp(-x)))


def _ffn_kernel(x_ref, g_ref, sc_ref, sh_ref, gate_ref, w1_ref, w3_ref, w2_ref, o_ref):
    x = x_ref[...]
    a = _modulated_norm(x, g_ref[...], sc_ref[0], sh_ref[0]).astype(BF16)
    h1 = jnp.dot(a, w1_ref[...], preferred_element_type=F32)
    h3 = jnp.dot(a, w3_ref[...], preferred_element_type=F32)
    y = jnp.dot((_silu(h1) * h3).astype(BF16), w2_ref[...], preferred_element_type=F32)
    o_ref[...] = x + gate_ref[0] * y


def _ffn(h, nb, g, sc, sh, gate, w1, w3, w2, tm):
    t, d = h.shape
    steps = t // nb // tm
    row = lambda b, i: (b * steps + i, 0)
    vec = pl.BlockSpec((1, 1, d), lambda b, i: (b, 0, 0))
    return pl.pallas_call(
        _ffn_kernel,
        grid=(nb, steps),
        in_specs=[pl.BlockSpec((tm, d), row), _resident((1, d), lambda b, i: (0, 0)), vec, vec, vec,
                  _resident(w1.shape, lambda b, i: (0, 0)),
                  _resident(w3.shape, lambda b, i: (0, 0)),
                  _resident(w2.shape, lambda b, i: (0, 0))],
        out_specs=pl.BlockSpec((tm, d), row),
        out_shape=jax.ShapeDtypeStruct((t, d), F32),
        compiler_params=_params(("parallel", "parallel")),
        name="ffn",
    )(h, g, sc, sh, gate, w1, w3, w2)


def _top2_gates(logits):
    lane = _lane_iota()
    big = jnp.int32(LANES)
    lg = jnp.where(lane < N_EXPERTS, logits, -jnp.inf)
    m1 = lg.max(axis=-1, keepdims=True)
    i1 = jnp.where(lg == m1, lane, big).min(axis=-1, keepdims=True)
    rest = jnp.where(lane == i1, -jnp.inf, lg)
    m2 = rest.max(axis=-1, keepdims=True)
    i2 = jnp.where(rest == m2, lane, big).min(axis=-1, keepdims=True)
    e2 = jnp.exp(m2 - m1)
    den = 1.0 + e2
    return jnp.where(lane == i1, 1.0 / den, 0.0) + jnp.where(lane == i2, e2 / den, 0.0)


def _moe_kernel(x_ref, g_ref, sc_ref, sh_ref, gate_ref, r_ref, w1_ref, w3_ref, w2_ref, o_ref,
                a_sc, gates_sc, rank_sc, acc_sc, *, chunk):
    e = pl.program_id(2)
    tm = x_ref.shape[0]

    @pl.when(e == 0)
    def _():
        a = _modulated_norm(x_ref[...], g_ref[...], sc_ref[0], sh_ref[0])
        a_hi = a.astype(BF16)
        a_lo = (a - a_hi.astype(F32)).astype(BF16)
        r = r_ref[...]
        r_hi = r.astype(BF16)
        r_lo = (r - r_hi.astype(F32)).astype(BF16)
        logits = (jnp.dot(a_hi, r_hi, preferred_element_type=F32)
                  + jnp.dot(a_hi, r_lo, preferred_element_type=F32)
                  + jnp.dot(a_lo, r_hi, preferred_element_type=F32))
        a_sc[...] = a_hi
        gates_t = _top2_gates(logits).T[:EXPERT_ROWS]
        gates_sc[...] = gates_t
        before = (lax.broadcasted_iota(jnp.int32, (tm, tm), 0)
                  < lax.broadcasted_iota(jnp.int32, (tm, tm), 1))
        routed = jnp.where(gates_t > 0.0, 1.0, 0.0).astype(BF16)
        rank_sc[...] = jnp.dot(routed, jnp.where(before, 1.0, 0.0).astype(BF16), preferred_element_type=F32)
        acc_sc[...] = jnp.zeros_like(acc_sc)

    gate_row = gates_sc[pl.ds(e, 1), :]
    rank_row = jnp.where(gate_row > 0.0, rank_sc[pl.ds(e, 1), :], -1.0)
    count = jnp.sum(jnp.where(gate_row > 0.0, 1.0, 0.0)).astype(jnp.int32)
    slot = lax.broadcasted_iota(jnp.int32, (chunk, 1), 0).astype(F32)

    for j in range(pl.cdiv(tm, chunk)):
        @pl.when(count > j * chunk)
        def _():
            pick = rank_row == (slot + float(j * chunk))
            pick_f = jnp.where(pick, 1.0, 0.0)
            xs = jnp.dot(pick_f.astype(BF16), a_sc[...], preferred_element_type=F32).astype(BF16)
            h1 = jnp.dot(xs, w1_ref[0], preferred_element_type=F32)
            h3 = jnp.dot(xs, w3_ref[0], preferred_element_type=F32)
            y = jnp.dot((_silu(h1) * h3).astype(BF16), w2_ref[0], preferred_element_type=F32)
            y = y * jnp.sum(pick_f * gate_row, axis=-1, keepdims=True)
            acc_sc[...] += lax.dot_general(pick_f.astype(BF16), y.astype(BF16), (((0,), (0,)), ((), ())),
                                           preferred_element_type=F32)

    @pl.when(e == pl.num_programs(2) - 1)
    def _():
        o_ref[...] = x_ref[...] + gate_ref[0] * acc_sc[...]


def _moe(h, nb, g, sc, sh, gate, router, w1, w3, w2, tm, chunk):
    t, d = h.shape
    steps = t // nb // tm
    n_e, _, f = w1.shape
    row = lambda b, i, e: (b * steps + i, 0)
    vec = pl.BlockSpec((1, 1, d), lambda b, i, e: (b, 0, 0))
    return pl.pallas_call(
        functools.partial(_moe_kernel, chunk=chunk),
        grid=(nb, steps, n_e),
        in_specs=[pl.BlockSpec((tm, d), row), _resident((1, d), lambda b, i, e: (0, 0)), vec, vec, vec,
                  _resident(router.shape, lambda b, i, e: (0, 0)),
                  pl.BlockSpec((1, d, f), lambda b, i, e: (e, 0, 0)),
                  pl.BlockSpec((1, d, f), lambda b, i, e: (e, 0, 0)),
                  pl.BlockSpec((1, f, d), lambda b, i, e: (e, 0, 0))],
        out_specs=pl.BlockSpec((tm, d), row),
        out_shape=jax.ShapeDtypeStruct((t, d), F32),
        scratch_shapes=[pltpu.VMEM((tm, d), BF16), pltpu.VMEM((EXPERT_ROWS, tm), F32),
                        pltpu.VMEM((EXPERT_ROWS, tm), F32), pltpu.VMEM((tm, d), F32)],
        compiler_params=_params(("parallel", "parallel", "arbitrary")),
        name="moe",
    )(h, g, sc, sh, gate, router, w1, w3, w2)


def _rope_tables(n):
    t = jnp.arange(n)
    pos = jnp.stack([t // GRID_W, t % GRID_W], -1).astype(F32)
    nq = HEAD_DIM // 4
    inv = ROPE_THETA ** (-jnp.arange(nq, dtype=F32) / nq)
    ang = pos[:, :, None] * inv
    cos = jnp.repeat(jnp.cos(ang)[:, :, None, :], 2, axis=2)
    sin = jnp.stack([-jnp.sin(ang), jnp.sin(ang)], axis=2)
    cos = jnp.tile(cos.reshape(n, HEAD_DIM), (1, LANES // HEAD_DIM))
    sin = jnp.tile(sin.reshape(n, HEAD_DIM), (1, LANES // HEAD_DIM))
    return cos, sin


def _head_gain(parts, n_out):
    row = jnp.ones((n_out,), F32)
    for col, n_heads, gain, scale in parts:
        row = lax.dynamic_update_slice(row, jnp.tile(gain.astype(F32) * scale, n_heads), (col,))
    return row.reshape(1, n_out)


def kernel(x, c, ctx, c_ctx, ada_w, ada_b, norm1_g, norm2_g, ev_w_in, ev_conv_w, ev_q_g, ev_k_g, ev_rpb,
           ev_w_out, ffn_w1, ffn_w3, ffn_w2, od_w_in, od_cq_g, od_ck_g, od_sink, od_dq_g, od_dk_g,
           od_lam_q1, od_lam_k1, od_lam_q2, od_lam_k2, od_subln_g, od_w_out, moe_router,
           moe_w1, moe_w3, moe_w2):
    nb, n, d = x.shape
    ctx_len = ctx.shape[1]
    depth = ada_w.shape[0]
    n_slots = d // HEAD_DIM
    conv_ch = d // 2
    na_heads = swa_heads = n_slots // 2
    swa_kv = max(1, swa_heads // 4)
    diff_heads = n_slots // 4
    rows = n // GRID_W
    assert rows % NA_QROWS == 0 and rows >= 2 * NA_QROWS and n % 512 == 0
    assert nb + 1 <= 8 and ctx_len % LANES == 0

    ev_q_col = 3 * conv_ch
    ev_k_col = ev_q_col + na_heads * HEAD_DIM
    ev_v_col = ev_k_col + na_heads * HEAD_DIM
    ev_n = ev_v_col + na_heads * HEAD_DIM
    od_dq_col = swa_heads * HEAD_DIM
    od_ck_col = od_dq_col + diff_heads * 2 * HEAD_DIM
    od_cv_col = od_ck_col + swa_kv * HEAD_DIM
    od_dk_col = od_cv_col + swa_kv * HEAD_DIM
    od_dv_col = od_dk_col + diff_heads * 2 * HEAD_DIM
    od_n = od_dv_col + diff_heads * 2 * HEAD_DIM
    ev_segs = ((0, ev_q_col, False, 0), (ev_q_col, ev_v_col, True, 0), (ev_v_col, ev_n, False, na_heads))
    od_segs = ((0, od_cv_col, True, 0), (od_cv_col, od_dk_col, False, swa_kv), (od_dk_col, od_dv_col, True, 0),
               (od_dv_col, od_n, False, diff_heads))

    h = x.reshape(nb * n, d)
    hc = ctx.reshape(nb * ctx_len, d)
    tm_lat = 512
    tm_ctx = ctx_len
    tm_moe = 1024

    s_rows = jnp.zeros((8, d), F32).at[:nb].set(c).at[nb].set(c_ctx)
    mod = _modulation(s_rows, ada_w, ada_b)
    rope_tabs = _rope_tables(n)

    def lat_vec(l, k):
        return mod[l, :nb, k * d:(k + 1) * d].reshape(nb, 1, d)

    def ctx_vec(l, k, copies):
        return jnp.broadcast_to(mod[l, nb, k * d:(k + 1) * d], (copies, 1, d))

    for l in range(depth):
        last = l == depth - 1
        i = l // 2
        g1 = norm1_g[l].reshape(1, d)
        g2 = norm2_g[l].reshape(1, d)
        if l % 2 == 0:
            w_in = ev_w_in[i].astype(BF16)
            gain = _head_gain([(ev_q_col, na_heads, ev_q_g[i], QK_SCALE * LOG2_E), (ev_k_col, na_heads, ev_k_g[i], 1.0)],
                              ev_n)
            p_lat, vt_lat = _norm_proj(h, nb, g1, lat_vec(l, 1), lat_vec(l, 0), w_in, gain, ev_segs, None, tm_lat)
            p_ctx, vt_ctx = _norm_proj(hc, nb, g1, ctx_vec(l, 1, nb), ctx_vec(l, 0, nb), w_in, gain, ev_segs, None,
                                       tm_ctx)
            bias = _na_bias_tables(ev_rpb[i], rows)
            y_na = _na_attention(p_lat, vt_lat, p_ctx, vt_ctx, bias, nb, n, ctx_len, ev_q_col // LANES,
                                 ev_k_col // LANES, na_heads)
            w_out = ev_w_out[i].astype(BF16)
            conv_w = ev_conv_w[i]
            h = _out_proj(h, nb, lat_vec(l, 2), p_lat, y_na, w_out, conv_w, tm_lat)
            if not last:
                y_na_c = _gqa_attention(p_ctx, None, p_ctx, vt_ctx, None, nb, ctx_len, ctx_len, ev_q_col // LANES,
                                        ev_k_col // LANES, na_heads, na_heads, False, ctx_len)
                hc = _out_proj(hc, nb, ctx_vec(l, 2, nb), p_ctx, y_na_c, w_out, conv_w, tm_ctx)
            w1, w3, w2 = ffn_w1[i].astype(BF16), ffn_w3[i].astype(BF16), ffn_w2[i].astype(BF16)
            h = _ffn(h, nb, g2, lat_vec(l, 4), lat_vec(l, 3), lat_vec(l, 5), w1, w3, w2, tm_lat)
            if not last:
                hc = _ffn(hc, 1, g2, ctx_vec(l, 4, 1), ctx_vec(l, 3, 1), ctx_vec(l, 5, 1), w1, w3, w2, tm_lat)
        else:
            lam_init = 0.8 - 0.6 * math.exp(-0.3 * l)
            w_in = od_w_in[i].astype(BF16)
            gain = _head_gain([(0, swa_heads, od_cq_g[i], QK_SCALE * LOG2_E),
                               (od_dq_col, 2 * diff_heads, od_dq_g[i], QK_SCALE * LOG2_E),
                               (od_ck_col, swa_kv, od_ck_g[i], 1.0),
                               (od_dk_col, 2 * diff_heads, od_dk_g[i], 1.0)], od_n)
            p_lat, cvt_lat, vt_lat = _norm_proj(h, nb, g1, lat_vec(l, 1), lat_vec(l, 0), w_in, gain, od_segs, rope_tabs,
                                                tm_lat)
            p_ctx, cvt_ctx, vt_ctx = _norm_proj(hc, nb, g1, ctx_vec(l, 1, nb), ctx_vec(l, 0, nb), w_in, gain, od_segs,
                                                None, tm_ctx)
            sink = jnp.broadcast_to(od_sink[i].astype(F32)[:, None] * LOG2_E, (swa_heads, LANES))
            lam_rows = jnp.zeros((8, LANES), F32).at[:4, :HEAD_DIM].set(
                jnp.stack([od_lam_q1[i], od_lam_k1[i], od_lam_q2[i], od_lam_k2[i]]).astype(F32))
            subg = od_subln_g[i].astype(F32).reshape(1, 2 * HEAD_DIM)
            cols = (od_dq_col // LANES, od_dk_col // LANES)
            score_bound = (HEAD_DIM * QK_SCALE * LOG2_E * BF16_NORM_SLACK * jnp.max(jnp.abs(od_dq_g[i]))
                           * jnp.max(jnp.abs(od_dk_g[i]))).astype(F32).reshape(1, 1)
            y_c = _gqa_attention(p_lat, cvt_lat, p_ctx, cvt_ctx, sink, nb, n, ctx_len, 0, od_ck_col // LANES,
                                 swa_heads, swa_kv, True, 2 * BLOCK)
            y_d = _diff_attention(p_lat, p_lat, vt_lat, p_ctx, vt_ctx, lam_rows, subg, score_bound, nb, n, n, ctx_len,
                                  *cols, diff_heads, lam_init, 512, 1024)
            w_out = od_w_out[i].astype(BF16)
            h = _out_proj(h, nb, lat_vec(l, 2), y_c, y_d, w_out, None, tm_lat)
            if not last:
                y_c_c = _gqa_attention(p_ctx, None, p_ctx, cvt_ctx, sink, nb, ctx_len, ctx_len, 0,
                                       od_ck_col // LANES, swa_heads, swa_kv, False, ctx_len)
                y_d_c = _diff_attention(p_ctx, None, None, p_ctx, vt_ctx, lam_rows, subg, score_bound, nb, ctx_len, 0,
                                        ctx_len, *cols, diff_heads, lam_init, ctx_len, ctx_len)
                hc = _out_proj(hc, nb, ctx_vec(l, 2, nb), y_c_c, y_d_c, w_out, None, tm_ctx)
            router = jnp.zeros((d, LANES), F32).at[:, :N_EXPERTS].set(moe_router[i])
            w1, w3, w2 = moe_w1[i].astype(BF16), moe_w3[i].astype(BF16), moe_w2[i].astype(BF16)
            h = _moe(h, nb, g2, lat_vec(l, 4), lat_vec(l, 3), lat_vec(l, 5), router, w1, w3, w2, tm_moe, MOE_CHUNK)
            if not last:
                hc = _moe(hc, 1, g2, ctx_vec(l, 4, 1), ctx_vec(l, 3, 1), ctx_vec(l, 5, 1), router, w1, w3, w2,
                          min(tm_moe, nb * ctx_len), MOE_CHUNK)
    return h.reshape(nb, n, d)
```

```python
import functools
import math

import jax
import jax.numpy as jnp
from jax import lax
from jax.experimental import pallas as pl
from jax.experimental.pallas import tpu as pltpu

F32 = jnp.float32
BF16 = jnp.bfloat16

LANES = 128
BF16_SUBLANES = 16
VMEM_LIMIT = 56 * 1024 * 1024

HEAD_DIM = 64
GRID_W = 64
CONV_W = 3
NA_ROWS = 8
NA_COLS = 16
NA_QROWS = 8
SWA_WINDOW = 128
MOE_CHUNK = 256
BLOCK = 128
N_EXPERTS = 8
EXPERT_ROWS = 16
ROPE_THETA = 10000.0
EPS = 1e-6
NEG = -1e30
QK_SCALE = HEAD_DIM ** -0.5
LOG2_E = math.log2(math.e)
BF16_NORM_SLACK = 1.02
RESCALE_MARGIN = 32.0

NT_DIMS = (((1,), (1,)), ((), ()))


def _params(sem):
    return pltpu.CompilerParams(dimension_semantics=sem, vmem_limit_bytes=VMEM_LIMIT)


def _resident(shape, index_map):
    return pl.BlockSpec(shape, index_map, pipeline_mode=pl.Buffered(1))


def _lane_iota():
    return lax.broadcasted_iota(jnp.int32, (1, LANES), 1)


def _swap_halves(x):
    return jnp.concatenate([x[:, HEAD_DIM:], x[:, :HEAD_DIM]], axis=1)


def _modulated_norm(x, g, sc, sh):
    ms = jnp.mean(x * x, axis=-1, keepdims=True)
    return (x * lax.rsqrt(ms + EPS)) * (g * (1.0 + sc)) + sh


def _mod_kernel(s_ref, w_ref, b_ref, o_ref):
    s = s_ref[...]
    s = s * (1.0 / (1.0 + jnp.exp(-s)))
    o_ref[0] = jnp.dot(s.astype(BF16), w_ref[0].astype(BF16), preferred_element_type=F32) + b_ref[0]


def _modulation(s_rows, ada_w, ada_b):
    depth, d, n_out = ada_w.shape
    tn = n_out // 4
    return pl.pallas_call(
        _mod_kernel,
        grid=(depth, n_out // tn),
        in_specs=[pl.BlockSpec(s_rows.shape, lambda l, j: (0, 0)),
                  pl.BlockSpec((1, d, tn), lambda l, j: (l, 0, j)),
                  pl.BlockSpec((1, 1, tn), lambda l, j: (l, 0, j))],
        out_specs=pl.BlockSpec((1, s_rows.shape[0], tn), lambda l, j: (l, 0, j)),
        out_shape=jax.ShapeDtypeStruct((depth, s_rows.shape[0], n_out), F32),
        compiler_params=_params(("arbitrary", "arbitrary")),
        name="modulation",
    )(s_rows, ada_w, ada_b.reshape(depth, 1, n_out))


def _head_sumsq(z):
    r = lax.broadcasted_iota(jnp.int32, (LANES, LANES), 0) // HEAD_DIM
    c = lax.broadcasted_iota(jnp.int32, (LANES, LANES), 1) // HEAD_DIM
    same_head = jnp.where(r == c, 1.0, 0.0).astype(BF16)
    z2 = z * z
    hi = z2.astype(BF16)
    lo = (z2 - hi.astype(F32)).astype(BF16)
    return (jnp.dot(hi, same_head, preferred_element_type=F32)
            + jnp.dot(lo, same_head, preferred_element_type=F32))


def _norm_proj_kernel(*refs, segs, rope, n_vt):
    refs = list(refs)
    vt_refs = [refs.pop() for _ in range(n_vt)][::-1]
    if rope:
        x_ref, g_ref, sc_ref, sh_ref, w_ref, gain_ref, cos_ref, sin_ref, o_ref = refs
    else:
        x_ref, g_ref, sc_ref, sh_ref, w_ref, gain_ref, o_ref = refs
    a = _modulated_norm(x_ref[...], g_ref[...], sc_ref[0], sh_ref[0]).astype(BF16)
    first_half = (_lane_iota() % (HEAD_DIM // 2)) < (HEAD_DIM // 4)
    for c0, c1, normed, vt_heads in segs:
        acc = jnp.dot(a, w_ref[:, c0:c1], preferred_element_type=F32)
        if vt_heads:
            vt_ref = vt_refs.pop(0)
            dv = (c1 - c0) // vt_heads
            vt = acc.T.astype(BF16)
            for hh in range(vt_heads):
                vt_ref[hh, :dv, :] = vt[hh * dv:(hh + 1) * dv]
                vt_ref[hh, dv:, :] = jnp.ones((BF16_SUBLANES, vt.shape[1]), BF16)
        if not normed:
            o_ref[:, c0:c1] = acc.astype(BF16)
            continue
        for j in range((c1 - c0) // LANES):
            z = acc[:, j * LANES:(j + 1) * LANES]
            lo = c0 + j * LANES
            z = z * lax.rsqrt(_head_sumsq(z) * (1.0 / HEAD_DIM) + EPS) * gain_ref[:, lo:lo + LANES]
            if rope:
                partner = jnp.where(first_half,
                                    pltpu.roll(z, LANES - HEAD_DIM // 4, axis=1),
                                    pltpu.roll(z, HEAD_DIM // 4, axis=1))
                z = z * cos_ref[...] + partner * sin_ref[...]
            o_ref[:, lo:lo + LANES] = z.astype(BF16)


def _norm_proj(h, nb, g, sc, sh, w, gain, segs, rope_tabs, tm):
    t, d = h.shape
    n_out = w.shape[1]
    steps = t // nb // tm
    in_specs = [pl.BlockSpec((tm, d), lambda b, i: (b * steps + i, 0)),
                _resident((1, d), lambda b, i: (0, 0)),
                pl.BlockSpec((1, 1, d), lambda b, i: (b, 0, 0)),
                pl.BlockSpec((1, 1, d), lambda b, i: (b, 0, 0)),
                _resident((d, n_out), lambda b, i: (0, 0)),
                _resident((1, n_out), lambda b, i: (0, 0))]
    args = [h, g, sc, sh, w, gain]
    if rope_tabs is not None:
        in_specs += [pl.BlockSpec((tm, LANES), lambda b, i: (i, 0))] * 2
        args += list(rope_tabs)
    out_specs = [pl.BlockSpec((tm, n_out), lambda b, i: (b * steps + i, 0))]
    out_shape = [jax.ShapeDtypeStruct((t, n_out), BF16)]
    for c0, c1, _, vt_heads in segs:
        if vt_heads:
            rows = (c1 - c0) // vt_heads + BF16_SUBLANES
            out_specs.append(pl.BlockSpec((vt_heads, rows, tm), lambda b, i: (0, 0, b * steps + i)))
            out_shape.append(jax.ShapeDtypeStruct((vt_heads, rows, t), BF16))
    return pl.pallas_call(
        functools.partial(_norm_proj_kernel, segs=segs, rope=rope_tabs is not None, n_vt=len(out_specs) - 1),
        grid=(nb, steps),
        in_specs=in_specs,
        out_specs=out_specs,
        out_shape=out_shape,
        compiler_params=_params(("parallel", "parallel")),
        name="norm_proj",
    )(*args)


def _gated_conv(gb_ref, gc_ref, u_ref, gcp_ref, up_ref, gcn_ref, un_ref, cw_ref):
    i, steps = pl.program_id(1), pl.num_programs(1)
    v = gc_ref[...].astype(F32) * u_ref[...].astype(F32)
    tm = v.shape[0]
    last = BF16_SUBLANES - 1
    prev_row = gcp_ref[last:last + 1, :].astype(F32) * up_ref[last:last + 1, :].astype(F32)
    next_row = gcn_ref[0:1, :].astype(F32) * un_ref[0:1, :].astype(F32)
    prev_row = jnp.where(i > 0, prev_row, 0.0)
    next_row = jnp.where(i < steps - 1, next_row, 0.0)
    row = lax.broadcasted_iota(jnp.int32, (tm, 1), 0)
    v_prev = jnp.where(row == 0, prev_row, pltpu.roll(v, 1, axis=0))
    v_next = jnp.where(row == tm - 1, next_row, pltpu.roll(v, tm - 1, axis=0))
    cw = cw_ref[...]
    conv = cw[0:1, :] * v_prev + cw[1:2, :] * v + cw[2:3, :] * v_next
    return gb_ref[...].astype(F32) * conv


def _out_proj_kernel(*refs, conv):
    if conv:
        (gb_ref, gc_ref, u_ref, gcp_ref, up_ref, gcn_ref, un_ref, cw_ref,
         yb_ref, wa_ref, wb_ref, h_ref, gate_ref, o_ref) = refs
        ya = _gated_conv(gb_ref, gc_ref, u_ref, gcp_ref, up_ref, gcn_ref, un_ref, cw_ref).astype(BF16)
    else:
        ya_ref, yb_ref, wa_ref, wb_ref, h_ref, gate_ref, o_ref = refs
        ya = ya_ref[...]
    y = (jnp.dot(ya, wa_ref[...], preferred_element_type=F32)
         + jnp.dot(yb_ref[...], wb_ref[...], preferred_element_type=F32))
    o_ref[...] = h_ref[...] + gate_ref[0] * y


def _out_proj(h, nb, gate, ya_src, yb, w_out, conv_w, tm):
    t, d = h.shape
    steps = t // nb // tm
    wa_rows = w_out.shape[0] - yb.shape[1]
    w_a, w_b = w_out[:wa_rows], w_out[wa_rows:]
    row = lambda b, i: (b * steps + i, 0)
    if conv_w is not None:
        cc = conv_w.shape[1]
        hb = tm // BF16_SUBLANES
        n_halo = t // BF16_SUBLANES
        prev = lambda col: (lambda b, i: (jnp.maximum((b * steps + i) * hb - 1, 0), col))
        nxt = lambda col: (lambda b, i: (jnp.minimum((b * steps + i + 1) * hb, n_halo - 1), col))
        in_specs = [pl.BlockSpec((tm, cc), lambda b, i: (b * steps + i, 0)),
                    pl.BlockSpec((tm, cc), lambda b, i: (b * steps + i, 1)),
                    pl.BlockSpec((tm, cc), lambda b, i: (b * steps + i, 2)),
                    pl.BlockSpec((BF16_SUBLANES, cc), prev(1)),
                    pl.BlockSpec((BF16_SUBLANES, cc), prev(2)),
                    pl.BlockSpec((BF16_SUBLANES, cc), nxt(1)),
                    pl.BlockSpec((BF16_SUBLANES, cc), nxt(2)),
                    _resident(conv_w.shape, lambda b, i: (0, 0))]
        args = [ya_src] * 7 + [conv_w]
    else:
        in_specs = [pl.BlockSpec((tm, wa_rows), row)]
        args = [ya_src]
    in_specs += [pl.BlockSpec((tm, yb.shape[1]), row),
                 _resident(w_a.shape, lambda b, i: (0, 0)),
                 _resident(w_b.shape, lambda b, i: (0, 0)),
                 pl.BlockSpec((tm, d), row),
                 pl.BlockSpec((1, 1, d), lambda b, i: (b, 0, 0))]
    args += [yb, w_a, w_b, h, gate]
    return pl.pallas_call(
        functools.partial(_out_proj_kernel, conv=conv_w is not None),
        grid=(nb, steps),
        in_specs=in_specs,
        out_specs=pl.BlockSpec((tm, d), row),
        out_shape=jax.ShapeDtypeStruct((t, d), F32),
        compiler_params=_params(("parallel", "parallel")),
        name="out_proj",
    )(*args)


def _na_bias_tables(rpb, rows):
    n_heads = rpb.shape[0]
    n_dr = 2 * NA_ROWS - 1
    c = jnp.arange(GRID_W)
    c0 = jnp.clip(c - NA_COLS // 2, 0, GRID_W - NA_COLS)
    col_ok = (c[None, :] >= c0[:, None]) & (c[None, :] < c0[:, None] + NA_COLS)
    dc = jnp.clip(c[None, :] - c[:, None] + NA_COLS - 1, 0, 2 * NA_COLS - 2)
    pick_dc = ((dc[None] == jnp.arange(2 * NA_COLS - 1)[:, None, None]) & col_ok[None]).astype(F32)
    table = jnp.einsum('hrd,dcx->hrxc', rpb.astype(F32), pick_dc, precision=lax.Precision.HIGHEST)
    table = jnp.where(col_ok.T[None, None], table * LOG2_E, NEG)

    def paired(tab):
        pad = jnp.full((n_heads, NA_ROWS // 2, GRID_W, GRID_W), NEG, F32)
        ext = jnp.concatenate([pad, tab, pad], axis=1)
        return jnp.concatenate([ext[:, 1:], ext[:, :-1]], axis=-1)

    dr = jnp.arange(n_dr)
    in_window = (dr >= NA_ROWS // 2 - 1) & (dr < NA_ROWS // 2 - 1 + NA_ROWS)
    interior = jnp.where(in_window[None, :, None, None], table, NEG)

    nblk = rows // NA_QROWS
    i = jnp.arange(NA_QROWS)
    j = jnp.arange(2 * NA_QROWS)
    masks = []
    for rb in (0, 1, nblk - 1):
        r = rb * NA_QROWS + i
        r0 = jnp.clip(r - NA_ROWS // 2, 0, rows - NA_ROWS)
        rk = rb * NA_QROWS - NA_ROWS // 2 + j
        ok = (rk[:, None] >= r0[None, :]) & (rk[:, None] < r0[None, :] + NA_ROWS)
        masks.append(jnp.repeat(jnp.where(ok, 0.0, NEG), GRID_W, axis=1))
    return paired(table), paired(interior), jnp.stack(masks)


def _na_bias_tile(tbl_ref, hh, mask_ref):
    rows = []
    for j in range(2 * NA_QROWS):
        blk = jnp.concatenate([tbl_ref[hh, j - 2 * ii + NA_QROWS - 2] for ii in range(NA_QROWS // 2)], axis=1)
        if mask_ref is not None:
            blk = blk + mask_ref[0, j:j + 1, :]
        rows.append(blk)
    return jnp.concatenate(rows, axis=0)


def _softmax_pv_t(s_parts, vt_parts, extra=None):
    m = s_parts[0].max(axis=0, keepdims=True)
    for s in s_parts[1:]:
        m = jnp.maximum(m, s.max(axis=0, keepdims=True))
    if extra is not None:
        m = jnp.maximum(m, extra)
    acc = None
    for s, vt in zip(s_parts, vt_parts):
        pv = jnp.dot(vt, jnp.exp2((s - m).astype(BF16)), preferred_element_type=F32)
        acc = pv if acc is None else acc + pv
    den = acc[HEAD_DIM:HEAD_DIM + 1]
    if extra is not None:
        den = den + jnp.exp2(extra - m)
    return acc[:HEAD_DIM] / den


def _na_kernel(q_ref, k0, k1, k2, k3, vt0, vt1, vt2, vt3, kc_ref, vct_ref, edge_tbl_ref, tbl_ref, mask_ref, o_ref):
    def attend(tbl, mask):
        q = q_ref[...]
        k = jnp.concatenate([k0[...], k1[...], k2[...], k3[...]], axis=0)
        vt = jnp.concatenate([vt0[...], vt1[...], vt2[...], vt3[...]], axis=2)
        kc, vct = kc_ref[...], vct_ref[...]
        lane = _lane_iota()
        scores = []
        for hh in range(2):
            qh = jnp.where((lane // HEAD_DIM) == hh, q, jnp.zeros_like(q))
            s_loc = lax.dot_general(k, qh, NT_DIMS, preferred_element_type=F32) + _na_bias_tile(tbl, hh, mask)
            scores.append([s_loc, lax.dot_general(kc, qh, NT_DIMS, preferred_element_type=F32)])
        outs = [_softmax_pv_t(scores[hh], [vt[hh], vct[hh]]) for hh in range(2)]
        o_ref[...] = jnp.concatenate(outs, axis=0).T.astype(BF16)

    rb = pl.program_id(2)
    is_edge = (rb == 0) | (rb == pl.num_programs(2) - 1)

    @pl.when(jnp.logical_not(is_edge))
    def _():
        attend(tbl_ref, None)

    @pl.when(is_edge)
    def _():
        attend(edge_tbl_ref, mask_ref)


def _na_attention(p_lat, vt_lat, p_ctx, vt_ctx, bias_tables, nb, n, ctx_len, q_col, k_col, n_heads):
    tq = NA_QROWS * GRID_W
    tw = tq // 2
    qsteps = n // tq
    wsteps = n // tw
    edge_tbl, tbl, masks = bias_tables
    wblock = lambda rb, j: jnp.clip(2 * rb - 1 + j, 0, wsteps - 1)
    kwin = lambda j: pl.BlockSpec((tw, LANES), lambda hp, b, rb: (b * wsteps + wblock(rb, j), k_col + hp))
    vrows = HEAD_DIM + BF16_SUBLANES
    vwin = lambda j: pl.BlockSpec((2, vrows, tw), lambda hp, b, rb: (hp, 0, b * wsteps + wblock(rb, j)))
    variant = lambda rb: jnp.where(rb == 0, 0, jnp.where(rb == qsteps - 1, 2, 1))
    tbl_spec = pl.BlockSpec((2,) + tbl.shape[1:], lambda hp, b, rb: (hp, 0, 0, 0))
    in_specs = ([pl.BlockSpec((tq, LANES), lambda hp, b, rb: (b * qsteps + rb, q_col + hp))]
                + [kwin(j) for j in range(4)] + [vwin(j) for j in range(4)]
                + [pl.BlockSpec((ctx_len, LANES), lambda hp, b, rb: (b, k_col + hp)),
                   pl.BlockSpec((2, vrows, ctx_len), lambda hp, b, rb: (hp, 0, b)),
                   tbl_spec, tbl_spec,
                   pl.BlockSpec((1,) + masks.shape[1:], lambda hp, b, rb: (variant(rb), 0, 0))])
    return pl.pallas_call(
        _na_kernel,
        grid=(n_heads // 2, nb, qsteps),
        in_specs=in_specs,
        out_specs=pl.BlockSpec((tq, LANES), lambda hp, b, rb: (b * qsteps + rb, hp)),
        out_shape=jax.ShapeDtypeStruct((nb * n, n_heads * HEAD_DIM), BF16),
        compiler_params=_params(("parallel", "parallel", "arbitrary")),
        name="na_attention",
    )(*([p_lat] * 5 + [vt_lat] * 4 + [p_ctx, vt_ctx, edge_tbl, tbl, masks]))


def _gqa_kernel(*refs, n_q, n_kv, has_local, has_sink, n_tokens):
    refs = list(refs)
    q_ref = refs.pop(0)
    if has_local:
        k_loc = jnp.concatenate([refs.pop(0)[...] for _ in range(4)], axis=0)
        vt_loc = jnp.concatenate([refs.pop(0)[...] for _ in range(4)], axis=2)
    kc_ref, vct_ref = refs.pop(0), refs.pop(0)
    sink_ref = refs.pop(0) if has_sink else None
    o_ref = refs.pop(0)

    tq = q_ref.shape[0]
    group = n_q // n_kv
    lane = _lane_iota()
    if has_local:
        base = pl.program_id(1) * tq
        kpos = base - BLOCK + lax.broadcasted_iota(jnp.int32, (tq + 2 * BLOCK, 1), 0)
        qpos = base + lax.broadcasted_iota(jnp.int32, (1, tq), 1)
        ok = (jnp.abs(kpos - qpos) <= SWA_WINDOW) & (kpos >= 0) & (kpos < n_tokens)
        band = jnp.where(ok, 0.0, NEG)
        band = jnp.concatenate([band] * group, axis=1)

    outs = []
    for kvh in range(n_kv):
        cg, half = kvh // 2, kvh % 2
        qs = []
        for h in range(kvh * group, (kvh + 1) * group):
            qg = q_ref[:, (h // 2) * LANES:(h // 2 + 1) * LANES]
            if h % 2 != half:
                qg = _swap_halves(qg)
            qs.append(jnp.where((lane // HEAD_DIM) == half, qg, jnp.zeros_like(qg)))
        qcat = jnp.concatenate(qs, axis=0) if group > 1 else qs[0]
        kc = kc_ref[:, cg * LANES:(cg + 1) * LANES]
        s_parts = [lax.dot_general(kc, qcat, NT_DIMS, preferred_element_type=F32)]
        vt_parts = [vct_ref[kvh]]
        if has_local:
            s_parts.append(lax.dot_general(k_loc, qcat, NT_DIMS, preferred_element_type=F32) + band)
            vt_parts.append(vt_loc[kvh])
        extra = None
        if has_sink:
            extra = jnp.concatenate([sink_ref[h:h + 1, :] for h in range(kvh * group, (kvh + 1) * group)
                                     for _ in range(tq // LANES)], axis=1)
        o = _softmax_pv_t(s_parts, vt_parts, extra)
        outs += [o[:, g * tq:(g + 1) * tq] for g in range(group)]
    o_ref[...] = jnp.concatenate(outs, axis=0).T.astype(BF16)


def _gqa_attention(p_q, vt_q, p_ctx, vt_ctx, sink, nb, n, ctx_len, q_col, k_col, n_q, n_kv, has_local, tq):
    qsteps = n // tq
    kv_w = n_kv * HEAD_DIM
    q_w = n_q * HEAD_DIM
    vrows = HEAD_DIM + BF16_SUBLANES
    in_specs = [pl.BlockSpec((tq, q_w), lambda b, i: (b * qsteps + i, q_col * LANES // q_w))]
    args = [p_q]
    if has_local:
        assert tq == 2 * BLOCK and kv_w == LANES
        wsteps = n // BLOCK
        wblock = lambda i, j: jnp.clip(2 * i - 1 + j, 0, wsteps - 1)
        in_specs += [pl.BlockSpec((BLOCK, LANES), functools.partial(
            lambda b, i, j: (b * wsteps + wblock(i, j), k_col), j=j)) for j in range(4)]
        in_specs += [pl.BlockSpec((n_kv, vrows, BLOCK), functools.partial(
            lambda b, i, j: (0, 0, b * wsteps + wblock(i, j)), j=j)) for j in range(4)]
        args += [p_q] * 4 + [vt_q] * 4
    in_specs += [pl.BlockSpec((ctx_len, kv_w), lambda b, i: (b, k_col * LANES // kv_w)),
                 pl.BlockSpec((n_kv, vrows, ctx_len), lambda b, i: (0, 0, b))]
    args += [p_ctx, vt_ctx]
    if sink is not None:
        in_specs.append(_resident(sink.shape, lambda b, i: (0, 0)))
        args.append(sink)
    return pl.pallas_call(
        functools.partial(_gqa_kernel, n_q=n_q, n_kv=n_kv, has_local=has_local,
                          has_sink=sink is not None, n_tokens=n),
        grid=(nb, qsteps),
        in_specs=in_specs,
        out_specs=pl.BlockSpec((tq, q_w), lambda b, i: (b * qsteps + i, 0)),
        out_shape=jax.ShapeDtypeStruct((nb * n, q_w), BF16),
        compiler_params=_params(("parallel", "parallel")),
        name="gqa_attention",
    )(*args)


def _diff_kernel(*refs, n_kblocks, tk, lam_init):
    if n_kblocks:
        q_ref, k_ref, vt_ref, kc_ref, vct_ref, lam_ref, subg_ref, bound_ref, o_ref, m_sc, acc_sc, p_sc = refs
    else:
        q_ref, kc_ref, vct_ref, lam_ref, subg_ref, bound_ref, o_ref, m_sc, acc_sc = refs
    lane = _lane_iota()
    q = q_ref[...]
    zero = jnp.zeros_like(q)
    q_maps = [jnp.where(lane < HEAD_DIM, q, zero), jnp.where(lane >= HEAD_DIM, q, zero)]

    kc, vct = kc_ref[...], vct_ref[0]
    for i in range(2):
        s = lax.dot_general(kc, q_maps[i], NT_DIMS, preferred_element_type=F32)
        m = s.max(axis=0, keepdims=True)
        p = jnp.exp2((s - m).astype(BF16))
        m_sc[i] = m
        acc_sc[i] = jnp.dot(vct, p, preferred_element_type=F32)

    if n_kblocks:
        def scores(kb):
            k = k_ref[pl.ds(pl.multiple_of(kb * tk, tk), tk), :]
            return [lax.dot_general(k, q_maps[i], NT_DIMS, preferred_element_type=F32) for i in range(2)]

        def probs(kb, slot):
            s = scores(kb)
            excess = None
            for i in range(2):
                m_ref = m_sc[i]
                p_sc[slot, i] = jnp.exp2((s[i] - m_ref).astype(BF16))
                over = jnp.max(s[i].max(axis=0, keepdims=True) - m_ref)
                excess = over if excess is None else jnp.maximum(excess, over)
            return excess

        def settle(kb, slot, excess):
            @pl.when(excess > RESCALE_MARGIN)
            def _():
                s = scores(kb)
                for i in range(2):
                    m_prev = m_sc[i]
                    m_new = jnp.maximum(m_prev, s[i].max(axis=0, keepdims=True))
                    p_sc[slot, i] = jnp.exp2((s[i] - m_new).astype(BF16))
                    acc_sc[i] = jnp.exp2(m_prev - m_new) * acc_sc[i]
                    m_sc[i] = m_new

        def accumulate(kb, slot):
            vt = vt_ref[0, :, pl.ds(pl.multiple_of(kb * tk, tk), tk)]
            for i in range(2):
                acc_sc[i] += jnp.dot(vt, p_sc[slot, i], preferred_element_type=F32)

        def probs_unchecked(kb, slot):
            s = scores(kb)
            for i in range(2):
                p_sc[slot, i] = jnp.exp2((s[i] - m_sc[i]).astype(BF16))

        lowest_ref = jnp.minimum(jnp.min(m_sc[0]), jnp.min(m_sc[1]))
        never_rescales = bound_ref[0, 0] - lowest_ref <= RESCALE_MARGIN
        first_excess = probs(0, 0)

        @pl.when(never_rescales)
        def _():
            def body(kb, carry):
                accumulate(kb - 1, (kb - 1) % 2)
                probs_unchecked(kb, kb % 2)
                return carry

            lax.fori_loop(1, n_kblocks, body, 0)
            accumulate(n_kblocks - 1, (n_kblocks - 1) % 2)

        @pl.when(jnp.logical_not(never_rescales))
        def _():
            def body(kb, excess):
                settle(kb - 1, (kb - 1) % 2, excess)
                accumulate(kb - 1, (kb - 1) % 2)
                return probs(kb, kb % 2)

            excess = lax.fori_loop(1, n_kblocks, body, first_excess)
            settle(n_kblocks - 1, (n_kblocks - 1) % 2, excess)
            accumulate(n_kblocks - 1, (n_kblocks - 1) % 2)

    lp = lam_ref[...]
    lam = (jnp.exp(jnp.sum(lp[0:1] * lp[1:2], axis=-1, keepdims=True))
           - jnp.exp(jnp.sum(lp[2:3] * lp[3:4], axis=-1, keepdims=True)) + lam_init)
    dv = 2 * HEAD_DIM
    y0 = acc_sc[0, :dv] / acc_sc[0, dv:dv + 1]
    y1 = acc_sc[1, :dv] / acc_sc[1, dv:dv + 1]
    y = (y0 - lam * y1).T
    ms = jnp.mean(y * y, axis=-1, keepdims=True)
    y = y * lax.rsqrt(ms + EPS) * subg_ref[...] * (1.0 - lam_init)
    o_ref[...] = y.astype(BF16)


def _diff_attention(p_q, p_lat, vt_lat, p_ctx, vt_ctx, lam_rows, subg, score_bound, nb, n_q, n_lat, ctx_len,
                    q_col, k_col, n_heads, lam_init, tq, tk):
    qsteps = n_q // tq
    has_latent = p_lat is not None
    vrows = 2 * HEAD_DIM + BF16_SUBLANES
    in_specs = [pl.BlockSpec((tq, LANES), lambda b, h, i: (b * qsteps + i, q_col + h))]
    args = [p_q]
    if has_latent:
        in_specs += [pl.BlockSpec((n_lat, LANES), lambda b, h, i: (b, k_col + h)),
                     pl.BlockSpec((1, vrows, n_lat), lambda b, h, i: (h, 0, b))]
        args += [p_lat, vt_lat]
    in_specs += [pl.BlockSpec((ctx_len, LANES), lambda b, h, i: (b, k_col + h)),
                 pl.BlockSpec((1, vrows, ctx_len), lambda b, h, i: (h, 0, b)),
                 _resident(lam_rows.shape, lambda b, h, i: (0, 0)),
                 _resident(subg.shape, lambda b, h, i: (0, 0)),
                 pl.BlockSpec(memory_space=pltpu.SMEM)]
    args += [p_ctx, vt_ctx, lam_rows, subg, score_bound]
    return pl.pallas_call(
        functools.partial(_diff_kernel, n_kblocks=n_lat // tk if has_latent else 0, tk=tk, lam_init=lam_init),
        grid=(nb, n_heads, qsteps),
        in_specs=in_specs,
        out_specs=pl.BlockSpec((tq, LANES), lambda b, h, i: (b * qsteps + i, h)),
        out_shape=jax.ShapeDtypeStruct((nb * n_q, n_heads * LANES), BF16),
        scratch_shapes=([pltpu.VMEM((2, 1, tq), F32), pltpu.VMEM((2, vrows, tq), F32)]
                        + ([pltpu.VMEM((2, 2, tk, tq), BF16)] if has_latent else [])),
        compiler_params=_params(("parallel", "parallel", "arbitrary")),
        name="diff_attention",
    )(*args)


def _silu(x):
    return x * (1.0 / (1.0 + jnp.exp(-x)))


def _ffn_kernel(x_ref, g_ref, sc_ref, sh_ref, gate_ref, w1_ref, w3_ref, w2_ref, o_ref):
    x = x_ref[...]
    a = _modulated_norm(x, g_ref[...], sc_ref[0], sh_ref[0]).astype(BF16)
    h1 = jnp.dot(a, w1_ref[...], preferred_element_type=F32)
    h3 = jnp.dot(a, w3_ref[...], preferred_element_type=F32)
    y = jnp.dot((_silu(h1) * h3).astype(BF16), w2_ref[...], preferred_element_type=F32)
    o_ref[...] = x + gate_ref[0] * y


def _ffn(h, nb, g, sc, sh, gate, w1, w3, w2, tm):
    t, d = h.shape
    steps = t // nb // tm
    row = lambda b, i: (b * steps + i, 0)
    vec = pl.BlockSpec((1, 1, d), lambda b, i: (b, 0, 0))
    return pl.pallas_call(
        _ffn_kernel,
        grid=(nb, steps),
        in_specs=[pl.BlockSpec((tm, d), row), _resident((1, d), lambda b, i: (0, 0)), vec, vec, vec,
                  _resident(w1.shape, lambda b, i: (0, 0)),
                  _resident(w3.shape, lambda b, i: (0, 0)),
                  _resident(w2.shape, lambda b, i: (0, 0))],
        out_specs=pl.BlockSpec((tm, d), row),
        out_shape=jax.ShapeDtypeStruct((t, d), F32),
        compiler_params=_params(("parallel", "parallel")),
        name="ffn",
    )(h, g, sc, sh, gate, w1, w3, w2)


def _top2_gates(logits):
    lane = _lane_iota()
    big = jnp.int32(LANES)
    lg = jnp.where(lane < N_EXPERTS, logits, -jnp.inf)
    m1 = lg.max(axis=-1, keepdims=True)
    i1 = jnp.where(lg == m1, lane, big).min(axis=-1, keepdims=True)
    rest = jnp.where(lane == i1, -jnp.inf, lg)
    m2 = rest.max(axis=-1, keepdims=True)
    i2 = jnp.where(rest == m2, lane, big).min(axis=-1, keepdims=True)
    e2 = jnp.exp(m2 - m1)
    den = 1.0 + e2
    return jnp.where(lane == i1, 1.0 / den, 0.0) + jnp.where(lane == i2, e2 / den, 0.0)


def _moe_kernel(x_ref, g_ref, sc_ref, sh_ref, gate_ref, r_ref, w1_ref, w3_ref, w2_ref, o_ref,
                a_sc, gates_sc, rank_sc, acc_sc, *, chunk):
    e = pl.program_id(2)
    tm = x_ref.shape[0]

    @pl.when(e == 0)
    def _():
        a = _modulated_norm(x_ref[...], g_ref[...], sc_ref[0], sh_ref[0])
        a_hi = a.astype(BF16)
        a_lo = (a - a_hi.astype(F32)).astype(BF16)
        r = r_ref[...]
        r_hi = r.astype(BF16)
        r_lo = (r - r_hi.astype(F32)).astype(BF16)
        logits = (jnp.dot(a_hi, r_hi, preferred_element_type=F32)
                  + jnp.dot(a_hi, r_lo, preferred_element_type=F32)
                  + jnp.dot(a_lo, r_hi, preferred_element_type=F32))
        a_sc[...] = a_hi
        gates_t = _top2_gates(logits).T[:EXPERT_ROWS]
        gates_sc[...] = gates_t
        before = (lax.broadcasted_iota(jnp.int32, (tm, tm), 0)
                  < lax.broadcasted_iota(jnp.int32, (tm, tm), 1))
        routed = jnp.where(gates_t > 0.0, 1.0, 0.0).astype(BF16)
        rank_sc[...] = jnp.dot(routed, jnp.where(before, 1.0, 0.0).astype(BF16), preferred_element_type=F32)
        acc_sc[...] = jnp.zeros_like(acc_sc)

    gate_row = gates_sc[pl.ds(e, 1), :]
    rank_row = jnp.where(gate_row > 0.0, rank_sc[pl.ds(e, 1), :], -1.0)
    count = jnp.sum(jnp.where(gate_row > 0.0, 1.0, 0.0)).astype(jnp.int32)
    slot = lax.broadcasted_iota(jnp.int32, (chunk, 1), 0).astype(F32)

    for j in range(pl.cdiv(tm, chunk)):
        @pl.when(count > j * chunk)
        def _():
            pick = rank_row == (slot + float(j * chunk))
            pick_f = jnp.where(pick, 1.0, 0.0)
            xs = jnp.dot(pick_f.astype(BF16), a_sc[...], preferred_element_type=F32).astype(BF16)
            h1 = jnp.dot(xs, w1_ref[0], preferred_element_type=F32)
            h3 = jnp.dot(xs, w3_ref[0], preferred_element_type=F32)
            y = jnp.dot((_silu(h1) * h3).astype(BF16), w2_ref[0], preferred_element_type=F32)
            y = y * jnp.sum(pick_f * gate_row, axis=-1, keepdims=True)
            acc_sc[...] += lax.dot_general(pick_f.astype(BF16), y.astype(BF16), (((0,), (0,)), ((), ())),
                                           preferred_element_type=F32)

    @pl.when(e == pl.num_programs(2) - 1)
    def _():
        o_ref[...] = x_ref[...] + gate_ref[0] * acc_sc[...]


def _moe(h, nb, g, sc, sh, gate, router, w1, w3, w2, tm, chunk):
    t, d = h.shape
    steps = t // nb // tm
    n_e, _, f = w1.shape
    row = lambda b, i, e: (b * steps + i, 0)
    vec = pl.BlockSpec((1, 1, d), lambda b, i, e: (b, 0, 0))
    return pl.pallas_call(
        functools.partial(_moe_kernel, chunk=chunk),
        grid=(nb, steps, n_e),
        in_specs=[pl.BlockSpec((tm, d), row), _resident((1, d), lambda b, i, e: (0, 0)), vec, vec, vec,
                  _resident(router.shape, lambda b, i, e: (0, 0)),
                  pl.BlockSpec((1, d, f), lambda b, i, e: (e, 0, 0)),
                  pl.BlockSpec((1, d, f), lambda b, i, e: (e, 0, 0)),
                  pl.BlockSpec((1, f, d), lambda b, i, e: (e, 0, 0))],
        out_specs=pl.BlockSpec((tm, d), row),
        out_shape=jax.ShapeDtypeStruct((t, d), F32),
        scratch_shapes=[pltpu.VMEM((tm, d), BF16), pltpu.VMEM((EXPERT_ROWS, tm), F32),
                        pltpu.VMEM((EXPERT_ROWS, tm), F32), pltpu.VMEM((tm, d), F32)],
        compiler_params=_params(("parallel", "parallel", "arbitrary")),
        name="moe",
    )(h, g, sc, sh, gate, router, w1, w3, w2)


def _rope_tables(n):
    t = jnp.arange(n)
    pos = jnp.stack([t // GRID_W, t % GRID_W], -1).astype(F32)
    nq = HEAD_DIM // 4
    inv = ROPE_THETA ** (-jnp.arange(nq, dtype=F32) / nq)
    ang = pos[:, :, None] * inv
    cos = jnp.repeat(jnp.cos(ang)[:, :, None, :], 2, axis=2)
    sin = jnp.stack([-jnp.sin(ang), jnp.sin(ang)], axis=2)
    cos = jnp.tile(cos.reshape(n, HEAD_DIM), (1, LANES // HEAD_DIM))
    sin = jnp.tile(sin.reshape(n, HEAD_DIM), (1, LANES // HEAD_DIM))
    return cos, sin


def _head_gain(parts, n_out):
    row = jnp.ones((n_out,), F32)
    for col, n_heads, gain, scale in parts:
        row = lax.dynamic_update_slice(row, jnp.tile(gain.astype(F32) * scale, n_heads), (col,))
    return row.reshape(1, n_out)


def kernel(x, c, ctx, c_ctx, ada_w, ada_b, norm1_g, norm2_g, ev_w_in, ev_conv_w, ev_q_g, ev_k_g, ev_rpb,
           ev_w_out, ffn_w1, ffn_w3, ffn_w2, od_w_in, od_cq_g, od_ck_g, od_sink, od_dq_g, od_dk_g,
           od_lam_q1, od_lam_k1, od_lam_q2, od_lam_k2, od_subln_g, od_w_out, moe_router,
           moe_w1, moe_w3, moe_w2):
    nb, n, d = x.shape
    ctx_len = ctx.shape[1]
    depth = ada_w.shape[0]
    n_slots = d // HEAD_DIM
    conv_ch = d // 2
    na_heads = swa_heads = n_slots // 2
    swa_kv = max(1, swa_heads // 4)
    diff_heads = n_slots // 4
    rows = n // GRID_W
    assert rows % NA_QROWS == 0 and rows >= 2 * NA_QROWS and n % 512 == 0
    assert nb + 1 <= 8 and ctx_len % LANES == 0

    ev_q_col = 3 * conv_ch
    ev_k_col = ev_q_col + na_heads * HEAD_DIM
    ev_v_col = ev_k_col + na_heads * HEAD_DIM
    ev_n = ev_v_col + na_heads * HEAD_DIM
    od_dq_col = swa_heads * HEAD_DIM
    od_ck_col = od_dq_col + diff_heads * 2 * HEAD_DIM
    od_cv_col = od_ck_col + swa_kv * HEAD_DIM
    od_dk_col = od_cv_col + swa_kv * HEAD_DIM
    od_dv_col = od_dk_col + diff_heads * 2 * HEAD_DIM
    od_n = od_dv_col + diff_heads * 2 * HEAD_DIM
    ev_segs = ((0, ev_q_col, False, 0), (ev_q_col, ev_v_col, True, 0), (ev_v_col, ev_n, False, na_heads))
    od_segs = ((0, od_cv_col, True, 0), (od_cv_col, od_dk_col, False, swa_kv), (od_dk_col, od_dv_col, True, 0),
               (od_dv_col, od_n, False, diff_heads))

    h = x.reshape(nb * n, d)
    hc = ctx.reshape(nb * ctx_len, d)
    tm_lat = 512
    tm_ctx = ctx_len
    tm_moe = 1024

    s_rows = jnp.zeros((8, d), F32).at[:nb].set(c).at[nb].set(c_ctx)
    mod = _modulation(s_rows, ada_w, ada_b)
    rope_tabs = _rope_tables(n)

    def lat_vec(l, k):
        return mod[l, :nb, k * d:(k + 1) * d].reshape(nb, 1, d)

    def ctx_vec(l, k, copies):
        return jnp.broadcast_to(mod[l, nb, k * d:(k + 1) * d], (copies, 1, d))

    for l in range(depth):
        last = l == depth - 1
        i = l // 2
        g1 = norm1_g[l].reshape(1, d)
        g2 = norm2_g[l].reshape(1, d)
        if l % 2 == 0:
            w_in = ev_w_in[i].astype(BF16)
            gain = _head_gain([(ev_q_col, na_heads, ev_q_g[i], QK_SCALE * LOG2_E), (ev_k_col, na_heads, ev_k_g[i], 1.0)],
                              ev_n)
            p_lat, vt_lat = _norm_proj(h, nb, g1, lat_vec(l, 1), lat_vec(l, 0), w_in, gain, ev_segs, None, tm_lat)
            p_ctx, vt_ctx = _norm_proj(hc, nb, g1, ctx_vec(l, 1, nb), ctx_vec(l, 0, nb), w_in, gain, ev_segs, None,
                                       tm_ctx)
            bias = _na_bias_tables(ev_rpb[i], rows)
            y_na = _na_attention(p_lat, vt_lat, p_ctx, vt_ctx, bias, nb, n, ctx_len, ev_q_col // LANES,
                                 ev_k_col // LANES, na_heads)
            w_out = ev_w_out[i].astype(BF16)
            conv_w = ev_conv_w[i]
            h = _out_proj(h, nb, lat_vec(l, 2), p_lat, y_na, w_out, conv_w, tm_lat)
            if not last:
                y_na_c = _gqa_attention(p_ctx, None, p_ctx, vt_ctx, None, nb, ctx_len, ctx_len, ev_q_col // LANES,
                                        ev_k_col // LANES, na_heads, na_heads, False, ctx_len)
                hc = _out_proj(hc, nb, ctx_vec(l, 2, nb), p_ctx, y_na_c, w_out, conv_w, tm_ctx)
            w1, w3, w2 = ffn_w1[i].astype(BF16), ffn_w3[i].astype(BF16), ffn_w2[i].astype(BF16)
            h = _ffn(h, nb, g2, lat_vec(l, 4), lat_vec(l, 3), lat_vec(l, 5), w1, w3, w2, tm_lat)
            if not last:
                hc = _ffn(hc, 1, g2, ctx_vec(l, 4, 1), ctx_vec(l, 3, 1), ctx_vec(l, 5, 1), w1, w3, w2, tm_lat)
        else:
            lam_init = 0.8 - 0.6 * math.exp(-0.3 * l)
            w_in = od_w_in[i].astype(BF16)
            gain = _head_gain([(0, swa_heads, od_cq_g[i], QK_SCALE * LOG2_E),
                               (od_dq_col, 2 * diff_heads, od_dq_g[i], QK_SCALE * LOG2_E),
                               (od_ck_col, swa_kv, od_ck_g[i], 1.0),
                               (od_dk_col, 2 * diff_heads, od_dk_g[i], 1.0)], od_n)
            p_lat, cvt_lat, vt_lat = _norm_proj(h, nb, g1, lat_vec(l, 1), lat_vec(l, 0), w_in, gain, od_segs, rope_tabs,
                                                tm_lat)
            p_ctx, cvt_ctx, vt_ctx = _norm_proj(hc, nb, g1, ctx_vec(l, 1, nb), ctx_vec(l, 0, nb), w_in, gain, od_segs,
                                                None, tm_ctx)
            sink = jnp.broadcast_to(od_sink[i].astype(F32)[:, None] * LOG2_E, (swa_heads, LANES))
            lam_rows = jnp.zeros((8, LANES), F32).at[:4, :HEAD_DIM].set(
                jnp.stack([od_lam_q1[i], od_lam_k1[i], od_lam_q2[i], od_lam_k2[i]]).astype(F32))
            subg = od_subln_g[i].astype(F32).reshape(1, 2 * HEAD_DIM)
            cols = (od_dq_col // LANES, od_dk_col // LANES)
            score_bound = (HEAD_DIM * QK_SCALE * LOG2_E * BF16_NORM_SLACK * jnp.max(jnp.abs(od_dq_g[i]))
                           * jnp.max(jnp.abs(od_dk_g[i]))).astype(F32).reshape(1, 1)
            y_c = _gqa_attention(p_lat, cvt_lat, p_ctx, cvt_ctx, sink, nb, n, ctx_len, 0, od_ck_col // LANES,
                                 swa_heads, swa_kv, True, 2 * BLOCK)
            y_d = _diff_attention(p_lat, p_lat, vt_lat, p_ctx, vt_ctx, lam_rows, subg, score_bound, nb, n, n, ctx_len,
                                  *cols, diff_heads, lam_init, 2048, 1024)
            w_out = od_w_out[i].astype(BF16)
            h = _out_proj(h, nb, lat_vec(l, 2), y_c, y_d, w_out, None, tm_lat)
            if not last:
                y_c_c = _gqa_attention(p_ctx, None, p_ctx, cvt_ctx, sink, nb, ctx_len, ctx_len, 0,
                                       od_ck_col // LANES, swa_heads, swa_kv, False, ctx_len)
                y_d_c = _diff_attention(p_ctx, None, None, p_ctx, vt_ctx, lam_rows, subg, score_bound, nb, ctx_len, 0,
                                        ctx_len, *cols, diff_heads, lam_init, ctx_len, ctx_len)
                hc = _out_proj(hc, nb, ctx_vec(l, 2, nb), y_c_c, y_d_c, w_out, None, tm_ctx)
            router = jnp.zeros((d, LANES), F32).at[:, :N_EXPERTS].set(moe_router[i])
            w1, w3, w2 = moe_w1[i].astype(BF16), moe_w3[i].astype(BF16), moe_w2[i].astype(BF16)
            h = _moe(h, nb, g2, lat_vec(l, 4), lat_vec(l, 3), lat_vec(l, 5), router, w1, w3, w2, tm_moe, MOE_CHUNK)
            if not last:
                hc = _moe(hc, 1, g2, ctx_vec(l, 4, 1), ctx_vec(l, 3, 1), ctx_vec(l, 5, 1), router, w1, w3, w2,
                          min(tm_moe, nb * ctx_len), MOE_CHUNK)
    return h.reshape(nb, n, d)
```

```python
import functools
import math

import jax
import jax.numpy as jnp
from jax import lax
from jax.experimental import pallas as pl
from jax.experimental.pallas import tpu as pltpu

F32 = jnp.float32
BF16 = jnp.bfloat16

LANES = 128
BF16_SUBLANES = 16
VMEM_LIMIT = 56 * 1024 * 1024

HEAD_DIM = 64
GRID_W = 64
CONV_W = 3
NA_ROWS = 8
NA_COLS = 16
NA_QROWS = 8
SWA_WINDOW = 128
MOE_CHUNK = 256
BLOCK = 128
N_EXPERTS = 8
EXPERT_ROWS = 16
ROPE_THETA = 10000.0
EPS = 1e-6
NEG = -1e30
QK_SCALE = HEAD_DIM ** -0.5
LOG2_E = math.log2(math.e)
BF16_NORM_SLACK = 1.02
RESCALE_MARGIN = 32.0

NT_DIMS = (((1,), (1,)), ((), ()))


def _params(sem):
    return pltpu.CompilerParams(dimension_semantics=sem, vmem_limit_bytes=VMEM_LIMIT)


def _resident(shape, index_map):
    return pl.BlockSpec(shape, index_map, pipeline_mode=pl.Buffered(1))


def _lane_iota():
    return lax.broadcasted_iota(jnp.int32, (1, LANES), 1)


def _swap_halves(x):
    return jnp.concatenate([x[:, HEAD_DIM:], x[:, :HEAD_DIM]], axis=1)


def _modulated_norm(x, g, sc, sh):
    ms = jnp.mean(x * x, axis=-1, keepdims=True)
    return (x * lax.rsqrt(ms + EPS)) * (g * (1.0 + sc)) + sh


def _mod_kernel(s_ref, w_ref, b_ref, o_ref):
    s = s_ref[...]
    s = s * (1.0 / (1.0 + jnp.exp(-s)))
    o_ref[0] = jnp.dot(s.astype(BF16), w_ref[0].astype(BF16), preferred_element_type=F32) + b_ref[0]


def _modulation(s_rows, ada_w, ada_b):
    depth, d, n_out = ada_w.shape
    tn = n_out // 4
    return pl.pallas_call(
        _mod_kernel,
        grid=(depth, n_out // tn),
        in_specs=[pl.BlockSpec(s_rows.shape, lambda l, j: (0, 0)),
                  pl.BlockSpec((1, d, tn), lambda l, j: (l, 0, j)),
                  pl.BlockSpec((1, 1, tn), lambda l, j: (l, 0, j))],
        out_specs=pl.BlockSpec((1, s_rows.shape[0], tn), lambda l, j: (l, 0, j)),
        out_shape=jax.ShapeDtypeStruct((depth, s_rows.shape[0], n_out), F32),
        compiler_params=_params(("arbitrary", "arbitrary")),
        name="modulation",
    )(s_rows, ada_w, ada_b.reshape(depth, 1, n_out))


def _head_sumsq(z):
    r = lax.broadcasted_iota(jnp.int32, (LANES, LANES), 0) // HEAD_DIM
    c = lax.broadcasted_iota(jnp.int32, (LANES, LANES), 1) // HEAD_DIM
    same_head = jnp.where(r == c, 1.0, 0.0).astype(BF16)
    z2 = z * z
    hi = z2.astype(BF16)
    lo = (z2 - hi.astype(F32)).astype(BF16)
    return (jnp.dot(hi, same_head, preferred_element_type=F32)
            + jnp.dot(lo, same_head, preferred_element_type=F32))


def _norm_proj_kernel(*refs, segs, rope, n_vt):
    refs = list(refs)
    vt_refs = [refs.pop() for _ in range(n_vt)][::-1]
    if rope:
        x_ref, g_ref, sc_ref, sh_ref, w_ref, gain_ref, cos_ref, sin_ref, o_ref = refs
    else:
        x_ref, g_ref, sc_ref, sh_ref, w_ref, gain_ref, o_ref = refs
    a = _modulated_norm(x_ref[...], g_ref[...], sc_ref[0], sh_ref[0]).astype(BF16)
    first_half = (_lane_iota() % (HEAD_DIM // 2)) < (HEAD_DIM // 4)
    for c0, c1, normed, vt_heads in segs:
        acc = jnp.dot(a, w_ref[:, c0:c1], preferred_element_type=F32)
        if vt_heads:
            vt_ref = vt_refs.pop(0)
            dv = (c1 - c0) // vt_heads
            vt = acc.T.astype(BF16)
            for hh in range(vt_heads):
                vt_ref[hh, :dv, :] = vt[hh * dv:(hh + 1) * dv]
                vt_ref[hh, dv:, :] = jnp.ones((BF16_SUBLANES, vt.shape[1]), BF16)
        if not normed:
            o_ref[:, c0:c1] = acc.astype(BF16)
            continue
        for j in range((c1 - c0) // LANES):
            z = acc[:, j * LANES:(j + 1) * LANES]
            lo = c0 + j * LANES
            z = z * lax.rsqrt(_head_sumsq(z) * (1.0 / HEAD_DIM) + EPS) * gain_ref[:, lo:lo + LANES]
            if rope:
                partner = jnp.where(first_half,
                                    pltpu.roll(z, LANES - HEAD_DIM // 4, axis=1),
                                    pltpu.roll(z, HEAD_DIM // 4, axis=1))
                z = z * cos_ref[...] + partner * sin_ref[...]
            o_ref[:, lo:lo + LANES] = z.astype(BF16)


def _norm_proj(h, nb, g, sc, sh, w, gain, segs, rope_tabs, tm):
    t, d = h.shape
    n_out = w.shape[1]
    steps = t // nb // tm
    in_specs = [pl.BlockSpec((tm, d), lambda b, i: (b * steps + i, 0)),
                _resident((1, d), lambda b, i: (0, 0)),
                pl.BlockSpec((1, 1, d), lambda b, i: (b, 0, 0)),
                pl.BlockSpec((1, 1, d), lambda b, i: (b, 0, 0)),
                _resident((d, n_out), lambda b, i: (0, 0)),
                _resident((1, n_out), lambda b, i: (0, 0))]
    args = [h, g, sc, sh, w, gain]
    if rope_tabs is not None:
        in_specs += [pl.BlockSpec((tm, LANES), lambda b, i: (i, 0))] * 2
        args += list(rope_tabs)
    out_specs = [pl.BlockSpec((tm, n_out), lambda b, i: (b * steps + i, 0))]
    out_shape = [jax.ShapeDtypeStruct((t, n_out), BF16)]
    for c0, c1, _, vt_heads in segs:
        if vt_heads:
            rows = (c1 - c0) // vt_heads + BF16_SUBLANES
            out_specs.append(pl.BlockSpec((vt_heads, rows, tm), lambda b, i: (0, 0, b * steps + i)))
            out_shape.append(jax.ShapeDtypeStruct((vt_heads, rows, t), BF16))
    return pl.pallas_call(
        functools.partial(_norm_proj_kernel, segs=segs, rope=rope_tabs is not None, n_vt=len(out_specs) - 1),
        grid=(nb, steps),
        in_specs=in_specs,
        out_specs=out_specs,
        out_shape=out_shape,
        compiler_params=_params(("parallel", "parallel")),
        name="norm_proj",
    )(*args)


def _gated_conv(gb_ref, gc_ref, u_ref, gcp_ref, up_ref, gcn_ref, un_ref, cw_ref):
    i, steps = pl.program_id(1), pl.num_programs(1)
    v = gc_ref[...].astype(F32) * u_ref[...].astype(F32)
    tm = v.shape[0]
    last = BF16_SUBLANES - 1
    prev_row = gcp_ref[last:last + 1, :].astype(F32) * up_ref[last:last + 1, :].astype(F32)
    next_row = gcn_ref[0:1, :].astype(F32) * un_ref[0:1, :].astype(F32)
    prev_row = jnp.where(i > 0, prev_row, 0.0)
    next_row = jnp.where(i < steps - 1, next_row, 0.0)
    row = lax.broadcasted_iota(jnp.int32, (tm, 1), 0)
    v_prev = jnp.where(row == 0, prev_row, pltpu.roll(v, 1, axis=0))
    v_next = jnp.where(row == tm - 1, next_row, pltpu.roll(v, tm - 1, axis=0))
    cw = cw_ref[...]
    conv = cw[0:1, :] * v_prev + cw[1:2, :] * v + cw[2:3, :] * v_next
    return gb_ref[...].astype(F32) * conv


def _out_proj_kernel(*refs, conv):
    if conv:
        (gb_ref, gc_ref, u_ref, gcp_ref, up_ref, gcn_ref, un_ref, cw_ref,
         yb_ref, wa_ref, wb_ref, h_ref, gate_ref, o_ref) = refs
        ya = _gated_conv(gb_ref, gc_ref, u_ref, gcp_ref, up_ref, gcn_ref, un_ref, cw_ref).astype(BF16)
    else:
        ya_ref, yb_ref, wa_ref, wb_ref, h_ref, gate_ref, o_ref = refs
        ya = ya_ref[...]
    y = (jnp.dot(ya, wa_ref[...], preferred_element_type=F32)
         + jnp.dot(yb_ref[...], wb_ref[...], preferred_element_type=F32))
    o_ref[...] = h_ref[...] + gate_ref[0] * y


def _out_proj(h, nb, gate, ya_src, yb, w_out, conv_w, tm):
    t, d = h.shape
    steps = t // nb // tm
    wa_rows = w_out.shape[0] - yb.shape[1]
    w_a, w_b = w_out[:wa_rows], w_out[wa_rows:]
    row = lambda b, i: (b * steps + i, 0)
    if conv_w is not None:
        cc = conv_w.shape[1]
        hb = tm // BF16_SUBLANES
        n_halo = t // BF16_SUBLANES
        prev = lambda col: (lambda b, i: (jnp.maximum((b * steps + i) * hb - 1, 0), col))
        nxt = lambda col: (lambda b, i: (jnp.minimum((b * steps + i + 1) * hb, n_halo - 1), col))
        in_specs = [pl.BlockSpec((tm, cc), lambda b, i: (b * steps + i, 0)),
                    pl.BlockSpec((tm, cc), lambda b, i: (b * steps + i, 1)),
                    pl.BlockSpec((tm, cc), lambda b, i: (b * steps + i, 2)),
                    pl.BlockSpec((BF16_SUBLANES, cc), prev(1)),
                    pl.BlockSpec((BF16_SUBLANES, cc), prev(2)),
                    pl.BlockSpec((BF16_SUBLANES, cc), nxt(1)),
                    pl.BlockSpec((BF16_SUBLANES, cc), nxt(2)),
                    _resident(conv_w.shape, lambda b, i: (0, 0))]
        args = [ya_src] * 7 + [conv_w]
    else:
        in_specs = [pl.BlockSpec((tm, wa_rows), row)]
        args = [ya_src]
    in_specs += [pl.BlockSpec((tm, yb.shape[1]), row),
                 _resident(w_a.shape, lambda b, i: (0, 0)),
                 _resident(w_b.shape, lambda b, i: (0, 0)),
                 pl.BlockSpec((tm, d), row),
                 pl.BlockSpec((1, 1, d), lambda b, i: (b, 0, 0))]
    args += [yb, w_a, w_b, h, gate]
    return pl.pallas_call(
        functools.partial(_out_proj_kernel, conv=conv_w is not None),
        grid=(nb, steps),
        in_specs=in_specs,
        out_specs=pl.BlockSpec((tm, d), row),
        out_shape=jax.ShapeDtypeStruct((t, d), F32),
        compiler_params=_params(("parallel", "parallel")),
        name="out_proj",
    )(*args)


def _na_bias_tables(rpb, rows):
    n_heads = rpb.shape[0]
    n_dr = 2 * NA_ROWS - 1
    c = jnp.arange(GRID_W)
    c0 = jnp.clip(c - NA_COLS // 2, 0, GRID_W - NA_COLS)
    col_ok = (c[None, :] >= c0[:, None]) & (c[None, :] < c0[:, None] + NA_COLS)
    dc = jnp.clip(c[None, :] - c[:, None] + NA_COLS - 1, 0, 2 * NA_COLS - 2)
    pick_dc = ((dc[None] == jnp.arange(2 * NA_COLS - 1)[:, None, None]) & col_ok[None]).astype(F32)
    table = jnp.einsum('hrd,dcx->hrxc', rpb.astype(F32), pick_dc, precision=lax.Precision.HIGHEST)
    table = jnp.where(col_ok.T[None, None], table * LOG2_E, NEG)

    def paired(tab):
        pad = jnp.full((n_heads, NA_ROWS // 2, GRID_W, GRID_W), NEG, F32)
        ext = jnp.concatenate([pad, tab, pad], axis=1)
        return jnp.concatenate([ext[:, 1:], ext[:, :-1]], axis=-1)

    dr = jnp.arange(n_dr)
    in_window = (dr >= NA_ROWS // 2 - 1) & (dr < NA_ROWS // 2 - 1 + NA_ROWS)
    interior = jnp.where(in_window[None, :, None, None], table, NEG)

    nblk = rows // NA_QROWS
    i = jnp.arange(NA_QROWS)
    j = jnp.arange(2 * NA_QROWS)
    masks = []
    for rb in (0, 1, nblk - 1):
        r = rb * NA_QROWS + i
        r0 = jnp.clip(r - NA_ROWS // 2, 0, rows - NA_ROWS)
        rk = rb * NA_QROWS - NA_ROWS // 2 + j
        ok = (rk[:, None] >= r0[None, :]) & (rk[:, None] < r0[None, :] + NA_ROWS)
        masks.append(jnp.repeat(jnp.where(ok, 0.0, NEG), GRID_W, axis=1))
    return paired(table), paired(interior), jnp.stack(masks)


def _na_bias_tile(tbl_ref, hh, mask_ref):
    rows = []
    for j in range(2 * NA_QROWS):
        blk = jnp.concatenate([tbl_ref[hh, j - 2 * ii + NA_QROWS - 2] for ii in range(NA_QROWS // 2)], axis=1)
        if mask_ref is not None:
            blk = blk + mask_ref[0, j:j + 1, :]
        rows.append(blk)
    return jnp.concatenate(rows, axis=0)


def _softmax_pv_t(s_parts, vt_parts, extra=None, exact_parts=None):
    n_exact = len(s_parts) if exact_parts is None else exact_parts
    m = s_parts[0].max(axis=0, keepdims=True)
    for s in s_parts[1:n_exact]:
        m = jnp.maximum(m, s.max(axis=0, keepdims=True))
    if extra is not None:
        m = jnp.maximum(m, extra)
    acc = None
    for s, vt in zip(s_parts, vt_parts):
        pv = jnp.dot(vt, jnp.exp2((s - m).astype(BF16)), preferred_element_type=F32)
        acc = pv if acc is None else acc + pv
    den = acc[HEAD_DIM:HEAD_DIM + 1]
    if extra is not None:
        den = den + jnp.exp2(extra - m)
    return acc[:HEAD_DIM] / den


def _na_kernel(q_ref, k0, k1, k2, k3, vt0, vt1, vt2, vt3, kc_ref, vct_ref, edge_tbl_ref, tbl_ref, mask_ref,
               bound_ref, o_ref):
    q = q_ref[...]
    kc, vct = kc_ref[...], vct_ref[...]
    lane = _lane_iota()
    q_heads = [jnp.where((lane // HEAD_DIM) == hh, q, jnp.zeros_like(q)) for hh in range(2)]
    s_ctx = [lax.dot_general(kc, qh, NT_DIMS, preferred_element_type=F32) for qh in q_heads]

    def attend(tbl, mask, window_sets_reference):
        k = jnp.concatenate([k0[...], k1[...], k2[...], k3[...]], axis=0)
        vt = jnp.concatenate([vt0[...], vt1[...], vt2[...], vt3[...]], axis=2)
        outs = []
        for hh in range(2):
            s_loc = lax.dot_general(k, q_heads[hh], NT_DIMS, preferred_element_type=F32)
            s_loc = s_loc + _na_bias_tile(tbl, hh, mask)
            outs.append(_softmax_pv_t([s_ctx[hh], s_loc], [vct[hh], vt[hh]],
                                      exact_parts=2 if window_sets_reference else 1))
        o_ref[...] = jnp.concatenate(outs, axis=0).T.astype(BF16)

    lowest_ref = jnp.minimum(jnp.min(s_ctx[0].max(axis=0, keepdims=True)),
                             jnp.min(s_ctx[1].max(axis=0, keepdims=True)))
    ctx_reference = bound_ref[0, 0] - lowest_ref <= RESCALE_MARGIN
    rb = pl.program_id(2)
    is_edge = (rb == 0) | (rb == pl.num_programs(2) - 1)

    @pl.when(jnp.logical_not(is_edge) & ctx_reference)
    def _():
        attend(tbl_ref, None, False)

    @pl.when(jnp.logical_not(is_edge) & jnp.logical_not(ctx_reference))
    def _():
        attend(tbl_ref, None, True)

    @pl.when(is_edge)
    def _():
        attend(edge_tbl_ref, mask_ref, True)


def _na_attention(p_lat, vt_lat, p_ctx, vt_ctx, bias_tables, score_bound, nb, n, ctx_len, q_col, k_col, n_heads):
    tq = NA_QROWS * GRID_W
    tw = tq // 2
    qsteps = n // tq
    wsteps = n // tw
    edge_tbl, tbl, masks = bias_tables
    wblock = lambda rb, j: jnp.clip(2 * rb - 1 + j, 0, wsteps - 1)
    kwin = lambda j: pl.BlockSpec((tw, LANES), lambda hp, b, rb: (b * wsteps + wblock(rb, j), k_col + hp))
    vrows = HEAD_DIM + BF16_SUBLANES
    vwin = lambda j: pl.BlockSpec((2, vrows, tw), lambda hp, b, rb: (hp, 0, b * wsteps + wblock(rb, j)))
    variant = lambda rb: jnp.where(rb == 0, 0, jnp.where(rb == qsteps - 1, 2, 1))
    tbl_spec = pl.BlockSpec((2,) + tbl.shape[1:], lambda hp, b, rb: (hp, 0, 0, 0))
    in_specs = ([pl.BlockSpec((tq, LANES), lambda hp, b, rb: (b * qsteps + rb, q_col + hp))]
                + [kwin(j) for j in range(4)] + [vwin(j) for j in range(4)]
                + [pl.BlockSpec((ctx_len, LANES), lambda hp, b, rb: (b, k_col + hp)),
                   pl.BlockSpec((2, vrows, ctx_len), lambda hp, b, rb: (hp, 0, b)),
                   tbl_spec, tbl_spec,
                   pl.BlockSpec((1,) + masks.shape[1:], lambda hp, b, rb: (variant(rb), 0, 0)),
                   pl.BlockSpec(memory_space=pltpu.SMEM)])
    return pl.pallas_call(
        _na_kernel,
        grid=(n_heads // 2, nb, qsteps),
        in_specs=in_specs,
        out_specs=pl.BlockSpec((tq, LANES), lambda hp, b, rb: (b * qsteps + rb, hp)),
        out_shape=jax.ShapeDtypeStruct((nb * n, n_heads * HEAD_DIM), BF16),
        compiler_params=_params(("parallel", "parallel", "arbitrary")),
        name="na_attention",
    )(*([p_lat] * 5 + [vt_lat] * 4 + [p_ctx, vt_ctx, edge_tbl, tbl, masks, score_bound]))


def _gqa_kernel(*refs, n_q, n_kv, has_local, has_sink, n_tokens):
    refs = list(refs)
    q_ref = refs.pop(0)
    if has_local:
        k_loc = jnp.concatenate([refs.pop(0)[...] for _ in range(4)], axis=0)
        vt_loc = jnp.concatenate([refs.pop(0)[...] for _ in range(4)], axis=2)
    kc_ref, vct_ref = refs.pop(0), refs.pop(0)
    sink_ref = refs.pop(0) if has_sink else None
    o_ref = refs.pop(0)

    tq = q_ref.shape[0]
    group = n_q // n_kv
    lane = _lane_iota()
    if has_local:
        base = pl.program_id(1) * tq
        kpos = base - BLOCK + lax.broadcasted_iota(jnp.int32, (tq + 2 * BLOCK, 1), 0)
        qpos = base + lax.broadcasted_iota(jnp.int32, (1, tq), 1)
        ok = (jnp.abs(kpos - qpos) <= SWA_WINDOW) & (kpos >= 0) & (kpos < n_tokens)
        band = jnp.where(ok, 0.0, NEG)
        band = jnp.concatenate([band] * group, axis=1)

    outs = []
    for kvh in range(n_kv):
        cg, half = kvh // 2, kvh % 2
        qs = []
        for h in range(kvh * group, (kvh + 1) * group):
            qg = q_ref[:, (h // 2) * LANES:(h // 2 + 1) * LANES]
            if h % 2 != half:
                qg = _swap_halves(qg)
            qs.append(jnp.where((lane // HEAD_DIM) == half, qg, jnp.zeros_like(qg)))
        qcat = jnp.concatenate(qs, axis=0) if group > 1 else qs[0]
        kc = kc_ref[:, cg * LANES:(cg + 1) * LANES]
        s_parts = [lax.dot_general(kc, qcat, NT_DIMS, preferred_element_type=F32)]
        vt_parts = [vct_ref[kvh]]
        if has_local:
            s_parts.append(lax.dot_general(k_loc, qcat, NT_DIMS, preferred_element_type=F32) + band)
            vt_parts.append(vt_loc[kvh])
        extra = None
        if has_sink:
            extra = jnp.concatenate([sink_ref[h:h + 1, :] for h in range(kvh * group, (kvh + 1) * group)
                                     for _ in range(tq // LANES)], axis=1)
        o = _softmax_pv_t(s_parts, vt_parts, extra)
        outs += [o[:, g * tq:(g + 1) * tq] for g in range(group)]
    o_ref[...] = jnp.concatenate(outs, axis=0).T.astype(BF16)


def _gqa_attention(p_q, vt_q, p_ctx, vt_ctx, sink, nb, n, ctx_len, q_col, k_col, n_q, n_kv, has_local, tq):
    qsteps = n // tq
    kv_w = n_kv * HEAD_DIM
    q_w = n_q * HEAD_DIM
    vrows = HEAD_DIM + BF16_SUBLANES
    in_specs = [pl.BlockSpec((tq, q_w), lambda b, i: (b * qsteps + i, q_col * LANES // q_w))]
    args = [p_q]
    if has_local:
        assert tq == 2 * BLOCK and kv_w == LANES
        wsteps = n // BLOCK
        wblock = lambda i, j: jnp.clip(2 * i - 1 + j, 0, wsteps - 1)
        in_specs += [pl.BlockSpec((BLOCK, LANES), functools.partial(
            lambda b, i, j: (b * wsteps + wblock(i, j), k_col), j=j)) for j in range(4)]
        in_specs += [pl.BlockSpec((n_kv, vrows, BLOCK), functools.partial(
            lambda b, i, j: (0, 0, b * wsteps + wblock(i, j)), j=j)) for j in range(4)]
        args += [p_q] * 4 + [vt_q] * 4
    in_specs += [pl.BlockSpec((ctx_len, kv_w), lambda b, i: (b, k_col * LANES // kv_w)),
                 pl.BlockSpec((n_kv, vrows, ctx_len), lambda b, i: (0, 0, b))]
    args += [p_ctx, vt_ctx]
    if sink is not None:
        in_specs.append(_resident(sink.shape, lambda b, i: (0, 0)))
        args.append(sink)
    return pl.pallas_call(
        functools.partial(_gqa_kernel, n_q=n_q, n_kv=n_kv, has_local=has_local,
                          has_sink=sink is not None, n_tokens=n),
        grid=(nb, qsteps),
        in_specs=in_specs,
        out_specs=pl.BlockSpec((tq, q_w), lambda b, i: (b * qsteps + i, 0)),
        out_shape=jax.ShapeDtypeStruct((nb * n, q_w), BF16),
        compiler_params=_params(("parallel", "parallel")),
        name="gqa_attention",
    )(*args)


def _diff_kernel(*refs, n_kblocks, tk, lam_init):
    if n_kblocks:
        q_ref, k_ref, vt_ref, kc_ref, vct_ref, lam_ref, subg_ref, bound_ref, o_ref, m_sc, acc_sc, p_sc = refs
    else:
        q_ref, kc_ref, vct_ref, lam_ref, subg_ref, bound_ref, o_ref, m_sc, acc_sc = refs
    lane = _lane_iota()
    q = q_ref[...]
    zero = jnp.zeros_like(q)
    q_maps = [jnp.where(lane < HEAD_DIM, q, zero), jnp.where(lane >= HEAD_DIM, q, zero)]

    kc, vct = kc_ref[...], vct_ref[0]
    for i in range(2):
        s = lax.dot_general(kc, q_maps[i], NT_DIMS, preferred_element_type=F32)
        m = s.max(axis=0, keepdims=True)
        p = jnp.exp2((s - m).astype(BF16))
        m_sc[i] = m
        acc_sc[i] = jnp.dot(vct, p, preferred_element_type=F32)

    if n_kblocks:
        def scores(kb):
            k = k_ref[pl.ds(pl.multiple_of(kb * tk, tk), tk), :]
            return [lax.dot_general(k, q_maps[i], NT_DIMS, preferred_element_type=F32) for i in range(2)]

        def probs(kb, slot):
            s = scores(kb)
            excess = None
            for i in range(2):
                m_ref = m_sc[i]
                p_sc[slot, i] = jnp.exp2((s[i] - m_ref).astype(BF16))
                over = jnp.max(s[i].max(axis=0, keepdims=True) - m_ref)
                excess = over if excess is None else jnp.maximum(excess, over)
            return excess

        def settle(kb, slot, excess):
            @pl.when(excess > RESCALE_MARGIN)
            def _():
                s = scores(kb)
                for i in range(2):
                    m_prev = m_sc[i]
                    m_new = jnp.maximum(m_prev, s[i].max(axis=0, keepdims=True))
                    p_sc[slot, i] = jnp.exp2((s[i] - m_new).astype(BF16))
                    acc_sc[i] = jnp.exp2(m_prev - m_new) * acc_sc[i]
                    m_sc[i] = m_new

        def accumulate(kb, slot):
            vt = vt_ref[0, :, pl.ds(pl.multiple_of(kb * tk, tk), tk)]
            for i in range(2):
                acc_sc[i] += jnp.dot(vt, p_sc[slot, i], preferred_element_type=F32)

        def probs_unchecked(kb, slot):
            s = scores(kb)
            for i in range(2):
                p_sc[slot, i] = jnp.exp2((s[i] - m_sc[i]).astype(BF16))

        lowest_ref = jnp.minimum(jnp.min(m_sc[0]), jnp.min(m_sc[1]))
        never_rescales = bound_ref[0, 0] - lowest_ref <= RESCALE_MARGIN
        first_excess = probs(0, 0)

        @pl.when(never_rescales)
        def _():
            def body(kb, carry):
                accumulate(kb - 1, (kb - 1) % 2)
                probs_unchecked(kb, kb % 2)
                return carry

            lax.fori_loop(1, n_kblocks, body, 0)
            accumulate(n_kblocks - 1, (n_kblocks - 1) % 2)

        @pl.when(jnp.logical_not(never_rescales))
        def _():
            def body(kb, excess):
                settle(kb - 1, (kb - 1) % 2, excess)
                accumulate(kb - 1, (kb - 1) % 2)
                return probs(kb, kb % 2)

            excess = lax.fori_loop(1, n_kblocks, body, first_excess)
            settle(n_kblocks - 1, (n_kblocks - 1) % 2, excess)
            accumulate(n_kblocks - 1, (n_kblocks - 1) % 2)

    lp = lam_ref[...]
    lam = (jnp.exp(jnp.sum(lp[0:1] * lp[1:2], axis=-1, keepdims=True))
           - jnp.exp(jnp.sum(lp[2:3] * lp[3:4], axis=-1, keepdims=True)) + lam_init)
    dv = 2 * HEAD_DIM
    y0 = acc_sc[0, :dv] / acc_sc[0, dv:dv + 1]
    y1 = acc_sc[1, :dv] / acc_sc[1, dv:dv + 1]
    y = (y0 - lam * y1).T
    ms = jnp.mean(y * y, axis=-1, keepdims=True)
    y = y * lax.rsqrt(ms + EPS) * subg_ref[...] * (1.0 - lam_init)
    o_ref[...] = y.astype(BF16)


def _diff_attention(p_q, p_lat, vt_lat, p_ctx, vt_ctx, lam_rows, subg, score_bound, nb, n_q, n_lat, ctx_len,
                    q_col, k_col, n_heads, lam_init, tq, tk):
    qsteps = n_q // tq
    has_latent = p_lat is not None
    vrows = 2 * HEAD_DIM + BF16_SUBLANES
    in_specs = [pl.BlockSpec((tq, LANES), lambda b, h, i: (b * qsteps + i, q_col + h))]
    args = [p_q]
    if has_latent:
        in_specs += [pl.BlockSpec((n_lat, LANES), lambda b, h, i: (b, k_col + h)),
                     pl.BlockSpec((1, vrows, n_lat), lambda b, h, i: (h, 0, b))]
        args += [p_lat, vt_lat]
    in_specs += [pl.BlockSpec((ctx_len, LANES), lambda b, h, i: (b, k_col + h)),
                 pl.BlockSpec((1, vrows, ctx_len), lambda b, h, i: (h, 0, b)),
                 _resident(lam_rows.shape, lambda b, h, i: (0, 0)),
                 _resident(subg.shape, lambda b, h, i: (0, 0)),
                 pl.BlockSpec(memory_space=pltpu.SMEM)]
    args += [p_ctx, vt_ctx, lam_rows, subg, score_bound]
    return pl.pallas_call(
        functools.partial(_diff_kernel, n_kblocks=n_lat // tk if has_latent else 0, tk=tk, lam_init=lam_init),
        grid=(nb, n_heads, qsteps),
        in_specs=in_specs,
        out_specs=pl.BlockSpec((tq, LANES), lambda b, h, i: (b * qsteps + i, h)),
        out_shape=jax.ShapeDtypeStruct((nb * n_q, n_heads * LANES), BF16),
        scratch_shapes=([pltpu.VMEM((2, 1, tq), F32), pltpu.VMEM((2, vrows, tq), F32)]
                        + ([pltpu.VMEM((2, 2, tk, tq), BF16)] if has_latent else [])),
        compiler_params=_params(("parallel", "parallel", "arbitrary")),
        name="diff_attention",
    )(*args)


def _silu(x):
    return x * (1.0 / (1.0 + jnp.exp(-x)))


def _ffn_kernel(x_ref, g_ref, sc_ref, sh_ref, gate_ref, w1_ref, w3_ref, w2_ref, o_ref):
    x = x_ref[...]
    a = _modulated_norm(x, g_ref[...], sc_ref[0], sh_ref[0]).astype(BF16)
    h1 = jnp.dot(a, w1_ref[...], preferred_element_type=F32)
    h3 = jnp.dot(a, w3_ref[...], preferred_element_type=F32)
    y = jnp.dot((_silu(h1) * h3).astype(BF16), w2_ref[...], preferred_element_type=F32)
    o_ref[...] = x + gate_ref[0] * y


def _ffn(h, nb, g, sc, sh, gate, w1, w3, w2, tm):
    t, d = h.shape
    steps = t // nb // tm
    row = lambda b, i: (b * steps + i, 0)
    vec = pl.BlockSpec((1, 1, d), lambda b, i: (b, 0, 0))
    return pl.pallas_call(
        _ffn_kernel,
        grid=(nb, steps),
        in_specs=[pl.BlockSpec((tm, d), row), _resident((1, d), lambda b, i: (0, 0)), vec, vec, vec,
                  _resident(w1.shape, lambda b, i: (0, 0)),
                  _resident(w3.shape, lambda b, i: (0, 0)),
                  _resident(w2.shape, lambda b, i: (0, 0))],
        out_specs=pl.BlockSpec((tm, d), row),
        out_shape=jax.ShapeDtypeStruct((t, d), F32),
        compiler_params=_params(("parallel", "parallel")),
        name="ffn",
    )(h, g, sc, sh, gate, w1, w3, w2)


def _top2_gates(logits):
    lane = _lane_iota()
    big = jnp.int32(LANES)
    lg = jnp.where(lane < N_EXPERTS, logits, -jnp.inf)
    m1 = lg.max(axis=-1, keepdims=True)
    i1 = jnp.where(lg == m1, lane, big).min(axis=-1, keepdims=True)
    rest = jnp.where(lane == i1, -jnp.inf, lg)
    m2 = rest.max(axis=-1, keepdims=True)
    i2 = jnp.where(rest == m2, lane, big).min(axis=-1, keepdims=True)
    e2 = jnp.exp(m2 - m1)
    den = 1.0 + e2
    return jnp.where(lane == i1, 1.0 / den, 0.0) + jnp.where(lane == i2, e2 / den, 0.0)


def _moe_kernel(x_ref, g_ref, sc_ref, sh_ref, gate_ref, r_ref, w1_ref, w3_ref, w2_ref, o_ref,
                a_sc, gates_sc, rank_sc, acc_sc, *, chunk):
    e = pl.program_id(2)
    tm = x_ref.shape[0]

    @pl.when(e == 0)
    def _():
        a = _modulated_norm(x_ref[...], g_ref[...], sc_ref[0], sh_ref[0])
        a_hi = a.astype(BF16)
        a_lo = (a - a_hi.astype(F32)).astype(BF16)
        r = r_ref[...]
        r_hi = r.astype(BF16)
        r_lo = (r - r_hi.astype(F32)).astype(BF16)
        logits = (jnp.dot(a_hi, r_hi, preferred_element_type=F32)
                  + jnp.dot(a_hi, r_lo, preferred_element_type=F32)
                  + jnp.dot(a_lo, r_hi, preferred_element_type=F32))
        a_sc[...] = a_hi
        gates_t = _top2_gates(logits).T[:EXPERT_ROWS]
        gates_sc[...] = gates_t
        before = (lax.broadcasted_iota(jnp.int32, (tm, tm), 0)
                  < lax.broadcasted_iota(jnp.int32, (tm, tm), 1))
        routed = jnp.where(gates_t > 0.0, 1.0, 0.0).astype(BF16)
        rank_sc[...] = jnp.dot(routed, jnp.where(before, 1.0, 0.0).astype(BF16), preferred_element_type=F32)
        acc_sc[...] = jnp.zeros_like(acc_sc)

    gate_row = gates_sc[pl.ds(e, 1), :]
    rank_row = jnp.where(gate_row > 0.0, rank_sc[pl.ds(e, 1), :], -1.0)
    count = jnp.sum(jnp.where(gate_row > 0.0, 1.0, 0.0)).astype(jnp.int32)
    slot = lax.broadcasted_iota(jnp.int32, (chunk, 1), 0).astype(F32)

    for j in range(pl.cdiv(tm, chunk)):
        @pl.when(count > j * chunk)
        def _():
            pick = rank_row == (slot + float(j * chunk))
            pick_f = jnp.where(pick, 1.0, 0.0)
            xs = jnp.dot(pick_f.astype(BF16), a_sc[...], preferred_element_type=F32).astype(BF16)
            h1 = jnp.dot(xs, w1_ref[0], preferred_element_type=F32)
            h3 = jnp.dot(xs, w3_ref[0], preferred_element_type=F32)
            y = jnp.dot((_silu(h1) * h3).astype(BF16), w2_ref[0], preferred_element_type=F32)
            y = y * jnp.sum(pick_f * gate_row, axis=-1, keepdims=True)
            acc_sc[...] += lax.dot_general(pick_f.astype(BF16), y.astype(BF16), (((0,), (0,)), ((), ())),
                                           preferred_element_type=F32)

    @pl.when(e == pl.num_programs(2) - 1)
    def _():
        o_ref[...] = x_ref[...] + gate_ref[0] * acc_sc[...]


def _moe(h, nb, g, sc, sh, gate, router, w1, w3, w2, tm, chunk):
    t, d = h.shape
    steps = t // nb // tm
    n_e, _, f = w1.shape
    row = lambda b, i, e: (b * steps + i, 0)
    vec = pl.BlockSpec((1, 1, d), lambda b, i, e: (b, 0, 0))
    return pl.pallas_call(
        functools.partial(_moe_kernel, chunk=chunk),
        grid=(nb, steps, n_e),
        in_specs=[pl.BlockSpec((tm, d), row), _resident((1, d), lambda b, i, e: (0, 0)), vec, vec, vec,
                  _resident(router.shape, lambda b, i, e: (0, 0)),
                  pl.BlockSpec((1, d, f), lambda b, i, e: (e, 0, 0)),
                  pl.BlockSpec((1, d, f), lambda b, i, e: (e, 0, 0)),
                  pl.BlockSpec((1, f, d), lambda b, i, e: (e, 0, 0))],
        out_specs=pl.BlockSpec((tm, d), row),
        out_shape=jax.ShapeDtypeStruct((t, d), F32),
        scratch_shapes=[pltpu.VMEM((tm, d), BF16), pltpu.VMEM((EXPERT_ROWS, tm), F32),
                        pltpu.VMEM((EXPERT_ROWS, tm), F32), pltpu.VMEM((tm, d), F32)],
        compiler_params=_params(("parallel", "parallel", "arbitrary")),
        name="moe",
    )(h, g, sc, sh, gate, router, w1, w3, w2)


def _rope_tables(n):
    t = jnp.arange(n)
    pos = jnp.stack([t // GRID_W, t % GRID_W], -1).astype(F32)
    nq = HEAD_DIM // 4
    inv = ROPE_THETA ** (-jnp.arange(nq, dtype=F32) / nq)
    ang = pos[:, :, None] * inv
    cos = jnp.repeat(jnp.cos(ang)[:, :, None, :], 2, axis=2)
    sin = jnp.stack([-jnp.sin(ang), jnp.sin(ang)], axis=2)
    cos = jnp.tile(cos.reshape(n, HEAD_DIM), (1, LANES // HEAD_DIM))
    sin = jnp.tile(sin.reshape(n, HEAD_DIM), (1, LANES // HEAD_DIM))
    return cos, sin


def _qk_bound(q_gain, k_gain):
    return (HEAD_DIM * QK_SCALE * LOG2_E * BF16_NORM_SLACK * jnp.max(jnp.abs(q_gain))
            * jnp.max(jnp.abs(k_gain))).astype(F32)


def _head_gain(parts, n_out):
    row = jnp.ones((n_out,), F32)
    for col, n_heads, gain, scale in parts:
        row = lax.dynamic_update_slice(row, jnp.tile(gain.astype(F32) * scale, n_heads), (col,))
    return row.reshape(1, n_out)


def kernel(x, c, ctx, c_ctx, ada_w, ada_b, norm1_g, norm2_g, ev_w_in, ev_conv_w, ev_q_g, ev_k_g, ev_rpb,
           ev_w_out, ffn_w1, ffn_w3, ffn_w2, od_w_in, od_cq_g, od_ck_g, od_sink, od_dq_g, od_dk_g,
           od_lam_q1, od_lam_k1, od_lam_q2, od_lam_k2, od_subln_g, od_w_out, moe_router,
           moe_w1, moe_w3, moe_w2):
    nb, n, d = x.shape
    ctx_len = ctx.shape[1]
    depth = ada_w.shape[0]
    n_slots = d // HEAD_DIM
    conv_ch = d // 2
    na_heads = swa_heads = n_slots // 2
    swa_kv = max(1, swa_heads // 4)
    diff_heads = n_slots // 4
    rows = n // GRID_W
    assert rows % NA_QROWS == 0 and rows >= 2 * NA_QROWS and n % 512 == 0
    assert nb + 1 <= 8 and ctx_len % LANES == 0

    ev_q_col = 3 * conv_ch
    ev_k_col = ev_q_col + na_heads * HEAD_DIM
    ev_v_col = ev_k_col + na_heads * HEAD_DIM
    ev_n = ev_v_col + na_heads * HEAD_DIM
    od_dq_col = swa_heads * HEAD_DIM
    od_ck_col = od_dq_col + diff_heads * 2 * HEAD_DIM
    od_cv_col = od_ck_col + swa_kv * HEAD_DIM
    od_dk_col = od_cv_col + swa_kv * HEAD_DIM
    od_dv_col = od_dk_col + diff_heads * 2 * HEAD_DIM
    od_n = od_dv_col + diff_heads * 2 * HEAD_DIM
    ev_segs = ((0, ev_q_col, False, 0), (ev_q_col, ev_v_col, True, 0), (ev_v_col, ev_n, False, na_heads))
    od_segs = ((0, od_cv_col, True, 0), (od_cv_col, od_dk_col, False, swa_kv), (od_dk_col, od_dv_col, True, 0),
               (od_dv_col, od_n, False, diff_heads))

    h = x.reshape(nb * n, d)
    hc = ctx.reshape(nb * ctx_len, d)
    tm_lat = 512
    tm_ctx = ctx_len
    tm_moe = 1024

    s_rows = jnp.zeros((8, d), F32).at[:nb].set(c).at[nb].set(c_ctx)
    mod = _modulation(s_rows, ada_w, ada_b)
    rope_tabs = _rope_tables(n)

    def lat_vec(l, k):
        return mod[l, :nb, k * d:(k + 1) * d].reshape(nb, 1, d)

    def ctx_vec(l, k, copies):
        return jnp.broadcast_to(mod[l, nb, k * d:(k + 1) * d], (copies, 1, d))

    for l in range(depth):
        last = l == depth - 1
        i = l // 2
        g1 = norm1_g[l].reshape(1, d)
        g2 = norm2_g[l].reshape(1, d)
        if l % 2 == 0:
            w_in = ev_w_in[i].astype(BF16)
            gain = _head_gain([(ev_q_col, na_heads, ev_q_g[i], QK_SCALE * LOG2_E), (ev_k_col, na_heads, ev_k_g[i], 1.0)],
                              ev_n)
            p_lat, vt_lat = _norm_proj(h, nb, g1, lat_vec(l, 1), lat_vec(l, 0), w_in, gain, ev_segs, None, tm_lat)
            p_ctx, vt_ctx = _norm_proj(hc, nb, g1, ctx_vec(l, 1, nb), ctx_vec(l, 0, nb), w_in, gain, ev_segs, None,
                                       tm_ctx)
            bias = _na_bias_tables(ev_rpb[i], rows)
            na_bound = (_qk_bound(ev_q_g[i], ev_k_g[i]) + LOG2_E * jnp.maximum(jnp.max(ev_rpb[i]), 0.0)).reshape(1, 1)
            y_na = _na_attention(p_lat, vt_lat, p_ctx, vt_ctx, bias, na_bound, nb, n, ctx_len, ev_q_col // LANES,
                                 ev_k_col // LANES, na_heads)
            w_out = ev_w_out[i].astype(BF16)
            conv_w = ev_conv_w[i]
            h = _out_proj(h, nb, lat_vec(l, 2), p_lat, y_na, w_out, conv_w, tm_lat)
            if not last:
                y_na_c = _gqa_attention(p_ctx, None, p_ctx, vt_ctx, None, nb, ctx_len, ctx_len, ev_q_col // LANES,
                                        ev_k_col // LANES, na_heads, na_heads, False, ctx_len)
                hc = _out_proj(hc, nb, ctx_vec(l, 2, nb), p_ctx, y_na_c, w_out, conv_w, tm_ctx)
            w1, w3, w2 = ffn_w1[i].astype(BF16), ffn_w3[i].astype(BF16), ffn_w2[i].astype(BF16)
            h = _ffn(h, nb, g2, lat_vec(l, 4), lat_vec(l, 3), lat_vec(l, 5), w1, w3, w2, tm_lat)
            if not last:
                hc = _ffn(hc, 1, g2, ctx_vec(l, 4, 1), ctx_vec(l, 3, 1), ctx_vec(l, 5, 1), w1, w3, w2, tm_lat)
        else:
            lam_init = 0.8 - 0.6 * math.exp(-0.3 * l)
            w_in = od_w_in[i].astype(BF16)
            gain = _head_gain([(0, swa_heads, od_cq_g[i], QK_SCALE * LOG2_E),
                               (od_dq_col, 2 * diff_heads, od_dq_g[i], QK_SCALE * LOG2_E),
                               (od_ck_col, swa_kv, od_ck_g[i], 1.0),
                               (od_dk_col, 2 * diff_heads, od_dk_g[i], 1.0)], od_n)
            p_lat, cvt_lat, vt_lat = _norm_proj(h, nb, g1, lat_vec(l, 1), lat_vec(l, 0), w_in, gain, od_segs, rope_tabs,
                                                tm_lat)
            p_ctx, cvt_ctx, vt_ctx = _norm_proj(hc, nb, g1, ctx_vec(l, 1, nb), ctx_vec(l, 0, nb), w_in, gain, od_segs,
                                                None, tm_ctx)
            sink = jnp.broadcast_to(od_sink[i].astype(F32)[:, None] * LOG2_E, (swa_heads, LANES))
            lam_rows = jnp.zeros((8, LANES), F32).at[:4, :HEAD_DIM].set(
                jnp.stack([od_lam_q1[i], od_lam_k1[i], od_lam_q2[i], od_lam_k2[i]]).astype(F32))
            subg = od_subln_g[i].astype(F32).reshape(1, 2 * HEAD_DIM)
            cols = (od_dq_col // LANES, od_dk_col // LANES)
            score_bound = _qk_bound(od_dq_g[i], od_dk_g[i]).reshape(1, 1)
            y_c = _gqa_attention(p_lat, cvt_lat, p_ctx, cvt_ctx, sink, nb, n, ctx_len, 0, od_ck_col // LANES,
                                 swa_heads, swa_kv, True, 2 * BLOCK)
            y_d = _diff_attention(p_lat, p_lat, vt_lat, p_ctx, vt_ctx, lam_rows, subg, score_bound, nb, n, n, ctx_len,
                                  *cols, diff_heads, lam_init, 2048, 1024)
            w_out = od_w_out[i].astype(BF16)
            h = _out_proj(h, nb, lat_vec(l, 2), y_c, y_d, w_out, None, tm_lat)
            if not last:
                y_c_c = _gqa_attention(p_ctx, None, p_ctx, cvt_ctx, sink, nb, ctx_len, ctx_len, 0,
                                       od_ck_col // LANES, swa_heads, swa_kv, False, ctx_len)
                y_d_c = _diff_attention(p_ctx, None, None, p_ctx, vt_ctx, lam_rows, subg, score_bound, nb, ctx_len, 0,
                                        ctx_len, *cols, diff_heads, lam_init, ctx_len, ctx_len)
                hc = _out_proj(hc, nb, ctx_vec(l, 2, nb), y_c_c, y_d_c, w_out, None, tm_ctx)
            router = jnp.zeros((d, LANES), F32).at[:, :N_EXPERTS].set(moe_router[i])
            w1, w3, w2 = moe_w1[i].astype(BF16), moe_w3[i].astype(BF16), moe_w2[i].astype(BF16)
            h = _moe(h, nb, g2, lat_vec(l, 4), lat_vec(l, 3), lat_vec(l, 5), router, w1, w3, w2, tm_moe, MOE_CHUNK)
            if not last:
                hc = _moe(hc, 1, g2, ctx_vec(l, 4, 1), ctx_vec(l, 3, 1), ctx_vec(l, 5, 1), router, w1, w3, w2,
                          min(tm_moe, nb * ctx_len), MOE_CHUNK)
    return h.reshape(nb, n, d)
```

```python
import functools
import math

import jax
import jax.numpy as jnp
from jax import lax
from jax.experimental import pallas as pl
from jax.experimental.pallas import tpu as pltpu

F32 = jnp.float32
BF16 = jnp.bfloat16

LANES = 128
F32_SUBLANES = 8
BF16_SUBLANES = 16
VMEM_LIMIT = 56 * 1024 * 1024

TOKEN_TILE = 512
MOE_TOKEN_TILE = 1024
DIFF_Q_TILE = 2048
DIFF_K_TILE = 1024

HEAD_DIM = 64
GRID_W = 64
CONV_W = 3
NA_ROWS = 8
NA_COLS = 16
NA_QROWS = 8
SWA_WINDOW = 128
MOE_CHUNK = 256
BLOCK = 128
N_EXPERTS = 8
EXPERT_ROWS = 16
ROPE_THETA = 10000.0
EPS = 1e-6
NEG = -1e30
QK_SCALE = HEAD_DIM ** -0.5
LOG2_E = math.log2(math.e)
BF16_NORM_SLACK = 1.02
RESCALE_MARGIN = 32.0

NT_DIMS = (((1,), (1,)), ((), ()))


def _params(sem):
    return pltpu.CompilerParams(dimension_semantics=sem, vmem_limit_bytes=VMEM_LIMIT)


def _resident(shape, index_map):
    return pl.BlockSpec(shape, index_map, pipeline_mode=pl.Buffered(1))


def _lane_iota():
    return lax.broadcasted_iota(jnp.int32, (1, LANES), 1)


def _swap_halves(x):
    return jnp.concatenate([x[:, HEAD_DIM:], x[:, :HEAD_DIM]], axis=1)


def _modulated_norm(x, g, sc, sh):
    ms = jnp.mean(x * x, axis=-1, keepdims=True)
    return (x * lax.rsqrt(ms + EPS)) * (g * (1.0 + sc)) + sh


def _mod_kernel(s_ref, w_ref, b_ref, o_ref):
    s = s_ref[...]
    s = s * (1.0 / (1.0 + jnp.exp(-s)))
    o_ref[0] = jnp.dot(s.astype(BF16), w_ref[0].astype(BF16), preferred_element_type=F32) + b_ref[0]


def _modulation(s_rows, ada_w, ada_b):
    depth, d, n_out = ada_w.shape
    tn = n_out // 4
    return pl.pallas_call(
        _mod_kernel,
        grid=(depth, n_out // tn),
        in_specs=[pl.BlockSpec(s_rows.shape, lambda l, j: (0, 0)),
                  pl.BlockSpec((1, d, tn), lambda l, j: (l, 0, j)),
                  pl.BlockSpec((1, 1, tn), lambda l, j: (l, 0, j))],
        out_specs=pl.BlockSpec((1, s_rows.shape[0], tn), lambda l, j: (l, 0, j)),
        out_shape=jax.ShapeDtypeStruct((depth, s_rows.shape[0], n_out), F32),
        compiler_params=_params(("arbitrary", "arbitrary")),
        name="modulation",
    )(s_rows, ada_w, ada_b.reshape(depth, 1, n_out))


def _head_sumsq(z):
    r = lax.broadcasted_iota(jnp.int32, (LANES, LANES), 0) // HEAD_DIM
    c = lax.broadcasted_iota(jnp.int32, (LANES, LANES), 1) // HEAD_DIM
    same_head = jnp.where(r == c, 1.0, 0.0).astype(BF16)
    z2 = z * z
    hi = z2.astype(BF16)
    lo = (z2 - hi.astype(F32)).astype(BF16)
    return (jnp.dot(hi, same_head, preferred_element_type=F32)
            + jnp.dot(lo, same_head, preferred_element_type=F32))


def _norm_proj_kernel(*refs, segs, rope, n_vt):
    refs = list(refs)
    vt_refs = [refs.pop() for _ in range(n_vt)][::-1]
    if rope:
        x_ref, g_ref, sc_ref, sh_ref, w_ref, gain_ref, cos_ref, sin_ref, o_ref = refs
    else:
        x_ref, g_ref, sc_ref, sh_ref, w_ref, gain_ref, o_ref = refs
    a = _modulated_norm(x_ref[...], g_ref[...], sc_ref[0], sh_ref[0]).astype(BF16)
    first_half = (_lane_iota() % (HEAD_DIM // 2)) < (HEAD_DIM // 4)
    for c0, c1, normed, vt_heads in segs:
        acc = jnp.dot(a, w_ref[:, c0:c1], preferred_element_type=F32)
        if vt_heads:
            vt_ref = vt_refs.pop(0)
            dv = (c1 - c0) // vt_heads
            vt = acc.T.astype(BF16)
            for hh in range(vt_heads):
                vt_ref[hh, :dv, :] = vt[hh * dv:(hh + 1) * dv]
                vt_ref[hh, dv:, :] = jnp.ones((BF16_SUBLANES, vt.shape[1]), BF16)
        if not normed:
            o_ref[:, c0:c1] = acc.astype(BF16)
            continue
        for j in range((c1 - c0) // LANES):
            z = acc[:, j * LANES:(j + 1) * LANES]
            lo = c0 + j * LANES
            z = z * lax.rsqrt(_head_sumsq(z) * (1.0 / HEAD_DIM) + EPS) * gain_ref[:, lo:lo + LANES]
            if rope:
                partner = jnp.where(first_half,
                                    pltpu.roll(z, LANES - HEAD_DIM // 4, axis=1),
                                    pltpu.roll(z, HEAD_DIM // 4, axis=1))
                z = z * cos_ref[...] + partner * sin_ref[...]
            o_ref[:, lo:lo + LANES] = z.astype(BF16)


def _norm_proj(h, nb, g, sc, sh, w, gain, segs, rope_tabs, tm):
    t, d = h.shape
    n_out = w.shape[1]
    steps = t // nb // tm
    in_specs = [pl.BlockSpec((tm, d), lambda b, i: (b * steps + i, 0)),
                _resident((1, d), lambda b, i: (0, 0)),
                pl.BlockSpec((1, 1, d), lambda b, i: (b, 0, 0)),
                pl.BlockSpec((1, 1, d), lambda b, i: (b, 0, 0)),
                _resident((d, n_out), lambda b, i: (0, 0)),
                _resident((1, n_out), lambda b, i: (0, 0))]
    args = [h, g, sc, sh, w, gain]
    if rope_tabs is not None:
        in_specs += [pl.BlockSpec((tm, LANES), lambda b, i: (i, 0))] * 2
        args += list(rope_tabs)
    out_specs = [pl.BlockSpec((tm, n_out), lambda b, i: (b * steps + i, 0))]
    out_shape = [jax.ShapeDtypeStruct((t, n_out), BF16)]
    for c0, c1, _, vt_heads in segs:
        if vt_heads:
            rows = (c1 - c0) // vt_heads + BF16_SUBLANES
            out_specs.append(pl.BlockSpec((vt_heads, rows, tm), lambda b, i: (0, 0, b * steps + i)))
            out_shape.append(jax.ShapeDtypeStruct((vt_heads, rows, t), BF16))
    return pl.pallas_call(
        functools.partial(_norm_proj_kernel, segs=segs, rope=rope_tabs is not None, n_vt=len(out_specs) - 1),
        grid=(nb, steps),
        in_specs=in_specs,
        out_specs=out_specs,
        out_shape=out_shape,
        compiler_params=_params(("parallel", "parallel")),
        name="norm_proj",
    )(*args)


def _gated_conv(gb_ref, gc_ref, u_ref, gcp_ref, up_ref, gcn_ref, un_ref, cw_ref):
    i, steps = pl.program_id(1), pl.num_programs(1)
    v = gc_ref[...].astype(F32) * u_ref[...].astype(F32)
    tm = v.shape[0]
    last = BF16_SUBLANES - 1
    prev_row = gcp_ref[last:last + 1, :].astype(F32) * up_ref[last:last + 1, :].astype(F32)
    next_row = gcn_ref[0:1, :].astype(F32) * un_ref[0:1, :].astype(F32)
    prev_row = jnp.where(i > 0, prev_row, 0.0)
    next_row = jnp.where(i < steps - 1, next_row, 0.0)
    row = lax.broadcasted_iota(jnp.int32, (tm, 1), 0)
    v_prev = jnp.where(row == 0, prev_row, pltpu.roll(v, 1, axis=0))
    v_next = jnp.where(row == tm - 1, next_row, pltpu.roll(v, tm - 1, axis=0))
    cw = cw_ref[...]
    conv = cw[0:1, :] * v_prev + cw[1:2, :] * v + cw[2:3, :] * v_next
    return gb_ref[...].astype(F32) * conv


def _out_proj_kernel(*refs, conv):
    if conv:
        (gb_ref, gc_ref, u_ref, gcp_ref, up_ref, gcn_ref, un_ref, cw_ref,
         yb_ref, wa_ref, wb_ref, h_ref, gate_ref, o_ref) = refs
        ya = _gated_conv(gb_ref, gc_ref, u_ref, gcp_ref, up_ref, gcn_ref, un_ref, cw_ref).astype(BF16)
    else:
        ya_ref, yb_ref, wa_ref, wb_ref, h_ref, gate_ref, o_ref = refs
        ya = ya_ref[...]
    y = (jnp.dot(ya, wa_ref[...], preferred_element_type=F32)
         + jnp.dot(yb_ref[...], wb_ref[...], preferred_element_type=F32))
    o_ref[...] = h_ref[...] + gate_ref[0] * y


def _out_proj(h, nb, gate, ya_src, yb, w_out, conv_w, tm):
    t, d = h.shape
    steps = t // nb // tm
    wa_rows = w_out.shape[0] - yb.shape[1]
    w_a, w_b = w_out[:wa_rows], w_out[wa_rows:]
    row = lambda b, i: (b * steps + i, 0)
    if conv_w is not None:
        cc = conv_w.shape[1]
        hb = tm // BF16_SUBLANES
        n_halo = t // BF16_SUBLANES
        prev = lambda col: (lambda b, i: (jnp.maximum((b * steps + i) * hb - 1, 0), col))
        nxt = lambda col: (lambda b, i: (jnp.minimum((b * steps + i + 1) * hb, n_halo - 1), col))
        in_specs = [pl.BlockSpec((tm, cc), lambda b, i: (b * steps + i, 0)),
                    pl.BlockSpec((tm, cc), lambda b, i: (b * steps + i, 1)),
                    pl.BlockSpec((tm, cc), lambda b, i: (b * steps + i, 2)),
                    pl.BlockSpec((BF16_SUBLANES, cc), prev(1)),
                    pl.BlockSpec((BF16_SUBLANES, cc), prev(2)),
                    pl.BlockSpec((BF16_SUBLANES, cc), nxt(1)),
                    pl.BlockSpec((BF16_SUBLANES, cc), nxt(2)),
                    _resident(conv_w.shape, lambda b, i: (0, 0))]
        args = [ya_src] * 7 + [conv_w]
    else:
        in_specs = [pl.BlockSpec((tm, wa_rows), row)]
        args = [ya_src]
    in_specs += [pl.BlockSpec((tm, yb.shape[1]), row),
                 _resident(w_a.shape, lambda b, i: (0, 0)),
                 _resident(w_b.shape, lambda b, i: (0, 0)),
                 pl.BlockSpec((tm, d), row),
                 pl.BlockSpec((1, 1, d), lambda b, i: (b, 0, 0))]
    args += [yb, w_a, w_b, h, gate]
    return pl.pallas_call(
        functools.partial(_out_proj_kernel, conv=conv_w is not None),
        grid=(nb, steps),
        in_specs=in_specs,
        out_specs=pl.BlockSpec((tm, d), row),
        out_shape=jax.ShapeDtypeStruct((t, d), F32),
        compiler_params=_params(("parallel", "parallel")),
        name="out_proj",
    )(*args)


def _na_bias_tables(rpb, rows):
    n_heads = rpb.shape[0]
    n_dr = 2 * NA_ROWS - 1
    c = jnp.arange(GRID_W)
    c0 = jnp.clip(c - NA_COLS // 2, 0, GRID_W - NA_COLS)
    col_ok = (c[None, :] >= c0[:, None]) & (c[None, :] < c0[:, None] + NA_COLS)
    dc = jnp.clip(c[None, :] - c[:, None] + NA_COLS - 1, 0, 2 * NA_COLS - 2)
    pick_dc = ((dc[None] == jnp.arange(2 * NA_COLS - 1)[:, None, None]) & col_ok[None]).astype(F32)
    table = jnp.einsum('hrd,dcx->hrxc', rpb.astype(F32), pick_dc, precision=lax.Precision.HIGHEST)
    table = jnp.where(col_ok.T[None, None], table * LOG2_E, NEG)

    def paired(tab):
        pad = jnp.full((n_heads, NA_ROWS // 2, GRID_W, GRID_W), NEG, F32)
        ext = jnp.concatenate([pad, tab, pad], axis=1)
        return jnp.concatenate([ext[:, 1:], ext[:, :-1]], axis=-1)

    dr = jnp.arange(n_dr)
    in_window = (dr >= NA_ROWS // 2 - 1) & (dr < NA_ROWS // 2 - 1 + NA_ROWS)
    interior = jnp.where(in_window[None, :, None, None], table, NEG)

    nblk = rows // NA_QROWS
    i = jnp.arange(NA_QROWS)
    j = jnp.arange(2 * NA_QROWS)
    masks = []
    for rb in (0, 1, nblk - 1):
        r = rb * NA_QROWS + i
        r0 = jnp.clip(r - NA_ROWS // 2, 0, rows - NA_ROWS)
        rk = rb * NA_QROWS - NA_ROWS // 2 + j
        ok = (rk[:, None] >= r0[None, :]) & (rk[:, None] < r0[None, :] + NA_ROWS)
        masks.append(jnp.repeat(jnp.where(ok, 0.0, NEG), GRID_W, axis=1))
    return paired(table), paired(interior), jnp.stack(masks)


def _na_bias_tile(tbl_ref, hh, mask_ref):
    rows = []
    for j in range(2 * NA_QROWS):
        blk = jnp.concatenate([tbl_ref[hh, j - 2 * ii + NA_QROWS - 2] for ii in range(NA_QROWS // 2)], axis=1)
        if mask_ref is not None:
            blk = blk + mask_ref[0, j:j + 1, :]
        rows.append(blk)
    return jnp.concatenate(rows, axis=0)


def _softmax_pv_t(s_parts, vt_parts, extra=None):
    m = s_parts[0].max(axis=0, keepdims=True)
    for s in s_parts[1:]:
        m = jnp.maximum(m, s.max(axis=0, keepdims=True))
    if extra is not None:
        m = jnp.maximum(m, extra)
    acc = None
    for s, vt in zip(s_parts, vt_parts):
        pv = jnp.dot(vt, jnp.exp2((s - m).astype(BF16)), preferred_element_type=F32)
        acc = pv if acc is None else acc + pv
    den = acc[HEAD_DIM:HEAD_DIM + 1]
    if extra is not None:
        den = den + jnp.exp2(extra - m)
    return acc[:HEAD_DIM] / den


def _na_kernel(q_ref, k0, k1, k2, k3, vt0, vt1, vt2, vt3, kc_ref, vct_ref, edge_tbl_ref, tbl_ref, mask_ref, o_ref):
    def attend(tbl, mask):
        q = q_ref[...]
        k = jnp.concatenate([k0[...], k1[...], k2[...], k3[...]], axis=0)
        vt = jnp.concatenate([vt0[...], vt1[...], vt2[...], vt3[...]], axis=2)
        kc, vct = kc_ref[...], vct_ref[...]
        lane = _lane_iota()
        scores = []
        for hh in range(2):
            qh = jnp.where((lane // HEAD_DIM) == hh, q, jnp.zeros_like(q))
            s_loc = lax.dot_general(k, qh, NT_DIMS, preferred_element_type=F32) + _na_bias_tile(tbl, hh, mask)
            scores.append([s_loc, lax.dot_general(kc, qh, NT_DIMS, preferred_element_type=F32)])
        outs = [_softmax_pv_t(scores[hh], [vt[hh], vct[hh]]) for hh in range(2)]
        o_ref[...] = jnp.concatenate(outs, axis=0).T.astype(BF16)

    rb = pl.program_id(2)
    is_edge = (rb == 0) | (rb == pl.num_programs(2) - 1)

    @pl.when(jnp.logical_not(is_edge))
    def _():
        attend(tbl_ref, None)

    @pl.when(is_edge)
    def _():
        attend(edge_tbl_ref, mask_ref)


def _na_attention(p_lat, vt_lat, p_ctx, vt_ctx, bias_tables, nb, n, ctx_len, q_col, k_col, n_heads):
    tq = NA_QROWS * GRID_W
    tw = tq // 2
    qsteps = n // tq
    wsteps = n // tw
    edge_tbl, tbl, masks = bias_tables
    wblock = lambda rb, j: jnp.clip(2 * rb - 1 + j, 0, wsteps - 1)
    kwin = lambda j: pl.BlockSpec((tw, LANES), lambda hp, b, rb: (b * wsteps + wblock(rb, j), k_col + hp))
    vrows = HEAD_DIM + BF16_SUBLANES
    vwin = lambda j: pl.BlockSpec((2, vrows, tw), lambda hp, b, rb: (hp, 0, b * wsteps + wblock(rb, j)))
    variant = lambda rb: jnp.where(rb == 0, 0, jnp.where(rb == qsteps - 1, 2, 1))
    tbl_spec = pl.BlockSpec((2,) + tbl.shape[1:], lambda hp, b, rb: (hp, 0, 0, 0))
    in_specs = ([pl.BlockSpec((tq, LANES), lambda hp, b, rb: (b * qsteps + rb, q_col + hp))]
                + [kwin(j) for j in range(4)] + [vwin(j) for j in range(4)]
                + [pl.BlockSpec((ctx_len, LANES), lambda hp, b, rb: (b, k_col + hp)),
                   pl.BlockSpec((2, vrows, ctx_len), lambda hp, b, rb: (hp, 0, b)),
                   tbl_spec, tbl_spec,
                   pl.BlockSpec((1,) + masks.shape[1:], lambda hp, b, rb: (variant(rb), 0, 0))])
    return pl.pallas_call(
        _na_kernel,
        grid=(n_heads // 2, nb, qsteps),
        in_specs=in_specs,
        out_specs=pl.BlockSpec((tq, LANES), lambda hp, b, rb: (b * qsteps + rb, hp)),
        out_shape=jax.ShapeDtypeStruct((nb * n, n_heads * HEAD_DIM), BF16),
        compiler_params=_params(("parallel", "parallel", "arbitrary")),
        name="na_attention",
    )(*([p_lat] * 5 + [vt_lat] * 4 + [p_ctx, vt_ctx, edge_tbl, tbl, masks]))


def _gqa_kernel(*refs, n_q, n_kv, has_local, has_sink, n_tokens):
    refs = list(refs)
    q_ref = refs.pop(0)
    if has_local:
        k_loc = jnp.concatenate([refs.pop(0)[...] for _ in range(4)], axis=0)
        vt_loc = jnp.concatenate([refs.pop(0)[...] for _ in range(4)], axis=2)
    kc_ref, vct_ref = refs.pop(0), refs.pop(0)
    sink_ref = refs.pop(0) if has_sink else None
    o_ref = refs.pop(0)

    tq = q_ref.shape[0]
    group = n_q // n_kv
    lane = _lane_iota()
    if has_local:
        base = pl.program_id(1) * tq
        kpos = base - BLOCK + lax.broadcasted_iota(jnp.int32, (tq + 2 * BLOCK, 1), 0)
        qpos = base + lax.broadcasted_iota(jnp.int32, (1, tq), 1)
        ok = (jnp.abs(kpos - qpos) <= SWA_WINDOW) & (kpos >= 0) & (kpos < n_tokens)
        band = jnp.where(ok, 0.0, NEG)
        band = jnp.concatenate([band] * group, axis=1)

    outs = []
    for kvh in range(n_kv):
        cg, half = kvh // 2, kvh % 2
        qs = []
        for h in range(kvh * group, (kvh + 1) * group):
            qg = q_ref[:, (h // 2) * LANES:(h // 2 + 1) * LANES]
            if h % 2 != half:
                qg = _swap_halves(qg)
            qs.append(jnp.where((lane // HEAD_DIM) == half, qg, jnp.zeros_like(qg)))
        qcat = jnp.concatenate(qs, axis=0) if group > 1 else qs[0]
        kc = kc_ref[:, cg * LANES:(cg + 1) * LANES]
        s_parts = [lax.dot_general(kc, qcat, NT_DIMS, preferred_element_type=F32)]
        vt_parts = [vct_ref[kvh]]
        if has_local:
            s_parts.append(lax.dot_general(k_loc, qcat, NT_DIMS, preferred_element_type=F32) + band)
            vt_parts.append(vt_loc[kvh])
        extra = None
        if has_sink:
            extra = jnp.concatenate([sink_ref[h:h + 1, :] for h in range(kvh * group, (kvh + 1) * group)
                                     for _ in range(tq // LANES)], axis=1)
        o = _softmax_pv_t(s_parts, vt_parts, extra)
        outs += [o[:, g * tq:(g + 1) * tq] for g in range(group)]
    o_ref[...] = jnp.concatenate(outs, axis=0).T.astype(BF16)


def _gqa_attention(p_q, vt_q, p_ctx, vt_ctx, sink, nb, n, ctx_len, q_col, k_col, n_q, n_kv, has_local, tq):
    qsteps = n // tq
    kv_w = n_kv * HEAD_DIM
    q_w = n_q * HEAD_DIM
    vrows = HEAD_DIM + BF16_SUBLANES
    in_specs = [pl.BlockSpec((tq, q_w), lambda b, i: (b * qsteps + i, q_col * LANES // q_w))]
    args = [p_q]
    if has_local:
        assert tq == 2 * BLOCK and kv_w == LANES
        wsteps = n // BLOCK
        wblock = lambda i, j: jnp.clip(2 * i - 1 + j, 0, wsteps - 1)
        in_specs += [pl.BlockSpec((BLOCK, LANES), functools.partial(
            lambda b, i, j: (b * wsteps + wblock(i, j), k_col), j=j)) for j in range(4)]
        in_specs += [pl.BlockSpec((n_kv, vrows, BLOCK), functools.partial(
            lambda b, i, j: (0, 0, b * wsteps + wblock(i, j)), j=j)) for j in range(4)]
        args += [p_q] * 4 + [vt_q] * 4
    in_specs += [pl.BlockSpec((ctx_len, kv_w), lambda b, i: (b, k_col * LANES // kv_w)),
                 pl.BlockSpec((n_kv, vrows, ctx_len), lambda b, i: (0, 0, b))]
    args += [p_ctx, vt_ctx]
    if sink is not None:
        in_specs.append(_resident(sink.shape, lambda b, i: (0, 0)))
        args.append(sink)
    return pl.pallas_call(
        functools.partial(_gqa_kernel, n_q=n_q, n_kv=n_kv, has_local=has_local,
                          has_sink=sink is not None, n_tokens=n),
        grid=(nb, qsteps),
        in_specs=in_specs,
        out_specs=pl.BlockSpec((tq, q_w), lambda b, i: (b * qsteps + i, 0)),
        out_shape=jax.ShapeDtypeStruct((nb * n, q_w), BF16),
        compiler_params=_params(("parallel", "parallel")),
        name="gqa_attention",
    )(*args)


def _diff_kernel(*refs, n_kblocks, tk, lam_init):
    if n_kblocks:
        q_ref, k_ref, vt_ref, kc_ref, vct_ref, lam_ref, subg_ref, bound_ref, o_ref, m_sc, acc_sc, p_sc = refs
    else:
        q_ref, kc_ref, vct_ref, lam_ref, subg_ref, bound_ref, o_ref, m_sc, acc_sc = refs
    lane = _lane_iota()
    q = q_ref[...]
    zero = jnp.zeros_like(q)
    q_maps = [jnp.where(lane < HEAD_DIM, q, zero), jnp.where(lane >= HEAD_DIM, q, zero)]

    kc, vct = kc_ref[...], vct_ref[0]
    for i in range(2):
        s = lax.dot_general(kc, q_maps[i], NT_DIMS, preferred_element_type=F32)
        m = s.max(axis=0, keepdims=True)
        p = jnp.exp2((s - m).astype(BF16))
        m_sc[i] = m
        acc_sc[i] = jnp.dot(vct, p, preferred_element_type=F32)

    if n_kblocks:
        def scores(kb):
            k = k_ref[pl.ds(pl.multiple_of(kb * tk, tk), tk), :]
            return [lax.dot_general(k, q_maps[i], NT_DIMS, preferred_element_type=F32) for i in range(2)]

        def probs(kb, slot):
            s = scores(kb)
            excess = None
            for i in range(2):
                m_ref = m_sc[i]
                p_sc[slot, i] = jnp.exp2((s[i] - m_ref).astype(BF16))
                over = jnp.max(s[i].max(axis=0, keepdims=True) - m_ref)
                excess = over if excess is None else jnp.maximum(excess, over)
            return excess

        def settle(kb, slot, excess):
            @pl.when(excess > RESCALE_MARGIN)
            def _():
                s = scores(kb)
                for i in range(2):
                    m_prev = m_sc[i]
                    m_new = jnp.maximum(m_prev, s[i].max(axis=0, keepdims=True))
                    p_sc[slot, i] = jnp.exp2((s[i] - m_new).astype(BF16))
                    acc_sc[i] = jnp.exp2(m_prev - m_new) * acc_sc[i]
                    m_sc[i] = m_new

        def accumulate(kb, slot):
            vt = vt_ref[0, :, pl.ds(pl.multiple_of(kb * tk, tk), tk)]
            for i in range(2):
                acc_sc[i] += jnp.dot(vt, p_sc[slot, i], preferred_element_type=F32)

        def probs_unchecked(kb, slot):
            s = scores(kb)
            for i in range(2):
                p_sc[slot, i] = jnp.exp2((s[i] - m_sc[i]).astype(BF16))

        lowest_ref = jnp.minimum(jnp.min(m_sc[0]), jnp.min(m_sc[1]))
        never_rescales = bound_ref[0, 0] - lowest_ref <= RESCALE_MARGIN
        first_excess = probs(0, 0)

        @pl.when(never_rescales)
        def _():
            def body(kb, carry):
                accumulate(kb - 1, (kb - 1) % 2)
                probs_unchecked(kb, kb % 2)
                return carry

            lax.fori_loop(1, n_kblocks, body, 0)
            accumulate(n_kblocks - 1, (n_kblocks - 1) % 2)

        @pl.when(jnp.logical_not(never_rescales))
        def _():
            def body(kb, excess):
                settle(kb - 1, (kb - 1) % 2, excess)
                accumulate(kb - 1, (kb - 1) % 2)
                return probs(kb, kb % 2)

            excess = lax.fori_loop(1, n_kblocks, body, first_excess)
            settle(n_kblocks - 1, (n_kblocks - 1) % 2, excess)
            accumulate(n_kblocks - 1, (n_kblocks - 1) % 2)

    lp = lam_ref[...]
    lam = (jnp.exp(jnp.sum(lp[0:1] * lp[1:2], axis=-1, keepdims=True))
           - jnp.exp(jnp.sum(lp[2:3] * lp[3:4], axis=-1, keepdims=True)) + lam_init)
    dv = 2 * HEAD_DIM
    y0 = acc_sc[0, :dv] / acc_sc[0, dv:dv + 1]
    y1 = acc_sc[1, :dv] / acc_sc[1, dv:dv + 1]
    y = (y0 - lam * y1).T
    ms = jnp.mean(y * y, axis=-1, keepdims=True)
    y = y * lax.rsqrt(ms + EPS) * subg_ref[...] * (1.0 - lam_init)
    o_ref[...] = y.astype(BF16)


def _diff_attention(p_q, p_lat, vt_lat, p_ctx, vt_ctx, lam_rows, subg, score_bound, nb, n_q, n_lat, ctx_len,
                    q_col, k_col, n_heads, lam_init, tq, tk):
    qsteps = n_q // tq
    has_latent = p_lat is not None
    vrows = 2 * HEAD_DIM + BF16_SUBLANES
    in_specs = [pl.BlockSpec((tq, LANES), lambda b, h, i: (b * qsteps + i, q_col + h))]
    args = [p_q]
    if has_latent:
        in_specs += [pl.BlockSpec((n_lat, LANES), lambda b, h, i: (b, k_col + h)),
                     pl.BlockSpec((1, vrows, n_lat), lambda b, h, i: (h, 0, b))]
        args += [p_lat, vt_lat]
    in_specs += [pl.BlockSpec((ctx_len, LANES), lambda b, h, i: (b, k_col + h)),
                 pl.BlockSpec((1, vrows, ctx_len), lambda b, h, i: (h, 0, b)),
                 _resident(lam_rows.shape, lambda b, h, i: (0, 0)),
                 _resident(subg.shape, lambda b, h, i: (0, 0)),
                 pl.BlockSpec(memory_space=pltpu.SMEM)]
    args += [p_ctx, vt_ctx, lam_rows, subg, score_bound]
    return pl.pallas_call(
        functools.partial(_diff_kernel, n_kblocks=n_lat // tk if has_latent else 0, tk=tk, lam_init=lam_init),
        grid=(nb, n_heads, qsteps),
        in_specs=in_specs,
        out_specs=pl.BlockSpec((tq, LANES), lambda b, h, i: (b * qsteps + i, h)),
        out_shape=jax.ShapeDtypeStruct((nb * n_q, n_heads * LANES), BF16),
        scratch_shapes=([pltpu.VMEM((2, 1, tq), F32), pltpu.VMEM((2, vrows, tq), F32)]
                        + ([pltpu.VMEM((2, 2, tk, tq), BF16)] if has_latent else [])),
        compiler_params=_params(("parallel", "parallel", "arbitrary")),
        name="diff_attention",
    )(*args)


def _silu(x):
    return x * (1.0 / (1.0 + jnp.exp(-x)))


def _ffn_kernel(x_ref, g_ref, sc_ref, sh_ref, gate_ref, w1_ref, w3_ref, w2_ref, o_ref):
    x = x_ref[...]
    a = _modulated_norm(x, g_ref[...], sc_ref[0], sh_ref[0]).astype(BF16)
    h1 = jnp.dot(a, w1_ref[...], preferred_element_type=F32)
    h3 = jnp.dot(a, w3_ref[...], preferred_element_type=F32)
    y = jnp.dot((_silu(h1) * h3).astype(BF16), w2_ref[...], preferred_element_type=F32)
    o_ref[...] = x + gate_ref[0] * y


def _ffn(h, nb, g, sc, sh, gate, w1, w3, w2, tm):
    t, d = h.shape
    steps = t // nb // tm
    row = lambda b, i: (b * steps + i, 0)
    vec = pl.BlockSpec((1, 1, d), lambda b, i: (b, 0, 0))
    return pl.pallas_call(
        _ffn_kernel,
        grid=(nb, steps),
        in_specs=[pl.BlockSpec((tm, d), row), _resident((1, d), lambda b, i: (0, 0)), vec, vec, vec,
                  _resident(w1.shape, lambda b, i: (0, 0)),
                  _resident(w3.shape, lambda b, i: (0, 0)),
                  _resident(w2.shape, lambda b, i: (0, 0))],
        out_specs=pl.BlockSpec((tm, d), row),
        out_shape=jax.ShapeDtypeStruct((t, d), F32),
        compiler_params=_params(("parallel", "parallel")),
        name="ffn",
    )(h, g, sc, sh, gate, w1, w3, w2)


def _top2_gates(logits):
    lane = _lane_iota()
    big = jnp.int32(LANES)
    lg = jnp.where(lane < N_EXPERTS, logits, -jnp.inf)
    m1 = lg.max(axis=-1, keepdims=True)
    i1 = jnp.where(lg == m1, lane, big).min(axis=-1, keepdims=True)
    rest = jnp.where(lane == i1, -jnp.inf, lg)
    m2 = rest.max(axis=-1, keepdims=True)
    i2 = jnp.where(rest == m2, lane, big).min(axis=-1, keepdims=True)
    e2 = jnp.exp(m2 - m1)
    den = 1.0 + e2
    return jnp.where(lane == i1, 1.0 / den, 0.0) + jnp.where(lane == i2, e2 / den, 0.0)


def _moe_kernel(x_ref, g_ref, sc_ref, sh_ref, gate_ref, r_ref, w1_ref, w3_ref, w2_ref, o_ref,
                a_sc, gates_sc, rank_sc, acc_sc, *, chunk):
    e = pl.program_id(2)
    tm = x_ref.shape[0]

    @pl.when(e == 0)
    def _():
        a = _modulated_norm(x_ref[...], g_ref[...], sc_ref[0], sh_ref[0])
        a_hi = a.astype(BF16)
        a_lo = (a - a_hi.astype(F32)).astype(BF16)
        r = r_ref[...]
        r_hi = r.astype(BF16)
        r_lo = (r - r_hi.astype(F32)).astype(BF16)
        logits = (jnp.dot(a_hi, r_hi, preferred_element_type=F32)
                  + jnp.dot(a_hi, r_lo, preferred_element_type=F32)
                  + jnp.dot(a_lo, r_hi, preferred_element_type=F32))
        a_sc[...] = a_hi
        gates_t = _top2_gates(logits).T[:EXPERT_ROWS]
        gates_sc[...] = gates_t
        before = (lax.broadcasted_iota(jnp.int32, (tm, tm), 0)
                  < lax.broadcasted_iota(jnp.int32, (tm, tm), 1))
        routed = jnp.where(gates_t > 0.0, 1.0, 0.0).astype(BF16)
        rank_sc[...] = jnp.dot(routed, jnp.where(before, 1.0, 0.0).astype(BF16), preferred_element_type=F32)
        acc_sc[...] = jnp.zeros_like(acc_sc)

    gate_row = gates_sc[pl.ds(e, 1), :]
    rank_row = jnp.where(gate_row > 0.0, rank_sc[pl.ds(e, 1), :], -1.0)
    count = jnp.sum(jnp.where(gate_row > 0.0, 1.0, 0.0)).astype(jnp.int32)
    slot = lax.broadcasted_iota(jnp.int32, (chunk, 1), 0).astype(F32)

    for j in range(pl.cdiv(tm, chunk)):
        @pl.when(count > j * chunk)
        def _():
            pick = rank_row == (slot + float(j * chunk))
            pick_f = jnp.where(pick, 1.0, 0.0)
            xs = jnp.dot(pick_f.astype(BF16), a_sc[...], preferred_element_type=F32).astype(BF16)
            h1 = jnp.dot(xs, w1_ref[0], preferred_element_type=F32)
            h3 = jnp.dot(xs, w3_ref[0], preferred_element_type=F32)
            y = jnp.dot((_silu(h1) * h3).astype(BF16), w2_ref[0], preferred_element_type=F32)
            y = y * jnp.sum(pick_f * gate_row, axis=-1, keepdims=True)
            acc_sc[...] += lax.dot_general(pick_f.astype(BF16), y.astype(BF16), (((0,), (0,)), ((), ())),
                                           preferred_element_type=F32)

    @pl.when(e == pl.num_programs(2) - 1)
    def _():
        o_ref[...] = x_ref[...] + gate_ref[0] * acc_sc[...]


def _moe(h, nb, g, sc, sh, gate, router, w1, w3, w2, tm, chunk):
    t, d = h.shape
    steps = t // nb // tm
    n_e, _, f = w1.shape
    row = lambda b, i, e: (b * steps + i, 0)
    vec = pl.BlockSpec((1, 1, d), lambda b, i, e: (b, 0, 0))
    return pl.pallas_call(
        functools.partial(_moe_kernel, chunk=chunk),
        grid=(nb, steps, n_e),
        in_specs=[pl.BlockSpec((tm, d), row), _resident((1, d), lambda b, i, e: (0, 0)), vec, vec, vec,
                  _resident(router.shape, lambda b, i, e: (0, 0)),
                  pl.BlockSpec((1, d, f), lambda b, i, e: (e, 0, 0)),
                  pl.BlockSpec((1, d, f), lambda b, i, e: (e, 0, 0)),
                  pl.BlockSpec((1, f, d), lambda b, i, e: (e, 0, 0))],
        out_specs=pl.BlockSpec((tm, d), row),
        out_shape=jax.ShapeDtypeStruct((t, d), F32),
        scratch_shapes=[pltpu.VMEM((tm, d), BF16), pltpu.VMEM((EXPERT_ROWS, tm), F32),
                        pltpu.VMEM((EXPERT_ROWS, tm), F32), pltpu.VMEM((tm, d), F32)],
        compiler_params=_params(("parallel", "parallel", "arbitrary")),
        name="moe",
    )(h, g, sc, sh, gate, router, w1, w3, w2)


def _rope_tables(n):
    t = jnp.arange(n)
    pos = jnp.stack([t // GRID_W, t % GRID_W], -1).astype(F32)
    nq = HEAD_DIM // 4
    inv = ROPE_THETA ** (-jnp.arange(nq, dtype=F32) / nq)
    ang = pos[:, :, None] * inv
    cos = jnp.repeat(jnp.cos(ang)[:, :, None, :], 2, axis=2)
    sin = jnp.stack([-jnp.sin(ang), jnp.sin(ang)], axis=2)
    cos = jnp.tile(cos.reshape(n, HEAD_DIM), (1, LANES // HEAD_DIM))
    sin = jnp.tile(sin.reshape(n, HEAD_DIM), (1, LANES // HEAD_DIM))
    return cos, sin


def _head_gain(parts, n_out):
    row = jnp.ones((n_out,), F32)
    for col, n_heads, gain, scale in parts:
        row = lax.dynamic_update_slice(row, jnp.tile(gain.astype(F32) * scale, n_heads), (col,))
    return row.reshape(1, n_out)


def kernel(x, c, ctx, c_ctx, ada_w, ada_b, norm1_g, norm2_g, ev_w_in, ev_conv_w, ev_q_g, ev_k_g, ev_rpb,
           ev_w_out, ffn_w1, ffn_w3, ffn_w2, od_w_in, od_cq_g, od_ck_g, od_sink, od_dq_g, od_dk_g,
           od_lam_q1, od_lam_k1, od_lam_q2, od_lam_k2, od_subln_g, od_w_out, moe_router,
           moe_w1, moe_w3, moe_w2):
    nb, n, d = x.shape
    ctx_len = ctx.shape[1]
    depth = ada_w.shape[0]
    n_slots = d // HEAD_DIM
    conv_ch = d // 2
    na_heads = swa_heads = n_slots // 2
    swa_kv = max(1, swa_heads // 4)
    diff_heads = n_slots // 4
    rows = n // GRID_W
    assert rows % NA_QROWS == 0 and rows >= 2 * NA_QROWS and ctx_len % LANES == 0
    assert n % max(TOKEN_TILE, DIFF_Q_TILE, DIFF_K_TILE, MOE_TOKEN_TILE) == 0
    assert nb + 1 <= F32_SUBLANES

    ev_q_col = 3 * conv_ch
    ev_k_col = ev_q_col + na_heads * HEAD_DIM
    ev_v_col = ev_k_col + na_heads * HEAD_DIM
    ev_n = ev_v_col + na_heads * HEAD_DIM
    od_dq_col = swa_heads * HEAD_DIM
    od_ck_col = od_dq_col + diff_heads * 2 * HEAD_DIM
    od_cv_col = od_ck_col + swa_kv * HEAD_DIM
    od_dk_col = od_cv_col + swa_kv * HEAD_DIM
    od_dv_col = od_dk_col + diff_heads * 2 * HEAD_DIM
    od_n = od_dv_col + diff_heads * 2 * HEAD_DIM
    ev_segs = ((0, ev_q_col, False, 0), (ev_q_col, ev_v_col, True, 0), (ev_v_col, ev_n, False, na_heads))
    od_segs = ((0, od_cv_col, True, 0), (od_cv_col, od_dk_col, False, swa_kv), (od_dk_col, od_dv_col, True, 0),
               (od_dv_col, od_n, False, diff_heads))

    h = x.reshape(nb * n, d)
    hc = ctx.reshape(nb * ctx_len, d)
    tm_lat = TOKEN_TILE
    tm_ctx = ctx_len
    tm_moe = MOE_TOKEN_TILE

    s_rows = jnp.zeros((F32_SUBLANES, d), F32).at[:nb].set(c).at[nb].set(c_ctx)
    mod = _modulation(s_rows, ada_w, ada_b)
    rope_tabs = _rope_tables(n)

    def lat_vec(l, k):
        return mod[l, :nb, k * d:(k + 1) * d].reshape(nb, 1, d)

    def ctx_vec(l, k, copies):
        return jnp.broadcast_to(mod[l, nb, k * d:(k + 1) * d], (copies, 1, d))

    for l in range(depth):
        last = l == depth - 1
        i = l // 2
        g1 = norm1_g[l].reshape(1, d)
        g2 = norm2_g[l].reshape(1, d)
        if l % 2 == 0:
            w_in = ev_w_in[i].astype(BF16)
            gain = _head_gain([(ev_q_col, na_heads, ev_q_g[i], QK_SCALE * LOG2_E), (ev_k_col, na_heads, ev_k_g[i], 1.0)],
                              ev_n)
            p_lat, vt_lat = _norm_proj(h, nb, g1, lat_vec(l, 1), lat_vec(l, 0), w_in, gain, ev_segs, None, tm_lat)
            p_ctx, vt_ctx = _norm_proj(hc, nb, g1, ctx_vec(l, 1, nb), ctx_vec(l, 0, nb), w_in, gain, ev_segs, None,
                                       tm_ctx)
            bias = _na_bias_tables(ev_rpb[i], rows)
            y_na = _na_attention(p_lat, vt_lat, p_ctx, vt_ctx, bias, nb, n, ctx_len, ev_q_col // LANES,
                                 ev_k_col // LANES, na_heads)
            w_out = ev_w_out[i].astype(BF16)
            conv_w = ev_conv_w[i]
            h = _out_proj(h, nb, lat_vec(l, 2), p_lat, y_na, w_out, conv_w, tm_lat)
            if not last:
                y_na_c = _gqa_attention(p_ctx, None, p_ctx, vt_ctx, None, nb, ctx_len, ctx_len, ev_q_col // LANES,
                                        ev_k_col // LANES, na_heads, na_heads, False, ctx_len)
                hc = _out_proj(hc, nb, ctx_vec(l, 2, nb), p_ctx, y_na_c, w_out, conv_w, tm_ctx)
            w1, w3, w2 = ffn_w1[i].astype(BF16), ffn_w3[i].astype(BF16), ffn_w2[i].astype(BF16)
            h = _ffn(h, nb, g2, lat_vec(l, 4), lat_vec(l, 3), lat_vec(l, 5), w1, w3, w2, tm_lat)
            if not last:
                hc = _ffn(hc, 1, g2, ctx_vec(l, 4, 1), ctx_vec(l, 3, 1), ctx_vec(l, 5, 1), w1, w3, w2, tm_lat)
        else:
            lam_init = 0.8 - 0.6 * math.exp(-0.3 * l)
            w_in = od_w_in[i].astype(BF16)
            gain = _head_gain([(0, swa_heads, od_cq_g[i], QK_SCALE * LOG2_E),
                               (od_dq_col, 2 * diff_heads, od_dq_g[i], QK_SCALE * LOG2_E),
                               (od_ck_col, swa_kv, od_ck_g[i], 1.0),
                               (od_dk_col, 2 * diff_heads, od_dk_g[i], 1.0)], od_n)
            p_lat, cvt_lat, vt_lat = _norm_proj(h, nb, g1, lat_vec(l, 1), lat_vec(l, 0), w_in, gain, od_segs, rope_tabs,
                                                tm_lat)
            p_ctx, cvt_ctx, vt_ctx = _norm_proj(hc, nb, g1, ctx_vec(l, 1, nb), ctx_vec(l, 0, nb), w_in, gain, od_segs,
                                                None, tm_ctx)
            sink = jnp.broadcast_to(od_sink[i].astype(F32)[:, None] * LOG2_E, (swa_heads, LANES))
            lam_rows = jnp.zeros((8, LANES), F32).at[:4, :HEAD_DIM].set(
                jnp.stack([od_lam_q1[i], od_lam_k1[i], od_lam_q2[i], od_lam_k2[i]]).astype(F32))
            subg = od_subln_g[i].astype(F32).reshape(1, 2 * HEAD_DIM)
            cols = (od_dq_col // LANES, od_dk_col // LANES)
            score_bound = (HEAD_DIM * QK_SCALE * LOG2_E * BF16_NORM_SLACK * jnp.max(jnp.abs(od_dq_g[i]))
                           * jnp.max(jnp.abs(od_dk_g[i]))).astype(F32).reshape(1, 1)
            y_c = _gqa_attention(p_lat, cvt_lat, p_ctx, cvt_ctx, sink, nb, n, ctx_len, 0, od_ck_col // LANES,
                                 swa_heads, swa_kv, True, 2 * BLOCK)
            y_d = _diff_attention(p_lat, p_lat, vt_lat, p_ctx, vt_ctx, lam_rows, subg, score_bound, nb, n, n, ctx_len,
                                  *cols, diff_heads, lam_init, DIFF_Q_TILE, DIFF_K_TILE)
            w_out = od_w_out[i].astype(BF16)
            h = _out_proj(h, nb, lat_vec(l, 2), y_c, y_d, w_out, None, tm_lat)
            if not last:
                y_c_c = _gqa_attention(p_ctx, None, p_ctx, cvt_ctx, sink, nb, ctx_len, ctx_len, 0,
                                       od_ck_col // LANES, swa_heads, swa_kv, False, ctx_len)
                y_d_c = _diff_attention(p_ctx, None, None, p_ctx, vt_ctx, lam_rows, subg, score_bound, nb, ctx_len, 0,
                                        ctx_len, *cols, diff_heads, lam_init, ctx_len, ctx_len)
                hc = _out_proj(hc, nb, ctx_vec(l, 2, nb), y_c_c, y_d_c, w_out, None, tm_ctx)
            router = jnp.zeros((d, LANES), F32).at[:, :N_EXPERTS].set(moe_router[i])
            w1, w3, w2 = moe_w1[i].astype(BF16), moe_w3[i].astype(BF16), moe_w2[i].astype(BF16)
            h = _moe(h, nb, g2, lat_vec(l, 4), lat_vec(l, 3), lat_vec(l, 5), router, w1, w3, w2, tm_moe, MOE_CHUNK)
            if not last:
                hc = _moe(hc, 1, g2, ctx_vec(l, 4, 1), ctx_vec(l, 3, 1), ctx_vec(l, 5, 1), router, w1, w3, w2,
                          min(tm_moe, nb * ctx_len), MOE_CHUNK)
    return h.reshape(nb, n, d)
```

```python
import functools
import math

import jax
import jax.numpy as jnp
from jax import lax
from jax.experimental import pallas as pl
from jax.experimental.pallas import tpu as pltpu

F32 = jnp.float32
BF16 = jnp.bfloat16

LANES = 128
F32_SUBLANES = 8
BF16_SUBLANES = 16
VMEM_LIMIT = 56 * 1024 * 1024

TOKEN_TILE = 512
MOE_TOKEN_TILE = 1024
DIFF_Q_TILE = 2048
DIFF_K_TILE = 1024

HEAD_DIM = 64
GRID_W = 64
CONV_W = 3
NA_ROWS = 8
NA_COLS = 16
NA_QROWS = 8
SWA_WINDOW = 128
MOE_CHUNK = 256
MOE_WIDE_CHUNK = 384
BLOCK = 128
N_EXPERTS = 8
EXPERT_ROWS = 16
ROPE_THETA = 10000.0
EPS = 1e-6
NEG = -1e30
QK_SCALE = HEAD_DIM ** -0.5
LOG2_E = math.log2(math.e)
BF16_NORM_SLACK = 1.02
RESCALE_MARGIN = 32.0

NT_DIMS = (((1,), (1,)), ((), ()))


def _params(sem):
    return pltpu.CompilerParams(dimension_semantics=sem, vmem_limit_bytes=VMEM_LIMIT)


def _resident(shape, index_map):
    return pl.BlockSpec(shape, index_map, pipeline_mode=pl.Buffered(1))


def _lane_iota():
    return lax.broadcasted_iota(jnp.int32, (1, LANES), 1)


def _swap_halves(x):
    return jnp.concatenate([x[:, HEAD_DIM:], x[:, :HEAD_DIM]], axis=1)


def _modulated_norm(x, g, sc, sh):
    ms = jnp.mean(x * x, axis=-1, keepdims=True)
    return (x * lax.rsqrt(ms + EPS)) * (g * (1.0 + sc)) + sh


def _mod_kernel(s_ref, w_ref, b_ref, o_ref):
    s = s_ref[...]
    s = s * (1.0 / (1.0 + jnp.exp(-s)))
    o_ref[0] = jnp.dot(s.astype(BF16), w_ref[0].astype(BF16), preferred_element_type=F32) + b_ref[0]


def _modulation(s_rows, ada_w, ada_b):
    depth, d, n_out = ada_w.shape
    tn = n_out // 4
    return pl.pallas_call(
        _mod_kernel,
        grid=(depth, n_out // tn),
        in_specs=[pl.BlockSpec(s_rows.shape, lambda l, j: (0, 0)),
                  pl.BlockSpec((1, d, tn), lambda l, j: (l, 0, j)),
                  pl.BlockSpec((1, 1, tn), lambda l, j: (l, 0, j))],
        out_specs=pl.BlockSpec((1, s_rows.shape[0], tn), lambda l, j: (l, 0, j)),
        out_shape=jax.ShapeDtypeStruct((depth, s_rows.shape[0], n_out), F32),
        compiler_params=_params(("arbitrary", "arbitrary")),
        name="modulation",
    )(s_rows, ada_w, ada_b.reshape(depth, 1, n_out))


def _head_sumsq(z):
    r = lax.broadcasted_iota(jnp.int32, (LANES, LANES), 0) // HEAD_DIM
    c = lax.broadcasted_iota(jnp.int32, (LANES, LANES), 1) // HEAD_DIM
    same_head = jnp.where(r == c, 1.0, 0.0).astype(BF16)
    z2 = z * z
    hi = z2.astype(BF16)
    lo = (z2 - hi.astype(F32)).astype(BF16)
    return (jnp.dot(hi, same_head, preferred_element_type=F32)
            + jnp.dot(lo, same_head, preferred_element_type=F32))


def _norm_proj_kernel(*refs, segs, rope, n_vt):
    refs = list(refs)
    vt_refs = [refs.pop() for _ in range(n_vt)][::-1]
    if rope:
        x_ref, g_ref, sc_ref, sh_ref, w_ref, gain_ref, cos_ref, sin_ref, o_ref = refs
    else:
        x_ref, g_ref, sc_ref, sh_ref, w_ref, gain_ref, o_ref = refs
    a = _modulated_norm(x_ref[...], g_ref[...], sc_ref[0], sh_ref[0]).astype(BF16)
    first_half = (_lane_iota() % (HEAD_DIM // 2)) < (HEAD_DIM // 4)
    for c0, c1, normed, vt_heads in segs:
        acc = jnp.dot(a, w_ref[:, c0:c1], preferred_element_type=F32)
        if vt_heads:
            vt_ref = vt_refs.pop(0)
            dv = (c1 - c0) // vt_heads
            vt = acc.T.astype(BF16)
            for hh in range(vt_heads):
                vt_ref[hh, :dv, :] = vt[hh * dv:(hh + 1) * dv]
                vt_ref[hh, dv:, :] = jnp.ones((BF16_SUBLANES, vt.shape[1]), BF16)
        if not normed:
            o_ref[:, c0:c1] = acc.astype(BF16)
            continue
        for j in range((c1 - c0) // LANES):
            z = acc[:, j * LANES:(j + 1) * LANES]
            lo = c0 + j * LANES
            z = z * lax.rsqrt(_head_sumsq(z) * (1.0 / HEAD_DIM) + EPS) * gain_ref[:, lo:lo + LANES]
            if rope:
                partner = jnp.where(first_half,
                                    pltpu.roll(z, LANES - HEAD_DIM // 4, axis=1),
                                    pltpu.roll(z, HEAD_DIM // 4, axis=1))
                z = z * cos_ref[...] + partner * sin_ref[...]
            o_ref[:, lo:lo + LANES] = z.astype(BF16)


def _norm_proj(h, nb, g, sc, sh, w, gain, segs, rope_tabs, tm):
    t, d = h.shape
    n_out = w.shape[1]
    steps = t // nb // tm
    in_specs = [pl.BlockSpec((tm, d), lambda b, i: (b * steps + i, 0)),
                _resident((1, d), lambda b, i: (0, 0)),
                pl.BlockSpec((1, 1, d), lambda b, i: (b, 0, 0)),
                pl.BlockSpec((1, 1, d), lambda b, i: (b, 0, 0)),
                _resident((d, n_out), lambda b, i: (0, 0)),
                _resident((1, n_out), lambda b, i: (0, 0))]
    args = [h, g, sc, sh, w, gain]
    if rope_tabs is not None:
        in_specs += [pl.BlockSpec((tm, LANES), lambda b, i: (i, 0))] * 2
        args += list(rope_tabs)
    out_specs = [pl.BlockSpec((tm, n_out), lambda b, i: (b * steps + i, 0))]
    out_shape = [jax.ShapeDtypeStruct((t, n_out), BF16)]
    for c0, c1, _, vt_heads in segs:
        if vt_heads:
            rows = (c1 - c0) // vt_heads + BF16_SUBLANES
            out_specs.append(pl.BlockSpec((vt_heads, rows, tm), lambda b, i: (0, 0, b * steps + i)))
            out_shape.append(jax.ShapeDtypeStruct((vt_heads, rows, t), BF16))
    return pl.pallas_call(
        functools.partial(_norm_proj_kernel, segs=segs, rope=rope_tabs is not None, n_vt=len(out_specs) - 1),
        grid=(nb, steps),
        in_specs=in_specs,
        out_specs=out_specs,
        out_shape=out_shape,
        compiler_params=_params(("parallel", "parallel")),
        name="norm_proj",
    )(*args)


def _gated_conv(gb_ref, gc_ref, u_ref, gcp_ref, up_ref, gcn_ref, un_ref, cw_ref):
    i, steps = pl.program_id(1), pl.num_programs(1)
    v = gc_ref[...].astype(F32) * u_ref[...].astype(F32)
    tm = v.shape[0]
    last = BF16_SUBLANES - 1
    prev_row = gcp_ref[last:last + 1, :].astype(F32) * up_ref[last:last + 1, :].astype(F32)
    next_row = gcn_ref[0:1, :].astype(F32) * un_ref[0:1, :].astype(F32)
    prev_row = jnp.where(i > 0, prev_row, 0.0)
    next_row = jnp.where(i < steps - 1, next_row, 0.0)
    row = lax.broadcasted_iota(jnp.int32, (tm, 1), 0)
    v_prev = jnp.where(row == 0, prev_row, pltpu.roll(v, 1, axis=0))
    v_next = jnp.where(row == tm - 1, next_row, pltpu.roll(v, tm - 1, axis=0))
    cw = cw_ref[...]
    conv = cw[0:1, :] * v_prev + cw[1:2, :] * v + cw[2:3, :] * v_next
    return gb_ref[...].astype(F32) * conv


def _out_proj_kernel(*refs, conv):
    if conv:
        (gb_ref, gc_ref, u_ref, gcp_ref, up_ref, gcn_ref, un_ref, cw_ref,
         yb_ref, wa_ref, wb_ref, h_ref, gate_ref, o_ref) = refs
        ya = _gated_conv(gb_ref, gc_ref, u_ref, gcp_ref, up_ref, gcn_ref, un_ref, cw_ref).astype(BF16)
    else:
        ya_ref, yb_ref, wa_ref, wb_ref, h_ref, gate_ref, o_ref = refs
        ya = ya_ref[...]
    y = (jnp.dot(ya, wa_ref[...], preferred_element_type=F32)
         + jnp.dot(yb_ref[...], wb_ref[...], preferred_element_type=F32))
    o_ref[...] = h_ref[...] + gate_ref[0] * y


def _out_proj(h, nb, gate, ya_src, yb, w_out, conv_w, tm):
    t, d = h.shape
    steps = t // nb // tm
    wa_rows = w_out.shape[0] - yb.shape[1]
    w_a, w_b = w_out[:wa_rows], w_out[wa_rows:]
    row = lambda b, i: (b * steps + i, 0)
    if conv_w is not None:
        cc = conv_w.shape[1]
        hb = tm // BF16_SUBLANES
        n_halo = t // BF16_SUBLANES
        prev = lambda col: (lambda b, i: (jnp.maximum((b * steps + i) * hb - 1, 0), col))
        nxt = lambda col: (lambda b, i: (jnp.minimum((b * steps + i + 1) * hb, n_halo - 1), col))
        in_specs = [pl.BlockSpec((tm, cc), lambda b, i: (b * steps + i, 0)),
                    pl.BlockSpec((tm, cc), lambda b, i: (b * steps + i, 1)),
                    pl.BlockSpec((tm, cc), lambda b, i: (b * steps + i, 2)),
                    pl.BlockSpec((BF16_SUBLANES, cc), prev(1)),
                    pl.BlockSpec((BF16_SUBLANES, cc), prev(2)),
                    pl.BlockSpec((BF16_SUBLANES, cc), nxt(1)),
                    pl.BlockSpec((BF16_SUBLANES, cc), nxt(2)),
                    _resident(conv_w.shape, lambda b, i: (0, 0))]
        args = [ya_src] * 7 + [conv_w]
    else:
        in_specs = [pl.BlockSpec((tm, wa_rows), row)]
        args = [ya_src]
    in_specs += [pl.BlockSpec((tm, yb.shape[1]), row),
                 _resident(w_a.shape, lambda b, i: (0, 0)),
                 _resident(w_b.shape, lambda b, i: (0, 0)),
                 pl.BlockSpec((tm, d), row),
                 pl.BlockSpec((1, 1, d), lambda b, i: (b, 0, 0))]
    args += [yb, w_a, w_b, h, gate]
    return pl.pallas_call(
        functools.partial(_out_proj_kernel, conv=conv_w is not None),
        grid=(nb, steps),
        in_specs=in_specs,
        out_specs=pl.BlockSpec((tm, d), row),
        out_shape=jax.ShapeDtypeStruct((t, d), F32),
        compiler_params=_params(("parallel", "parallel")),
        name="out_proj",
    )(*args)


def _na_bias_tables(rpb, rows):
    n_heads = rpb.shape[0]
    n_dr = 2 * NA_ROWS - 1
    c = jnp.arange(GRID_W)
    c0 = jnp.clip(c - NA_COLS // 2, 0, GRID_W - NA_COLS)
    col_ok = (c[None, :] >= c0[:, None]) & (c[None, :] < c0[:, None] + NA_COLS)
    dc = jnp.clip(c[None, :] - c[:, None] + NA_COLS - 1, 0, 2 * NA_COLS - 2)
    pick_dc = ((dc[None] == jnp.arange(2 * NA_COLS - 1)[:, None, None]) & col_ok[None]).astype(F32)
    table = jnp.einsum('hrd,dcx->hrxc', rpb.astype(F32), pick_dc, precision=lax.Precision.HIGHEST)
    table = jnp.where(col_ok.T[None, None], table * LOG2_E, NEG)

    def paired(tab):
        pad = jnp.full((n_heads, NA_ROWS // 2, GRID_W, GRID_W), NEG, F32)
        ext = jnp.concatenate([pad, tab, pad], axis=1)
        return jnp.concatenate([ext[:, 1:], ext[:, :-1]], axis=-1)

    dr = jnp.arange(n_dr)
    in_window = (dr >= NA_ROWS // 2 - 1) & (dr < NA_ROWS // 2 - 1 + NA_ROWS)
    interior = jnp.where(in_window[None, :, None, None], table, NEG)

    nblk = rows // NA_QROWS
    i = jnp.arange(NA_QROWS)
    j = jnp.arange(2 * NA_QROWS)
    masks = []
    for rb in (0, 1, nblk - 1):
        r = rb * NA_QROWS + i
        r0 = jnp.clip(r - NA_ROWS // 2, 0, rows - NA_ROWS)
        rk = rb * NA_QROWS - NA_ROWS // 2 + j
        ok = (rk[:, None] >= r0[None, :]) & (rk[:, None] < r0[None, :] + NA_ROWS)
        masks.append(jnp.repeat(jnp.where(ok, 0.0, NEG), GRID_W, axis=1))
    return paired(table), paired(interior), jnp.stack(masks)


def _na_bias_tile(tbl_ref, hh, mask_ref):
    rows = []
    for j in range(2 * NA_QROWS):
        blk = jnp.concatenate([tbl_ref[hh, j - 2 * ii + NA_QROWS - 2] for ii in range(NA_QROWS // 2)], axis=1)
        if mask_ref is not None:
            blk = blk + mask_ref[0, j:j + 1, :]
        rows.append(blk)
    return jnp.concatenate(rows, axis=0)


def _softmax_pv_t(s_parts, vt_parts, extra=None):
    m = s_parts[0].max(axis=0, keepdims=True)
    for s in s_parts[1:]:
        m = jnp.maximum(m, s.max(axis=0, keepdims=True))
    if extra is not None:
        m = jnp.maximum(m, extra)
    acc = None
    for s, vt in zip(s_parts, vt_parts):
        pv = jnp.dot(vt, jnp.exp2((s - m).astype(BF16)), preferred_element_type=F32)
        acc = pv if acc is None else acc + pv
    den = acc[HEAD_DIM:HEAD_DIM + 1]
    if extra is not None:
        den = den + jnp.exp2(extra - m)
    return acc[:HEAD_DIM] / den


def _na_kernel(q_ref, k0, k1, k2, k3, vt0, vt1, vt2, vt3, kc_ref, vct_ref, edge_tbl_ref, tbl_ref, mask_ref, o_ref):
    def attend(tbl, mask):
        q = q_ref[...]
        k = jnp.concatenate([k0[...], k1[...], k2[...], k3[...]], axis=0)
        vt = jnp.concatenate([vt0[...], vt1[...], vt2[...], vt3[...]], axis=2)
        kc, vct = kc_ref[...], vct_ref[...]
        lane = _lane_iota()
        scores = []
        for hh in range(2):
            qh = jnp.where((lane // HEAD_DIM) == hh, q, jnp.zeros_like(q))
            s_loc = lax.dot_general(k, qh, NT_DIMS, preferred_element_type=F32) + _na_bias_tile(tbl, hh, mask)
            scores.append([s_loc, lax.dot_general(kc, qh, NT_DIMS, preferred_element_type=F32)])
        outs = [_softmax_pv_t(scores[hh], [vt[hh], vct[hh]]) for hh in range(2)]
        o_ref[...] = jnp.concatenate(outs, axis=0).T.astype(BF16)

    rb = pl.program_id(2)
    is_edge = (rb == 0) | (rb == pl.num_programs(2) - 1)

    @pl.when(jnp.logical_not(is_edge))
    def _():
        attend(tbl_ref, None)

    @pl.when(is_edge)
    def _():
        attend(edge_tbl_ref, mask_ref)


def _na_attention(p_lat, vt_lat, p_ctx, vt_ctx, bias_tables, nb, n, ctx_len, q_col, k_col, n_heads):
    tq = NA_QROWS * GRID_W
    tw = tq // 2
    qsteps = n // tq
    wsteps = n // tw
    edge_tbl, tbl, masks = bias_tables
    wblock = lambda rb, j: jnp.clip(2 * rb - 1 + j, 0, wsteps - 1)
    kwin = lambda j: pl.BlockSpec((tw, LANES), lambda hp, b, rb: (b * wsteps + wblock(rb, j), k_col + hp))
    vrows = HEAD_DIM + BF16_SUBLANES
    vwin = lambda j: pl.BlockSpec((2, vrows, tw), lambda hp, b, rb: (hp, 0, b * wsteps + wblock(rb, j)))
    variant = lambda rb: jnp.where(rb == 0, 0, jnp.where(rb == qsteps - 1, 2, 1))
    tbl_spec = pl.BlockSpec((2,) + tbl.shape[1:], lambda hp, b, rb: (hp, 0, 0, 0))
    in_specs = ([pl.BlockSpec((tq, LANES), lambda hp, b, rb: (b * qsteps + rb, q_col + hp))]
                + [kwin(j) for j in range(4)] + [vwin(j) for j in range(4)]
                + [pl.BlockSpec((ctx_len, LANES), lambda hp, b, rb: (b, k_col + hp)),
                   pl.BlockSpec((2, vrows, ctx_len), lambda hp, b, rb: (hp, 0, b)),
                   tbl_spec, tbl_spec,
                   pl.BlockSpec((1,) + masks.shape[1:], lambda hp, b, rb: (variant(rb), 0, 0))])
    return pl.pallas_call(
        _na_kernel,
        grid=(n_heads // 2, nb, qsteps),
        in_specs=in_specs,
        out_specs=pl.BlockSpec((tq, LANES), lambda hp, b, rb: (b * qsteps + rb, hp)),
        out_shape=jax.ShapeDtypeStruct((nb * n, n_heads * HEAD_DIM), BF16),
        compiler_params=_params(("parallel", "parallel", "arbitrary")),
        name="na_attention",
    )(*([p_lat] * 5 + [vt_lat] * 4 + [p_ctx, vt_ctx, edge_tbl, tbl, masks]))


def _gqa_kernel(*refs, n_q, n_kv, has_local, has_sink, n_tokens):
    refs = list(refs)
    q_ref = refs.pop(0)
    if has_local:
        k_loc = jnp.concatenate([refs.pop(0)[...] for _ in range(4)], axis=0)
        vt_loc = jnp.concatenate([refs.pop(0)[...] for _ in range(4)], axis=2)
    kc_ref, vct_ref = refs.pop(0), refs.pop(0)
    sink_ref = refs.pop(0) if has_sink else None
    o_ref = refs.pop(0)

    tq = q_ref.shape[0]
    group = n_q // n_kv
    lane = _lane_iota()
    if has_local:
        base = pl.program_id(1) * tq
        kpos = base - BLOCK + lax.broadcasted_iota(jnp.int32, (tq + 2 * BLOCK, 1), 0)
        qpos = base + lax.broadcasted_iota(jnp.int32, (1, tq), 1)
        ok = (jnp.abs(kpos - qpos) <= SWA_WINDOW) & (kpos >= 0) & (kpos < n_tokens)
        band = jnp.where(ok, 0.0, NEG)
        band = jnp.concatenate([band] * group, axis=1)

    outs = []
    for kvh in range(n_kv):
        cg, half = kvh // 2, kvh % 2
        qs = []
        for h in range(kvh * group, (kvh + 1) * group):
            qg = q_ref[:, (h // 2) * LANES:(h // 2 + 1) * LANES]
            if h % 2 != half:
                qg = _swap_halves(qg)
            qs.append(jnp.where((lane // HEAD_DIM) == half, qg, jnp.zeros_like(qg)))
        qcat = jnp.concatenate(qs, axis=0) if group > 1 else qs[0]
        kc = kc_ref[:, cg * LANES:(cg + 1) * LANES]
        s_parts = [lax.dot_general(kc, qcat, NT_DIMS, preferred_element_type=F32)]
        vt_parts = [vct_ref[kvh]]
        if has_local:
            s_parts.append(lax.dot_general(k_loc, qcat, NT_DIMS, preferred_element_type=F32) + band)
            vt_parts.append(vt_loc[kvh])
        extra = None
        if has_sink:
            extra = jnp.concatenate([sink_ref[h:h + 1, :] for h in range(kvh * group, (kvh + 1) * group)
                                     for _ in range(tq // LANES)], axis=1)
        o = _softmax_pv_t(s_parts, vt_parts, extra)
        outs += [o[:, g * tq:(g + 1) * tq] for g in range(group)]
    o_ref[...] = jnp.concatenate(outs, axis=0).T.astype(BF16)


def _gqa_attention(p_q, vt_q, p_ctx, vt_ctx, sink, nb, n, ctx_len, q_col, k_col, n_q, n_kv, has_local, tq):
    qsteps = n // tq
    kv_w = n_kv * HEAD_DIM
    q_w = n_q * HEAD_DIM
    vrows = HEAD_DIM + BF16_SUBLANES
    in_specs = [pl.BlockSpec((tq, q_w), lambda b, i: (b * qsteps + i, q_col * LANES // q_w))]
    args = [p_q]
    if has_local:
        assert tq == 2 * BLOCK and kv_w == LANES
        wsteps = n // BLOCK
        wblock = lambda i, j: jnp.clip(2 * i - 1 + j, 0, wsteps - 1)
        in_specs += [pl.BlockSpec((BLOCK, LANES), functools.partial(
            lambda b, i, j: (b * wsteps + wblock(i, j), k_col), j=j)) for j in range(4)]
        in_specs += [pl.BlockSpec((n_kv, vrows, BLOCK), functools.partial(
            lambda b, i, j: (0, 0, b * wsteps + wblock(i, j)), j=j)) for j in range(4)]
        args += [p_q] * 4 + [vt_q] * 4
    in_specs += [pl.BlockSpec((ctx_len, kv_w), lambda b, i: (b, k_col * LANES // kv_w)),
                 pl.BlockSpec((n_kv, vrows, ctx_len), lambda b, i: (0, 0, b))]
    args += [p_ctx, vt_ctx]
    if sink is not None:
        in_specs.append(_resident(sink.shape, lambda b, i: (0, 0)))
        args.append(sink)
    return pl.pallas_call(
        functools.partial(_gqa_kernel, n_q=n_q, n_kv=n_kv, has_local=has_local,
                          has_sink=sink is not None, n_tokens=n),
        grid=(nb, qsteps),
        in_specs=in_specs,
        out_specs=pl.BlockSpec((tq, q_w), lambda b, i: (b * qsteps + i, 0)),
        out_shape=jax.ShapeDtypeStruct((nb * n, q_w), BF16),
        compiler_params=_params(("parallel", "parallel")),
        name="gqa_attention",
    )(*args)


def _diff_kernel(*refs, n_kblocks, tk, lam_init):
    if n_kblocks:
        q_ref, k_ref, vt_ref, kc_ref, vct_ref, lam_ref, subg_ref, bound_ref, o_ref, m_sc, acc_sc, p_sc = refs
    else:
        q_ref, kc_ref, vct_ref, lam_ref, subg_ref, bound_ref, o_ref, m_sc, acc_sc = refs
    lane = _lane_iota()
    q = q_ref[...]
    zero = jnp.zeros_like(q)
    q_maps = [jnp.where(lane < HEAD_DIM, q, zero), jnp.where(lane >= HEAD_DIM, q, zero)]

    kc, vct = kc_ref[...], vct_ref[0]
    for i in range(2):
        s = lax.dot_general(kc, q_maps[i], NT_DIMS, preferred_element_type=F32)
        m = s.max(axis=0, keepdims=True)
        p = jnp.exp2((s - m).astype(BF16))
        m_sc[i] = m
        acc_sc[i] = jnp.dot(vct, p, preferred_element_type=F32)

    if n_kblocks:
        def scores(kb):
            k = k_ref[pl.ds(pl.multiple_of(kb * tk, tk), tk), :]
            return [lax.dot_general(k, q_maps[i], NT_DIMS, preferred_element_type=F32) for i in range(2)]

        def probs(kb, slot):
            s = scores(kb)
            excess = None
            for i in range(2):
                m_ref = m_sc[i]
                p_sc[slot, i] = jnp.exp2((s[i] - m_ref).astype(BF16))
                over = jnp.max(s[i].max(axis=0, keepdims=True) - m_ref)
                excess = over if excess is None else jnp.maximum(excess, over)
            return excess

        def settle(kb, slot, excess):
            @pl.when(excess > RESCALE_MARGIN)
            def _():
                s = scores(kb)
                for i in range(2):
                    m_prev = m_sc[i]
                    m_new = jnp.maximum(m_prev, s[i].max(axis=0, keepdims=True))
                    p_sc[slot, i] = jnp.exp2((s[i] - m_new).astype(BF16))
                    acc_sc[i] = jnp.exp2(m_prev - m_new) * acc_sc[i]
                    m_sc[i] = m_new

        def accumulate(kb, slot):
            vt = vt_ref[0, :, pl.ds(pl.multiple_of(kb * tk, tk), tk)]
            for i in range(2):
                acc_sc[i] += jnp.dot(vt, p_sc[slot, i], preferred_element_type=F32)

        def probs_unchecked(kb, slot):
            s = scores(kb)
            for i in range(2):
                p_sc[slot, i] = jnp.exp2((s[i] - m_sc[i]).astype(BF16))

        lowest_ref = jnp.minimum(jnp.min(m_sc[0]), jnp.min(m_sc[1]))
        never_rescales = bound_ref[0, 0] - lowest_ref <= RESCALE_MARGIN
        first_excess = probs(0, 0)

        @pl.when(never_rescales)
        def _():
            def body(kb, carry):
                accumulate(kb - 1, (kb - 1) % 2)
                probs_unchecked(kb, kb % 2)
                return carry

            lax.fori_loop(1, n_kblocks, body, 0)
            accumulate(n_kblocks - 1, (n_kblocks - 1) % 2)

        @pl.when(jnp.logical_not(never_rescales))
        def _():
            def body(kb, excess):
                settle(kb - 1, (kb - 1) % 2, excess)
                accumulate(kb - 1, (kb - 1) % 2)
                return probs(kb, kb % 2)

            excess = lax.fori_loop(1, n_kblocks, body, first_excess)
            settle(n_kblocks - 1, (n_kblocks - 1) % 2, excess)
            accumulate(n_kblocks - 1, (n_kblocks - 1) % 2)

    lp = lam_ref[...]
    lam = (jnp.exp(jnp.sum(lp[0:1] * lp[1:2], axis=-1, keepdims=True))
           - jnp.exp(jnp.sum(lp[2:3] * lp[3:4], axis=-1, keepdims=True)) + lam_init)
    dv = 2 * HEAD_DIM
    y0 = acc_sc[0, :dv] / acc_sc[0, dv:dv + 1]
    y1 = acc_sc[1, :dv] / acc_sc[1, dv:dv + 1]
    y = (y0 - lam * y1).T
    ms = jnp.mean(y * y, axis=-1, keepdims=True)
    y = y * lax.rsqrt(ms + EPS) * subg_ref[...] * (1.0 - lam_init)
    o_ref[...] = y.astype(BF16)


def _diff_attention(p_q, p_lat, vt_lat, p_ctx, vt_ctx, lam_rows, subg, score_bound, nb, n_q, n_lat, ctx_len,
                    q_col, k_col, n_heads, lam_init, tq, tk):
    qsteps = n_q // tq
    has_latent = p_lat is not None
    vrows = 2 * HEAD_DIM + BF16_SUBLANES
    in_specs = [pl.BlockSpec((tq, LANES), lambda b, h, i: (b * qsteps + i, q_col + h))]
    args = [p_q]
    if has_latent:
        in_specs += [pl.BlockSpec((n_lat, LANES), lambda b, h, i: (b, k_col + h)),
                     pl.BlockSpec((1, vrows, n_lat), lambda b, h, i: (h, 0, b))]
        args += [p_lat, vt_lat]
    in_specs += [pl.BlockSpec((ctx_len, LANES), lambda b, h, i: (b, k_col + h)),
                 pl.BlockSpec((1, vrows, ctx_len), lambda b, h, i: (h, 0, b)),
                 _resident(lam_rows.shape, lambda b, h, i: (0, 0)),
                 _resident(subg.shape, lambda b, h, i: (0, 0)),
                 pl.BlockSpec(memory_space=pltpu.SMEM)]
    args += [p_ctx, vt_ctx, lam_rows, subg, score_bound]
    return pl.pallas_call(
        functools.partial(_diff_kernel, n_kblocks=n_lat // tk if has_latent else 0, tk=tk, lam_init=lam_init),
        grid=(nb, n_heads, qsteps),
        in_specs=in_specs,
        out_specs=pl.BlockSpec((tq, LANES), lambda b, h, i: (b * qsteps + i, h)),
        out_shape=jax.ShapeDtypeStruct((nb * n_q, n_heads * LANES), BF16),
        scratch_shapes=([pltpu.VMEM((2, 1, tq), F32), pltpu.VMEM((2, vrows, tq), F32)]
                        + ([pltpu.VMEM((2, 2, tk, tq), BF16)] if has_latent else [])),
        compiler_params=_params(("parallel", "parallel", "arbitrary")),
        name="diff_attention",
    )(*args)


def _silu(x):
    return x * (1.0 / (1.0 + jnp.exp(-x)))


def _ffn_kernel(x_ref, g_ref, sc_ref, sh_ref, gate_ref, w1_ref, w3_ref, w2_ref, o_ref):
    x = x_ref[...]
    a = _modulated_norm(x, g_ref[...], sc_ref[0], sh_ref[0]).astype(BF16)
    h1 = jnp.dot(a, w1_ref[...], preferred_element_type=F32)
    h3 = jnp.dot(a, w3_ref[...], preferred_element_type=F32)
    y = jnp.dot((_silu(h1) * h3).astype(BF16), w2_ref[...], preferred_element_type=F32)
    o_ref[...] = x + gate_ref[0] * y


def _ffn(h, nb, g, sc, sh, gate, w1, w3, w2, tm):
    t, d = h.shape
    steps = t // nb // tm
    row = lambda b, i: (b * steps + i, 0)
    vec = pl.BlockSpec((1, 1, d), lambda b, i: (b, 0, 0))
    return pl.pallas_call(
        _ffn_kernel,
        grid=(nb, steps),
        in_specs=[pl.BlockSpec((tm, d), row), _resident((1, d), lambda b, i: (0, 0)), vec, vec, vec,
                  _resident(w1.shape, lambda b, i: (0, 0)),
                  _resident(w3.shape, lambda b, i: (0, 0)),
                  _resident(w2.shape, lambda b, i: (0, 0))],
        out_specs=pl.BlockSpec((tm, d), row),
        out_shape=jax.ShapeDtypeStruct((t, d), F32),
        compiler_params=_params(("parallel", "parallel")),
        name="ffn",
    )(h, g, sc, sh, gate, w1, w3, w2)


def _top2_gates(logits):
    lane = _lane_iota()
    big = jnp.int32(LANES)
    lg = jnp.where(lane < N_EXPERTS, logits, -jnp.inf)
    m1 = lg.max(axis=-1, keepdims=True)
    i1 = jnp.where(lg == m1, lane, big).min(axis=-1, keepdims=True)
    rest = jnp.where(lane == i1, -jnp.inf, lg)
    m2 = rest.max(axis=-1, keepdims=True)
    i2 = jnp.where(rest == m2, lane, big).min(axis=-1, keepdims=True)
    e2 = jnp.exp(m2 - m1)
    den = 1.0 + e2
    return jnp.where(lane == i1, 1.0 / den, 0.0) + jnp.where(lane == i2, e2 / den, 0.0)


def _moe_kernel(x_ref, g_ref, sc_ref, sh_ref, gate_ref, r_ref, w1_ref, w3_ref, w2_ref, o_ref,
                a_sc, gates_sc, rank_sc, acc_sc, *, chunk, wide_chunk):
    e = pl.program_id(2)
    tm = x_ref.shape[0]

    @pl.when(e == 0)
    def _():
        a = _modulated_norm(x_ref[...], g_ref[...], sc_ref[0], sh_ref[0])
        a_hi = a.astype(BF16)
        a_lo = (a - a_hi.astype(F32)).astype(BF16)
        r = r_ref[...]
        r_hi = r.astype(BF16)
        r_lo = (r - r_hi.astype(F32)).astype(BF16)
        logits = (jnp.dot(a_hi, r_hi, preferred_element_type=F32)
                  + jnp.dot(a_hi, r_lo, preferred_element_type=F32)
                  + jnp.dot(a_lo, r_hi, preferred_element_type=F32))
        a_sc[...] = a_hi
        gates_t = _top2_gates(logits).T[:EXPERT_ROWS]
        gates_sc[...] = gates_t
        before = (lax.broadcasted_iota(jnp.int32, (tm, tm), 0)
                  < lax.broadcasted_iota(jnp.int32, (tm, tm), 1))
        routed = jnp.where(gates_t > 0.0, 1.0, 0.0).astype(BF16)
        rank_sc[...] = jnp.dot(routed, jnp.where(before, 1.0, 0.0).astype(BF16), preferred_element_type=F32)
        acc_sc[...] = jnp.zeros_like(acc_sc)

    gate_row = gates_sc[pl.ds(e, 1), :]
    rank_row = jnp.where(gate_row > 0.0, rank_sc[pl.ds(e, 1), :], -1.0)
    count = jnp.sum(jnp.where(gate_row > 0.0, 1.0, 0.0)).astype(jnp.int32)

    def run_expert(first_rank, n_rows):
        slot = lax.broadcasted_iota(jnp.int32, (n_rows, 1), 0).astype(F32) + float(first_rank)
        pick_f = jnp.where(rank_row == slot, 1.0, 0.0)
        xs = jnp.dot(pick_f.astype(BF16), a_sc[...], preferred_element_type=F32).astype(BF16)
        h1 = jnp.dot(xs, w1_ref[0], preferred_element_type=F32)
        h3 = jnp.dot(xs, w3_ref[0], preferred_element_type=F32)
        y = jnp.dot((_silu(h1) * h3).astype(BF16), w2_ref[0], preferred_element_type=F32)
        y = y * jnp.sum(pick_f * gate_row, axis=-1, keepdims=True)
        acc_sc[...] += lax.dot_general(pick_f.astype(BF16), y.astype(BF16), (((0,), (0,)), ((), ())),
                                       preferred_element_type=F32)

    one_wide_pass = (count > chunk) & (count <= wide_chunk)

    @pl.when(one_wide_pass)
    def _():
        run_expert(0, wide_chunk)

    for j in range(pl.cdiv(tm, chunk)):
        @pl.when((count > j * chunk) & jnp.logical_not(one_wide_pass))
        def _():
            run_expert(j * chunk, chunk)

    @pl.when(e == pl.num_programs(2) - 1)
    def _():
        o_ref[...] = x_ref[...] + gate_ref[0] * acc_sc[...]


def _moe(h, nb, g, sc, sh, gate, router, w1, w3, w2, tm, chunk):
    t, d = h.shape
    steps = t // nb // tm
    n_e, _, f = w1.shape
    row = lambda b, i, e: (b * steps + i, 0)
    vec = pl.BlockSpec((1, 1, d), lambda b, i, e: (b, 0, 0))
    return pl.pallas_call(
        functools.partial(_moe_kernel, chunk=chunk, wide_chunk=MOE_WIDE_CHUNK),
        grid=(nb, steps, n_e),
        in_specs=[pl.BlockSpec((tm, d), row), _resident((1, d), lambda b, i, e: (0, 0)), vec, vec, vec,
                  _resident(router.shape, lambda b, i, e: (0, 0)),
                  pl.BlockSpec((1, d, f), lambda b, i, e: (e, 0, 0)),
                  pl.BlockSpec((1, d, f), lambda b, i, e: (e, 0, 0)),
                  pl.BlockSpec((1, f, d), lambda b, i, e: (e, 0, 0))],
        out_specs=pl.BlockSpec((tm, d), row),
        out_shape=jax.ShapeDtypeStruct((t, d), F32),
        scratch_shapes=[pltpu.VMEM((tm, d), BF16), pltpu.VMEM((EXPERT_ROWS, tm), F32),
                        pltpu.VMEM((EXPERT_ROWS, tm), F32), pltpu.VMEM((tm, d), F32)],
        compiler_params=_params(("parallel", "parallel", "arbitrary")),
        name="moe",
    )(h, g, sc, sh, gate, router, w1, w3, w2)


def _rope_tables(n):
    t = jnp.arange(n)
    pos = jnp.stack([t // GRID_W, t % GRID_W], -1).astype(F32)
    nq = HEAD_DIM // 4
    inv = ROPE_THETA ** (-jnp.arange(nq, dtype=F32) / nq)
    ang = pos[:, :, None] * inv
    cos = jnp.repeat(jnp.cos(ang)[:, :, None, :], 2, axis=2)
    sin = jnp.stack([-jnp.sin(ang), jnp.sin(ang)], axis=2)
    cos = jnp.tile(cos.reshape(n, HEAD_DIM), (1, LANES // HEAD_DIM))
    sin = jnp.tile(sin.reshape(n, HEAD_DIM), (1, LANES // HEAD_DIM))
    return cos, sin


def _head_gain(parts, n_out):
    row = jnp.ones((n_out,), F32)
    for col, n_heads, gain, scale in parts:
        row = lax.dynamic_update_slice(row, jnp.tile(gain.astype(F32) * scale, n_heads), (col,))
    return row.reshape(1, n_out)


def kernel(x, c, ctx, c_ctx, ada_w, ada_b, norm1_g, norm2_g, ev_w_in, ev_conv_w, ev_q_g, ev_k_g, ev_rpb,
           ev_w_out, ffn_w1, ffn_w3, ffn_w2, od_w_in, od_cq_g, od_ck_g, od_sink, od_dq_g, od_dk_g,
           od_lam_q1, od_lam_k1, od_lam_q2, od_lam_k2, od_subln_g, od_w_out, moe_router,
           moe_w1, moe_w3, moe_w2):
    nb, n, d = x.shape
    ctx_len = ctx.shape[1]
    depth = ada_w.shape[0]
    n_slots = d // HEAD_DIM
    conv_ch = d // 2
    na_heads = swa_heads = n_slots // 2
    swa_kv = max(1, swa_heads // 4)
    diff_heads = n_slots // 4
    rows = n // GRID_W
    assert rows % NA_QROWS == 0 and rows >= 2 * NA_QROWS and ctx_len % LANES == 0
    assert n % max(TOKEN_TILE, DIFF_Q_TILE, DIFF_K_TILE, MOE_TOKEN_TILE) == 0
    assert nb + 1 <= F32_SUBLANES

    ev_q_col = 3 * conv_ch
    ev_k_col = ev_q_col + na_heads * HEAD_DIM
    ev_v_col = ev_k_col + na_heads * HEAD_DIM
    ev_n = ev_v_col + na_heads * HEAD_DIM
    od_dq_col = swa_heads * HEAD_DIM
    od_ck_col = od_dq_col + diff_heads * 2 * HEAD_DIM
    od_cv_col = od_ck_col + swa_kv * HEAD_DIM
    od_dk_col = od_cv_col + swa_kv * HEAD_DIM
    od_dv_col = od_dk_col + diff_heads * 2 * HEAD_DIM
    od_n = od_dv_col + diff_heads * 2 * HEAD_DIM
    ev_segs = ((0, ev_q_col, False, 0), (ev_q_col, ev_v_col, True, 0), (ev_v_col, ev_n, False, na_heads))
    od_segs = ((0, od_cv_col, True, 0), (od_cv_col, od_dk_col, False, swa_kv), (od_dk_col, od_dv_col, True, 0),
               (od_dv_col, od_n, False, diff_heads))

    h = x.reshape(nb * n, d)
    hc = ctx.reshape(nb * ctx_len, d)
    tm_lat = TOKEN_TILE
    tm_ctx = ctx_len
    tm_moe = MOE_TOKEN_TILE

    s_rows = jnp.zeros((F32_SUBLANES, d), F32).at[:nb].set(c).at[nb].set(c_ctx)
    mod = _modulation(s_rows, ada_w, ada_b)
    rope_tabs = _rope_tables(n)

    def lat_vec(l, k):
        return mod[l, :nb, k * d:(k + 1) * d].reshape(nb, 1, d)

    def ctx_vec(l, k, copies):
        return jnp.broadcast_to(mod[l, nb, k * d:(k + 1) * d], (copies, 1, d))

    for l in range(depth):
        last = l == depth - 1
        i = l // 2
        g1 = norm1_g[l].reshape(1, d)
        g2 = norm2_g[l].reshape(1, d)
        if l % 2 == 0:
            w_in = ev_w_in[i].astype(BF16)
            gain = _head_gain([(ev_q_col, na_heads, ev_q_g[i], QK_SCALE * LOG2_E), (ev_k_col, na_heads, ev_k_g[i], 1.0)],
                              ev_n)
            p_lat, vt_lat = _norm_proj(h, nb, g1, lat_vec(l, 1), lat_vec(l, 0), w_in, gain, ev_segs, None, tm_lat)
            p_ctx, vt_ctx = _norm_proj(hc, nb, g1, ctx_vec(l, 1, nb), ctx_vec(l, 0, nb), w_in, gain, ev_segs, None,
                                       tm_ctx)
            bias = _na_bias_tables(ev_rpb[i], rows)
            y_na = _na_attention(p_lat, vt_lat, p_ctx, vt_ctx, bias, nb, n, ctx_len, ev_q_col // LANES,
                                 ev_k_col // LANES, na_heads)
            w_out = ev_w_out[i].astype(BF16)
            conv_w = ev_conv_w[i]
            h = _out_proj(h, nb, lat_vec(l, 2), p_lat, y_na, w_out, conv_w, tm_lat)
            if not last:
                y_na_c = _gqa_attention(p_ctx, None, p_ctx, vt_ctx, None, nb, ctx_len, ctx_len, ev_q_col // LANES,
                                        ev_k_col // LANES, na_heads, na_heads, False, ctx_len)
                hc = _out_proj(hc, nb, ctx_vec(l, 2, nb), p_ctx, y_na_c, w_out, conv_w, tm_ctx)
            w1, w3, w2 = ffn_w1[i].astype(BF16), ffn_w3[i].astype(BF16), ffn_w2[i].astype(BF16)
            h = _ffn(h, nb, g2, lat_vec(l, 4), lat_vec(l, 3), lat_vec(l, 5), w1, w3, w2, tm_lat)
            if not last:
                hc = _ffn(hc, 1, g2, ctx_vec(l, 4, 1), ctx_vec(l, 3, 1), ctx_vec(l, 5, 1), w1, w3, w2, tm_lat)
        else:
            lam_init = 0.8 - 0.6 * math.exp(-0.3 * l)
            w_in = od_w_in[i].astype(BF16)
            gain = _head_gain([(0, swa_heads, od_cq_g[i], QK_SCALE * LOG2_E),
                               (od_dq_col, 2 * diff_heads, od_dq_g[i], QK_SCALE * LOG2_E),
                               (od_ck_col, swa_kv, od_ck_g[i], 1.0),
                               (od_dk_col, 2 * diff_heads, od_dk_g[i], 1.0)], od_n)
            p_lat, cvt_lat, vt_lat = _norm_proj(h, nb, g1, lat_vec(l, 1), lat_vec(l, 0), w_in, gain, od_segs, rope_tabs,
                                                tm_lat)
            p_ctx, cvt_ctx, vt_ctx = _norm_proj(hc, nb, g1, ctx_vec(l, 1, nb), ctx_vec(l, 0, nb), w_in, gain, od_segs,
                                                None, tm_ctx)
            sink = jnp.broadcast_to(od_sink[i].astype(F32)[:, None] * LOG2_E, (swa_heads, LANES))
            lam_rows = jnp.zeros((8, LANES), F32).at[:4, :HEAD_DIM].set(
                jnp.stack([od_lam_q1[i], od_lam_k1[i], od_lam_q2[i], od_lam_k2[i]]).astype(F32))
            subg = od_subln_g[i].astype(F32).reshape(1, 2 * HEAD_DIM)
            cols = (od_dq_col // LANES, od_dk_col // LANES)
            score_bound = (HEAD_DIM * QK_SCALE * LOG2_E * BF16_NORM_SLACK * jnp.max(jnp.abs(od_dq_g[i]))
                           * jnp.max(jnp.abs(od_dk_g[i]))).astype(F32).reshape(1, 1)
            y_c = _gqa_attention(p_lat, cvt_lat, p_ctx, cvt_ctx, sink, nb, n, ctx_len, 0, od_ck_col // LANES,
                                 swa_heads, swa_kv, True, 2 * BLOCK)
            y_d = _diff_attention(p_lat, p_lat, vt_lat, p_ctx, vt_ctx, lam_rows, subg, score_bound, nb, n, n, ctx_len,
                                  *cols, diff_heads, lam_init, DIFF_Q_TILE, DIFF_K_TILE)
            w_out = od_w_out[i].astype(BF16)
            h = _out_proj(h, nb, lat_vec(l, 2), y_c, y_d, w_out, None, tm_lat)
            if not last:
                y_c_c = _gqa_attention(p_ctx, None, p_ctx, cvt_ctx, sink, nb, ctx_len, ctx_len, 0,
                                       od_ck_col // LANES, swa_heads, swa_kv, False, ctx_len)
                y_d_c = _diff_attention(p_ctx, None, None, p_ctx, vt_ctx, lam_rows, subg, score_bound, nb, ctx_len, 0,
                                        ctx_len, *cols, diff_heads, lam_init, ctx_len, ctx_len)
                hc = _out_proj(hc, nb, ctx_vec(l, 2, nb), y_c_c, y_d_c, w_out, None, tm_ctx)
            router = jnp.zeros((d, LANES), F32).at[:, :N_EXPERTS].set(moe_router[i])
            w1, w3, w2 = moe_w1[i].astype(BF16), moe_w3[i].astype(BF16), moe_w2[i].astype(BF16)
            h = _moe(h, nb, g2, lat_vec(l, 4), lat_vec(l, 3), lat_vec(l, 5), router, w1, w3, w2, tm_moe, MOE_CHUNK)
            if not last:
                hc = _moe(hc, 1, g2, ctx_vec(l, 4, 1), ctx_vec(l, 3, 1), ctx_vec(l, 5, 1), router, w1, w3, w2,
                          min(tm_moe, nb * ctx_len), MOE_CHUNK)
    return h.reshape(nb, n, d)
```

```python
import functools
import math

import jax
import jax.numpy as jnp
from jax import lax
from jax.experimental import pallas as pl
from jax.experimental.pallas import tpu as pltpu

F32 = jnp.float32
BF16 = jnp.bfloat16

LANES = 128
F32_SUBLANES = 8
BF16_SUBLANES = 16
VMEM_LIMIT = 56 * 1024 * 1024

TOKEN_TILE = 512
MOE_TOKEN_TILE = 1024
DIFF_Q_TILE = 2048
DIFF_K_TILE = 1024

HEAD_DIM = 64
GRID_W = 64
CONV_W = 3
NA_ROWS = 8
NA_COLS = 16
NA_QROWS = 8
SWA_WINDOW = 128
MOE_CHUNK = 256
MOE_WIDE_CHUNK = 384
BLOCK = 128
N_EXPERTS = 8
EXPERT_ROWS = 16
ROPE_THETA = 10000.0
EPS = 1e-6
NEG = -1e30
QK_SCALE = HEAD_DIM ** -0.5
LOG2_E = math.log2(math.e)
BF16_NORM_SLACK = 1.02
RESCALE_MARGIN = 32.0

NT_DIMS = (((1,), (1,)), ((), ()))


def _params(sem):
    return pltpu.CompilerParams(dimension_semantics=sem, vmem_limit_bytes=VMEM_LIMIT)


def _resident(shape, index_map):
    return pl.BlockSpec(shape, index_map, pipeline_mode=pl.Buffered(1))


def _lane_iota():
    return lax.broadcasted_iota(jnp.int32, (1, LANES), 1)


def _swap_halves(x):
    return jnp.concatenate([x[:, HEAD_DIM:], x[:, :HEAD_DIM]], axis=1)


def _modulated_norm(x, g, sc, sh):
    ms = jnp.mean(x * x, axis=-1, keepdims=True)
    return (x * lax.rsqrt(ms + EPS)) * (g * (1.0 + sc)) + sh


def _mod_kernel(s_ref, w_ref, b_ref, o_ref):
    s = s_ref[...]
    s = s * (1.0 / (1.0 + jnp.exp(-s)))
    o_ref[0] = jnp.dot(s.astype(BF16), w_ref[0].astype(BF16), preferred_element_type=F32) + b_ref[0]


def _modulation(s_rows, ada_w, ada_b):
    depth, d, n_out = ada_w.shape
    tn = n_out // 4
    return pl.pallas_call(
        _mod_kernel,
        grid=(depth, n_out // tn),
        in_specs=[pl.BlockSpec(s_rows.shape, lambda l, j: (0, 0)),
                  pl.BlockSpec((1, d, tn), lambda l, j: (l, 0, j)),
                  pl.BlockSpec((1, 1, tn), lambda l, j: (l, 0, j))],
        out_specs=pl.BlockSpec((1, s_rows.shape[0], tn), lambda l, j: (l, 0, j)),
        out_shape=jax.ShapeDtypeStruct((depth, s_rows.shape[0], n_out), F32),
        compiler_params=_params(("arbitrary", "arbitrary")),
        name="modulation",
    )(s_rows, ada_w, ada_b.reshape(depth, 1, n_out))


def _head_sumsq(z):
    r = lax.broadcasted_iota(jnp.int32, (LANES, LANES), 0) // HEAD_DIM
    c = lax.broadcasted_iota(jnp.int32, (LANES, LANES), 1) // HEAD_DIM
    same_head = jnp.where(r == c, 1.0, 0.0).astype(BF16)
    z2 = z * z
    hi = z2.astype(BF16)
    lo = (z2 - hi.astype(F32)).astype(BF16)
    return (jnp.dot(hi, same_head, preferred_element_type=F32)
            + jnp.dot(lo, same_head, preferred_element_type=F32))


def _norm_proj_kernel(*refs, segs, rope, n_vt):
    refs = list(refs)
    vt_refs = [refs.pop() for _ in range(n_vt)][::-1]
    if rope:
        x_ref, g_ref, sc_ref, sh_ref, w_ref, gain_ref, cos_ref, sin_ref, o_ref = refs
    else:
        x_ref, g_ref, sc_ref, sh_ref, w_ref, gain_ref, o_ref = refs
    a = _modulated_norm(x_ref[...], g_ref[...], sc_ref[0], sh_ref[0]).astype(BF16)
    first_half = (_lane_iota() % (HEAD_DIM // 2)) < (HEAD_DIM // 4)
    for c0, c1, normed, vt_heads in segs:
        acc = jnp.dot(a, w_ref[:, c0:c1], preferred_element_type=F32)
        if vt_heads:
            vt_ref = vt_refs.pop(0)
            dv = (c1 - c0) // vt_heads
            vt = acc.T.astype(BF16)
            for hh in range(vt_heads):
                vt_ref[hh, :dv, :] = vt[hh * dv:(hh + 1) * dv]
                vt_ref[hh, dv:, :] = jnp.ones((BF16_SUBLANES, vt.shape[1]), BF16)
        if not normed:
            o_ref[:, c0:c1] = acc.astype(BF16)
            continue
        for j in range((c1 - c0) // LANES):
            z = acc[:, j * LANES:(j + 1) * LANES]
            lo = c0 + j * LANES
            z = z * lax.rsqrt(_head_sumsq(z) * (1.0 / HEAD_DIM) + EPS) * gain_ref[:, lo:lo + LANES]
            if rope:
                partner = jnp.where(first_half,
                                    pltpu.roll(z, LANES - HEAD_DIM // 4, axis=1),
                                    pltpu.roll(z, HEAD_DIM // 4, axis=1))
                z = z * cos_ref[...] + partner * sin_ref[...]
            o_ref[:, lo:lo + LANES] = z.astype(BF16)


def _norm_proj(h, nb, g, sc, sh, w, gain, segs, rope_tabs, tm):
    t, d = h.shape
    n_out = w.shape[1]
    steps = t // nb // tm
    in_specs = [pl.BlockSpec((tm, d), lambda b, i: (b * steps + i, 0)),
                _resident((1, d), lambda b, i: (0, 0)),
                pl.BlockSpec((1, 1, d), lambda b, i: (b, 0, 0)),
                pl.BlockSpec((1, 1, d), lambda b, i: (b, 0, 0)),
                _resident((d, n_out), lambda b, i: (0, 0)),
                _resident((1, n_out), lambda b, i: (0, 0))]
    args = [h, g, sc, sh, w, gain]
    if rope_tabs is not None:
        in_specs += [pl.BlockSpec((tm, LANES), lambda b, i: (i, 0))] * 2
        args += list(rope_tabs)
    out_specs = [pl.BlockSpec((tm, n_out), lambda b, i: (b * steps + i, 0))]
    out_shape = [jax.ShapeDtypeStruct((t, n_out), BF16)]
    for c0, c1, _, vt_heads in segs:
        if vt_heads:
            rows = (c1 - c0) // vt_heads + BF16_SUBLANES
            out_specs.append(pl.BlockSpec((vt_heads, rows, tm), lambda b, i: (0, 0, b * steps + i)))
            out_shape.append(jax.ShapeDtypeStruct((vt_heads, rows, t), BF16))
    return pl.pallas_call(
        functools.partial(_norm_proj_kernel, segs=segs, rope=rope_tabs is not None, n_vt=len(out_specs) - 1),
        grid=(nb, steps),
        in_specs=in_specs,
        out_specs=out_specs,
        out_shape=out_shape,
        compiler_params=_params(("parallel", "parallel")),
        name="norm_proj",
    )(*args)


def _gated_conv(gb_ref, gc_ref, u_ref, gcp_ref, up_ref, gcn_ref, un_ref, cw_ref):
    i, steps = pl.program_id(1), pl.num_programs(1)
    v = gc_ref[...].astype(F32) * u_ref[...].astype(F32)
    tm = v.shape[0]
    last = BF16_SUBLANES - 1
    prev_row = gcp_ref[last:last + 1, :].astype(F32) * up_ref[last:last + 1, :].astype(F32)
    next_row = gcn_ref[0:1, :].astype(F32) * un_ref[0:1, :].astype(F32)
    prev_row = jnp.where(i > 0, prev_row, 0.0)
    next_row = jnp.where(i < steps - 1, next_row, 0.0)
    row = lax.broadcasted_iota(jnp.int32, (tm, 1), 0)
    v_prev = jnp.where(row == 0, prev_row, pltpu.roll(v, 1, axis=0))
    v_next = jnp.where(row == tm - 1, next_row, pltpu.roll(v, tm - 1, axis=0))
    cw = cw_ref[...]
    conv = cw[0:1, :] * v_prev + cw[1:2, :] * v + cw[2:3, :] * v_next
    return gb_ref[...].astype(F32) * conv


def _projected_residual(refs, conv):
    if conv:
        (gb_ref, gc_ref, u_ref, gcp_ref, up_ref, gcn_ref, un_ref, cw_ref,
         yb_ref, wa_ref, wb_ref, h_ref, gate_ref) = refs
        ya = _gated_conv(gb_ref, gc_ref, u_ref, gcp_ref, up_ref, gcn_ref, un_ref, cw_ref).astype(BF16)
    else:
        ya_ref, yb_ref, wa_ref, wb_ref, h_ref, gate_ref = refs
        ya = ya_ref[...]
    y = (jnp.dot(ya, wa_ref[...], preferred_element_type=F32)
         + jnp.dot(yb_ref[...], wb_ref[...], preferred_element_type=F32))
    return h_ref[...] + gate_ref[0] * y


def _out_proj_kernel(*refs, conv):
    refs[-1][...] = _projected_residual(refs[:-1], conv)


def _out_proj_operands(h, nb, gate, ya_src, yb, w_out, conv_w, tm):
    t, d = h.shape
    steps = t // nb // tm
    wa_rows = w_out.shape[0] - yb.shape[1]
    w_a, w_b = w_out[:wa_rows], w_out[wa_rows:]
    row = lambda b, i: (b * steps + i, 0)
    if conv_w is not None:
        cc = conv_w.shape[1]
        hb = tm // BF16_SUBLANES
        n_halo = t // BF16_SUBLANES
        prev = lambda col: (lambda b, i: (jnp.maximum((b * steps + i) * hb - 1, 0), col))
        nxt = lambda col: (lambda b, i: (jnp.minimum((b * steps + i + 1) * hb, n_halo - 1), col))
        in_specs = [pl.BlockSpec((tm, cc), lambda b, i: (b * steps + i, 0)),
                    pl.BlockSpec((tm, cc), lambda b, i: (b * steps + i, 1)),
                    pl.BlockSpec((tm, cc), lambda b, i: (b * steps + i, 2)),
                    pl.BlockSpec((BF16_SUBLANES, cc), prev(1)),
                    pl.BlockSpec((BF16_SUBLANES, cc), prev(2)),
                    pl.BlockSpec((BF16_SUBLANES, cc), nxt(1)),
                    pl.BlockSpec((BF16_SUBLANES, cc), nxt(2)),
                    _resident(conv_w.shape, lambda b, i: (0, 0))]
        args = [ya_src] * 7 + [conv_w]
    else:
        in_specs = [pl.BlockSpec((tm, wa_rows), row)]
        args = [ya_src]
    in_specs += [pl.BlockSpec((tm, yb.shape[1]), row),
                 _resident(w_a.shape, lambda b, i: (0, 0)),
                 _resident(w_b.shape, lambda b, i: (0, 0)),
                 pl.BlockSpec((tm, d), row),
                 pl.BlockSpec((1, 1, d), lambda b, i: (b, 0, 0))]
    args += [yb, w_a, w_b, h, gate]
    return in_specs, args


def _out_proj(h, nb, gate, ya_src, yb, w_out, conv_w, tm):
    t, d = h.shape
    steps = t // nb // tm
    in_specs, args = _out_proj_operands(h, nb, gate, ya_src, yb, w_out, conv_w, tm)
    return pl.pallas_call(
        functools.partial(_out_proj_kernel, conv=conv_w is not None),
        grid=(nb, steps),
        in_specs=in_specs,
        out_specs=pl.BlockSpec((tm, d), lambda b, i: (b * steps + i, 0)),
        out_shape=jax.ShapeDtypeStruct((t, d), F32),
        compiler_params=_params(("parallel", "parallel")),
        name="out_proj",
    )(*args)


def _na_bias_tables(rpb, rows):
    n_heads = rpb.shape[0]
    n_dr = 2 * NA_ROWS - 1
    c = jnp.arange(GRID_W)
    c0 = jnp.clip(c - NA_COLS // 2, 0, GRID_W - NA_COLS)
    col_ok = (c[None, :] >= c0[:, None]) & (c[None, :] < c0[:, None] + NA_COLS)
    dc = jnp.clip(c[None, :] - c[:, None] + NA_COLS - 1, 0, 2 * NA_COLS - 2)
    pick_dc = ((dc[None] == jnp.arange(2 * NA_COLS - 1)[:, None, None]) & col_ok[None]).astype(F32)
    table = jnp.einsum('hrd,dcx->hrxc', rpb.astype(F32), pick_dc, precision=lax.Precision.HIGHEST)
    table = jnp.where(col_ok.T[None, None], table * LOG2_E, NEG)

    def paired(tab):
        pad = jnp.full((n_heads, NA_ROWS // 2, GRID_W, GRID_W), NEG, F32)
        ext = jnp.concatenate([pad, tab, pad], axis=1)
        return jnp.concatenate([ext[:, 1:], ext[:, :-1]], axis=-1)

    dr = jnp.arange(n_dr)
    in_window = (dr >= NA_ROWS // 2 - 1) & (dr < NA_ROWS // 2 - 1 + NA_ROWS)
    interior = jnp.where(in_window[None, :, None, None], table, NEG)

    nblk = rows // NA_QROWS
    i = jnp.arange(NA_QROWS)
    j = jnp.arange(2 * NA_QROWS)
    masks = []
    for rb in (0, 1, nblk - 1):
        r = rb * NA_QROWS + i
        r0 = jnp.clip(r - NA_ROWS // 2, 0, rows - NA_ROWS)
        rk = rb * NA_QROWS - NA_ROWS // 2 + j
        ok = (rk[:, None] >= r0[None, :]) & (rk[:, None] < r0[None, :] + NA_ROWS)
        masks.append(jnp.repeat(jnp.where(ok, 0.0, NEG), GRID_W, axis=1))
    return paired(table), paired(interior), jnp.stack(masks)


def _na_bias_tile(tbl_ref, hh, mask_ref):
    rows = []
    for j in range(2 * NA_QROWS):
        blk = jnp.concatenate([tbl_ref[hh, j - 2 * ii + NA_QROWS - 2] for ii in range(NA_QROWS // 2)], axis=1)
        if mask_ref is not None:
            blk = blk + mask_ref[0, j:j + 1, :]
        rows.append(blk)
    return jnp.concatenate(rows, axis=0)


def _softmax_pv_t(s_parts, vt_parts, extra=None):
    m = s_parts[0].max(axis=0, keepdims=True)
    for s in s_parts[1:]:
        m = jnp.maximum(m, s.max(axis=0, keepdims=True))
    if extra is not None:
        m = jnp.maximum(m, extra)
    acc = None
    for s, vt in zip(s_parts, vt_parts):
        pv = jnp.dot(vt, jnp.exp2((s - m).astype(BF16)), preferred_element_type=F32)
        acc = pv if acc is None else acc + pv
    den = acc[HEAD_DIM:HEAD_DIM + 1]
    if extra is not None:
        den = den + jnp.exp2(extra - m)
    return acc[:HEAD_DIM] / den


def _na_kernel(q_ref, k0, k1, k2, k3, vt0, vt1, vt2, vt3, kc_ref, vct_ref, edge_tbl_ref, tbl_ref, mask_ref, o_ref):
    def attend(tbl, mask):
        q = q_ref[...]
        k = jnp.concatenate([k0[...], k1[...], k2[...], k3[...]], axis=0)
        vt = jnp.concatenate([vt0[...], vt1[...], vt2[...], vt3[...]], axis=2)
        kc, vct = kc_ref[...], vct_ref[...]
        lane = _lane_iota()
        scores = []
        for hh in range(2):
            qh = jnp.where((lane // HEAD_DIM) == hh, q, jnp.zeros_like(q))
            s_loc = lax.dot_general(k, qh, NT_DIMS, preferred_element_type=F32) + _na_bias_tile(tbl, hh, mask)
            scores.append([s_loc, lax.dot_general(kc, qh, NT_DIMS, preferred_element_type=F32)])
        outs = [_softmax_pv_t(scores[hh], [vt[hh], vct[hh]]) for hh in range(2)]
        o_ref[...] = jnp.concatenate(outs, axis=0).T.astype(BF16)

    rb = pl.program_id(2)
    is_edge = (rb == 0) | (rb == pl.num_programs(2) - 1)

    @pl.when(jnp.logical_not(is_edge))
    def _():
        attend(tbl_ref, None)

    @pl.when(is_edge)
    def _():
        attend(edge_tbl_ref, mask_ref)


def _na_attention(p_lat, vt_lat, p_ctx, vt_ctx, bias_tables, nb, n, ctx_len, q_col, k_col, n_heads):
    tq = NA_QROWS * GRID_W
    tw = tq // 2
    qsteps = n // tq
    wsteps = n // tw
    edge_tbl, tbl, masks = bias_tables
    wblock = lambda rb, j: jnp.clip(2 * rb - 1 + j, 0, wsteps - 1)
    kwin = lambda j: pl.BlockSpec((tw, LANES), lambda hp, b, rb: (b * wsteps + wblock(rb, j), k_col + hp))
    vrows = HEAD_DIM + BF16_SUBLANES
    vwin = lambda j: pl.BlockSpec((2, vrows, tw), lambda hp, b, rb: (hp, 0, b * wsteps + wblock(rb, j)))
    variant = lambda rb: jnp.where(rb == 0, 0, jnp.where(rb == qsteps - 1, 2, 1))
    tbl_spec = pl.BlockSpec((2,) + tbl.shape[1:], lambda hp, b, rb: (hp, 0, 0, 0))
    in_specs = ([pl.BlockSpec((tq, LANES), lambda hp, b, rb: (b * qsteps + rb, q_col + hp))]
                + [kwin(j) for j in range(4)] + [vwin(j) for j in range(4)]
                + [pl.BlockSpec((ctx_len, LANES), lambda hp, b, rb: (b, k_col + hp)),
                   pl.BlockSpec((2, vrows, ctx_len), lambda hp, b, rb: (hp, 0, b)),
                   tbl_spec, tbl_spec,
                   pl.BlockSpec((1,) + masks.shape[1:], lambda hp, b, rb: (variant(rb), 0, 0))])
    return pl.pallas_call(
        _na_kernel,
        grid=(n_heads // 2, nb, qsteps),
        in_specs=in_specs,
        out_specs=pl.BlockSpec((tq, LANES), lambda hp, b, rb: (b * qsteps + rb, hp)),
        out_shape=jax.ShapeDtypeStruct((nb * n, n_heads * HEAD_DIM), BF16),
        compiler_params=_params(("parallel", "parallel", "arbitrary")),
        name="na_attention",
    )(*([p_lat] * 5 + [vt_lat] * 4 + [p_ctx, vt_ctx, edge_tbl, tbl, masks]))


def _gqa_kernel(*refs, n_q, n_kv, has_local, has_sink, n_tokens):
    refs = list(refs)
    q_ref = refs.pop(0)
    if has_local:
        k_loc = jnp.concatenate([refs.pop(0)[...] for _ in range(4)], axis=0)
        vt_loc = jnp.concatenate([refs.pop(0)[...] for _ in range(4)], axis=2)
    kc_ref, vct_ref = refs.pop(0), refs.pop(0)
    sink_ref = refs.pop(0) if has_sink else None
    o_ref = refs.pop(0)

    tq = q_ref.shape[0]
    group = n_q // n_kv
    lane = _lane_iota()
    if has_local:
        base = pl.program_id(1) * tq
        kpos = base - BLOCK + lax.broadcasted_iota(jnp.int32, (tq + 2 * BLOCK, 1), 0)
        qpos = base + lax.broadcasted_iota(jnp.int32, (1, tq), 1)
        ok = (jnp.abs(kpos - qpos) <= SWA_WINDOW) & (kpos >= 0) & (kpos < n_tokens)
        band = jnp.where(ok, 0.0, NEG)
        band = jnp.concatenate([band] * group, axis=1)

    outs = []
    for kvh in range(n_kv):
        cg, half = kvh // 2, kvh % 2
        qs = []
        for h in range(kvh * group, (kvh + 1) * group):
            qg = q_ref[:, (h // 2) * LANES:(h // 2 + 1) * LANES]
            if h % 2 != half:
                qg = _swap_halves(qg)
            qs.append(jnp.where((lane // HEAD_DIM) == half, qg, jnp.zeros_like(qg)))
        qcat = jnp.concatenate(qs, axis=0) if group > 1 else qs[0]
        kc = kc_ref[:, cg * LANES:(cg + 1) * LANES]
        s_parts = [lax.dot_general(kc, qcat, NT_DIMS, preferred_element_type=F32)]
        vt_parts = [vct_ref[kvh]]
        if has_local:
            s_parts.append(lax.dot_general(k_loc, qcat, NT_DIMS, preferred_element_type=F32) + band)
            vt_parts.append(vt_loc[kvh])
        extra = None
        if has_sink:
            extra = jnp.concatenate([sink_ref[h:h + 1, :] for h in range(kvh * group, (kvh + 1) * group)
                                     for _ in range(tq // LANES)], axis=1)
        o = _softmax_pv_t(s_parts, vt_parts, extra)
        outs += [o[:, g * tq:(g + 1) * tq] for g in range(group)]
    o_ref[...] = jnp.concatenate(outs, axis=0).T.astype(BF16)


def _gqa_attention(p_q, vt_q, p_ctx, vt_ctx, sink, nb, n, ctx_len, q_col, k_col, n_q, n_kv, has_local, tq):
    qsteps = n // tq
    kv_w = n_kv * HEAD_DIM
    q_w = n_q * HEAD_DIM
    vrows = HEAD_DIM + BF16_SUBLANES
    in_specs = [pl.BlockSpec((tq, q_w), lambda b, i: (b * qsteps + i, q_col * LANES // q_w))]
    args = [p_q]
    if has_local:
        assert tq == 2 * BLOCK and kv_w == LANES
        wsteps = n // BLOCK
        wblock = lambda i, j: jnp.clip(2 * i - 1 + j, 0, wsteps - 1)
        in_specs += [pl.BlockSpec((BLOCK, LANES), functools.partial(
            lambda b, i, j: (b * wsteps + wblock(i, j), k_col), j=j)) for j in range(4)]
        in_specs += [pl.BlockSpec((n_kv, vrows, BLOCK), functools.partial(
            lambda b, i, j: (0, 0, b * wsteps + wblock(i, j)), j=j)) for j in range(4)]
        args += [p_q] * 4 + [vt_q] * 4
    in_specs += [pl.BlockSpec((ctx_len, kv_w), lambda b, i: (b, k_col * LANES // kv_w)),
                 pl.BlockSpec((n_kv, vrows, ctx_len), lambda b, i: (0, 0, b))]
    args += [p_ctx, vt_ctx]
    if sink is not None:
        in_specs.append(_resident(sink.shape, lambda b, i: (0, 0)))
        args.append(sink)
    return pl.pallas_call(
        functools.partial(_gqa_kernel, n_q=n_q, n_kv=n_kv, has_local=has_local,
                          has_sink=sink is not None, n_tokens=n),
        grid=(nb, qsteps),
        in_specs=in_specs,
        out_specs=pl.BlockSpec((tq, q_w), lambda b, i: (b * qsteps + i, 0)),
        out_shape=jax.ShapeDtypeStruct((nb * n, q_w), BF16),
        compiler_params=_params(("parallel", "parallel")),
        name="gqa_attention",
    )(*args)


def _diff_kernel(*refs, n_kblocks, tk, lam_init):
    if n_kblocks:
        q_ref, k_ref, vt_ref, kc_ref, vct_ref, lam_ref, subg_ref, bound_ref, o_ref, m_sc, acc_sc, p_sc = refs
    else:
        q_ref, kc_ref, vct_ref, lam_ref, subg_ref, bound_ref, o_ref, m_sc, acc_sc = refs
    lane = _lane_iota()
    q = q_ref[...]
    zero = jnp.zeros_like(q)
    q_maps = [jnp.where(lane < HEAD_DIM, q, zero), jnp.where(lane >= HEAD_DIM, q, zero)]

    kc, vct = kc_ref[...], vct_ref[0]
    for i in range(2):
        s = lax.dot_general(kc, q_maps[i], NT_DIMS, preferred_element_type=F32)
        m = s.max(axis=0, keepdims=True)
        p = jnp.exp2((s - m).astype(BF16))
        m_sc[i] = m
        acc_sc[i] = jnp.dot(vct, p, preferred_element_type=F32)

    if n_kblocks:
        def scores(kb):
            k = k_ref[pl.ds(pl.multiple_of(kb * tk, tk), tk), :]
            return [lax.dot_general(k, q_maps[i], NT_DIMS, preferred_element_type=F32) for i in range(2)]

        def probs(kb, slot):
            s = scores(kb)
            excess = None
            for i in range(2):
                m_ref = m_sc[i]
                p_sc[slot, i] = jnp.exp2((s[i] - m_ref).astype(BF16))
                over = jnp.max(s[i].max(axis=0, keepdims=True) - m_ref)
                excess = over if excess is None else jnp.maximum(excess, over)
            return excess

        def settle(kb, slot, excess):
            @pl.when(excess > RESCALE_MARGIN)
            def _():
                s = scores(kb)
                for i in range(2):
                    m_prev = m_sc[i]
                    m_new = jnp.maximum(m_prev, s[i].max(axis=0, keepdims=True))
                    p_sc[slot, i] = jnp.exp2((s[i] - m_new).astype(BF16))
                    acc_sc[i] = jnp.exp2(m_prev - m_new) * acc_sc[i]
                    m_sc[i] = m_new

        def accumulate(kb, slot):
            vt = vt_ref[0, :, pl.ds(pl.multiple_of(kb * tk, tk), tk)]
            for i in range(2):
                acc_sc[i] += jnp.dot(vt, p_sc[slot, i], preferred_element_type=F32)

        def probs_unchecked(kb, slot):
            s = scores(kb)
            for i in range(2):
                p_sc[slot, i] = jnp.exp2((s[i] - m_sc[i]).astype(BF16))

        lowest_ref = jnp.minimum(jnp.min(m_sc[0]), jnp.min(m_sc[1]))
        never_rescales = bound_ref[0, 0] - lowest_ref <= RESCALE_MARGIN
        first_excess = probs(0, 0)

        @pl.when(never_rescales)
        def _():
            def body(kb, carry):
                accumulate(kb - 1, (kb - 1) % 2)
                probs_unchecked(kb, kb % 2)
                return carry

            lax.fori_loop(1, n_kblocks, body, 0)
            accumulate(n_kblocks - 1, (n_kblocks - 1) % 2)

        @pl.when(jnp.logical_not(never_rescales))
        def _():
            def body(kb, excess):
                settle(kb - 1, (kb - 1) % 2, excess)
                accumulate(kb - 1, (kb - 1) % 2)
                return probs(kb, kb % 2)

            excess = lax.fori_loop(1, n_kblocks, body, first_excess)
            settle(n_kblocks - 1, (n_kblocks - 1) % 2, excess)
            accumulate(n_kblocks - 1, (n_kblocks - 1) % 2)

    lp = lam_ref[...]
    lam = (jnp.exp(jnp.sum(lp[0:1] * lp[1:2], axis=-1, keepdims=True))
           - jnp.exp(jnp.sum(lp[2:3] * lp[3:4], axis=-1, keepdims=True)) + lam_init)
    dv = 2 * HEAD_DIM
    y0 = acc_sc[0, :dv] / acc_sc[0, dv:dv + 1]
    y1 = acc_sc[1, :dv] / acc_sc[1, dv:dv + 1]
    y = (y0 - lam * y1).T
    ms = jnp.mean(y * y, axis=-1, keepdims=True)
    y = y * lax.rsqrt(ms + EPS) * subg_ref[...] * (1.0 - lam_init)
    o_ref[...] = y.astype(BF16)


def _diff_attention(p_q, p_lat, vt_lat, p_ctx, vt_ctx, lam_rows, subg, score_bound, nb, n_q, n_lat, ctx_len,
                    q_col, k_col, n_heads, lam_init, tq, tk):
    qsteps = n_q // tq
    has_latent = p_lat is not None
    vrows = 2 * HEAD_DIM + BF16_SUBLANES
    in_specs = [pl.BlockSpec((tq, LANES), lambda b, h, i: (b * qsteps + i, q_col + h))]
    args = [p_q]
    if has_latent:
        in_specs += [pl.BlockSpec((n_lat, LANES), lambda b, h, i: (b, k_col + h)),
                     pl.BlockSpec((1, vrows, n_lat), lambda b, h, i: (h, 0, b))]
        args += [p_lat, vt_lat]
    in_specs += [pl.BlockSpec((ctx_len, LANES), lambda b, h, i: (b, k_col + h)),
                 pl.BlockSpec((1, vrows, ctx_len), lambda b, h, i: (h, 0, b)),
                 _resident(lam_rows.shape, lambda b, h, i: (0, 0)),
                 _resident(subg.shape, lambda b, h, i: (0, 0)),
                 pl.BlockSpec(memory_space=pltpu.SMEM)]
    args += [p_ctx, vt_ctx, lam_rows, subg, score_bound]
    return pl.pallas_call(
        functools.partial(_diff_kernel, n_kblocks=n_lat // tk if has_latent else 0, tk=tk, lam_init=lam_init),
        grid=(nb, n_heads, qsteps),
        in_specs=in_specs,
        out_specs=pl.BlockSpec((tq, LANES), lambda b, h, i: (b * qsteps + i, h)),
        out_shape=jax.ShapeDtypeStruct((nb * n_q, n_heads * LANES), BF16),
        scratch_shapes=([pltpu.VMEM((2, 1, tq), F32), pltpu.VMEM((2, vrows, tq), F32)]
                        + ([pltpu.VMEM((2, 2, tk, tq), BF16)] if has_latent else [])),
        compiler_params=_params(("parallel", "parallel", "arbitrary")),
        name="diff_attention",
    )(*args)


def _silu(x):
    return x * (1.0 / (1.0 + jnp.exp(-x)))


def _ffn_kernel(*refs, n_proj, conv):
    g_ref, sc_ref, sh_ref, gate_ref, w1_ref, w3_ref, w2_ref, o_ref = refs[n_proj:]
    x = _projected_residual(refs[:n_proj], conv) if n_proj > 1 else refs[0][...]
    a = _modulated_norm(x, g_ref[...], sc_ref[0], sh_ref[0]).astype(BF16)
    h1 = jnp.dot(a, w1_ref[...], preferred_element_type=F32)
    h3 = jnp.dot(a, w3_ref[...], preferred_element_type=F32)
    y = jnp.dot((_silu(h1) * h3).astype(BF16), w2_ref[...], preferred_element_type=F32)
    o_ref[...] = x + gate_ref[0] * y


def _ffn(h, nb, g, sc, sh, gate, w1, w3, w2, tm, proj=None):
    t, d = h.shape
    steps = t // nb // tm
    row = lambda b, i: (b * steps + i, 0)
    vec = pl.BlockSpec((1, 1, d), lambda b, i: (b, 0, 0))
    if proj is None:
        in_specs, args, conv = [pl.BlockSpec((tm, d), row)], [h], False
    else:
        in_specs, args = _out_proj_operands(h, nb, *proj, tm)
        conv = proj[-1] is not None
    n_proj = len(args)
    in_specs = in_specs + [_resident((1, d), lambda b, i: (0, 0)), vec, vec, vec,
                           _resident(w1.shape, lambda b, i: (0, 0)),
                           _resident(w3.shape, lambda b, i: (0, 0)),
                           _resident(w2.shape, lambda b, i: (0, 0))]
    return pl.pallas_call(
        functools.partial(_ffn_kernel, n_proj=n_proj, conv=conv),
        grid=(nb, steps),
        in_specs=in_specs,
        out_specs=pl.BlockSpec((tm, d), row),
        out_shape=jax.ShapeDtypeStruct((t, d), F32),
        compiler_params=_params(("parallel", "parallel")),
        name="ffn",
    )(*args, g, sc, sh, gate, w1, w3, w2)


def _top2_gates(logits):
    lane = _lane_iota()
    big = jnp.int32(LANES)
    lg = jnp.where(lane < N_EXPERTS, logits, -jnp.inf)
    m1 = lg.max(axis=-1, keepdims=True)
    i1 = jnp.where(lg == m1, lane, big).min(axis=-1, keepdims=True)
    rest = jnp.where(lane == i1, -jnp.inf, lg)
    m2 = rest.max(axis=-1, keepdims=True)
    i2 = jnp.where(rest == m2, lane, big).min(axis=-1, keepdims=True)
    e2 = jnp.exp(m2 - m1)
    den = 1.0 + e2
    return jnp.where(lane == i1, 1.0 / den, 0.0) + jnp.where(lane == i2, e2 / den, 0.0)


def _moe_kernel(x_ref, g_ref, sc_ref, sh_ref, gate_ref, r_ref, w1_ref, w3_ref, w2_ref, o_ref,
                a_sc, gates_sc, rank_sc, acc_sc, *, chunk, wide_chunk):
    e = pl.program_id(2)
    tm = x_ref.shape[0]

    @pl.when(e == 0)
    def _():
        a = _modulated_norm(x_ref[...], g_ref[...], sc_ref[0], sh_ref[0])
        a_hi = a.astype(BF16)
        a_lo = (a - a_hi.astype(F32)).astype(BF16)
        r = r_ref[...]
        r_hi = r.astype(BF16)
        r_lo = (r - r_hi.astype(F32)).astype(BF16)
        logits = (jnp.dot(a_hi, r_hi, preferred_element_type=F32)
                  + jnp.dot(a_hi, r_lo, preferred_element_type=F32)
                  + jnp.dot(a_lo, r_hi, preferred_element_type=F32))
        a_sc[...] = a_hi
        gates_t = _top2_gates(logits).T[:EXPERT_ROWS]
        gates_sc[...] = gates_t
        before = (lax.broadcasted_iota(jnp.int32, (tm, tm), 0)
                  < lax.broadcasted_iota(jnp.int32, (tm, tm), 1))
        routed = jnp.where(gates_t > 0.0, 1.0, 0.0).astype(BF16)
        rank_sc[...] = jnp.dot(routed, jnp.where(before, 1.0, 0.0).astype(BF16), preferred_element_type=F32)
        acc_sc[...] = jnp.zeros_like(acc_sc)

    gate_row = gates_sc[pl.ds(e, 1), :]
    rank_row = jnp.where(gate_row > 0.0, rank_sc[pl.ds(e, 1), :], -1.0)
    count = jnp.sum(jnp.where(gate_row > 0.0, 1.0, 0.0)).astype(jnp.int32)

    def run_expert(first_rank, n_rows):
        slot = lax.broadcasted_iota(jnp.int32, (n_rows, 1), 0).astype(F32) + float(first_rank)
        pick_f = jnp.where(rank_row == slot, 1.0, 0.0)
        xs = jnp.dot(pick_f.astype(BF16), a_sc[...], preferred_element_type=F32).astype(BF16)
        h1 = jnp.dot(xs, w1_ref[0], preferred_element_type=F32)
        h3 = jnp.dot(xs, w3_ref[0], preferred_element_type=F32)
        y = jnp.dot((_silu(h1) * h3).astype(BF16), w2_ref[0], preferred_element_type=F32)
        y = y * jnp.sum(pick_f * gate_row, axis=-1, keepdims=True)
        acc_sc[...] += lax.dot_general(pick_f.astype(BF16), y.astype(BF16), (((0,), (0,)), ((), ())),
                                       preferred_element_type=F32)

    one_wide_pass = (count > chunk) & (count <= wide_chunk)

    @pl.when(one_wide_pass)
    def _():
        run_expert(0, wide_chunk)

    for j in range(pl.cdiv(tm, chunk)):
        @pl.when((count > j * chunk) & jnp.logical_not(one_wide_pass))
        def _():
            run_expert(j * chunk, chunk)

    @pl.when(e == pl.num_programs(2) - 1)
    def _():
        o_ref[...] = x_ref[...] + gate_ref[0] * acc_sc[...]


def _moe(h, nb, g, sc, sh, gate, router, w1, w3, w2, tm, chunk):
    t, d = h.shape
    steps = t // nb // tm
    n_e, _, f = w1.shape
    row = lambda b, i, e: (b * steps + i, 0)
    vec = pl.BlockSpec((1, 1, d), lambda b, i, e: (b, 0, 0))
    return pl.pallas_call(
        functools.partial(_moe_kernel, chunk=chunk, wide_chunk=MOE_WIDE_CHUNK),
        grid=(nb, steps, n_e),
        in_specs=[pl.BlockSpec((tm, d), row), _resident((1, d), lambda b, i, e: (0, 0)), vec, vec, vec,
                  _resident(router.shape, lambda b, i, e: (0, 0)),
                  pl.BlockSpec((1, d, f), lambda b, i, e: (e, 0, 0)),
                  pl.BlockSpec((1, d, f), lambda b, i, e: (e, 0, 0)),
                  pl.BlockSpec((1, f, d), lambda b, i, e: (e, 0, 0))],
        out_specs=pl.BlockSpec((tm, d), row),
        out_shape=jax.ShapeDtypeStruct((t, d), F32),
        scratch_shapes=[pltpu.VMEM((tm, d), BF16), pltpu.VMEM((EXPERT_ROWS, tm), F32),
                        pltpu.VMEM((EXPERT_ROWS, tm), F32), pltpu.VMEM((tm, d), F32)],
        compiler_params=_params(("parallel", "parallel", "arbitrary")),
        name="moe",
    )(h, g, sc, sh, gate, router, w1, w3, w2)


def _rope_tables(n):
    t = jnp.arange(n)
    pos = jnp.stack([t // GRID_W, t % GRID_W], -1).astype(F32)
    nq = HEAD_DIM // 4
    inv = ROPE_THETA ** (-jnp.arange(nq, dtype=F32) / nq)
    ang = pos[:, :, None] * inv
    cos = jnp.repeat(jnp.cos(ang)[:, :, None, :], 2, axis=2)
    sin = jnp.stack([-jnp.sin(ang), jnp.sin(ang)], axis=2)
    cos = jnp.tile(cos.reshape(n, HEAD_DIM), (1, LANES // HEAD_DIM))
    sin = jnp.tile(sin.reshape(n, HEAD_DIM), (1, LANES // HEAD_DIM))
    return cos, sin


def _head_gain(parts, n_out):
    row = jnp.ones((n_out,), F32)
    for col, n_heads, gain, scale in parts:
        row = lax.dynamic_update_slice(row, jnp.tile(gain.astype(F32) * scale, n_heads), (col,))
    return row.reshape(1, n_out)


def kernel(x, c, ctx, c_ctx, ada_w, ada_b, norm1_g, norm2_g, ev_w_in, ev_conv_w, ev_q_g, ev_k_g, ev_rpb,
           ev_w_out, ffn_w1, ffn_w3, ffn_w2, od_w_in, od_cq_g, od_ck_g, od_sink, od_dq_g, od_dk_g,
           od_lam_q1, od_lam_k1, od_lam_q2, od_lam_k2, od_subln_g, od_w_out, moe_router,
           moe_w1, moe_w3, moe_w2):
    nb, n, d = x.shape
    ctx_len = ctx.shape[1]
    depth = ada_w.shape[0]
    n_slots = d // HEAD_DIM
    conv_ch = d // 2
    na_heads = swa_heads = n_slots // 2
    swa_kv = max(1, swa_heads // 4)
    diff_heads = n_slots // 4
    rows = n // GRID_W
    assert rows % NA_QROWS == 0 and rows >= 2 * NA_QROWS and ctx_len % LANES == 0
    assert n % max(TOKEN_TILE, DIFF_Q_TILE, DIFF_K_TILE, MOE_TOKEN_TILE) == 0
    assert nb + 1 <= F32_SUBLANES

    ev_q_col = 3 * conv_ch
    ev_k_col = ev_q_col + na_heads * HEAD_DIM
    ev_v_col = ev_k_col + na_heads * HEAD_DIM
    ev_n = ev_v_col + na_heads * HEAD_DIM
    od_dq_col = swa_heads * HEAD_DIM
    od_ck_col = od_dq_col + diff_heads * 2 * HEAD_DIM
    od_cv_col = od_ck_col + swa_kv * HEAD_DIM
    od_dk_col = od_cv_col + swa_kv * HEAD_DIM
    od_dv_col = od_dk_col + diff_heads * 2 * HEAD_DIM
    od_n = od_dv_col + diff_heads * 2 * HEAD_DIM
    ev_segs = ((0, ev_q_col, False, 0), (ev_q_col, ev_v_col, True, 0), (ev_v_col, ev_n, False, na_heads))
    od_segs = ((0, od_cv_col, True, 0), (od_cv_col, od_dk_col, False, swa_kv), (od_dk_col, od_dv_col, True, 0),
               (od_dv_col, od_n, False, diff_heads))

    h = x.reshape(nb * n, d)
    hc = ctx.reshape(nb * ctx_len, d)
    tm_lat = TOKEN_TILE
    tm_ctx = ctx_len
    tm_moe = MOE_TOKEN_TILE

    s_rows = jnp.zeros((F32_SUBLANES, d), F32).at[:nb].set(c).at[nb].set(c_ctx)
    mod = _modulation(s_rows, ada_w, ada_b)
    rope_tabs = _rope_tables(n)

    def lat_vec(l, k):
        return mod[l, :nb, k * d:(k + 1) * d].reshape(nb, 1, d)

    def ctx_vec(l, k, copies):
        return jnp.broadcast_to(mod[l, nb, k * d:(k + 1) * d], (copies, 1, d))

    for l in range(depth):
        last = l == depth - 1
        i = l // 2
        g1 = norm1_g[l].reshape(1, d)
        g2 = norm2_g[l].reshape(1, d)
        if l % 2 == 0:
            w_in = ev_w_in[i].astype(BF16)
            gain = _head_gain([(ev_q_col, na_heads, ev_q_g[i], QK_SCALE * LOG2_E), (ev_k_col, na_heads, ev_k_g[i], 1.0)],
                              ev_n)
            p_lat, vt_lat = _norm_proj(h, nb, g1, lat_vec(l, 1), lat_vec(l, 0), w_in, gain, ev_segs, None, tm_lat)
            p_ctx, vt_ctx = _norm_proj(hc, nb, g1, ctx_vec(l, 1, nb), ctx_vec(l, 0, nb), w_in, gain, ev_segs, None,
                                       tm_ctx)
            bias = _na_bias_tables(ev_rpb[i], rows)
            y_na = _na_attention(p_lat, vt_lat, p_ctx, vt_ctx, bias, nb, n, ctx_len, ev_q_col // LANES,
                                 ev_k_col // LANES, na_heads)
            w_out = ev_w_out[i].astype(BF16)
            conv_w = ev_conv_w[i]
            if not last:
                y_na_c = _gqa_attention(p_ctx, None, p_ctx, vt_ctx, None, nb, ctx_len, ctx_len, ev_q_col // LANES,
                                        ev_k_col // LANES, na_heads, na_heads, False, ctx_len)
                hc = _out_proj(hc, nb, ctx_vec(l, 2, nb), p_ctx, y_na_c, w_out, conv_w, tm_ctx)
            w1, w3, w2 = ffn_w1[i].astype(BF16), ffn_w3[i].astype(BF16), ffn_w2[i].astype(BF16)
            h = _ffn(h, nb, g2, lat_vec(l, 4), lat_vec(l, 3), lat_vec(l, 5), w1, w3, w2, tm_lat,
                     proj=(lat_vec(l, 2), p_lat, y_na, w_out, conv_w))
            if not last:
                hc = _ffn(hc, 1, g2, ctx_vec(l, 4, 1), ctx_vec(l, 3, 1), ctx_vec(l, 5, 1), w1, w3, w2, tm_lat)
        else:
            lam_init = 0.8 - 0.6 * math.exp(-0.3 * l)
            w_in = od_w_in[i].astype(BF16)
            gain = _head_gain([(0, swa_heads, od_cq_g[i], QK_SCALE * LOG2_E),
                               (od_dq_col, 2 * diff_heads, od_dq_g[i], QK_SCALE * LOG2_E),
                               (od_ck_col, swa_kv, od_ck_g[i], 1.0),
                               (od_dk_col, 2 * diff_heads, od_dk_g[i], 1.0)], od_n)
            p_lat, cvt_lat, vt_lat = _norm_proj(h, nb, g1, lat_vec(l, 1), lat_vec(l, 0), w_in, gain, od_segs, rope_tabs,
                                                tm_lat)
            p_ctx, cvt_ctx, vt_ctx = _norm_proj(hc, nb, g1, ctx_vec(l, 1, nb), ctx_vec(l, 0, nb), w_in, gain, od_segs,
                                                None, tm_ctx)
            sink = jnp.broadcast_to(od_sink[i].astype(F32)[:, None] * LOG2_E, (swa_heads, LANES))
            lam_rows = jnp.zeros((8, LANES), F32).at[:4, :HEAD_DIM].set(
                jnp.stack([od_lam_q1[i], od_lam_k1[i], od_lam_q2[i], od_lam_k2[i]]).astype(F32))
            subg = od_subln_g[i].astype(F32).reshape(1, 2 * HEAD_DIM)
            cols = (od_dq_col // LANES, od_dk_col // LANES)
            score_bound = (HEAD_DIM * QK_SCALE * LOG2_E * BF16_NORM_SLACK * jnp.max(jnp.abs(od_dq_g[i]))
                           * jnp.max(jnp.abs(od_dk_g[i]))).astype(F32).reshape(1, 1)
            y_c = _gqa_attention(p_lat, cvt_lat, p_ctx, cvt_ctx, sink, nb, n, ctx_len, 0, od_ck_col // LANES,
                                 swa_heads, swa_kv, True, 2 * BLOCK)
            y_d = _diff_attention(p_lat, p_lat, vt_lat, p_ctx, vt_ctx, lam_rows, subg, score_bound, nb, n, n, ctx_len,
                                  *cols, diff_heads, lam_init, DIFF_Q_TILE, DIFF_K_TILE)
            w_out = od_w_out[i].astype(BF16)
            h = _out_proj(h, nb, lat_vec(l, 2), y_c, y_d, w_out, None, tm_lat)
            if not last:
                y_c_c = _gqa_attention(p_ctx, None, p_ctx, cvt_ctx, sink, nb, ctx_len, ctx_len, 0,
                                       od_ck_col // LANES, swa_heads, swa_kv, False, ctx_len)
                y_d_c = _diff_attention(p_ctx, None, None, p_ctx, vt_ctx, lam_rows, subg, score_bound, nb, ctx_len, 0,
                                        ctx_len, *cols, diff_heads, lam_init, ctx_len, ctx_len)
                hc = _out_proj(hc, nb, ctx_vec(l, 2, nb), y_c_c, y_d_c, w_out, None, tm_ctx)
            router = jnp.zeros((d, LANES), F32).at[:, :N_EXPERTS].set(moe_router[i])
            w1, w3, w2 = moe_w1[i].astype(BF16), moe_w3[i].astype(BF16), moe_w2[i].astype(BF16)
            h = _moe(h, nb, g2, lat_vec(l, 4), lat_vec(l, 3), lat_vec(l, 5), router, w1, w3, w2, tm_moe, MOE_CHUNK)
            if not last:
                hc = _moe(hc, 1, g2, ctx_vec(l, 4, 1), ctx_vec(l, 3, 1), ctx_vec(l, 5, 1), router, w1, w3, w2,
                          min(tm_moe, nb * ctx_len), MOE_CHUNK)
    return h.reshape(nb, n, d)
```

```python
import functools
import math

import jax
import jax.numpy as jnp
from jax import lax
from jax.experimental import pallas as pl
from jax.experimental.pallas import tpu as pltpu

F32 = jnp.float32
BF16 = jnp.bfloat16

LANES = 128
F32_SUBLANES = 8
BF16_SUBLANES = 16
VMEM_LIMIT = 56 * 1024 * 1024

TOKEN_TILE = 512
MOE_TOKEN_TILE = 1024
DIFF_Q_TILE = 2048
DIFF_K_TILE = 1024

HEAD_DIM = 64
GRID_W = 64
CONV_W = 3
NA_ROWS = 8
NA_COLS = 16
NA_QROWS = 8
SWA_WINDOW = 128
MOE_CHUNK = 256
MOE_WIDE_CHUNK = 384
BLOCK = 128
N_EXPERTS = 8
EXPERT_ROWS = 16
ROPE_THETA = 10000.0
EPS = 1e-6
NEG = -1e30
QK_SCALE = HEAD_DIM ** -0.5
LOG2_E = math.log2(math.e)
BF16_NORM_SLACK = 1.02
RESCALE_MARGIN = 32.0

NT_DIMS = (((1,), (1,)), ((), ()))


def _params(sem):
    return pltpu.CompilerParams(dimension_semantics=sem, vmem_limit_bytes=VMEM_LIMIT)


def _resident(shape, index_map):
    return pl.BlockSpec(shape, index_map, pipeline_mode=pl.Buffered(1))


def _lane_iota():
    return lax.broadcasted_iota(jnp.int32, (1, LANES), 1)


def _swap_halves(x):
    return jnp.concatenate([x[:, HEAD_DIM:], x[:, :HEAD_DIM]], axis=1)


def _modulated_norm(x, g, sc, sh):
    ms = jnp.mean(x * x, axis=-1, keepdims=True)
    return (x * lax.rsqrt(ms + EPS)) * (g * (1.0 + sc)) + sh


def _mod_kernel(s_ref, w_ref, b_ref, o_ref):
    s = s_ref[...]
    s = s * (1.0 / (1.0 + jnp.exp(-s)))
    o_ref[0] = jnp.dot(s.astype(BF16), w_ref[0].astype(BF16), preferred_element_type=F32) + b_ref[0]


def _modulation(s_rows, ada_w, ada_b):
    depth, d, n_out = ada_w.shape
    tn = n_out // 4
    return pl.pallas_call(
        _mod_kernel,
        grid=(depth, n_out // tn),
        in_specs=[pl.BlockSpec(s_rows.shape, lambda l, j: (0, 0)),
                  pl.BlockSpec((1, d, tn), lambda l, j: (l, 0, j)),
                  pl.BlockSpec((1, 1, tn), lambda l, j: (l, 0, j))],
        out_specs=pl.BlockSpec((1, s_rows.shape[0], tn), lambda l, j: (l, 0, j)),
        out_shape=jax.ShapeDtypeStruct((depth, s_rows.shape[0], n_out), F32),
        compiler_params=_params(("arbitrary", "arbitrary")),
        name="modulation",
    )(s_rows, ada_w, ada_b.reshape(depth, 1, n_out))


def _head_sumsq(z):
    r = lax.broadcasted_iota(jnp.int32, (LANES, LANES), 0) // HEAD_DIM
    c = lax.broadcasted_iota(jnp.int32, (LANES, LANES), 1) // HEAD_DIM
    same_head = jnp.where(r == c, 1.0, 0.0).astype(BF16)
    z2 = z * z
    hi = z2.astype(BF16)
    lo = (z2 - hi.astype(F32)).astype(BF16)
    return (jnp.dot(hi, same_head, preferred_element_type=F32)
            + jnp.dot(lo, same_head, preferred_element_type=F32))


def _norm_proj_kernel(*refs, segs, rope, n_vt):
    refs = list(refs)
    vt_refs = [refs.pop() for _ in range(n_vt)][::-1]
    if rope:
        x_ref, g_ref, sc_ref, sh_ref, w_ref, gain_ref, cos_ref, sin_ref, o_ref = refs
    else:
        x_ref, g_ref, sc_ref, sh_ref, w_ref, gain_ref, o_ref = refs
    a = _modulated_norm(x_ref[...], g_ref[...], sc_ref[0], sh_ref[0]).astype(BF16)
    first_half = (_lane_iota() % (HEAD_DIM // 2)) < (HEAD_DIM // 4)
    for c0, c1, normed, vt_heads in segs:
        acc = jnp.dot(a, w_ref[:, c0:c1], preferred_element_type=F32)
        if vt_heads:
            vt_ref = vt_refs.pop(0)
            dv = (c1 - c0) // vt_heads
            vt = acc.T.astype(BF16)
            for hh in range(vt_heads):
                vt_ref[hh, :dv, :] = vt[hh * dv:(hh + 1) * dv]
                vt_ref[hh, dv:, :] = jnp.ones((BF16_SUBLANES, vt.shape[1]), BF16)
        if not normed:
            o_ref[:, c0:c1] = acc.astype(BF16)
            continue
        for j in range((c1 - c0) // LANES):
            z = acc[:, j * LANES:(j + 1) * LANES]
            lo = c0 + j * LANES
            z = z * lax.rsqrt(_head_sumsq(z) * (1.0 / HEAD_DIM) + EPS) * gain_ref[:, lo:lo + LANES]
            if rope:
                partner = jnp.where(first_half,
                                    pltpu.roll(z, LANES - HEAD_DIM // 4, axis=1),
                                    pltpu.roll(z, HEAD_DIM // 4, axis=1))
                z = z * cos_ref[...] + partner * sin_ref[...]
            o_ref[:, lo:lo + LANES] = z.astype(BF16)


def _norm_proj(h, nb, g, sc, sh, w, gain, segs, rope_tabs, tm):
    t, d = h.shape
    n_out = w.shape[1]
    steps = t // nb // tm
    in_specs = [pl.BlockSpec((tm, d), lambda b, i: (b * steps + i, 0)),
                _resident((1, d), lambda b, i: (0, 0)),
                pl.BlockSpec((1, 1, d), lambda b, i: (b, 0, 0)),
                pl.BlockSpec((1, 1, d), lambda b, i: (b, 0, 0)),
                _resident((d, n_out), lambda b, i: (0, 0)),
                _resident((1, n_out), lambda b, i: (0, 0))]
    args = [h, g, sc, sh, w, gain]
    if rope_tabs is not None:
        in_specs += [pl.BlockSpec((tm, LANES), lambda b, i: (i, 0))] * 2
        args += list(rope_tabs)
    out_specs = [pl.BlockSpec((tm, n_out), lambda b, i: (b * steps + i, 0))]
    out_shape = [jax.ShapeDtypeStruct((t, n_out), BF16)]
    for c0, c1, _, vt_heads in segs:
        if vt_heads:
            rows = (c1 - c0) // vt_heads + BF16_SUBLANES
            out_specs.append(pl.BlockSpec((vt_heads, rows, tm), lambda b, i: (0, 0, b * steps + i)))
            out_shape.append(jax.ShapeDtypeStruct((vt_heads, rows, t), BF16))
    return pl.pallas_call(
        functools.partial(_norm_proj_kernel, segs=segs, rope=rope_tabs is not None, n_vt=len(out_specs) - 1),
        grid=(nb, steps),
        in_specs=in_specs,
        out_specs=out_specs,
        out_shape=out_shape,
        compiler_params=_params(("parallel", "parallel")),
        name="norm_proj",
    )(*args)


def _gated_conv(gb_ref, gc_ref, u_ref, gcp_ref, up_ref, gcn_ref, un_ref, cw_ref):
    i, steps = pl.program_id(1), pl.num_programs(1)
    v = gc_ref[...].astype(F32) * u_ref[...].astype(F32)
    tm = v.shape[0]
    last = BF16_SUBLANES - 1
    prev_row = gcp_ref[last:last + 1, :].astype(F32) * up_ref[last:last + 1, :].astype(F32)
    next_row = gcn_ref[0:1, :].astype(F32) * un_ref[0:1, :].astype(F32)
    prev_row = jnp.where(i > 0, prev_row, 0.0)
    next_row = jnp.where(i < steps - 1, next_row, 0.0)
    row = lax.broadcasted_iota(jnp.int32, (tm, 1), 0)
    v_prev = jnp.where(row == 0, prev_row, pltpu.roll(v, 1, axis=0))
    v_next = jnp.where(row == tm - 1, next_row, pltpu.roll(v, tm - 1, axis=0))
    cw = cw_ref[...]
    conv = cw[0:1, :] * v_prev + cw[1:2, :] * v + cw[2:3, :] * v_next
    return gb_ref[...].astype(F32) * conv


def _projected_residual(refs, conv):
    if conv:
        (gb_ref, gc_ref, u_ref, gcp_ref, up_ref, gcn_ref, un_ref, cw_ref,
         yb_ref, wa_ref, wb_ref, h_ref, gate_ref) = refs
        ya = _gated_conv(gb_ref, gc_ref, u_ref, gcp_ref, up_ref, gcn_ref, un_ref, cw_ref).astype(BF16)
    else:
        ya_ref, yb_ref, wa_ref, wb_ref, h_ref, gate_ref = refs
        ya = ya_ref[...]
    y = (jnp.dot(ya, wa_ref[...], preferred_element_type=F32)
         + jnp.dot(yb_ref[...], wb_ref[...], preferred_element_type=F32))
    return h_ref[...] + gate_ref[0] * y


def _out_proj_kernel(*refs, conv):
    refs[-1][...] = _projected_residual(refs[:-1], conv)


def _out_proj_operands(h, nb, gate, ya_src, yb, w_out, conv_w, tm):
    t, d = h.shape
    steps = t // nb // tm
    wa_rows = w_out.shape[0] - yb.shape[1]
    w_a, w_b = w_out[:wa_rows], w_out[wa_rows:]
    row = lambda b, i: (b * steps + i, 0)
    if conv_w is not None:
        cc = conv_w.shape[1]
        hb = tm // BF16_SUBLANES
        n_halo = t // BF16_SUBLANES
        prev = lambda col: (lambda b, i: (jnp.maximum((b * steps + i) * hb - 1, 0), col))
        nxt = lambda col: (lambda b, i: (jnp.minimum((b * steps + i + 1) * hb, n_halo - 1), col))
        in_specs = [pl.BlockSpec((tm, cc), lambda b, i: (b * steps + i, 0)),
                    pl.BlockSpec((tm, cc), lambda b, i: (b * steps + i, 1)),
                    pl.BlockSpec((tm, cc), lambda b, i: (b * steps + i, 2)),
                    pl.BlockSpec((BF16_SUBLANES, cc), prev(1)),
                    pl.BlockSpec((BF16_SUBLANES, cc), prev(2)),
                    pl.BlockSpec((BF16_SUBLANES, cc), nxt(1)),
                    pl.BlockSpec((BF16_SUBLANES, cc), nxt(2)),
                    _resident(conv_w.shape, lambda b, i: (0, 0))]
        args = [ya_src] * 7 + [conv_w]
    else:
        in_specs = [pl.BlockSpec((tm, wa_rows), row)]
        args = [ya_src]
    in_specs += [pl.BlockSpec((tm, yb.shape[1]), row),
                 _resident(w_a.shape, lambda b, i: (0, 0)),
                 _resident(w_b.shape, lambda b, i: (0, 0)),
                 pl.BlockSpec((tm, d), row),
                 pl.BlockSpec((1, 1, d), lambda b, i: (b, 0, 0))]
    args += [yb, w_a, w_b, h, gate]
    return in_specs, args


def _out_proj(h, nb, gate, ya_src, yb, w_out, conv_w, tm):
    t, d = h.shape
    steps = t // nb // tm
    in_specs, args = _out_proj_operands(h, nb, gate, ya_src, yb, w_out, conv_w, tm)
    return pl.pallas_call(
        functools.partial(_out_proj_kernel, conv=conv_w is not None),
        grid=(nb, steps),
        in_specs=in_specs,
        out_specs=pl.BlockSpec((tm, d), lambda b, i: (b * steps + i, 0)),
        out_shape=jax.ShapeDtypeStruct((t, d), F32),
        compiler_params=_params(("parallel", "parallel")),
        name="out_proj",
    )(*args)


def _na_bias_tables(rpb, rows):
    n_heads = rpb.shape[0]
    n_dr = 2 * NA_ROWS - 1
    c = jnp.arange(GRID_W)
    c0 = jnp.clip(c - NA_COLS // 2, 0, GRID_W - NA_COLS)
    col_ok = (c[None, :] >= c0[:, None]) & (c[None, :] < c0[:, None] + NA_COLS)
    dc = jnp.clip(c[None, :] - c[:, None] + NA_COLS - 1, 0, 2 * NA_COLS - 2)
    pick_dc = ((dc[None] == jnp.arange(2 * NA_COLS - 1)[:, None, None]) & col_ok[None]).astype(F32)
    table = jnp.einsum('hrd,dcx->hrxc', rpb.astype(F32), pick_dc, precision=lax.Precision.HIGHEST)
    table = jnp.where(col_ok.T[None, None], table * LOG2_E, NEG)

    def paired(tab):
        pad = jnp.full((n_heads, NA_ROWS // 2, GRID_W, GRID_W), NEG, F32)
        ext = jnp.concatenate([pad, tab, pad], axis=1)
        return jnp.concatenate([ext[:, 1:], ext[:, :-1]], axis=-1)

    dr = jnp.arange(n_dr)
    in_window = (dr >= NA_ROWS // 2 - 1) & (dr < NA_ROWS // 2 - 1 + NA_ROWS)
    interior = jnp.where(in_window[None, :, None, None], table, NEG)

    nblk = rows // NA_QROWS
    i = jnp.arange(NA_QROWS)
    j = jnp.arange(2 * NA_QROWS)
    masks = []
    for rb in (0, 1, nblk - 1):
        r = rb * NA_QROWS + i
        r0 = jnp.clip(r - NA_ROWS // 2, 0, rows - NA_ROWS)
        rk = rb * NA_QROWS - NA_ROWS // 2 + j
        ok = (rk[:, None] >= r0[None, :]) & (rk[:, None] < r0[None, :] + NA_ROWS)
        masks.append(jnp.repeat(jnp.where(ok, 0.0, NEG), GRID_W, axis=1))
    return paired(table), paired(interior), jnp.stack(masks)


def _na_bias_tile(tbl_ref, hh, mask_ref):
    rows = []
    for j in range(2 * NA_QROWS):
        blk = jnp.concatenate([tbl_ref[hh, j - 2 * ii + NA_QROWS - 2] for ii in range(NA_QROWS // 2)], axis=1)
        if mask_ref is not None:
            blk = blk + mask_ref[0, j:j + 1, :]
        rows.append(blk)
    return jnp.concatenate(rows, axis=0)


def _softmax_pv_t(s_parts, vt_parts, extra=None):
    m = s_parts[0].max(axis=0, keepdims=True)
    for s in s_parts[1:]:
        m = jnp.maximum(m, s.max(axis=0, keepdims=True))
    if extra is not None:
        m = jnp.maximum(m, extra)
    acc = None
    for s, vt in zip(s_parts, vt_parts):
        pv = jnp.dot(vt, jnp.exp2((s - m).astype(BF16)), preferred_element_type=F32)
        acc = pv if acc is None else acc + pv
    den = acc[HEAD_DIM:HEAD_DIM + 1]
    if extra is not None:
        den = den + jnp.exp2(extra - m)
    return acc[:HEAD_DIM] / den


def _na_kernel(q_ref, k0, k1, k2, k3, vt0, vt1, vt2, vt3, kc_ref, vct_ref, edge_tbl_ref, tbl_ref, mask_ref, o_ref):
    def attend(tbl, mask):
        q = q_ref[...]
        k = jnp.concatenate([k0[...], k1[...], k2[...], k3[...]], axis=0)
        vt = jnp.concatenate([vt0[...], vt1[...], vt2[...], vt3[...]], axis=2)
        kc, vct = kc_ref[...], vct_ref[...]
        lane = _lane_iota()
        scores = []
        for hh in range(2):
            qh = jnp.where((lane // HEAD_DIM) == hh, q, jnp.zeros_like(q))
            s_loc = lax.dot_general(k, qh, NT_DIMS, preferred_element_type=F32) + _na_bias_tile(tbl, hh, mask)
            scores.append([s_loc, lax.dot_general(kc, qh, NT_DIMS, preferred_element_type=F32)])
        outs = [_softmax_pv_t(scores[hh], [vt[hh], vct[hh]]) for hh in range(2)]
        o_ref[...] = jnp.concatenate(outs, axis=0).T.astype(BF16)

    rb = pl.program_id(2)
    is_edge = (rb == 0) | (rb == pl.num_programs(2) - 1)

    @pl.when(jnp.logical_not(is_edge))
    def _():
        attend(tbl_ref, None)

    @pl.when(is_edge)
    def _():
        attend(edge_tbl_ref, mask_ref)


def _na_attention(p_lat, vt_lat, p_ctx, vt_ctx, bias_tables, nb, n, ctx_len, q_col, k_col, n_heads):
    tq = NA_QROWS * GRID_W
    tw = tq // 2
    qsteps = n // tq
    wsteps = n // tw
    edge_tbl, tbl, masks = bias_tables
    wblock = lambda rb, j: jnp.clip(2 * rb - 1 + j, 0, wsteps - 1)
    kwin = lambda j: pl.BlockSpec((tw, LANES), lambda hp, b, rb: (b * wsteps + wblock(rb, j), k_col + hp))
    vrows = HEAD_DIM + BF16_SUBLANES
    vwin = lambda j: pl.BlockSpec((2, vrows, tw), lambda hp, b, rb: (hp, 0, b * wsteps + wblock(rb, j)))
    variant = lambda rb: jnp.where(rb == 0, 0, jnp.where(rb == qsteps - 1, 2, 1))
    tbl_spec = pl.BlockSpec((2,) + tbl.shape[1:], lambda hp, b, rb: (hp, 0, 0, 0))
    in_specs = ([pl.BlockSpec((tq, LANES), lambda hp, b, rb: (b * qsteps + rb, q_col + hp))]
                + [kwin(j) for j in range(4)] + [vwin(j) for j in range(4)]
                + [pl.BlockSpec((ctx_len, LANES), lambda hp, b, rb: (b, k_col + hp)),
                   pl.BlockSpec((2, vrows, ctx_len), lambda hp, b, rb: (hp, 0, b)),
                   tbl_spec, tbl_spec,
                   pl.BlockSpec((1,) + masks.shape[1:], lambda hp, b, rb: (variant(rb), 0, 0))])
    return pl.pallas_call(
        _na_kernel,
        grid=(n_heads // 2, nb, qsteps),
        in_specs=in_specs,
        out_specs=pl.BlockSpec((tq, LANES), lambda hp, b, rb: (b * qsteps + rb, hp)),
        out_shape=jax.ShapeDtypeStruct((nb * n, n_heads * HEAD_DIM), BF16),
        compiler_params=_params(("parallel", "parallel", "arbitrary")),
        name="na_attention",
    )(*([p_lat] * 5 + [vt_lat] * 4 + [p_ctx, vt_ctx, edge_tbl, tbl, masks]))


def _gqa_kernel(*refs, n_q, n_kv, has_local, has_sink, n_tokens):
    refs = list(refs)
    q_ref = refs.pop(0)
    if has_local:
        k_loc = jnp.concatenate([refs.pop(0)[...] for _ in range(4)], axis=0)
        vt_loc = jnp.concatenate([refs.pop(0)[...] for _ in range(4)], axis=2)
    kc_ref, vct_ref = refs.pop(0), refs.pop(0)
    sink_ref = refs.pop(0) if has_sink else None
    o_ref = refs.pop(0)

    tq = q_ref.shape[0]
    group = n_q // n_kv
    lane = _lane_iota()
    if has_local:
        base = pl.program_id(1) * tq
        kpos = base - BLOCK + lax.broadcasted_iota(jnp.int32, (tq + 2 * BLOCK, 1), 0)
        qpos = base + lax.broadcasted_iota(jnp.int32, (1, tq), 1)
        ok = (jnp.abs(kpos - qpos) <= SWA_WINDOW) & (kpos >= 0) & (kpos < n_tokens)
        band = jnp.where(ok, 0.0, NEG)
        band = jnp.concatenate([band] * group, axis=1)

    outs = []
    for kvh in range(n_kv):
        cg, half = kvh // 2, kvh % 2
        qs = []
        for h in range(kvh * group, (kvh + 1) * group):
            qg = q_ref[:, (h // 2) * LANES:(h // 2 + 1) * LANES]
            if h % 2 != half:
                qg = _swap_halves(qg)
            qs.append(jnp.where((lane // HEAD_DIM) == half, qg, jnp.zeros_like(qg)))
        qcat = jnp.concatenate(qs, axis=0) if group > 1 else qs[0]
        kc = kc_ref[:, cg * LANES:(cg + 1) * LANES]
        s_parts = [lax.dot_general(kc, qcat, NT_DIMS, preferred_element_type=F32)]
        vt_parts = [vct_ref[kvh]]
        if has_local:
            s_parts.append(lax.dot_general(k_loc, qcat, NT_DIMS, preferred_element_type=F32) + band)
            vt_parts.append(vt_loc[kvh])
        extra = None
        if has_sink:
            extra = jnp.concatenate([sink_ref[h:h + 1, :] for h in range(kvh * group, (kvh + 1) * group)
                                     for _ in range(tq // LANES)], axis=1)
        o = _softmax_pv_t(s_parts, vt_parts, extra)
        outs += [o[:, g * tq:(g + 1) * tq] for g in range(group)]
    o_ref[...] = jnp.concatenate(outs, axis=0).T.astype(BF16)


def _gqa_attention(p_q, vt_q, p_ctx, vt_ctx, sink, nb, n, ctx_len, q_col, k_col, n_q, n_kv, has_local, tq):
    qsteps = n // tq
    kv_w = n_kv * HEAD_DIM
    q_w = n_q * HEAD_DIM
    vrows = HEAD_DIM + BF16_SUBLANES
    in_specs = [pl.BlockSpec((tq, q_w), lambda b, i: (b * qsteps + i, q_col * LANES // q_w))]
    args = [p_q]
    if has_local:
        assert tq == 2 * BLOCK and kv_w == LANES
        wsteps = n // BLOCK
        wblock = lambda i, j: jnp.clip(2 * i - 1 + j, 0, wsteps - 1)
        in_specs += [pl.BlockSpec((BLOCK, LANES), functools.partial(
            lambda b, i, j: (b * wsteps + wblock(i, j), k_col), j=j)) for j in range(4)]
        in_specs += [pl.BlockSpec((n_kv, vrows, BLOCK), functools.partial(
            lambda b, i, j: (0, 0, b * wsteps + wblock(i, j)), j=j)) for j in range(4)]
        args += [p_q] * 4 + [vt_q] * 4
    in_specs += [pl.BlockSpec((ctx_len, kv_w), lambda b, i: (b, k_col * LANES // kv_w)),
                 pl.BlockSpec((n_kv, vrows, ctx_len), lambda b, i: (0, 0, b))]
    args += [p_ctx, vt_ctx]
    if sink is not None:
        in_specs.append(_resident(sink.shape, lambda b, i: (0, 0)))
        args.append(sink)
    return pl.pallas_call(
        functools.partial(_gqa_kernel, n_q=n_q, n_kv=n_kv, has_local=has_local,
                          has_sink=sink is not None, n_tokens=n),
        grid=(nb, qsteps),
        in_specs=in_specs,
        out_specs=pl.BlockSpec((tq, q_w), lambda b, i: (b * qsteps + i, 0)),
        out_shape=jax.ShapeDtypeStruct((nb * n, q_w), BF16),
        compiler_params=_params(("parallel", "parallel")),
        name="gqa_attention",
    )(*args)


def _diff_kernel(*refs, n_kblocks, tk, lam_init):
    if n_kblocks:
        q_ref, k_ref, vt_ref, kc_ref, vct_ref, lam_ref, subg_ref, bound_ref, o_ref, m_sc, acc_sc, p_sc = refs
    else:
        q_ref, kc_ref, vct_ref, lam_ref, subg_ref, bound_ref, o_ref, m_sc, acc_sc = refs
    lane = _lane_iota()
    q = q_ref[...]
    zero = jnp.zeros_like(q)
    q_maps = [jnp.where(lane < HEAD_DIM, q, zero), jnp.where(lane >= HEAD_DIM, q, zero)]

    kc, vct = kc_ref[...], vct_ref[0]
    for i in range(2):
        s = lax.dot_general(kc, q_maps[i], NT_DIMS, preferred_element_type=F32)
        m = s.max(axis=0, keepdims=True)
        p = jnp.exp2((s - m).astype(BF16))
        m_sc[i] = m
        acc_sc[i] = jnp.dot(vct, p, preferred_element_type=F32)

    if n_kblocks:
        def scores(kb):
            k = k_ref[pl.ds(pl.multiple_of(kb * tk, tk), tk), :]
            return [lax.dot_general(k, q_maps[i], NT_DIMS, preferred_element_type=F32) for i in range(2)]

        def probs(kb, slot):
            s = scores(kb)
            excess = None
            for i in range(2):
                m_ref = m_sc[i]
                p_sc[slot, i] = jnp.exp2((s[i] - m_ref).astype(BF16))
                over = jnp.max(s[i].max(axis=0, keepdims=True) - m_ref)
                excess = over if excess is None else jnp.maximum(excess, over)
            return excess

        def settle(kb, slot, excess):
            @pl.when(excess > RESCALE_MARGIN)
            def _():
                s = scores(kb)
                for i in range(2):
                    m_prev = m_sc[i]
                    m_new = jnp.maximum(m_prev, s[i].max(axis=0, keepdims=True))
                    p_sc[slot, i] = jnp.exp2((s[i] - m_new).astype(BF16))
                    acc_sc[i] = jnp.exp2(m_prev - m_new) * acc_sc[i]
                    m_sc[i] = m_new

        def accumulate(kb, slot):
            vt = vt_ref[0, :, pl.ds(pl.multiple_of(kb * tk, tk), tk)]
            for i in range(2):
                acc_sc[i] += jnp.dot(vt, p_sc[slot, i], preferred_element_type=F32)

        def probs_unchecked(kb, slot):
            s = scores(kb)
            for i in range(2):
                p_sc[slot, i] = jnp.exp2((s[i] - m_sc[i]).astype(BF16))

        lowest_ref = jnp.minimum(jnp.min(m_sc[0]), jnp.min(m_sc[1]))
        never_rescales = bound_ref[0, 0] - lowest_ref <= RESCALE_MARGIN
        first_excess = probs(0, 0)

        @pl.when(never_rescales)
        def _():
            def body(kb, carry):
                accumulate(kb - 1, (kb - 1) % 2)
                probs_unchecked(kb, kb % 2)
                return carry

            lax.fori_loop(1, n_kblocks, body, 0)
            accumulate(n_kblocks - 1, (n_kblocks - 1) % 2)

        @pl.when(jnp.logical_not(never_rescales))
        def _():
            def body(kb, excess):
                settle(kb - 1, (kb - 1) % 2, excess)
                accumulate(kb - 1, (kb - 1) % 2)
                return probs(kb, kb % 2)

            excess = lax.fori_loop(1, n_kblocks, body, first_excess)
            settle(n_kblocks - 1, (n_kblocks - 1) % 2, excess)
            accumulate(n_kblocks - 1, (n_kblocks - 1) % 2)

    lp = lam_ref[...]
    lam = (jnp.exp(jnp.sum(lp[0:1] * lp[1:2], axis=-1, keepdims=True))
           - jnp.exp(jnp.sum(lp[2:3] * lp[3:4], axis=-1, keepdims=True)) + lam_init)
    dv = 2 * HEAD_DIM
    y0 = acc_sc[0, :dv] / acc_sc[0, dv:dv + 1]
    y1 = acc_sc[1, :dv] / acc_sc[1, dv:dv + 1]
    y = (y0 - lam * y1).T
    ms = jnp.mean(y * y, axis=-1, keepdims=True)
    y = y * lax.rsqrt(ms + EPS) * subg_ref[...] * (1.0 - lam_init)
    o_ref[...] = y.astype(BF16)


def _diff_attention(p_q, p_lat, vt_lat, p_ctx, vt_ctx, lam_rows, subg, score_bound, nb, n_q, n_lat, ctx_len,
                    q_col, k_col, n_heads, lam_init, tq, tk):
    qsteps = n_q // tq
    has_latent = p_lat is not None
    vrows = 2 * HEAD_DIM + BF16_SUBLANES
    in_specs = [pl.BlockSpec((tq, LANES), lambda b, h, i: (b * qsteps + i, q_col + h))]
    args = [p_q]
    if has_latent:
        in_specs += [pl.BlockSpec((n_lat, LANES), lambda b, h, i: (b, k_col + h)),
                     pl.BlockSpec((1, vrows, n_lat), lambda b, h, i: (h, 0, b))]
        args += [p_lat, vt_lat]
    in_specs += [pl.BlockSpec((ctx_len, LANES), lambda b, h, i: (b, k_col + h)),
                 pl.BlockSpec((1, vrows, ctx_len), lambda b, h, i: (h, 0, b)),
                 _resident(lam_rows.shape, lambda b, h, i: (0, 0)),
                 _resident(subg.shape, lambda b, h, i: (0, 0)),
                 pl.BlockSpec(memory_space=pltpu.SMEM)]
    args += [p_ctx, vt_ctx, lam_rows, subg, score_bound]
    return pl.pallas_call(
        functools.partial(_diff_kernel, n_kblocks=n_lat // tk if has_latent else 0, tk=tk, lam_init=lam_init),
        grid=(nb, n_heads, qsteps),
        in_specs=in_specs,
        out_specs=pl.BlockSpec((tq, LANES), lambda b, h, i: (b * qsteps + i, h)),
        out_shape=jax.ShapeDtypeStruct((nb * n_q, n_heads * LANES), BF16),
        scratch_shapes=([pltpu.VMEM((2, 1, tq), F32), pltpu.VMEM((2, vrows, tq), F32)]
                        + ([pltpu.VMEM((2, 2, tk, tq), BF16)] if has_latent else [])),
        compiler_params=_params(("parallel", "parallel", "arbitrary")),
        name="diff_attention",
    )(*args)


def _silu(x):
    return x * (1.0 / (1.0 + jnp.exp(-x)))


def _ffn_kernel(*refs, n_proj, conv):
    g_ref, sc_ref, sh_ref, gate_ref, w1_ref, w3_ref, w2_ref, o_ref = refs[n_proj:]
    x = _projected_residual(refs[:n_proj], conv) if n_proj > 1 else refs[0][...]
    a = _modulated_norm(x, g_ref[...], sc_ref[0], sh_ref[0]).astype(BF16)
    h1 = jnp.dot(a, w1_ref[...], preferred_element_type=F32)
    h3 = jnp.dot(a, w3_ref[...], preferred_element_type=F32)
    y = jnp.dot((_silu(h1) * h3).astype(BF16), w2_ref[...], preferred_element_type=F32)
    o_ref[...] = x + gate_ref[0] * y


def _ffn(h, nb, g, sc, sh, gate, w1, w3, w2, tm, proj=None):
    t, d = h.shape
    steps = t // nb // tm
    row = lambda b, i: (b * steps + i, 0)
    vec = pl.BlockSpec((1, 1, d), lambda b, i: (b, 0, 0))
    if proj is None:
        in_specs, args, conv = [pl.BlockSpec((tm, d), row)], [h], False
    else:
        in_specs, args = _out_proj_operands(h, nb, *proj, tm)
        conv = proj[-1] is not None
    n_proj = len(args)
    in_specs = in_specs + [_resident((1, d), lambda b, i: (0, 0)), vec, vec, vec,
                           _resident(w1.shape, lambda b, i: (0, 0)),
                           _resident(w3.shape, lambda b, i: (0, 0)),
                           _resident(w2.shape, lambda b, i: (0, 0))]
    return pl.pallas_call(
        functools.partial(_ffn_kernel, n_proj=n_proj, conv=conv),
        grid=(nb, steps),
        in_specs=in_specs,
        out_specs=pl.BlockSpec((tm, d), row),
        out_shape=jax.ShapeDtypeStruct((t, d), F32),
        compiler_params=_params(("parallel", "parallel")),
        name="ffn",
    )(*args, g, sc, sh, gate, w1, w3, w2)


def _top2_gates(logits):
    lane = _lane_iota()
    big = jnp.int32(LANES)
    lg = jnp.where(lane < N_EXPERTS, logits, -jnp.inf)
    m1 = lg.max(axis=-1, keepdims=True)
    i1 = jnp.where(lg == m1, lane, big).min(axis=-1, keepdims=True)
    rest = jnp.where(lane == i1, -jnp.inf, lg)
    m2 = rest.max(axis=-1, keepdims=True)
    i2 = jnp.where(rest == m2, lane, big).min(axis=-1, keepdims=True)
    e2 = jnp.exp(m2 - m1)
    den = 1.0 + e2
    return jnp.where(lane == i1, 1.0 / den, 0.0) + jnp.where(lane == i2, e2 / den, 0.0)


def _moe_kernel(*refs, n_proj, chunk, wide_chunk):
    (g_ref, sc_ref, sh_ref, gate_ref, r_ref, w1_ref, w3_ref, w2_ref, o_ref,
     a_sc, gates_sc, rank_sc, acc_sc) = refs[n_proj:]
    e = pl.program_id(2)
    tm = o_ref.shape[0]

    @pl.when(e == 0)
    def _():
        x = _projected_residual(refs[:n_proj], False) if n_proj > 1 else refs[0][...]
        o_ref[...] = x
        a = _modulated_norm(x, g_ref[...], sc_ref[0], sh_ref[0])
        a_hi = a.astype(BF16)
        a_lo = (a - a_hi.astype(F32)).astype(BF16)
        r = r_ref[...]
        r_hi = r.astype(BF16)
        r_lo = (r - r_hi.astype(F32)).astype(BF16)
        logits = (jnp.dot(a_hi, r_hi, preferred_element_type=F32)
                  + jnp.dot(a_hi, r_lo, preferred_element_type=F32)
                  + jnp.dot(a_lo, r_hi, preferred_element_type=F32))
        a_sc[...] = a_hi
        gates_t = _top2_gates(logits).T[:EXPERT_ROWS]
        gates_sc[...] = gates_t
        before = (lax.broadcasted_iota(jnp.int32, (tm, tm), 0)
                  < lax.broadcasted_iota(jnp.int32, (tm, tm), 1))
        routed = jnp.where(gates_t > 0.0, 1.0, 0.0).astype(BF16)
        rank_sc[...] = jnp.dot(routed, jnp.where(before, 1.0, 0.0).astype(BF16), preferred_element_type=F32)
        acc_sc[...] = jnp.zeros_like(acc_sc)

    gate_row = gates_sc[pl.ds(e, 1), :]
    rank_row = jnp.where(gate_row > 0.0, rank_sc[pl.ds(e, 1), :], -1.0)
    count = jnp.sum(jnp.where(gate_row > 0.0, 1.0, 0.0)).astype(jnp.int32)

    def run_expert(first_rank, n_rows):
        slot = lax.broadcasted_iota(jnp.int32, (n_rows, 1), 0).astype(F32) + float(first_rank)
        pick_f = jnp.where(rank_row == slot, 1.0, 0.0)
        xs = jnp.dot(pick_f.astype(BF16), a_sc[...], preferred_element_type=F32).astype(BF16)
        h1 = jnp.dot(xs, w1_ref[0], preferred_element_type=F32)
        h3 = jnp.dot(xs, w3_ref[0], preferred_element_type=F32)
        y = jnp.dot((_silu(h1) * h3).astype(BF16), w2_ref[0], preferred_element_type=F32)
        y = y * jnp.sum(pick_f * gate_row, axis=-1, keepdims=True)
        acc_sc[...] += lax.dot_general(pick_f.astype(BF16), y.astype(BF16), (((0,), (0,)), ((), ())),
                                       preferred_element_type=F32)

    one_wide_pass = (count > chunk) & (count <= wide_chunk)

    @pl.when(one_wide_pass)
    def _():
        run_expert(0, wide_chunk)

    for j in range(pl.cdiv(tm, chunk)):
        @pl.when((count > j * chunk) & jnp.logical_not(one_wide_pass))
        def _():
            run_expert(j * chunk, chunk)

    @pl.when(e == pl.num_programs(2) - 1)
    def _():
        o_ref[...] = o_ref[...] + gate_ref[0] * acc_sc[...]


def _moe(h, nb, g, sc, sh, gate, router, w1, w3, w2, tm, chunk, proj=None):
    t, d = h.shape
    steps = t // nb // tm
    n_e, _, f = w1.shape
    row = lambda b, i, e: (b * steps + i, 0)
    vec = pl.BlockSpec((1, 1, d), lambda b, i, e: (b, 0, 0))
    if proj is None:
        in_specs, args = [pl.BlockSpec((tm, d), row)], [h]
    else:
        specs, args = _out_proj_operands(h, nb, *proj, None, tm)
        in_specs = [pl.BlockSpec(s.block_shape, functools.partial(lambda b, i, e, f: f(b, i), f=s.index_map),
                                 pipeline_mode=s.pipeline_mode) for s in specs]
    in_specs = in_specs + [_resident((1, d), lambda b, i, e: (0, 0)), vec, vec, vec,
                           _resident(router.shape, lambda b, i, e: (0, 0)),
                           pl.BlockSpec((1, d, f), lambda b, i, e: (e, 0, 0)),
                           pl.BlockSpec((1, d, f), lambda b, i, e: (e, 0, 0)),
                           pl.BlockSpec((1, f, d), lambda b, i, e: (e, 0, 0))]
    return pl.pallas_call(
        functools.partial(_moe_kernel, n_proj=len(args), chunk=chunk, wide_chunk=MOE_WIDE_CHUNK),
        grid=(nb, steps, n_e),
        in_specs=in_specs,
        out_specs=pl.BlockSpec((tm, d), row),
        out_shape=jax.ShapeDtypeStruct((t, d), F32),
        scratch_shapes=[pltpu.VMEM((tm, d), BF16), pltpu.VMEM((EXPERT_ROWS, tm), F32),
                        pltpu.VMEM((EXPERT_ROWS, tm), F32), pltpu.VMEM((tm, d), F32)],
        compiler_params=_params(("parallel", "parallel", "arbitrary")),
        name="moe",
    )(*args, g, sc, sh, gate, router, w1, w3, w2)


def _rope_tables(n):
    t = jnp.arange(n)
    pos = jnp.stack([t // GRID_W, t % GRID_W], -1).astype(F32)
    nq = HEAD_DIM // 4
    inv = ROPE_THETA ** (-jnp.arange(nq, dtype=F32) / nq)
    ang = pos[:, :, None] * inv
    cos = jnp.repeat(jnp.cos(ang)[:, :, None, :], 2, axis=2)
    sin = jnp.stack([-jnp.sin(ang), jnp.sin(ang)], axis=2)
    cos = jnp.tile(cos.reshape(n, HEAD_DIM), (1, LANES // HEAD_DIM))
    sin = jnp.tile(sin.reshape(n, HEAD_DIM), (1, LANES // HEAD_DIM))
    return cos, sin


def _head_gain(parts, n_out):
    row = jnp.ones((n_out,), F32)
    for col, n_heads, gain, scale in parts:
        row = lax.dynamic_update_slice(row, jnp.tile(gain.astype(F32) * scale, n_heads), (col,))
    return row.reshape(1, n_out)


def kernel(x, c, ctx, c_ctx, ada_w, ada_b, norm1_g, norm2_g, ev_w_in, ev_conv_w, ev_q_g, ev_k_g, ev_rpb,
           ev_w_out, ffn_w1, ffn_w3, ffn_w2, od_w_in, od_cq_g, od_ck_g, od_sink, od_dq_g, od_dk_g,
           od_lam_q1, od_lam_k1, od_lam_q2, od_lam_k2, od_subln_g, od_w_out, moe_router,
           moe_w1, moe_w3, moe_w2):
    nb, n, d = x.shape
    ctx_len = ctx.shape[1]
    depth = ada_w.shape[0]
    n_slots = d // HEAD_DIM
    conv_ch = d // 2
    na_heads = swa_heads = n_slots // 2
    swa_kv = max(1, swa_heads // 4)
    diff_heads = n_slots // 4
    rows = n // GRID_W
    assert rows % NA_QROWS == 0 and rows >= 2 * NA_QROWS and ctx_len % LANES == 0
    assert n % max(TOKEN_TILE, DIFF_Q_TILE, DIFF_K_TILE, MOE_TOKEN_TILE) == 0
    assert nb + 1 <= F32_SUBLANES

    ev_q_col = 3 * conv_ch
    ev_k_col = ev_q_col + na_heads * HEAD_DIM
    ev_v_col = ev_k_col + na_heads * HEAD_DIM
    ev_n = ev_v_col + na_heads * HEAD_DIM
    od_dq_col = swa_heads * HEAD_DIM
    od_ck_col = od_dq_col + diff_heads * 2 * HEAD_DIM
    od_cv_col = od_ck_col + swa_kv * HEAD_DIM
    od_dk_col = od_cv_col + swa_kv * HEAD_DIM
    od_dv_col = od_dk_col + diff_heads * 2 * HEAD_DIM
    od_n = od_dv_col + diff_heads * 2 * HEAD_DIM
    ev_segs = ((0, ev_q_col, False, 0), (ev_q_col, ev_v_col, True, 0), (ev_v_col, ev_n, False, na_heads))
    od_segs = ((0, od_cv_col, True, 0), (od_cv_col, od_dk_col, False, swa_kv), (od_dk_col, od_dv_col, True, 0),
               (od_dv_col, od_n, False, diff_heads))

    h = x.reshape(nb * n, d)
    hc = ctx.reshape(nb * ctx_len, d)
    tm_lat = TOKEN_TILE
    tm_ctx = ctx_len
    tm_moe = MOE_TOKEN_TILE

    s_rows = jnp.zeros((F32_SUBLANES, d), F32).at[:nb].set(c).at[nb].set(c_ctx)
    mod = _modulation(s_rows, ada_w, ada_b)
    rope_tabs = _rope_tables(n)

    def lat_vec(l, k):
        return mod[l, :nb, k * d:(k + 1) * d].reshape(nb, 1, d)

    def ctx_vec(l, k, copies):
        return jnp.broadcast_to(mod[l, nb, k * d:(k + 1) * d], (copies, 1, d))

    for l in range(depth):
        last = l == depth - 1
        i = l // 2
        g1 = norm1_g[l].reshape(1, d)
        g2 = norm2_g[l].reshape(1, d)
        if l % 2 == 0:
            w_in = ev_w_in[i].astype(BF16)
            gain = _head_gain([(ev_q_col, na_heads, ev_q_g[i], QK_SCALE * LOG2_E), (ev_k_col, na_heads, ev_k_g[i], 1.0)],
                              ev_n)
            p_lat, vt_lat = _norm_proj(h, nb, g1, lat_vec(l, 1), lat_vec(l, 0), w_in, gain, ev_segs, None, tm_lat)
            p_ctx, vt_ctx = _norm_proj(hc, nb, g1, ctx_vec(l, 1, nb), ctx_vec(l, 0, nb), w_in, gain, ev_segs, None,
                                       tm_ctx)
            bias = _na_bias_tables(ev_rpb[i], rows)
            y_na = _na_attention(p_lat, vt_lat, p_ctx, vt_ctx, bias, nb, n, ctx_len, ev_q_col // LANES,
                                 ev_k_col // LANES, na_heads)
            w_out = ev_w_out[i].astype(BF16)
            conv_w = ev_conv_w[i]
            if not last:
                y_na_c = _gqa_attention(p_ctx, None, p_ctx, vt_ctx, None, nb, ctx_len, ctx_len, ev_q_col // LANES,
                                        ev_k_col // LANES, na_heads, na_heads, False, ctx_len)
                hc = _out_proj(hc, nb, ctx_vec(l, 2, nb), p_ctx, y_na_c, w_out, conv_w, tm_ctx)
            w1, w3, w2 = ffn_w1[i].astype(BF16), ffn_w3[i].astype(BF16), ffn_w2[i].astype(BF16)
            h = _ffn(h, nb, g2, lat_vec(l, 4), lat_vec(l, 3), lat_vec(l, 5), w1, w3, w2, tm_lat,
                     proj=(lat_vec(l, 2), p_lat, y_na, w_out, conv_w))
            if not last:
                hc = _ffn(hc, 1, g2, ctx_vec(l, 4, 1), ctx_vec(l, 3, 1), ctx_vec(l, 5, 1), w1, w3, w2, tm_lat)
        else:
            lam_init = 0.8 - 0.6 * math.exp(-0.3 * l)
            w_in = od_w_in[i].astype(BF16)
            gain = _head_gain([(0, swa_heads, od_cq_g[i], QK_SCALE * LOG2_E),
                               (od_dq_col, 2 * diff_heads, od_dq_g[i], QK_SCALE * LOG2_E),
                               (od_ck_col, swa_kv, od_ck_g[i], 1.0),
                               (od_dk_col, 2 * diff_heads, od_dk_g[i], 1.0)], od_n)
            p_lat, cvt_lat, vt_lat = _norm_proj(h, nb, g1, lat_vec(l, 1), lat_vec(l, 0), w_in, gain, od_segs, rope_tabs,
                                                tm_lat)
            p_ctx, cvt_ctx, vt_ctx = _norm_proj(hc, nb, g1, ctx_vec(l, 1, nb), ctx_vec(l, 0, nb), w_in, gain, od_segs,
                                                None, tm_ctx)
            sink = jnp.broadcast_to(od_sink[i].astype(F32)[:, None] * LOG2_E, (swa_heads, LANES))
            lam_rows = jnp.zeros((8, LANES), F32).at[:4, :HEAD_DIM].set(
                jnp.stack([od_lam_q1[i], od_lam_k1[i], od_lam_q2[i], od_lam_k2[i]]).astype(F32))
            subg = od_subln_g[i].astype(F32).reshape(1, 2 * HEAD_DIM)
            cols = (od_dq_col // LANES, od_dk_col // LANES)
            score_bound = (HEAD_DIM * QK_SCALE * LOG2_E * BF16_NORM_SLACK * jnp.max(jnp.abs(od_dq_g[i]))
                           * jnp.max(jnp.abs(od_dk_g[i]))).astype(F32).reshape(1, 1)
            y_c = _gqa_attention(p_lat, cvt_lat, p_ctx, cvt_ctx, sink, nb, n, ctx_len, 0, od_ck_col // LANES,
                                 swa_heads, swa_kv, True, 2 * BLOCK)
            y_d = _diff_attention(p_lat, p_lat, vt_lat, p_ctx, vt_ctx, lam_rows, subg, score_bound, nb, n, n, ctx_len,
                                  *cols, diff_heads, lam_init, DIFF_Q_TILE, DIFF_K_TILE)
            w_out = od_w_out[i].astype(BF16)
            if not last:
                y_c_c = _gqa_attention(p_ctx, None, p_ctx, cvt_ctx, sink, nb, ctx_len, ctx_len, 0,
                                       od_ck_col // LANES, swa_heads, swa_kv, False, ctx_len)
                y_d_c = _diff_attention(p_ctx, None, None, p_ctx, vt_ctx, lam_rows, subg, score_bound, nb, ctx_len, 0,
                                        ctx_len, *cols, diff_heads, lam_init, ctx_len, ctx_len)
                hc = _out_proj(hc, nb, ctx_vec(l, 2, nb), y_c_c, y_d_c, w_out, None, tm_ctx)
            router = jnp.zeros((d, LANES), F32).at[:, :N_EXPERTS].set(moe_router[i])
            w1, w3, w2 = moe_w1[i].astype(BF16), moe_w3[i].astype(BF16), moe_w2[i].astype(BF16)
            h = _moe(h, nb, g2, lat_vec(l, 4), lat_vec(l, 3), lat_vec(l, 5), router, w1, w3, w2, tm_moe, MOE_CHUNK,
                     proj=(lat_vec(l, 2), y_c, y_d, w_out))
            if not last:
                hc = _moe(hc, 1, g2, ctx_vec(l, 4, 1), ctx_vec(l, 3, 1), ctx_vec(l, 5, 1), router, w1, w3, w2,
                          min(tm_moe, nb * ctx_len), MOE_CHUNK)
    return h.reshape(nb, n, d)
```

```python
import functools
import math

import jax
import jax.numpy as jnp
from jax import lax
from jax.experimental import pallas as pl
from jax.experimental.pallas import tpu as pltpu

F32 = jnp.float32
BF16 = jnp.bfloat16

LANES = 128
F32_SUBLANES = 8
BF16_SUBLANES = 16
VMEM_LIMIT = 56 * 1024 * 1024

TOKEN_TILE = 512
MOE_TOKEN_TILE = 1024
DIFF_Q_TILE = 2048
DIFF_K_TILE = 1024

HEAD_DIM = 64
GRID_W = 64
CONV_W = 3
NA_ROWS = 8
NA_COLS = 16
NA_QROWS = 8
SWA_WINDOW = 128
MOE_CHUNK = 256
MOE_WIDE_CHUNKS = (320, 384)
BLOCK = 128
N_EXPERTS = 8
EXPERT_ROWS = 16
ROPE_THETA = 10000.0
EPS = 1e-6
NEG = -1e30
QK_SCALE = HEAD_DIM ** -0.5
LOG2_E = math.log2(math.e)
BF16_NORM_SLACK = 1.02
RESCALE_MARGIN = 32.0

NT_DIMS = (((1,), (1,)), ((), ()))


def _params(sem):
    return pltpu.CompilerParams(dimension_semantics=sem, vmem_limit_bytes=VMEM_LIMIT)


def _resident(shape, index_map):
    return pl.BlockSpec(shape, index_map, pipeline_mode=pl.Buffered(1))


def _lane_iota():
    return lax.broadcasted_iota(jnp.int32, (1, LANES), 1)


def _swap_halves(x):
    return jnp.concatenate([x[:, HEAD_DIM:], x[:, :HEAD_DIM]], axis=1)


def _modulated_norm(x, g, sc, sh):
    ms = jnp.mean(x * x, axis=-1, keepdims=True)
    return (x * lax.rsqrt(ms + EPS)) * (g * (1.0 + sc)) + sh


def _mod_kernel(s_ref, w_ref, b_ref, o_ref):
    s = s_ref[...]
    s = s * (1.0 / (1.0 + jnp.exp(-s)))
    o_ref[0] = jnp.dot(s.astype(BF16), w_ref[0].astype(BF16), preferred_element_type=F32) + b_ref[0]


def _modulation(s_rows, ada_w, ada_b):
    depth, d, n_out = ada_w.shape
    tn = n_out // 4
    return pl.pallas_call(
        _mod_kernel,
        grid=(depth, n_out // tn),
        in_specs=[pl.BlockSpec(s_rows.shape, lambda l, j: (0, 0)),
                  pl.BlockSpec((1, d, tn), lambda l, j: (l, 0, j)),
                  pl.BlockSpec((1, 1, tn), lambda l, j: (l, 0, j))],
        out_specs=pl.BlockSpec((1, s_rows.shape[0], tn), lambda l, j: (l, 0, j)),
        out_shape=jax.ShapeDtypeStruct((depth, s_rows.shape[0], n_out), F32),
        compiler_params=_params(("arbitrary", "arbitrary")),
        name="modulation",
    )(s_rows, ada_w, ada_b.reshape(depth, 1, n_out))


def _head_sumsq(z):
    r = lax.broadcasted_iota(jnp.int32, (LANES, LANES), 0) // HEAD_DIM
    c = lax.broadcasted_iota(jnp.int32, (LANES, LANES), 1) // HEAD_DIM
    same_head = jnp.where(r == c, 1.0, 0.0).astype(BF16)
    z2 = z * z
    hi = z2.astype(BF16)
    lo = (z2 - hi.astype(F32)).astype(BF16)
    return (jnp.dot(hi, same_head, preferred_element_type=F32)
            + jnp.dot(lo, same_head, preferred_element_type=F32))


def _norm_proj_kernel(*refs, segs, rope, n_vt):
    refs = list(refs)
    vt_refs = [refs.pop() for _ in range(n_vt)][::-1]
    if rope:
        x_ref, g_ref, sc_ref, sh_ref, w_ref, gain_ref, cos_ref, sin_ref, o_ref = refs
    else:
        x_ref, g_ref, sc_ref, sh_ref, w_ref, gain_ref, o_ref = refs
    a = _modulated_norm(x_ref[...], g_ref[...], sc_ref[0], sh_ref[0]).astype(BF16)
    first_half = (_lane_iota() % (HEAD_DIM // 2)) < (HEAD_DIM // 4)
    for c0, c1, normed, vt_heads in segs:
        acc = jnp.dot(a, w_ref[:, c0:c1], preferred_element_type=F32)
        if vt_heads:
            vt_ref = vt_refs.pop(0)
            dv = (c1 - c0) // vt_heads
            vt = acc.T.astype(BF16)
            for hh in range(vt_heads):
                vt_ref[hh, :dv, :] = vt[hh * dv:(hh + 1) * dv]
                vt_ref[hh, dv:, :] = jnp.ones((BF16_SUBLANES, vt.shape[1]), BF16)
        if not normed:
            o_ref[:, c0:c1] = acc.astype(BF16)
            continue
        for j in range((c1 - c0) // LANES):
            z = acc[:, j * LANES:(j + 1) * LANES]
            lo = c0 + j * LANES
            z = z * lax.rsqrt(_head_sumsq(z) * (1.0 / HEAD_DIM) + EPS) * gain_ref[:, lo:lo + LANES]
            if rope:
                partner = jnp.where(first_half,
                                    pltpu.roll(z, LANES - HEAD_DIM // 4, axis=1),
                                    pltpu.roll(z, HEAD_DIM // 4, axis=1))
                z = z * cos_ref[...] + partner * sin_ref[...]
            o_ref[:, lo:lo + LANES] = z.astype(BF16)


def _norm_proj(h, nb, g, sc, sh, w, gain, segs, rope_tabs, tm):
    t, d = h.shape
    n_out = w.shape[1]
    steps = t // nb // tm
    in_specs = [pl.BlockSpec((tm, d), lambda b, i: (b * steps + i, 0)),
                _resident((1, d), lambda b, i: (0, 0)),
                pl.BlockSpec((1, 1, d), lambda b, i: (b, 0, 0)),
                pl.BlockSpec((1, 1, d), lambda b, i: (b, 0, 0)),
                _resident((d, n_out), lambda b, i: (0, 0)),
                _resident((1, n_out), lambda b, i: (0, 0))]
    args = [h, g, sc, sh, w, gain]
    if rope_tabs is not None:
        in_specs += [pl.BlockSpec((tm, LANES), lambda b, i: (i, 0))] * 2
        args += list(rope_tabs)
    out_specs = [pl.BlockSpec((tm, n_out), lambda b, i: (b * steps + i, 0))]
    out_shape = [jax.ShapeDtypeStruct((t, n_out), BF16)]
    for c0, c1, _, vt_heads in segs:
        if vt_heads:
            rows = (c1 - c0) // vt_heads + BF16_SUBLANES
            out_specs.append(pl.BlockSpec((vt_heads, rows, tm), lambda b, i: (0, 0, b * steps + i)))
            out_shape.append(jax.ShapeDtypeStruct((vt_heads, rows, t), BF16))
    return pl.pallas_call(
        functools.partial(_norm_proj_kernel, segs=segs, rope=rope_tabs is not None, n_vt=len(out_specs) - 1),
        grid=(nb, steps),
        in_specs=in_specs,
        out_specs=out_specs,
        out_shape=out_shape,
        compiler_params=_params(("parallel", "parallel")),
        name="norm_proj",
    )(*args)


def _gated_conv(gb_ref, gc_ref, u_ref, gcp_ref, up_ref, gcn_ref, un_ref, cw_ref):
    i, steps = pl.program_id(1), pl.num_programs(1)
    v = gc_ref[...].astype(F32) * u_ref[...].astype(F32)
    tm = v.shape[0]
    last = BF16_SUBLANES - 1
    prev_row = gcp_ref[last:last + 1, :].astype(F32) * up_ref[last:last + 1, :].astype(F32)
    next_row = gcn_ref[0:1, :].astype(F32) * un_ref[0:1, :].astype(F32)
    prev_row = jnp.where(i > 0, prev_row, 0.0)
    next_row = jnp.where(i < steps - 1, next_row, 0.0)
    row = lax.broadcasted_iota(jnp.int32, (tm, 1), 0)
    v_prev = jnp.where(row == 0, prev_row, pltpu.roll(v, 1, axis=0))
    v_next = jnp.where(row == tm - 1, next_row, pltpu.roll(v, tm - 1, axis=0))
    cw = cw_ref[...]
    conv = cw[0:1, :] * v_prev + cw[1:2, :] * v + cw[2:3, :] * v_next
    return gb_ref[...].astype(F32) * conv


def _projected_residual(refs, conv):
    if conv:
        (gb_ref, gc_ref, u_ref, gcp_ref, up_ref, gcn_ref, un_ref, cw_ref,
         yb_ref, wa_ref, wb_ref, h_ref, gate_ref) = refs
        ya = _gated_conv(gb_ref, gc_ref, u_ref, gcp_ref, up_ref, gcn_ref, un_ref, cw_ref).astype(BF16)
    else:
        ya_ref, yb_ref, wa_ref, wb_ref, h_ref, gate_ref = refs
        ya = ya_ref[...]
    y = (jnp.dot(ya, wa_ref[...], preferred_element_type=F32)
         + jnp.dot(yb_ref[...], wb_ref[...], preferred_element_type=F32))
    return h_ref[...] + gate_ref[0] * y


def _out_proj_kernel(*refs, conv):
    refs[-1][...] = _projected_residual(refs[:-1], conv)


def _out_proj_operands(h, nb, gate, ya_src, yb, w_out, conv_w, tm):
    t, d = h.shape
    steps = t // nb // tm
    wa_rows = w_out.shape[0] - yb.shape[1]
    w_a, w_b = w_out[:wa_rows], w_out[wa_rows:]
    row = lambda b, i: (b * steps + i, 0)
    if conv_w is not None:
        cc = conv_w.shape[1]
        hb = tm // BF16_SUBLANES
        n_halo = t // BF16_SUBLANES
        prev = lambda col: (lambda b, i: (jnp.maximum((b * steps + i) * hb - 1, 0), col))
        nxt = lambda col: (lambda b, i: (jnp.minimum((b * steps + i + 1) * hb, n_halo - 1), col))
        in_specs = [pl.BlockSpec((tm, cc), lambda b, i: (b * steps + i, 0)),
                    pl.BlockSpec((tm, cc), lambda b, i: (b * steps + i, 1)),
                    pl.BlockSpec((tm, cc), lambda b, i: (b * steps + i, 2)),
                    pl.BlockSpec((BF16_SUBLANES, cc), prev(1)),
                    pl.BlockSpec((BF16_SUBLANES, cc), prev(2)),
                    pl.BlockSpec((BF16_SUBLANES, cc), nxt(1)),
                    pl.BlockSpec((BF16_SUBLANES, cc), nxt(2)),
                    _resident(conv_w.shape, lambda b, i: (0, 0))]
        args = [ya_src] * 7 + [conv_w]
    else:
        in_specs = [pl.BlockSpec((tm, wa_rows), row)]
        args = [ya_src]
    in_specs += [pl.BlockSpec((tm, yb.shape[1]), row),
                 _resident(w_a.shape, lambda b, i: (0, 0)),
                 _resident(w_b.shape, lambda b, i: (0, 0)),
                 pl.BlockSpec((tm, d), row),
                 pl.BlockSpec((1, 1, d), lambda b, i: (b, 0, 0))]
    args += [yb, w_a, w_b, h, gate]
    return in_specs, args


def _out_proj(h, nb, gate, ya_src, yb, w_out, conv_w, tm):
    t, d = h.shape
    steps = t // nb // tm
    in_specs, args = _out_proj_operands(h, nb, gate, ya_src, yb, w_out, conv_w, tm)
    return pl.pallas_call(
        functools.partial(_out_proj_kernel, conv=conv_w is not None),
        grid=(nb, steps),
        in_specs=in_specs,
        out_specs=pl.BlockSpec((tm, d), lambda b, i: (b * steps + i, 0)),
        out_shape=jax.ShapeDtypeStruct((t, d), F32),
        compiler_params=_params(("parallel", "parallel")),
        name="out_proj",
    )(*args)


def _na_bias_tables(rpb, rows):
    n_heads = rpb.shape[0]
    n_dr = 2 * NA_ROWS - 1
    c = jnp.arange(GRID_W)
    c0 = jnp.clip(c - NA_COLS // 2, 0, GRID_W - NA_COLS)
    col_ok = (c[None, :] >= c0[:, None]) & (c[None, :] < c0[:, None] + NA_COLS)
    dc = jnp.clip(c[None, :] - c[:, None] + NA_COLS - 1, 0, 2 * NA_COLS - 2)
    pick_dc = ((dc[None] == jnp.arange(2 * NA_COLS - 1)[:, None, None]) & col_ok[None]).astype(F32)
    table = jnp.einsum('hrd,dcx->hrxc', rpb.astype(F32), pick_dc, precision=lax.Precision.HIGHEST)
    table = jnp.where(col_ok.T[None, None], table * LOG2_E, NEG)

    def paired(tab):
        pad = jnp.full((n_heads, NA_ROWS // 2, GRID_W, GRID_W), NEG, F32)
        ext = jnp.concatenate([pad, tab, pad], axis=1)
        return jnp.concatenate([ext[:, 1:], ext[:, :-1]], axis=-1)

    dr = jnp.arange(n_dr)
    in_window = (dr >= NA_ROWS // 2 - 1) & (dr < NA_ROWS // 2 - 1 + NA_ROWS)
    interior = jnp.where(in_window[None, :, None, None], table, NEG)

    nblk = rows // NA_QROWS
    i = jnp.arange(NA_QROWS)
    j = jnp.arange(2 * NA_QROWS)
    masks = []
    for rb in (0, 1, nblk - 1):
        r = rb * NA_QROWS + i
        r0 = jnp.clip(r - NA_ROWS // 2, 0, rows - NA_ROWS)
        rk = rb * NA_QROWS - NA_ROWS // 2 + j
        ok = (rk[:, None] >= r0[None, :]) & (rk[:, None] < r0[None, :] + NA_ROWS)
        masks.append(jnp.repeat(jnp.where(ok, 0.0, NEG), GRID_W, axis=1))
    return paired(table), paired(interior), jnp.stack(masks)


def _na_bias_tile(tbl_ref, hh, mask_ref):
    rows = []
    for j in range(2 * NA_QROWS):
        blk = jnp.concatenate([tbl_ref[hh, j - 2 * ii + NA_QROWS - 2] for ii in range(NA_QROWS // 2)], axis=1)
        if mask_ref is not None:
            blk = blk + mask_ref[0, j:j + 1, :]
        rows.append(blk)
    return jnp.concatenate(rows, axis=0)


def _softmax_pv_t(s_parts, vt_parts, extra=None):
    m = s_parts[0].max(axis=0, keepdims=True)
    for s in s_parts[1:]:
        m = jnp.maximum(m, s.max(axis=0, keepdims=True))
    if extra is not None:
        m = jnp.maximum(m, extra)
    acc = None
    for s, vt in zip(s_parts, vt_parts):
        pv = jnp.dot(vt, jnp.exp2((s - m).astype(BF16)), preferred_element_type=F32)
        acc = pv if acc is None else acc + pv
    den = acc[HEAD_DIM:HEAD_DIM + 1]
    if extra is not None:
        den = den + jnp.exp2(extra - m)
    return acc[:HEAD_DIM] / den


def _na_kernel(q_ref, k0, k1, k2, k3, vt0, vt1, vt2, vt3, kc_ref, vct_ref, edge_tbl_ref, tbl_ref, mask_ref, o_ref):
    def attend(tbl, mask):
        q = q_ref[...]
        k = jnp.concatenate([k0[...], k1[...], k2[...], k3[...]], axis=0)
        vt = jnp.concatenate([vt0[...], vt1[...], vt2[...], vt3[...]], axis=2)
        kc, vct = kc_ref[...], vct_ref[...]
        lane = _lane_iota()
        scores = []
        for hh in range(2):
            qh = jnp.where((lane // HEAD_DIM) == hh, q, jnp.zeros_like(q))
            s_loc = lax.dot_general(k, qh, NT_DIMS, preferred_element_type=F32) + _na_bias_tile(tbl, hh, mask)
            scores.append([s_loc, lax.dot_general(kc, qh, NT_DIMS, preferred_element_type=F32)])
        outs = [_softmax_pv_t(scores[hh], [vt[hh], vct[hh]]) for hh in range(2)]
        o_ref[...] = jnp.concatenate(outs, axis=0).T.astype(BF16)

    rb = pl.program_id(2)
    is_edge = (rb == 0) | (rb == pl.num_programs(2) - 1)

    @pl.when(jnp.logical_not(is_edge))
    def _():
        attend(tbl_ref, None)

    @pl.when(is_edge)
    def _():
        attend(edge_tbl_ref, mask_ref)


def _na_attention(p_lat, vt_lat, p_ctx, vt_ctx, bias_tables, nb, n, ctx_len, q_col, k_col, n_heads):
    tq = NA_QROWS * GRID_W
    tw = tq // 2
    qsteps = n // tq
    wsteps = n // tw
    edge_tbl, tbl, masks = bias_tables
    wblock = lambda rb, j: jnp.clip(2 * rb - 1 + j, 0, wsteps - 1)
    kwin = lambda j: pl.BlockSpec((tw, LANES), lambda hp, b, rb: (b * wsteps + wblock(rb, j), k_col + hp))
    vrows = HEAD_DIM + BF16_SUBLANES
    vwin = lambda j: pl.BlockSpec((2, vrows, tw), lambda hp, b, rb: (hp, 0, b * wsteps + wblock(rb, j)))
    variant = lambda rb: jnp.where(rb == 0, 0, jnp.where(rb == qsteps - 1, 2, 1))
    tbl_spec = pl.BlockSpec((2,) + tbl.shape[1:], lambda hp, b, rb: (hp, 0, 0, 0))
    in_specs = ([pl.BlockSpec((tq, LANES), lambda hp, b, rb: (b * qsteps + rb, q_col + hp))]
                + [kwin(j) for j in range(4)] + [vwin(j) for j in range(4)]
                + [pl.BlockSpec((ctx_len, LANES), lambda hp, b, rb: (b, k_col + hp)),
                   pl.BlockSpec((2, vrows, ctx_len), lambda hp, b, rb: (hp, 0, b)),
                   tbl_spec, tbl_spec,
                   pl.BlockSpec((1,) + masks.shape[1:], lambda hp, b, rb: (variant(rb), 0, 0))])
    return pl.pallas_call(
        _na_kernel,
        grid=(n_heads // 2, nb, qsteps),
        in_specs=in_specs,
        out_specs=pl.BlockSpec((tq, LANES), lambda hp, b, rb: (b * qsteps + rb, hp)),
        out_shape=jax.ShapeDtypeStruct((nb * n, n_heads * HEAD_DIM), BF16),
        compiler_params=_params(("parallel", "parallel", "arbitrary")),
        name="na_attention",
    )(*([p_lat] * 5 + [vt_lat] * 4 + [p_ctx, vt_ctx, edge_tbl, tbl, masks]))


def _gqa_kernel(*refs, n_q, n_kv, has_local, has_sink, n_tokens):
    refs = list(refs)
    q_ref = refs.pop(0)
    if has_local:
        k_loc = jnp.concatenate([refs.pop(0)[...] for _ in range(4)], axis=0)
        vt_loc = jnp.concatenate([refs.pop(0)[...] for _ in range(4)], axis=2)
    kc_ref, vct_ref = refs.pop(0), refs.pop(0)
    sink_ref = refs.pop(0) if has_sink else None
    o_ref = refs.pop(0)

    tq = q_ref.shape[0]
    group = n_q // n_kv
    lane = _lane_iota()
    if has_local:
        base = pl.program_id(1) * tq
        kpos = base - BLOCK + lax.broadcasted_iota(jnp.int32, (tq + 2 * BLOCK, 1), 0)
        qpos = base + lax.broadcasted_iota(jnp.int32, (1, tq), 1)
        ok = (jnp.abs(kpos - qpos) <= SWA_WINDOW) & (kpos >= 0) & (kpos < n_tokens)
        band = jnp.where(ok, 0.0, NEG)
        band = jnp.concatenate([band] * group, axis=1)

    outs = []
    for kvh in range(n_kv):
        cg, half = kvh // 2, kvh % 2
        qs = []
        for h in range(kvh * group, (kvh + 1) * group):
            qg = q_ref[:, (h // 2) * LANES:(h // 2 + 1) * LANES]
            if h % 2 != half:
                qg = _swap_halves(qg)
            qs.append(jnp.where((lane // HEAD_DIM) == half, qg, jnp.zeros_like(qg)))
        qcat = jnp.concatenate(qs, axis=0) if group > 1 else qs[0]
        kc = kc_ref[:, cg * LANES:(cg + 1) * LANES]
        s_parts = [lax.dot_general(kc, qcat, NT_DIMS, preferred_element_type=F32)]
        vt_parts = [vct_ref[kvh]]
        if has_local:
            s_parts.append(lax.dot_general(k_loc, qcat, NT_DIMS, preferred_element_type=F32) + band)
            vt_parts.append(vt_loc[kvh])
        extra = None
        if has_sink:
            extra = jnp.concatenate([sink_ref[h:h + 1, :] for h in range(kvh * group, (kvh + 1) * group)
                                     for _ in range(tq // LANES)], axis=1)
        o = _softmax_pv_t(s_parts, vt_parts, extra)
        outs += [o[:, g * tq:(g + 1) * tq] for g in range(group)]
    o_ref[...] = jnp.concatenate(outs, axis=0).T.astype(BF16)


def _gqa_attention(p_q, vt_q, p_ctx, vt_ctx, sink, nb, n, ctx_len, q_col, k_col, n_q, n_kv, has_local, tq):
    qsteps = n // tq
    kv_w = n_kv * HEAD_DIM
    q_w = n_q * HEAD_DIM
    vrows = HEAD_DIM + BF16_SUBLANES
    in_specs = [pl.BlockSpec((tq, q_w), lambda b, i: (b * qsteps + i, q_col * LANES // q_w))]
    args = [p_q]
    if has_local:
        assert tq == 2 * BLOCK and kv_w == LANES
        wsteps = n // BLOCK
        wblock = lambda i, j: jnp.clip(2 * i - 1 + j, 0, wsteps - 1)
        in_specs += [pl.BlockSpec((BLOCK, LANES), functools.partial(
            lambda b, i, j: (b * wsteps + wblock(i, j), k_col), j=j)) for j in range(4)]
        in_specs += [pl.BlockSpec((n_kv, vrows, BLOCK), functools.partial(
            lambda b, i, j: (0, 0, b * wsteps + wblock(i, j)), j=j)) for j in range(4)]
        args += [p_q] * 4 + [vt_q] * 4
    in_specs += [pl.BlockSpec((ctx_len, kv_w), lambda b, i: (b, k_col * LANES // kv_w)),
                 pl.BlockSpec((n_kv, vrows, ctx_len), lambda b, i: (0, 0, b))]
    args += [p_ctx, vt_ctx]
    if sink is not None:
        in_specs.append(_resident(sink.shape, lambda b, i: (0, 0)))
        args.append(sink)
    return pl.pallas_call(
        functools.partial(_gqa_kernel, n_q=n_q, n_kv=n_kv, has_local=has_local,
                          has_sink=sink is not None, n_tokens=n),
        grid=(nb, qsteps),
        in_specs=in_specs,
        out_specs=pl.BlockSpec((tq, q_w), lambda b, i: (b * qsteps + i, 0)),
        out_shape=jax.ShapeDtypeStruct((nb * n, q_w), BF16),
        compiler_params=_params(("parallel", "parallel")),
        name="gqa_attention",
    )(*args)


def _diff_kernel(*refs, n_kblocks, tk, lam_init):
    if n_kblocks:
        q_ref, k_ref, vt_ref, kc_ref, vct_ref, lam_ref, subg_ref, bound_ref, o_ref, m_sc, acc_sc, p_sc = refs
    else:
        q_ref, kc_ref, vct_ref, lam_ref, subg_ref, bound_ref, o_ref, m_sc, acc_sc = refs
    lane = _lane_iota()
    q = q_ref[...]
    zero = jnp.zeros_like(q)
    q_maps = [jnp.where(lane < HEAD_DIM, q, zero), jnp.where(lane >= HEAD_DIM, q, zero)]

    kc, vct = kc_ref[...], vct_ref[0]
    for i in range(2):
        s = lax.dot_general(kc, q_maps[i], NT_DIMS, preferred_element_type=F32)
        m = s.max(axis=0, keepdims=True)
        p = jnp.exp2((s - m).astype(BF16))
        m_sc[i] = m
        acc_sc[i] = jnp.dot(vct, p, preferred_element_type=F32)

    if n_kblocks:
        def scores(kb):
            k = k_ref[pl.ds(pl.multiple_of(kb * tk, tk), tk), :]
            return [lax.dot_general(k, q_maps[i], NT_DIMS, preferred_element_type=F32) for i in range(2)]

        def probs(kb, slot):
            s = scores(kb)
            excess = None
            for i in range(2):
                m_ref = m_sc[i]
                p_sc[slot, i] = jnp.exp2((s[i] - m_ref).astype(BF16))
                over = jnp.max(s[i].max(axis=0, keepdims=True) - m_ref)
                excess = over if excess is None else jnp.maximum(excess, over)
            return excess

        def settle(kb, slot, excess):
            @pl.when(excess > RESCALE_MARGIN)
            def _():
                s = scores(kb)
                for i in range(2):
                    m_prev = m_sc[i]
                    m_new = jnp.maximum(m_prev, s[i].max(axis=0, keepdims=True))
                    p_sc[slot, i] = jnp.exp2((s[i] - m_new).astype(BF16))
                    acc_sc[i] = jnp.exp2(m_prev - m_new) * acc_sc[i]
                    m_sc[i] = m_new

        def accumulate(kb, slot):
            vt = vt_ref[0, :, pl.ds(pl.multiple_of(kb * tk, tk), tk)]
            for i in range(2):
                acc_sc[i] += jnp.dot(vt, p_sc[slot, i], preferred_element_type=F32)

        def probs_unchecked(kb, slot):
            s = scores(kb)
            for i in range(2):
                p_sc[slot, i] = jnp.exp2((s[i] - m_sc[i]).astype(BF16))

        lowest_ref = jnp.minimum(jnp.min(m_sc[0]), jnp.min(m_sc[1]))
        never_rescales = bound_ref[0, 0] - lowest_ref <= RESCALE_MARGIN
        first_excess = probs(0, 0)

        @pl.when(never_rescales)
        def _():
            def body(kb, carry):
                accumulate(kb - 1, (kb - 1) % 2)
                probs_unchecked(kb, kb % 2)
                return carry

            lax.fori_loop(1, n_kblocks, body, 0)
            accumulate(n_kblocks - 1, (n_kblocks - 1) % 2)

        @pl.when(jnp.logical_not(never_rescales))
        def _():
            def body(kb, excess):
                settle(kb - 1, (kb - 1) % 2, excess)
                accumulate(kb - 1, (kb - 1) % 2)
                return probs(kb, kb % 2)

            excess = lax.fori_loop(1, n_kblocks, body, first_excess)
            settle(n_kblocks - 1, (n_kblocks - 1) % 2, excess)
            accumulate(n_kblocks - 1, (n_kblocks - 1) % 2)

    lp = lam_ref[...]
    lam = (jnp.exp(jnp.sum(lp[0:1] * lp[1:2], axis=-1, keepdims=True))
           - jnp.exp(jnp.sum(lp[2:3] * lp[3:4], axis=-1, keepdims=True)) + lam_init)
    dv = 2 * HEAD_DIM
    y0 = acc_sc[0, :dv] / acc_sc[0, dv:dv + 1]
    y1 = acc_sc[1, :dv] / acc_sc[1, dv:dv + 1]
    y = (y0 - lam * y1).T
    ms = jnp.mean(y * y, axis=-1, keepdims=True)
    y = y * lax.rsqrt(ms + EPS) * subg_ref[...] * (1.0 - lam_init)
    o_ref[...] = y.astype(BF16)


def _diff_attention(p_q, p_lat, vt_lat, p_ctx, vt_ctx, lam_rows, subg, score_bound, nb, n_q, n_lat, ctx_len,
                    q_col, k_col, n_heads, lam_init, tq, tk):
    qsteps = n_q // tq
    has_latent = p_lat is not None
    vrows = 2 * HEAD_DIM + BF16_SUBLANES
    in_specs = [pl.BlockSpec((tq, LANES), lambda b, h, i: (b * qsteps + i, q_col + h))]
    args = [p_q]
    if has_latent:
        in_specs += [pl.BlockSpec((n_lat, LANES), lambda b, h, i: (b, k_col + h)),
                     pl.BlockSpec((1, vrows, n_lat), lambda b, h, i: (h, 0, b))]
        args += [p_lat, vt_lat]
    in_specs += [pl.BlockSpec((ctx_len, LANES), lambda b, h, i: (b, k_col + h)),
                 pl.BlockSpec((1, vrows, ctx_len), lambda b, h, i: (h, 0, b)),
                 _resident(lam_rows.shape, lambda b, h, i: (0, 0)),
                 _resident(subg.shape, lambda b, h, i: (0, 0)),
                 pl.BlockSpec(memory_space=pltpu.SMEM)]
    args += [p_ctx, vt_ctx, lam_rows, subg, score_bound]
    return pl.pallas_call(
        functools.partial(_diff_kernel, n_kblocks=n_lat // tk if has_latent else 0, tk=tk, lam_init=lam_init),
        grid=(nb, n_heads, qsteps),
        in_specs=in_specs,
        out_specs=pl.BlockSpec((tq, LANES), lambda b, h, i: (b * qsteps + i, h)),
        out_shape=jax.ShapeDtypeStruct((nb * n_q, n_heads * LANES), BF16),
        scratch_shapes=([pltpu.VMEM((2, 1, tq), F32), pltpu.VMEM((2, vrows, tq), F32)]
                        + ([pltpu.VMEM((2, 2, tk, tq), BF16)] if has_latent else [])),
        compiler_params=_params(("parallel", "parallel", "arbitrary")),
        name="diff_attention",
    )(*args)


def _silu(x):
    return x * (1.0 / (1.0 + jnp.exp(-x)))


def _ffn_kernel(*refs, n_proj, conv):
    g_ref, sc_ref, sh_ref, gate_ref, w1_ref, w3_ref, w2_ref, o_ref = refs[n_proj:]
    x = _projected_residual(refs[:n_proj], conv) if n_proj > 1 else refs[0][...]
    a = _modulated_norm(x, g_ref[...], sc_ref[0], sh_ref[0]).astype(BF16)
    h1 = jnp.dot(a, w1_ref[...], preferred_element_type=F32)
    h3 = jnp.dot(a, w3_ref[...], preferred_element_type=F32)
    y = jnp.dot((_silu(h1) * h3).astype(BF16), w2_ref[...], preferred_element_type=F32)
    o_ref[...] = x + gate_ref[0] * y


def _ffn(h, nb, g, sc, sh, gate, w1, w3, w2, tm, proj=None):
    t, d = h.shape
    steps = t // nb // tm
    row = lambda b, i: (b * steps + i, 0)
    vec = pl.BlockSpec((1, 1, d), lambda b, i: (b, 0, 0))
    if proj is None:
        in_specs, args, conv = [pl.BlockSpec((tm, d), row)], [h], False
    else:
        in_specs, args = _out_proj_operands(h, nb, *proj, tm)
        conv = proj[-1] is not None
    n_proj = len(args)
    in_specs = in_specs + [_resident((1, d), lambda b, i: (0, 0)), vec, vec, vec,
                           _resident(w1.shape, lambda b, i: (0, 0)),
                           _resident(w3.shape, lambda b, i: (0, 0)),
                           _resident(w2.shape, lambda b, i: (0, 0))]
    return pl.pallas_call(
        functools.partial(_ffn_kernel, n_proj=n_proj, conv=conv),
        grid=(nb, steps),
        in_specs=in_specs,
        out_specs=pl.BlockSpec((tm, d), row),
        out_shape=jax.ShapeDtypeStruct((t, d), F32),
        compiler_params=_params(("parallel", "parallel")),
        name="ffn",
    )(*args, g, sc, sh, gate, w1, w3, w2)


def _top2_gates(logits):
    lane = _lane_iota()
    big = jnp.int32(LANES)
    lg = jnp.where(lane < N_EXPERTS, logits, -jnp.inf)
    m1 = lg.max(axis=-1, keepdims=True)
    i1 = jnp.where(lg == m1, lane, big).min(axis=-1, keepdims=True)
    rest = jnp.where(lane == i1, -jnp.inf, lg)
    m2 = rest.max(axis=-1, keepdims=True)
    i2 = jnp.where(rest == m2, lane, big).min(axis=-1, keepdims=True)
    e2 = jnp.exp(m2 - m1)
    den = 1.0 + e2
    return jnp.where(lane == i1, 1.0 / den, 0.0) + jnp.where(lane == i2, e2 / den, 0.0)


def _moe_kernel(x_ref, g_ref, sc_ref, sh_ref, gate_ref, r_ref, w1_ref, w3_ref, w2_ref, o_ref,
                a_sc, gates_sc, rank_sc, acc_sc, *, chunk, wide_chunks):
    e = pl.program_id(2)
    tm = x_ref.shape[0]

    @pl.when(e == 0)
    def _():
        a = _modulated_norm(x_ref[...], g_ref[...], sc_ref[0], sh_ref[0])
        a_hi = a.astype(BF16)
        a_lo = (a - a_hi.astype(F32)).astype(BF16)
        r = r_ref[...]
        r_hi = r.astype(BF16)
        r_lo = (r - r_hi.astype(F32)).astype(BF16)
        logits = (jnp.dot(a_hi, r_hi, preferred_element_type=F32)
                  + jnp.dot(a_hi, r_lo, preferred_element_type=F32)
                  + jnp.dot(a_lo, r_hi, preferred_element_type=F32))
        a_sc[...] = a_hi
        gates_t = _top2_gates(logits).T[:EXPERT_ROWS]
        gates_sc[...] = gates_t
        before = (lax.broadcasted_iota(jnp.int32, (tm, tm), 0)
                  < lax.broadcasted_iota(jnp.int32, (tm, tm), 1))
        routed = jnp.where(gates_t > 0.0, 1.0, 0.0).astype(BF16)
        rank_sc[...] = jnp.dot(routed, jnp.where(before, 1.0, 0.0).astype(BF16), preferred_element_type=F32)
        acc_sc[...] = jnp.zeros_like(acc_sc)

    gate_row = gates_sc[pl.ds(e, 1), :]
    rank_row = jnp.where(gate_row > 0.0, rank_sc[pl.ds(e, 1), :], -1.0)
    count = jnp.sum(jnp.where(gate_row > 0.0, 1.0, 0.0)).astype(jnp.int32)

    def run_expert(first_rank, n_rows):
        slot = lax.broadcasted_iota(jnp.int32, (n_rows, 1), 0).astype(F32) + float(first_rank)
        pick_f = jnp.where(rank_row == slot, 1.0, 0.0)
        xs = jnp.dot(pick_f.astype(BF16), a_sc[...], preferred_element_type=F32).astype(BF16)
        h1 = jnp.dot(xs, w1_ref[0], preferred_element_type=F32)
        h3 = jnp.dot(xs, w3_ref[0], preferred_element_type=F32)
        y = jnp.dot((_silu(h1) * h3).astype(BF16), w2_ref[0], preferred_element_type=F32)
        y = y * jnp.sum(pick_f * gate_row, axis=-1, keepdims=True)
        acc_sc[...] += lax.dot_general(pick_f.astype(BF16), y.astype(BF16), (((0,), (0,)), ((), ())),
                                       preferred_element_type=F32)

    one_wide_pass = (count > chunk) & (count <= wide_chunks[-1])

    for fits_above, rows in zip((chunk,) + wide_chunks[:-1], wide_chunks):
        @pl.when((count > fits_above) & (count <= rows))
        def _():
            run_expert(0, rows)

    for j in range(pl.cdiv(tm, chunk)):
        @pl.when((count > j * chunk) & jnp.logical_not(one_wide_pass))
        def _():
            run_expert(j * chunk, chunk)

    @pl.when(e == pl.num_programs(2) - 1)
    def _():
        o_ref[...] = x_ref[...] + gate_ref[0] * acc_sc[...]


def _moe(h, nb, g, sc, sh, gate, router, w1, w3, w2, tm, chunk):
    t, d = h.shape
    steps = t // nb // tm
    n_e, _, f = w1.shape
    row = lambda b, i, e: (b * steps + i, 0)
    vec = pl.BlockSpec((1, 1, d), lambda b, i, e: (b, 0, 0))
    return pl.pallas_call(
        functools.partial(_moe_kernel, chunk=chunk, wide_chunks=MOE_WIDE_CHUNKS),
        grid=(nb, steps, n_e),
        in_specs=[pl.BlockSpec((tm, d), row), _resident((1, d), lambda b, i, e: (0, 0)), vec, vec, vec,
                  _resident(router.shape, lambda b, i, e: (0, 0)),
                  pl.BlockSpec((1, d, f), lambda b, i, e: (e, 0, 0)),
                  pl.BlockSpec((1, d, f), lambda b, i, e: (e, 0, 0)),
                  pl.BlockSpec((1, f, d), lambda b, i, e: (e, 0, 0))],
        out_specs=pl.BlockSpec((tm, d), row),
        out_shape=jax.ShapeDtypeStruct((t, d), F32),
        scratch_shapes=[pltpu.VMEM((tm, d), BF16), pltpu.VMEM((EXPERT_ROWS, tm), F32),
                        pltpu.VMEM((EXPERT_ROWS, tm), F32), pltpu.VMEM((tm, d), F32)],
        compiler_params=_params(("parallel", "parallel", "arbitrary")),
        name="moe",
    )(h, g, sc, sh, gate, router, w1, w3, w2)


def _rope_tables(n):
    t = jnp.arange(n)
    pos = jnp.stack([t // GRID_W, t % GRID_W], -1).astype(F32)
    nq = HEAD_DIM // 4
    inv = ROPE_THETA ** (-jnp.arange(nq, dtype=F32) / nq)
    ang = pos[:, :, None] * inv
    cos = jnp.repeat(jnp.cos(ang)[:, :, None, :], 2, axis=2)
    sin = jnp.stack([-jnp.sin(ang), jnp.sin(ang)], axis=2)
    cos = jnp.tile(cos.reshape(n, HEAD_DIM), (1, LANES // HEAD_DIM))
    sin = jnp.tile(sin.reshape(n, HEAD_DIM), (1, LANES // HEAD_DIM))
    return cos, sin


def _head_gain(parts, n_out):
    row = jnp.ones((n_out,), F32)
    for col, n_heads, gain, scale in parts:
        row = lax.dynamic_update_slice(row, jnp.tile(gain.astype(F32) * scale, n_heads), (col,))
    return row.reshape(1, n_out)


def kernel(x, c, ctx, c_ctx, ada_w, ada_b, norm1_g, norm2_g, ev_w_in, ev_conv_w, ev_q_g, ev_k_g, ev_rpb,
           ev_w_out, ffn_w1, ffn_w3, ffn_w2, od_w_in, od_cq_g, od_ck_g, od_sink, od_dq_g, od_dk_g,
           od_lam_q1, od_lam_k1, od_lam_q2, od_lam_k2, od_subln_g, od_w_out, moe_router,
           moe_w1, moe_w3, moe_w2):
    nb, n, d = x.shape
    ctx_len = ctx.shape[1]
    depth = ada_w.shape[0]
    n_slots = d // HEAD_DIM
    conv_ch = d // 2
    na_heads = swa_heads = n_slots // 2
    swa_kv = max(1, swa_heads // 4)
    diff_heads = n_slots // 4
    rows = n // GRID_W
    assert rows % NA_QROWS == 0 and rows >= 2 * NA_QROWS and ctx_len % LANES == 0
    assert n % max(TOKEN_TILE, DIFF_Q_TILE, DIFF_K_TILE, MOE_TOKEN_TILE) == 0
    assert nb + 1 <= F32_SUBLANES

    ev_q_col = 3 * conv_ch
    ev_k_col = ev_q_col + na_heads * HEAD_DIM
    ev_v_col = ev_k_col + na_heads * HEAD_DIM
    ev_n = ev_v_col + na_heads * HEAD_DIM
    od_dq_col = swa_heads * HEAD_DIM
    od_ck_col = od_dq_col + diff_heads * 2 * HEAD_DIM
    od_cv_col = od_ck_col + swa_kv * HEAD_DIM
    od_dk_col = od_cv_col + swa_kv * HEAD_DIM
    od_dv_col = od_dk_col + diff_heads * 2 * HEAD_DIM
    od_n = od_dv_col + diff_heads * 2 * HEAD_DIM
    ev_segs = ((0, ev_q_col, False, 0), (ev_q_col, ev_v_col, True, 0), (ev_v_col, ev_n, False, na_heads))
    od_segs = ((0, od_cv_col, True, 0), (od_cv_col, od_dk_col, False, swa_kv), (od_dk_col, od_dv_col, True, 0),
               (od_dv_col, od_n, False, diff_heads))

    h = x.reshape(nb * n, d)
    hc = ctx.reshape(nb * ctx_len, d)
    tm_lat = TOKEN_TILE
    tm_ctx = ctx_len
    tm_moe = MOE_TOKEN_TILE

    s_rows = jnp.zeros((F32_SUBLANES, d), F32).at[:nb].set(c).at[nb].set(c_ctx)
    mod = _modulation(s_rows, ada_w, ada_b)
    rope_tabs = _rope_tables(n)

    def lat_vec(l, k):
        return mod[l, :nb, k * d:(k + 1) * d].reshape(nb, 1, d)

    def ctx_vec(l, k, copies):
        return jnp.broadcast_to(mod[l, nb, k * d:(k + 1) * d], (copies, 1, d))

    for l in range(depth):
        last = l == depth - 1
        i = l // 2
        g1 = norm1_g[l].reshape(1, d)
        g2 = norm2_g[l].reshape(1, d)
        if l % 2 == 0:
            w_in = ev_w_in[i].astype(BF16)
            gain = _head_gain([(ev_q_col, na_heads, ev_q_g[i], QK_SCALE * LOG2_E), (ev_k_col, na_heads, ev_k_g[i], 1.0)],
                              ev_n)
            p_lat, vt_lat = _norm_proj(h, nb, g1, lat_vec(l, 1), lat_vec(l, 0), w_in, gain, ev_segs, None, tm_lat)
            p_ctx, vt_ctx = _norm_proj(hc, nb, g1, ctx_vec(l, 1, nb), ctx_vec(l, 0, nb), w_in, gain, ev_segs, None,
                                       tm_ctx)
            bias = _na_bias_tables(ev_rpb[i], rows)
            y_na = _na_attention(p_lat, vt_lat, p_ctx, vt_ctx, bias, nb, n, ctx_len, ev_q_col // LANES,
                                 ev_k_col // LANES, na_heads)
            w_out = ev_w_out[i].astype(BF16)
            conv_w = ev_conv_w[i]
            if not last:
                y_na_c = _gqa_attention(p_ctx, None, p_ctx, vt_ctx, None, nb, ctx_len, ctx_len, ev_q_col // LANES,
                                        ev_k_col // LANES, na_heads, na_heads, False, ctx_len)
                hc = _out_proj(hc, nb, ctx_vec(l, 2, nb), p_ctx, y_na_c, w_out, conv_w, tm_ctx)
            w1, w3, w2 = ffn_w1[i].astype(BF16), ffn_w3[i].astype(BF16), ffn_w2[i].astype(BF16)
            h = _ffn(h, nb, g2, lat_vec(l, 4), lat_vec(l, 3), lat_vec(l, 5), w1, w3, w2, tm_lat,
                     proj=(lat_vec(l, 2), p_lat, y_na, w_out, conv_w))
            if not last:
                hc = _ffn(hc, 1, g2, ctx_vec(l, 4, 1), ctx_vec(l, 3, 1), ctx_vec(l, 5, 1), w1, w3, w2, tm_lat)
        else:
            lam_init = 0.8 - 0.6 * math.exp(-0.3 * l)
            w_in = od_w_in[i].astype(BF16)
            gain = _head_gain([(0, swa_heads, od_cq_g[i], QK_SCALE * LOG2_E),
                               (od_dq_col, 2 * diff_heads, od_dq_g[i], QK_SCALE * LOG2_E),
                               (od_ck_col, swa_kv, od_ck_g[i], 1.0),
                               (od_dk_col, 2 * diff_heads, od_dk_g[i], 1.0)], od_n)
            p_lat, cvt_lat, vt_lat = _norm_proj(h, nb, g1, lat_vec(l, 1), lat_vec(l, 0), w_in, gain, od_segs, rope_tabs,
                                                tm_lat)
            p_ctx, cvt_ctx, vt_ctx = _norm_proj(hc, nb, g1, ctx_vec(l, 1, nb), ctx_vec(l, 0, nb), w_in, gain, od_segs,
                                                None, tm_ctx)
            sink = jnp.broadcast_to(od_sink[i].astype(F32)[:, None] * LOG2_E, (swa_heads, LANES))
            lam_rows = jnp.zeros((8, LANES), F32).at[:4, :HEAD_DIM].set(
                jnp.stack([od_lam_q1[i], od_lam_k1[i], od_lam_q2[i], od_lam_k2[i]]).astype(F32))
            subg = od_subln_g[i].astype(F32).reshape(1, 2 * HEAD_DIM)
            cols = (od_dq_col // LANES, od_dk_col // LANES)
            score_bound = (HEAD_DIM * QK_SCALE * LOG2_E * BF16_NORM_SLACK * jnp.max(jnp.abs(od_dq_g[i]))
                           * jnp.max(jnp.abs(od_dk_g[i]))).astype(F32).reshape(1, 1)
            y_c = _gqa_attention(p_lat, cvt_lat, p_ctx, cvt_ctx, sink, nb, n, ctx_len, 0, od_ck_col // LANES,
                                 swa_heads, swa_kv, True, 2 * BLOCK)
            y_d = _diff_attention(p_lat, p_lat, vt_lat, p_ctx, vt_ctx, lam_rows, subg, score_bound, nb, n, n, ctx_len,
                                  *cols, diff_heads, lam_init, DIFF_Q_TILE, DIFF_K_TILE)
            w_out = od_w_out[i].astype(BF16)
            h = _out_proj(h, nb, lat_vec(l, 2), y_c, y_d, w_out, None, tm_lat)
            if not last:
                y_c_c = _gqa_attention(p_ctx, None, p_ctx, cvt_ctx, sink, nb, ctx_len, ctx_len, 0,
                                       od_ck_col // LANES, swa_heads, swa_kv, False, ctx_len)
                y_d_c = _diff_attention(p_ctx, None, None, p_ctx, vt_ctx, lam_rows, subg, score_bound, nb, ctx_len, 0,
                                        ctx_len, *cols, diff_heads, lam_init, ctx_len, ctx_len)
                hc = _out_proj(hc, nb, ctx_vec(l, 2, nb), y_c_c, y_d_c, w_out, None, tm_ctx)
            router = jnp.zeros((d, LANES), F32).at[:, :N_EXPERTS].set(moe_router[i])
            w1, w3, w2 = moe_w1[i].astype(BF16), moe_w3[i].astype(BF16), moe_w2[i].astype(BF16)
            h = _moe(h, nb, g2, lat_vec(l, 4), lat_vec(l, 3), lat_vec(l, 5), router, w1, w3, w2, tm_moe, MOE_CHUNK)
            if not last:
                hc = _moe(hc, 1, g2, ctx_vec(l, 4, 1), ctx_vec(l, 3, 1), ctx_vec(l, 5, 1), router, w1, w3, w2,
                          min(tm_moe, nb * ctx_len), MOE_CHUNK)
    return h.reshape(nb, n, d)
```

```python
import functools
import math

import jax
import jax.numpy as jnp
from jax import lax
from jax.experimental import pallas as pl
from jax.experimental.pallas import tpu as pltpu

F32 = jnp.float32
BF16 = jnp.bfloat16

LANES = 128
F32_SUBLANES = 8
BF16_SUBLANES = 16
VMEM_LIMIT = 56 * 1024 * 1024

TOKEN_TILE = 512
MOE_TOKEN_TILE = 1024
DIFF_Q_TILE = 2048
DIFF_K_TILE = 1024

HEAD_DIM = 64
GRID_W = 64
CONV_W = 3
NA_ROWS = 8
NA_COLS = 16
NA_QROWS = 8
SWA_WINDOW = 128
MOE_CHUNK = 256
MOE_WIDE_CHUNKS = (288, 320, 352, 384, 448)
BLOCK = 128
N_EXPERTS = 8
EXPERT_ROWS = 16
ROPE_THETA = 10000.0
EPS = 1e-6
NEG = -1e30
QK_SCALE = HEAD_DIM ** -0.5
LOG2_E = math.log2(math.e)
BF16_NORM_SLACK = 1.02
RESCALE_MARGIN = 32.0

NT_DIMS = (((1,), (1,)), ((), ()))


def _params(sem):
    return pltpu.CompilerParams(dimension_semantics=sem, vmem_limit_bytes=VMEM_LIMIT)


def _resident(shape, index_map):
    return pl.BlockSpec(shape, index_map, pipeline_mode=pl.Buffered(1))


def _lane_iota():
    return lax.broadcasted_iota(jnp.int32, (1, LANES), 1)


def _swap_halves(x):
    return jnp.concatenate([x[:, HEAD_DIM:], x[:, :HEAD_DIM]], axis=1)


def _modulated_norm(x, g, sc, sh):
    ms = jnp.mean(x * x, axis=-1, keepdims=True)
    return (x * lax.rsqrt(ms + EPS)) * (g * (1.0 + sc)) + sh


def _mod_kernel(s_ref, w_ref, b_ref, o_ref):
    s = s_ref[...]
    s = s * (1.0 / (1.0 + jnp.exp(-s)))
    o_ref[0] = jnp.dot(s.astype(BF16), w_ref[0].astype(BF16), preferred_element_type=F32) + b_ref[0]


def _modulation(s_rows, ada_w, ada_b):
    depth, d, n_out = ada_w.shape
    tn = n_out // 4
    return pl.pallas_call(
        _mod_kernel,
        grid=(depth, n_out // tn),
        in_specs=[pl.BlockSpec(s_rows.shape, lambda l, j: (0, 0)),
                  pl.BlockSpec((1, d, tn), lambda l, j: (l, 0, j)),
                  pl.BlockSpec((1, 1, tn), lambda l, j: (l, 0, j))],
        out_specs=pl.BlockSpec((1, s_rows.shape[0], tn), lambda l, j: (l, 0, j)),
        out_shape=jax.ShapeDtypeStruct((depth, s_rows.shape[0], n_out), F32),
        compiler_params=_params(("arbitrary", "arbitrary")),
        name="modulation",
    )(s_rows, ada_w, ada_b.reshape(depth, 1, n_out))


def _head_sumsq(z):
    r = lax.broadcasted_iota(jnp.int32, (LANES, LANES), 0) // HEAD_DIM
    c = lax.broadcasted_iota(jnp.int32, (LANES, LANES), 1) // HEAD_DIM
    same_head = jnp.where(r == c, 1.0, 0.0).astype(BF16)
    z2 = z * z
    hi = z2.astype(BF16)
    lo = (z2 - hi.astype(F32)).astype(BF16)
    return (jnp.dot(hi, same_head, preferred_element_type=F32)
            + jnp.dot(lo, same_head, preferred_element_type=F32))


def _norm_proj_kernel(*refs, segs, rope, n_vt):
    refs = list(refs)
    vt_refs = [refs.pop() for _ in range(n_vt)][::-1]
    if rope:
        x_ref, g_ref, sc_ref, sh_ref, w_ref, gain_ref, cos_ref, sin_ref, o_ref = refs
    else:
        x_ref, g_ref, sc_ref, sh_ref, w_ref, gain_ref, o_ref = refs
    a = _modulated_norm(x_ref[...], g_ref[...], sc_ref[0], sh_ref[0]).astype(BF16)
    first_half = (_lane_iota() % (HEAD_DIM // 2)) < (HEAD_DIM // 4)
    for c0, c1, normed, vt_heads in segs:
        acc = jnp.dot(a, w_ref[:, c0:c1], preferred_element_type=F32)
        if vt_heads:
            vt_ref = vt_refs.pop(0)
            dv = (c1 - c0) // vt_heads
            vt = acc.T.astype(BF16)
            for hh in range(vt_heads):
                vt_ref[hh, :dv, :] = vt[hh * dv:(hh + 1) * dv]
                vt_ref[hh, dv:, :] = jnp.ones((BF16_SUBLANES, vt.shape[1]), BF16)
        if not normed:
            o_ref[:, c0:c1] = acc.astype(BF16)
            continue
        for j in range((c1 - c0) // LANES):
            z = acc[:, j * LANES:(j + 1) * LANES]
            lo = c0 + j * LANES
            z = z * lax.rsqrt(_head_sumsq(z) * (1.0 / HEAD_DIM) + EPS) * gain_ref[:, lo:lo + LANES]
            if rope:
                partner = jnp.where(first_half,
                                    pltpu.roll(z, LANES - HEAD_DIM // 4, axis=1),
                                    pltpu.roll(z, HEAD_DIM // 4, axis=1))
                z = z * cos_ref[...] + partner * sin_ref[...]
            o_ref[:, lo:lo + LANES] = z.astype(BF16)


def _norm_proj(h, nb, g, sc, sh, w, gain, segs, rope_tabs, tm):
    t, d = h.shape
    n_out = w.shape[1]
    steps = t // nb // tm
    in_specs = [pl.BlockSpec((tm, d), lambda b, i: (b * steps + i, 0)),
                _resident((1, d), lambda b, i: (0, 0)),
                pl.BlockSpec((1, 1, d), lambda b, i: (b, 0, 0)),
                pl.BlockSpec((1, 1, d), lambda b, i: (b, 0, 0)),
                _resident((d, n_out), lambda b, i: (0, 0)),
                _resident((1, n_out), lambda b, i: (0, 0))]
    args = [h, g, sc, sh, w, gain]
    if rope_tabs is not None:
        in_specs += [pl.BlockSpec((tm, LANES), lambda b, i: (i, 0))] * 2
        args += list(rope_tabs)
    out_specs = [pl.BlockSpec((tm, n_out), lambda b, i: (b * steps + i, 0))]
    out_shape = [jax.ShapeDtypeStruct((t, n_out), BF16)]
    for c0, c1, _, vt_heads in segs:
        if vt_heads:
            rows = (c1 - c0) // vt_heads + BF16_SUBLANES
            out_specs.append(pl.BlockSpec((vt_heads, rows, tm), lambda b, i: (0, 0, b * steps + i)))
            out_shape.append(jax.ShapeDtypeStruct((vt_heads, rows, t), BF16))
    return pl.pallas_call(
        functools.partial(_norm_proj_kernel, segs=segs, rope=rope_tabs is not None, n_vt=len(out_specs) - 1),
        grid=(nb, steps),
        in_specs=in_specs,
        out_specs=out_specs,
        out_shape=out_shape,
        compiler_params=_params(("parallel", "parallel")),
        name="norm_proj",
    )(*args)


def _gated_conv(gb_ref, gc_ref, u_ref, gcp_ref, up_ref, gcn_ref, un_ref, cw_ref):
    i, steps = pl.program_id(1), pl.num_programs(1)
    v = gc_ref[...].astype(F32) * u_ref[...].astype(F32)
    tm = v.shape[0]
    last = BF16_SUBLANES - 1
    prev_row = gcp_ref[last:last + 1, :].astype(F32) * up_ref[last:last + 1, :].astype(F32)
    next_row = gcn_ref[0:1, :].astype(F32) * un_ref[0:1, :].astype(F32)
    prev_row = jnp.where(i > 0, prev_row, 0.0)
    next_row = jnp.where(i < steps - 1, next_row, 0.0)
    row = lax.broadcasted_iota(jnp.int32, (tm, 1), 0)
    v_prev = jnp.where(row == 0, prev_row, pltpu.roll(v, 1, axis=0))
    v_next = jnp.where(row == tm - 1, next_row, pltpu.roll(v, tm - 1, axis=0))
    cw = cw_ref[...]
    conv = cw[0:1, :] * v_prev + cw[1:2, :] * v + cw[2:3, :] * v_next
    return gb_ref[...].astype(F32) * conv


def _projected_residual(refs, conv):
    if conv:
        (gb_ref, gc_ref, u_ref, gcp_ref, up_ref, gcn_ref, un_ref, cw_ref,
         yb_ref, wa_ref, wb_ref, h_ref, gate_ref) = refs
        ya = _gated_conv(gb_ref, gc_ref, u_ref, gcp_ref, up_ref, gcn_ref, un_ref, cw_ref).astype(BF16)
    else:
        ya_ref, yb_ref, wa_ref, wb_ref, h_ref, gate_ref = refs
        ya = ya_ref[...]
    y = (jnp.dot(ya, wa_ref[...], preferred_element_type=F32)
         + jnp.dot(yb_ref[...], wb_ref[...], preferred_element_type=F32))
    return h_ref[...] + gate_ref[0] * y


def _out_proj_kernel(*refs, conv):
    refs[-1][...] = _projected_residual(refs[:-1], conv)


def _out_proj_operands(h, nb, gate, ya_src, yb, w_out, conv_w, tm):
    t, d = h.shape
    steps = t // nb // tm
    wa_rows = w_out.shape[0] - yb.shape[1]
    w_a, w_b = w_out[:wa_rows], w_out[wa_rows:]
    row = lambda b, i: (b * steps + i, 0)
    if conv_w is not None:
        cc = conv_w.shape[1]
        hb = tm // BF16_SUBLANES
        n_halo = t // BF16_SUBLANES
        prev = lambda col: (lambda b, i: (jnp.maximum((b * steps + i) * hb - 1, 0), col))
        nxt = lambda col: (lambda b, i: (jnp.minimum((b * steps + i + 1) * hb, n_halo - 1), col))
        in_specs = [pl.BlockSpec((tm, cc), lambda b, i: (b * steps + i, 0)),
                    pl.BlockSpec((tm, cc), lambda b, i: (b * steps + i, 1)),
                    pl.BlockSpec((tm, cc), lambda b, i: (b * steps + i, 2)),
                    pl.BlockSpec((BF16_SUBLANES, cc), prev(1)),
                    pl.BlockSpec((BF16_SUBLANES, cc), prev(2)),
                    pl.BlockSpec((BF16_SUBLANES, cc), nxt(1)),
                    pl.BlockSpec((BF16_SUBLANES, cc), nxt(2)),
                    _resident(conv_w.shape, lambda b, i: (0, 0))]
        args = [ya_src] * 7 + [conv_w]
    else:
        in_specs = [pl.BlockSpec((tm, wa_rows), row)]
        args = [ya_src]
    in_specs += [pl.BlockSpec((tm, yb.shape[1]), row),
                 _resident(w_a.shape, lambda b, i: (0, 0)),
                 _resident(w_b.shape, lambda b, i: (0, 0)),
                 pl.BlockSpec((tm, d), row),
                 pl.BlockSpec((1, 1, d), lambda b, i: (b, 0, 0))]
    args += [yb, w_a, w_b, h, gate]
    return in_specs, args


def _out_proj(h, nb, gate, ya_src, yb, w_out, conv_w, tm):
    t, d = h.shape
    steps = t // nb // tm
    in_specs, args = _out_proj_operands(h, nb, gate, ya_src, yb, w_out, conv_w, tm)
    return pl.pallas_call(
        functools.partial(_out_proj_kernel, conv=conv_w is not None),
        grid=(nb, steps),
        in_specs=in_specs,
        out_specs=pl.BlockSpec((tm, d), lambda b, i: (b * steps + i, 0)),
        out_shape=jax.ShapeDtypeStruct((t, d), F32),
        compiler_params=_params(("parallel", "parallel")),
        name="out_proj",
    )(*args)


def _na_bias_tables(rpb, rows):
    n_heads = rpb.shape[0]
    n_dr = 2 * NA_ROWS - 1
    c = jnp.arange(GRID_W)
    c0 = jnp.clip(c - NA_COLS // 2, 0, GRID_W - NA_COLS)
    col_ok = (c[None, :] >= c0[:, None]) & (c[None, :] < c0[:, None] + NA_COLS)
    dc = jnp.clip(c[None, :] - c[:, None] + NA_COLS - 1, 0, 2 * NA_COLS - 2)
    pick_dc = ((dc[None] == jnp.arange(2 * NA_COLS - 1)[:, None, None]) & col_ok[None]).astype(F32)
    table = jnp.einsum('hrd,dcx->hrxc', rpb.astype(F32), pick_dc, precision=lax.Precision.HIGHEST)
    table = jnp.where(col_ok.T[None, None], table * LOG2_E, NEG)

    def paired(tab):
        pad = jnp.full((n_heads, NA_ROWS // 2, GRID_W, GRID_W), NEG, F32)
        ext = jnp.concatenate([pad, tab, pad], axis=1)
        return jnp.concatenate([ext[:, 1:], ext[:, :-1]], axis=-1)

    dr = jnp.arange(n_dr)
    in_window = (dr >= NA_ROWS // 2 - 1) & (dr < NA_ROWS // 2 - 1 + NA_ROWS)
    interior = jnp.where(in_window[None, :, None, None], table, NEG)

    nblk = rows // NA_QROWS
    i = jnp.arange(NA_QROWS)
    j = jnp.arange(2 * NA_QROWS)
    masks = []
    for rb in (0, 1, nblk - 1):
        r = rb * NA_QROWS + i
        r0 = jnp.clip(r - NA_ROWS // 2, 0, rows - NA_ROWS)
        rk = rb * NA_QROWS - NA_ROWS // 2 + j
        ok = (rk[:, None] >= r0[None, :]) & (rk[:, None] < r0[None, :] + NA_ROWS)
        masks.append(jnp.repeat(jnp.where(ok, 0.0, NEG), GRID_W, axis=1))
    return paired(table), paired(interior), jnp.stack(masks)


def _na_bias_tile(tbl_ref, hh, mask_ref):
    rows = []
    for j in range(2 * NA_QROWS):
        blk = jnp.concatenate([tbl_ref[hh, j - 2 * ii + NA_QROWS - 2] for ii in range(NA_QROWS // 2)], axis=1)
        if mask_ref is not None:
            blk = blk + mask_ref[0, j:j + 1, :]
        rows.append(blk)
    return jnp.concatenate(rows, axis=0)


def _softmax_pv_t(s_parts, vt_parts, extra=None):
    m = s_parts[0].max(axis=0, keepdims=True)
    for s in s_parts[1:]:
        m = jnp.maximum(m, s.max(axis=0, keepdims=True))
    if extra is not None:
        m = jnp.maximum(m, extra)
    acc = None
    for s, vt in zip(s_parts, vt_parts):
        pv = jnp.dot(vt, jnp.exp2((s - m).astype(BF16)), preferred_element_type=F32)
        acc = pv if acc is None else acc + pv
    den = acc[HEAD_DIM:HEAD_DIM + 1]
    if extra is not None:
        den = den + jnp.exp2(extra - m)
    return acc[:HEAD_DIM] / den


def _na_kernel(q_ref, k0, k1, k2, k3, vt0, vt1, vt2, vt3, kc_ref, vct_ref, edge_tbl_ref, tbl_ref, mask_ref, o_ref):
    def attend(tbl, mask):
        q = q_ref[...]
        k = jnp.concatenate([k0[...], k1[...], k2[...], k3[...]], axis=0)
        vt = jnp.concatenate([vt0[...], vt1[...], vt2[...], vt3[...]], axis=2)
        kc, vct = kc_ref[...], vct_ref[...]
        lane = _lane_iota()
        scores = []
        for hh in range(2):
            qh = jnp.where((lane // HEAD_DIM) == hh, q, jnp.zeros_like(q))
            s_loc = lax.dot_general(k, qh, NT_DIMS, preferred_element_type=F32) + _na_bias_tile(tbl, hh, mask)
            scores.append([s_loc, lax.dot_general(kc, qh, NT_DIMS, preferred_element_type=F32)])
        outs = [_softmax_pv_t(scores[hh], [vt[hh], vct[hh]]) for hh in range(2)]
        o_ref[...] = jnp.concatenate(outs, axis=0).T.astype(BF16)

    rb = pl.program_id(2)
    is_edge = (rb == 0) | (rb == pl.num_programs(2) - 1)

    @pl.when(jnp.logical_not(is_edge))
    def _():
        attend(tbl_ref, None)

    @pl.when(is_edge)
    def _():
        attend(edge_tbl_ref, mask_ref)


def _na_attention(p_lat, vt_lat, p_ctx, vt_ctx, bias_tables, nb, n, ctx_len, q_col, k_col, n_heads):
    tq = NA_QROWS * GRID_W
    tw = tq // 2
    qsteps = n // tq
    wsteps = n // tw
    edge_tbl, tbl, masks = bias_tables
    wblock = lambda rb, j: jnp.clip(2 * rb - 1 + j, 0, wsteps - 1)
    kwin = lambda j: pl.BlockSpec((tw, LANES), lambda hp, b, rb: (b * wsteps + wblock(rb, j), k_col + hp))
    vrows = HEAD_DIM + BF16_SUBLANES
    vwin = lambda j: pl.BlockSpec((2, vrows, tw), lambda hp, b, rb: (hp, 0, b * wsteps + wblock(rb, j)))
    variant = lambda rb: jnp.where(rb == 0, 0, jnp.where(rb == qsteps - 1, 2, 1))
    tbl_spec = pl.BlockSpec((2,) + tbl.shape[1:], lambda hp, b, rb: (hp, 0, 0, 0))
    in_specs = ([pl.BlockSpec((tq, LANES), lambda hp, b, rb: (b * qsteps + rb, q_col + hp))]
                + [kwin(j) for j in range(4)] + [vwin(j) for j in range(4)]
                + [pl.BlockSpec((ctx_len, LANES), lambda hp, b, rb: (b, k_col + hp)),
                   pl.BlockSpec((2, vrows, ctx_len), lambda hp, b, rb: (hp, 0, b)),
                   tbl_spec, tbl_spec,
                   pl.BlockSpec((1,) + masks.shape[1:], lambda hp, b, rb: (variant(rb), 0, 0))])
    return pl.pallas_call(
        _na_kernel,
        grid=(n_heads // 2, nb, qsteps),
        in_specs=in_specs,
        out_specs=pl.BlockSpec((tq, LANES), lambda hp, b, rb: (b * qsteps + rb, hp)),
        out_shape=jax.ShapeDtypeStruct((nb * n, n_heads * HEAD_DIM), BF16),
        compiler_params=_params(("parallel", "parallel", "arbitrary")),
        name="na_attention",
    )(*([p_lat] * 5 + [vt_lat] * 4 + [p_ctx, vt_ctx, edge_tbl, tbl, masks]))


def _gqa_kernel(*refs, n_q, n_kv, has_local, has_sink, n_tokens):
    refs = list(refs)
    q_ref = refs.pop(0)
    if has_local:
        k_loc = jnp.concatenate([refs.pop(0)[...] for _ in range(4)], axis=0)
        vt_loc = jnp.concatenate([refs.pop(0)[...] for _ in range(4)], axis=2)
    kc_ref, vct_ref = refs.pop(0), refs.pop(0)
    sink_ref = refs.pop(0) if has_sink else None
    o_ref = refs.pop(0)

    tq = q_ref.shape[0]
    group = n_q // n_kv
    lane = _lane_iota()
    if has_local:
        base = pl.program_id(1) * tq
        kpos = base - BLOCK + lax.broadcasted_iota(jnp.int32, (tq + 2 * BLOCK, 1), 0)
        qpos = base + lax.broadcasted_iota(jnp.int32, (1, tq), 1)
        ok = (jnp.abs(kpos - qpos) <= SWA_WINDOW) & (kpos >= 0) & (kpos < n_tokens)
        band = jnp.where(ok, 0.0, NEG)
        band = jnp.concatenate([band] * group, axis=1)

    outs = []
    for kvh in range(n_kv):
        cg, half = kvh // 2, kvh % 2
        qs = []
        for h in range(kvh * group, (kvh + 1) * group):
            qg = q_ref[:, (h // 2) * LANES:(h // 2 + 1) * LANES]
            if h % 2 != half:
                qg = _swap_halves(qg)
            qs.append(jnp.where((lane // HEAD_DIM) == half, qg, jnp.zeros_like(qg)))
        qcat = jnp.concatenate(qs, axis=0) if group > 1 else qs[0]
        kc = kc_ref[:, cg * LANES:(cg + 1) * LANES]
        s_parts = [lax.dot_general(kc, qcat, NT_DIMS, preferred_element_type=F32)]
        vt_parts = [vct_ref[kvh]]
        if has_local:
            s_parts.append(lax.dot_general(k_loc, qcat, NT_DIMS, preferred_element_type=F32) + band)
            vt_parts.append(vt_loc[kvh])
        extra = None
        if has_sink:
            extra = jnp.concatenate([sink_ref[h:h + 1, :] for h in range(kvh * group, (kvh + 1) * group)
                                     for _ in range(tq // LANES)], axis=1)
        o = _softmax_pv_t(s_parts, vt_parts, extra)
        outs += [o[:, g * tq:(g + 1) * tq] for g in range(group)]
    o_ref[...] = jnp.concatenate(outs, axis=0).T.astype(BF16)


def _gqa_attention(p_q, vt_q, p_ctx, vt_ctx, sink, nb, n, ctx_len, q_col, k_col, n_q, n_kv, has_local, tq):
    qsteps = n // tq
    kv_w = n_kv * HEAD_DIM
    q_w = n_q * HEAD_DIM
    vrows = HEAD_DIM + BF16_SUBLANES
    in_specs = [pl.BlockSpec((tq, q_w), lambda b, i: (b * qsteps + i, q_col * LANES // q_w))]
    args = [p_q]
    if has_local:
        assert tq == 2 * BLOCK and kv_w == LANES
        wsteps = n // BLOCK
        wblock = lambda i, j: jnp.clip(2 * i - 1 + j, 0, wsteps - 1)
        in_specs += [pl.BlockSpec((BLOCK, LANES), functools.partial(
            lambda b, i, j: (b * wsteps + wblock(i, j), k_col), j=j)) for j in range(4)]
        in_specs += [pl.BlockSpec((n_kv, vrows, BLOCK), functools.partial(
            lambda b, i, j: (0, 0, b * wsteps + wblock(i, j)), j=j)) for j in range(4)]
        args += [p_q] * 4 + [vt_q] * 4
    in_specs += [pl.BlockSpec((ctx_len, kv_w), lambda b, i: (b, k_col * LANES // kv_w)),
                 pl.BlockSpec((n_kv, vrows, ctx_len), lambda b, i: (0, 0, b))]
    args += [p_ctx, vt_ctx]
    if sink is not None:
        in_specs.append(_resident(sink.shape, lambda b, i: (0, 0)))
        args.append(sink)
    return pl.pallas_call(
        functools.partial(_gqa_kernel, n_q=n_q, n_kv=n_kv, has_local=has_local,
                          has_sink=sink is not None, n_tokens=n),
        grid=(nb, qsteps),
        in_specs=in_specs,
        out_specs=pl.BlockSpec((tq, q_w), lambda b, i: (b * qsteps + i, 0)),
        out_shape=jax.ShapeDtypeStruct((nb * n, q_w), BF16),
        compiler_params=_params(("parallel", "parallel")),
        name="gqa_attention",
    )(*args)


def _diff_kernel(*refs, n_kblocks, tk, lam_init):
    if n_kblocks:
        q_ref, k_ref, vt_ref, kc_ref, vct_ref, lam_ref, subg_ref, bound_ref, o_ref, m_sc, acc_sc, p_sc = refs
    else:
        q_ref, kc_ref, vct_ref, lam_ref, subg_ref, bound_ref, o_ref, m_sc, acc_sc = refs
    lane = _lane_iota()
    q = q_ref[...]
    zero = jnp.zeros_like(q)
    q_maps = [jnp.where(lane < HEAD_DIM, q, zero), jnp.where(lane >= HEAD_DIM, q, zero)]

    kc, vct = kc_ref[...], vct_ref[0]
    for i in range(2):
        s = lax.dot_general(kc, q_maps[i], NT_DIMS, preferred_element_type=F32)
        m = s.max(axis=0, keepdims=True)
        p = jnp.exp2((s - m).astype(BF16))
        m_sc[i] = m
        acc_sc[i] = jnp.dot(vct, p, preferred_element_type=F32)

    if n_kblocks:
        def scores(kb):
            k = k_ref[pl.ds(pl.multiple_of(kb * tk, tk), tk), :]
            return [lax.dot_general(k, q_maps[i], NT_DIMS, preferred_element_type=F32) for i in range(2)]

        def probs(kb, slot):
            s = scores(kb)
            excess = None
            for i in range(2):
                m_ref = m_sc[i]
                p_sc[slot, i] = jnp.exp2((s[i] - m_ref).astype(BF16))
                over = jnp.max(s[i].max(axis=0, keepdims=True) - m_ref)
                excess = over if excess is None else jnp.maximum(excess, over)
            return excess

        def settle(kb, slot, excess):
            @pl.when(excess > RESCALE_MARGIN)
            def _():
                s = scores(kb)
                for i in range(2):
                    m_prev = m_sc[i]
                    m_new = jnp.maximum(m_prev, s[i].max(axis=0, keepdims=True))
                    p_sc[slot, i] = jnp.exp2((s[i] - m_new).astype(BF16))
                    acc_sc[i] = jnp.exp2(m_prev - m_new) * acc_sc[i]
                    m_sc[i] = m_new

        def accumulate(kb, slot):
            vt = vt_ref[0, :, pl.ds(pl.multiple_of(kb * tk, tk), tk)]
            for i in range(2):
                acc_sc[i] += jnp.dot(vt, p_sc[slot, i], preferred_element_type=F32)

        def probs_unchecked(kb, slot):
            s = scores(kb)
            for i in range(2):
                p_sc[slot, i] = jnp.exp2((s[i] - m_sc[i]).astype(BF16))

        lowest_ref = jnp.minimum(jnp.min(m_sc[0]), jnp.min(m_sc[1]))
        never_rescales = bound_ref[0, 0] - lowest_ref <= RESCALE_MARGIN
        first_excess = probs(0, 0)

        @pl.when(never_rescales)
        def _():
            def body(kb, carry):
                accumulate(kb - 1, (kb - 1) % 2)
                probs_unchecked(kb, kb % 2)
                return carry

            lax.fori_loop(1, n_kblocks, body, 0)
            accumulate(n_kblocks - 1, (n_kblocks - 1) % 2)

        @pl.when(jnp.logical_not(never_rescales))
        def _():
            def body(kb, excess):
                settle(kb - 1, (kb - 1) % 2, excess)
                accumulate(kb - 1, (kb - 1) % 2)
                return probs(kb, kb % 2)

            excess = lax.fori_loop(1, n_kblocks, body, first_excess)
            settle(n_kblocks - 1, (n_kblocks - 1) % 2, excess)
            accumulate(n_kblocks - 1, (n_kblocks - 1) % 2)

    lp = lam_ref[...]
    lam = (jnp.exp(jnp.sum(lp[0:1] * lp[1:2], axis=-1, keepdims=True))
           - jnp.exp(jnp.sum(lp[2:3] * lp[3:4], axis=-1, keepdims=True)) + lam_init)
    dv = 2 * HEAD_DIM
    y0 = acc_sc[0, :dv] / acc_sc[0, dv:dv + 1]
    y1 = acc_sc[1, :dv] / acc_sc[1, dv:dv + 1]
    y = (y0 - lam * y1).T
    ms = jnp.mean(y * y, axis=-1, keepdims=True)
    y = y * lax.rsqrt(ms + EPS) * subg_ref[...] * (1.0 - lam_init)
    o_ref[...] = y.astype(BF16)


def _diff_attention(p_q, p_lat, vt_lat, p_ctx, vt_ctx, lam_rows, subg, score_bound, nb, n_q, n_lat, ctx_len,
                    q_col, k_col, n_heads, lam_init, tq, tk):
    qsteps = n_q // tq
    has_latent = p_lat is not None
    vrows = 2 * HEAD_DIM + BF16_SUBLANES
    in_specs = [pl.BlockSpec((tq, LANES), lambda b, h, i: (b * qsteps + i, q_col + h))]
    args = [p_q]
    if has_latent:
        in_specs += [pl.BlockSpec((n_lat, LANES), lambda b, h, i: (b, k_col + h)),
                     pl.BlockSpec((1, vrows, n_lat), lambda b, h, i: (h, 0, b))]
        args += [p_lat, vt_lat]
    in_specs += [pl.BlockSpec((ctx_len, LANES), lambda b, h, i: (b, k_col + h)),
                 pl.BlockSpec((1, vrows, ctx_len), lambda b, h, i: (h, 0, b)),
                 _resident(lam_rows.shape, lambda b, h, i: (0, 0)),
                 _resident(subg.shape, lambda b, h, i: (0, 0)),
                 pl.BlockSpec(memory_space=pltpu.SMEM)]
    args += [p_ctx, vt_ctx, lam_rows, subg, score_bound]
    return pl.pallas_call(
        functools.partial(_diff_kernel, n_kblocks=n_lat // tk if has_latent else 0, tk=tk, lam_init=lam_init),
        grid=(nb, n_heads, qsteps),
        in_specs=in_specs,
        out_specs=pl.BlockSpec((tq, LANES), lambda b, h, i: (b * qsteps + i, h)),
        out_shape=jax.ShapeDtypeStruct((nb * n_q, n_heads * LANES), BF16),
        scratch_shapes=([pltpu.VMEM((2, 1, tq), F32), pltpu.VMEM((2, vrows, tq), F32)]
                        + ([pltpu.VMEM((2, 2, tk, tq), BF16)] if has_latent else [])),
        compiler_params=_params(("parallel", "parallel", "arbitrary")),
        name="diff_attention",
    )(*args)


def _silu(x):
    return x * (1.0 / (1.0 + jnp.exp(-x)))


def _ffn_kernel(*refs, n_proj, conv):
    g_ref, sc_ref, sh_ref, gate_ref, w1_ref, w3_ref, w2_ref, o_ref = refs[n_proj:]
    x = _projected_residual(refs[:n_proj], conv) if n_proj > 1 else refs[0][...]
    a = _modulated_norm(x, g_ref[...], sc_ref[0], sh_ref[0]).astype(BF16)
    h1 = jnp.dot(a, w1_ref[...], preferred_element_type=F32)
    h3 = jnp.dot(a, w3_ref[...], preferred_element_type=F32)
    y = jnp.dot((_silu(h1) * h3).astype(BF16), w2_ref[...], preferred_element_type=F32)
    o_ref[...] = x + gate_ref[0] * y


def _ffn(h, nb, g, sc, sh, gate, w1, w3, w2, tm, proj=None):
    t, d = h.shape
    steps = t // nb // tm
    row = lambda b, i: (b * steps + i, 0)
    vec = pl.BlockSpec((1, 1, d), lambda b, i: (b, 0, 0))
    if proj is None:
        in_specs, args, conv = [pl.BlockSpec((tm, d), row)], [h], False
    else:
        in_specs, args = _out_proj_operands(h, nb, *proj, tm)
        conv = proj[-1] is not None
    n_proj = len(args)
    in_specs = in_specs + [_resident((1, d), lambda b, i: (0, 0)), vec, vec, vec,
                           _resident(w1.shape, lambda b, i: (0, 0)),
                           _resident(w3.shape, lambda b, i: (0, 0)),
                           _resident(w2.shape, lambda b, i: (0, 0))]
    return pl.pallas_call(
        functools.partial(_ffn_kernel, n_proj=n_proj, conv=conv),
        grid=(nb, steps),
        in_specs=in_specs,
        out_specs=pl.BlockSpec((tm, d), row),
        out_shape=jax.ShapeDtypeStruct((t, d), F32),
        compiler_params=_params(("parallel", "parallel")),
        name="ffn",
    )(*args, g, sc, sh, gate, w1, w3, w2)


def _top2_gates(logits):
    lane = _lane_iota()
    big = jnp.int32(LANES)
    lg = jnp.where(lane < N_EXPERTS, logits, -jnp.inf)
    m1 = lg.max(axis=-1, keepdims=True)
    i1 = jnp.where(lg == m1, lane, big).min(axis=-1, keepdims=True)
    rest = jnp.where(lane == i1, -jnp.inf, lg)
    m2 = rest.max(axis=-1, keepdims=True)
    i2 = jnp.where(rest == m2, lane, big).min(axis=-1, keepdims=True)
    e2 = jnp.exp(m2 - m1)
    den = 1.0 + e2
    return jnp.where(lane == i1, 1.0 / den, 0.0) + jnp.where(lane == i2, e2 / den, 0.0)


def _moe_kernel(x_ref, g_ref, sc_ref, sh_ref, gate_ref, r_ref, w1_ref, w3_ref, w2_ref, o_ref,
                a_sc, gates_sc, rank_sc, acc_sc, *, chunk, wide_chunks):
    e = pl.program_id(2)
    tm = x_ref.shape[0]

    @pl.when(e == 0)
    def _():
        a = _modulated_norm(x_ref[...], g_ref[...], sc_ref[0], sh_ref[0])
        a_hi = a.astype(BF16)
        a_lo = (a - a_hi.astype(F32)).astype(BF16)
        r = r_ref[...]
        r_hi = r.astype(BF16)
        r_lo = (r - r_hi.astype(F32)).astype(BF16)
        logits = (jnp.dot(a_hi, r_hi, preferred_element_type=F32)
                  + jnp.dot(a_hi, r_lo, preferred_element_type=F32)
                  + jnp.dot(a_lo, r_hi, preferred_element_type=F32))
        a_sc[...] = a_hi
        gates_t = _top2_gates(logits).T[:EXPERT_ROWS]
        gates_sc[...] = gates_t
        before = (lax.broadcasted_iota(jnp.int32, (tm, tm), 0)
                  < lax.broadcasted_iota(jnp.int32, (tm, tm), 1))
        routed = jnp.where(gates_t > 0.0, 1.0, 0.0).astype(BF16)
        rank_sc[...] = jnp.dot(routed, jnp.where(before, 1.0, 0.0).astype(BF16), preferred_element_type=F32)
        acc_sc[...] = jnp.zeros_like(acc_sc)

    gate_row = gates_sc[pl.ds(e, 1), :]
    rank_row = jnp.where(gate_row > 0.0, rank_sc[pl.ds(e, 1), :], -1.0)
    count = jnp.sum(jnp.where(gate_row > 0.0, 1.0, 0.0)).astype(jnp.int32)

    def run_expert(first_rank, n_rows):
        slot = lax.broadcasted_iota(jnp.int32, (n_rows, 1), 0).astype(F32) + float(first_rank)
        pick_f = jnp.where(rank_row == slot, 1.0, 0.0)
        xs = jnp.dot(pick_f.astype(BF16), a_sc[...], preferred_element_type=F32).astype(BF16)
        h1 = jnp.dot(xs, w1_ref[0], preferred_element_type=F32)
        h3 = jnp.dot(xs, w3_ref[0], preferred_element_type=F32)
        y = jnp.dot((_silu(h1) * h3).astype(BF16), w2_ref[0], preferred_element_type=F32)
        y = y * jnp.sum(pick_f * gate_row, axis=-1, keepdims=True)
        acc_sc[...] += lax.dot_general(pick_f.astype(BF16), y.astype(BF16), (((0,), (0,)), ((), ())),
                                       preferred_element_type=F32)

    one_wide_pass = (count > chunk) & (count <= wide_chunks[-1])

    for fits_above, rows in zip((chunk,) + wide_chunks[:-1], wide_chunks):
        @pl.when((count > fits_above) & (count <= rows))
        def _():
            run_expert(0, rows)

    for j in range(pl.cdiv(tm, chunk)):
        @pl.when((count > j * chunk) & jnp.logical_not(one_wide_pass))
        def _():
            run_expert(j * chunk, chunk)

    @pl.when(e == pl.num_programs(2) - 1)
    def _():
        o_ref[...] = x_ref[...] + gate_ref[0] * acc_sc[...]


def _moe(h, nb, g, sc, sh, gate, router, w1, w3, w2, tm, chunk):
    t, d = h.shape
    steps = t // nb // tm
    n_e, _, f = w1.shape
    row = lambda b, i, e: (b * steps + i, 0)
    vec = pl.BlockSpec((1, 1, d), lambda b, i, e: (b, 0, 0))
    return pl.pallas_call(
        functools.partial(_moe_kernel, chunk=chunk, wide_chunks=MOE_WIDE_CHUNKS),
        grid=(nb, steps, n_e),
        in_specs=[pl.BlockSpec((tm, d), row), _resident((1, d), lambda b, i, e: (0, 0)), vec, vec, vec,
                  _resident(router.shape, lambda b, i, e: (0, 0)),
                  pl.BlockSpec((1, d, f), lambda b, i, e: (e, 0, 0)),
                  pl.BlockSpec((1, d, f), lambda b, i, e: (e, 0, 0)),
                  pl.BlockSpec((1, f, d), lambda b, i, e: (e, 0, 0))],
        out_specs=pl.BlockSpec((tm, d), row),
        out_shape=jax.ShapeDtypeStruct((t, d), F32),
        scratch_shapes=[pltpu.VMEM((tm, d), BF16), pltpu.VMEM((EXPERT_ROWS, tm), F32),
                        pltpu.VMEM((EXPERT_ROWS, tm), F32), pltpu.VMEM((tm, d), F32)],
        compiler_params=_params(("parallel", "parallel", "arbitrary")),
        name="moe",
    )(h, g, sc, sh, gate, router, w1, w3, w2)


def _rope_tables(n):
    t = jnp.arange(n)
    pos = jnp.stack([t // GRID_W, t % GRID_W], -1).astype(F32)
    nq = HEAD_DIM // 4
    inv = ROPE_THETA ** (-jnp.arange(nq, dtype=F32) / nq)
    ang = pos[:, :, None] * inv
    cos = jnp.repeat(jnp.cos(ang)[:, :, None, :], 2, axis=2)
    sin = jnp.stack([-jnp.sin(ang), jnp.sin(ang)], axis=2)
    cos = jnp.tile(cos.reshape(n, HEAD_DIM), (1, LANES // HEAD_DIM))
    sin = jnp.tile(sin.reshape(n, HEAD_DIM), (1, LANES // HEAD_DIM))
    return cos, sin


def _head_gain(parts, n_out):
    row = jnp.ones((n_out,), F32)
    for col, n_heads, gain, scale in parts:
        row = lax.dynamic_update_slice(row, jnp.tile(gain.astype(F32) * scale, n_heads), (col,))
    return row.reshape(1, n_out)


def kernel(x, c, ctx, c_ctx, ada_w, ada_b, norm1_g, norm2_g, ev_w_in, ev_conv_w, ev_q_g, ev_k_g, ev_rpb,
           ev_w_out, ffn_w1, ffn_w3, ffn_w2, od_w_in, od_cq_g, od_ck_g, od_sink, od_dq_g, od_dk_g,
           od_lam_q1, od_lam_k1, od_lam_q2, od_lam_k2, od_subln_g, od_w_out, moe_router,
           moe_w1, moe_w3, moe_w2):
    nb, n, d = x.shape
    ctx_len = ctx.shape[1]
    depth = ada_w.shape[0]
    n_slots = d // HEAD_DIM
    conv_ch = d // 2
    na_heads = swa_heads = n_slots // 2
    swa_kv = max(1, swa_heads // 4)
    diff_heads = n_slots // 4
    rows = n // GRID_W
    assert rows % NA_QROWS == 0 and rows >= 2 * NA_QROWS and ctx_len % LANES == 0
    assert n % max(TOKEN_TILE, DIFF_Q_TILE, DIFF_K_TILE, MOE_TOKEN_TILE) == 0
    assert nb + 1 <= F32_SUBLANES

    ev_q_col = 3 * conv_ch
    ev_k_col = ev_q_col + na_heads * HEAD_DIM
    ev_v_col = ev_k_col + na_heads * HEAD_DIM
    ev_n = ev_v_col + na_heads * HEAD_DIM
    od_dq_col = swa_heads * HEAD_DIM
    od_ck_col = od_dq_col + diff_heads * 2 * HEAD_DIM
    od_cv_col = od_ck_col + swa_kv * HEAD_DIM
    od_dk_col = od_cv_col + swa_kv * HEAD_DIM
    od_dv_col = od_dk_col + diff_heads * 2 * HEAD_DIM
    od_n = od_dv_col + diff_heads * 2 * HEAD_DIM
    ev_segs = ((0, ev_q_col, False, 0), (ev_q_col, ev_v_col, True, 0), (ev_v_col, ev_n, False, na_heads))
    od_segs = ((0, od_cv_col, True, 0), (od_cv_col, od_dk_col, False, swa_kv), (od_dk_col, od_dv_col, True, 0),
               (od_dv_col, od_n, False, diff_heads))

    h = x.reshape(nb * n, d)
    hc = ctx.reshape(nb * ctx_len, d)
    tm_lat = TOKEN_TILE
    tm_ctx = ctx_len
    tm_moe = MOE_TOKEN_TILE

    s_rows = jnp.zeros((F32_SUBLANES, d), F32).at[:nb].set(c).at[nb].set(c_ctx)
    mod = _modulation(s_rows, ada_w, ada_b)
    rope_tabs = _rope_tables(n)

    def lat_vec(l, k):
        return mod[l, :nb, k * d:(k + 1) * d].reshape(nb, 1, d)

    def ctx_vec(l, k, copies):
        return jnp.broadcast_to(mod[l, nb, k * d:(k + 1) * d], (copies, 1, d))

    for l in range(depth):
        last = l == depth - 1
        i = l // 2
        g1 = norm1_g[l].reshape(1, d)
        g2 = norm2_g[l].reshape(1, d)
        if l % 2 == 0:
            w_in = ev_w_in[i].astype(BF16)
            gain = _head_gain([(ev_q_col, na_heads, ev_q_g[i], QK_SCALE * LOG2_E), (ev_k_col, na_heads, ev_k_g[i], 1.0)],
                              ev_n)
            p_lat, vt_lat = _norm_proj(h, nb, g1, lat_vec(l, 1), lat_vec(l, 0), w_in, gain, ev_segs, None, tm_lat)
            p_ctx, vt_ctx = _norm_proj(hc, nb, g1, ctx_vec(l, 1, nb), ctx_vec(l, 0, nb), w_in, gain, ev_segs, None,
                                       tm_ctx)
            bias = _na_bias_tables(ev_rpb[i], rows)
            y_na = _na_attention(p_lat, vt_lat, p_ctx, vt_ctx, bias, nb, n, ctx_len, ev_q_col // LANES,
                                 ev_k_col // LANES, na_heads)
            w_out = ev_w_out[i].astype(BF16)
            conv_w = ev_conv_w[i]
            if not last:
                y_na_c = _gqa_attention(p_ctx, None, p_ctx, vt_ctx, None, nb, ctx_len, ctx_len, ev_q_col // LANES,
                                        ev_k_col // LANES, na_heads, na_heads, False, ctx_len)
                hc = _out_proj(hc, nb, ctx_vec(l, 2, nb), p_ctx, y_na_c, w_out, conv_w, tm_ctx)
            w1, w3, w2 = ffn_w1[i].astype(BF16), ffn_w3[i].astype(BF16), ffn_w2[i].astype(BF16)
            h = _ffn(h, nb, g2, lat_vec(l, 4), lat_vec(l, 3), lat_vec(l, 5), w1, w3, w2, tm_lat,
                     proj=(lat_vec(l, 2), p_lat, y_na, w_out, conv_w))
            if not last:
                hc = _ffn(hc, 1, g2, ctx_vec(l, 4, 1), ctx_vec(l, 3, 1), ctx_vec(l, 5, 1), w1, w3, w2, tm_lat)
        else:
            lam_init = 0.8 - 0.6 * math.exp(-0.3 * l)
            w_in = od_w_in[i].astype(BF16)
            gain = _head_gain([(0, swa_heads, od_cq_g[i], QK_SCALE * LOG2_E),
                               (od_dq_col, 2 * diff_heads, od_dq_g[i], QK_SCALE * LOG2_E),
                               (od_ck_col, swa_kv, od_ck_g[i], 1.0),
                               (od_dk_col, 2 * diff_heads, od_dk_g[i], 1.0)], od_n)
            p_lat, cvt_lat, vt_lat = _norm_proj(h, nb, g1, lat_vec(l, 1), lat_vec(l, 0), w_in, gain, od_segs, rope_tabs,
                                                tm_lat)
            p_ctx, cvt_ctx, vt_ctx = _norm_proj(hc, nb, g1, ctx_vec(l, 1, nb), ctx_vec(l, 0, nb), w_in, gain, od_segs,
                                                None, tm_ctx)
            sink = jnp.broadcast_to(od_sink[i].astype(F32)[:, None] * LOG2_E, (swa_heads, LANES))
            lam_rows = jnp.zeros((8, LANES), F32).at[:4, :HEAD_DIM].set(
                jnp.stack([od_lam_q1[i], od_lam_k1[i], od_lam_q2[i], od_lam_k2[i]]).astype(F32))
            subg = od_subln_g[i].astype(F32).reshape(1, 2 * HEAD_DIM)
            cols = (od_dq_col // LANES, od_dk_col // LANES)
            score_bound = (HEAD_DIM * QK_SCALE * LOG2_E * BF16_NORM_SLACK * jnp.max(jnp.abs(od_dq_g[i]))
                           * jnp.max(jnp.abs(od_dk_g[i]))).astype(F32).reshape(1, 1)
            y_c = _gqa_attention(p_lat, cvt_lat, p_ctx, cvt_ctx, sink, nb, n, ctx_len, 0, od_ck_col // LANES,
                                 swa_heads, swa_kv, True, 2 * BLOCK)
            y_d = _diff_attention(p_lat, p_lat, vt_lat, p_ctx, vt_ctx, lam_rows, subg, score_bound, nb, n, n, ctx_len,
                                  *cols, diff_heads, lam_init, DIFF_Q_TILE, DIFF_K_TILE)
            w_out = od_w_out[i].astype(BF16)
            h = _out_proj(h, nb, lat_vec(l, 2), y_c, y_d, w_out, None, tm_lat)
            if not last:
                y_c_c = _gqa_attention(p_ctx, None, p_ctx, cvt_ctx, sink, nb, ctx_len, ctx_len, 0,
                                       od_ck_col // LANES, swa_heads, swa_kv, False, ctx_len)
                y_d_c = _diff_attention(p_ctx, None, None, p_ctx, vt_ctx, lam_rows, subg, score_bound, nb, ctx_len, 0,
                                        ctx_len, *cols, diff_heads, lam_init, ctx_len, ctx_len)
                hc = _out_proj(hc, nb, ctx_vec(l, 2, nb), y_c_c, y_d_c, w_out, None, tm_ctx)
            router = jnp.zeros((d, LANES), F32).at[:, :N_EXPERTS].set(moe_router[i])
            w1, w3, w2 = moe_w1[i].astype(BF16), moe_w3[i].astype(BF16), moe_w2[i].astype(BF16)
            h = _moe(h, nb, g2, lat_vec(l, 4), lat_vec(l, 3), lat_vec(l, 5), router, w1, w3, w2, tm_moe, MOE_CHUNK)
            if not last:
                hc = _moe(hc, 1, g2, ctx_vec(l, 4, 1), ctx_vec(l, 3, 1), ctx_vec(l, 5, 1), router, w1, w3, w2,
                          min(tm_moe, nb * ctx_len), MOE_CHUNK)
    return h.reshape(nb, n, d)
```
